```python
import jax
import jax.numpy as jnp
from jax import lax
import numpy as np

D_MODEL = 1024
BATCH = 8
SEQ = 2048
DEPTH = 1

HEAD_DIM = 64
NSA_HEADS = 8
NSA_KV_GROUPS = 2
NSA_CMP_BLOCK = 32
NSA_CMP_STRIDE = 16
NSA_CMP_HIDDEN = 128
NSA_SEL_BLOCK = 64
NSA_SEL_TOPK = 16
NSA_WINDOW = 512
WIN_QBLOCK = 128
NSA_QUERY_CHUNK = 32
MOBA_HEADS = 8
MOBA_BLOCK = 256
MOBA_TOPK = 3
MOBA_QUERY_CHUNK = 16
D_FF = 2816
PLE_DIM = 256
RMS_EPS = 1e-6
NEG_INF = -1e30
TINY = 1e-30
FORCE_SCORE = 1e9

NSA_WIDTH = NSA_HEADS * HEAD_DIM
NSA_KV_WIDTH = NSA_KV_GROUPS * HEAD_DIM
MOBA_WIDTH = MOBA_HEADS * HEAD_DIM
IN_SPLITS = (NSA_WIDTH,) + (NSA_KV_WIDTH,) * 6 + (3 * NSA_HEADS,) + (MOBA_WIDTH,) * 3 + (D_MODEL, D_MODEL)
IN_WIDTH = sum(IN_SPLITS)

kernel_name = "nsa_moba_macaron_hybrid"


def _rmsnorm(x, g):
    xf = x.astype(jnp.float32)
    y = xf * lax.rsqrt(jnp.mean(xf * xf, axis=-1, keepdims=True) + RMS_EPS)
    return (y * g.astype(jnp.float32)).astype(x.dtype)


def _swiglu(x, w1, w3, w2):
    return (jax.nn.silu(x @ w1) * (x @ w3)) @ w2


def _softmax_stats(s, mask):
    s = jnp.where(mask, s.astype(jnp.float32), NEG_INF)
    m = jnp.max(s, axis=-1, keepdims=True)
    e = jnp.where(mask, jnp.exp(s - m), 0.0)
    return e, m, jnp.sum(e, axis=-1, keepdims=True)


def _masked_softmax(s, mask):
    e, _, l = _softmax_stats(s, mask)
    return e / jnp.maximum(l, TINY)


def _alibi_slopes():
    n = NSA_HEADS + MOBA_HEADS
    slopes = jnp.exp2(-8.0 * (jnp.arange(n, dtype=jnp.float32) + 1.0) / n)
    return slopes[0::2], slopes[1::2]


def _nsa_attention(q, k_cmp, v_cmp, k_slc, v_slc, k_win, v_win, gate_logits,
                   cmp_pos_k, cmp_w1_k, cmp_w2_k, cmp_pos_v, cmp_w1_v, cmp_w2_v, slopes):
    B, S = q.shape[0], q.shape[1]
    G, Hg, hd = NSA_KV_GROUPS, NSA_HEADS // NSA_KV_GROUPS, HEAD_DIM
    scale = hd ** -0.5
    qg = q.reshape(B, S, G, Hg, hd).transpose(0, 2, 3, 1, 4)
    k_cmp, v_cmp, k_slc, v_slc, k_win, v_win = [t.transpose(0, 2, 1, 3) for t in
                                                (k_cmp, v_cmp, k_slc, v_slc, k_win, v_win)]
    sl = slopes.reshape(G, Hg)
    t_pos = jnp.arange(S)

    n_c = (S - NSA_CMP_BLOCK) // NSA_CMP_STRIDE + 1
    blk_idx = np.arange(n_c)[:, None] * NSA_CMP_STRIDE + np.arange(NSA_CMP_BLOCK)[None, :]

    def compress(t, pos, w1, w2):
        blocks = t[:, :, blk_idx] + pos
        flat = blocks.reshape(B, G, n_c, NSA_CMP_BLOCK * hd)
        return jax.nn.gelu(flat @ w1) @ w2

    kc = compress(k_cmp, cmp_pos_k, cmp_w1_k, cmp_w2_k)
    vc = compress(v_cmp, cmp_pos_v, cmp_w1_v, cmp_w2_v)
    c_end = jnp.arange(n_c) * NSA_CMP_STRIDE + NSA_CMP_BLOCK - 1
    dist_c = t_pos[:, None] - c_end[None, :]
    s_c = (jnp.einsum('bghtd,bgcd->bghtc', qg, kc).astype(jnp.float32) * scale
           - sl[:, :, None, None] * dist_c.astype(jnp.float32))
    p_c = _masked_softmax(s_c, dist_c >= 0)
    o_cmp = jnp.einsum('bghtc,bgcd->bghtd', p_c.astype(vc.dtype), vc)

    n_sel = S // NSA_SEL_BLOCK
    c_start_np = np.arange(n_c) * NSA_CMP_STRIDE
    c_end_np = c_start_np + NSA_CMP_BLOCK - 1
    s_start_np = np.arange(n_sel) * NSA_SEL_BLOCK
    s_end_np = s_start_np + NSA_SEL_BLOCK - 1
    overlap = ((c_start_np[:, None] <= s_end_np[None, :]) &
               (c_end_np[:, None] >= s_start_np[None, :])).astype(np.float32)
    imp = jnp.einsum('bghtc,cj->bgtj', p_c, jnp.asarray(overlap))
    blk = jnp.arange(n_sel)[None, :]
    cur = (t_pos // NSA_SEL_BLOCK)[:, None]
    causal = blk * NSA_SEL_BLOCK <= t_pos[:, None]
    forced = (blk == 0) | (blk == cur) | (blk == cur - 1)
    imp = jnp.where(causal, jnp.where(forced, FORCE_SCORE, imp), NEG_INF)
    k_top = min(NSA_SEL_TOPK, n_sel)
    _, sel_idx = lax.top_k(imp, k_top)
    k_blk = k_slc.reshape(B, G, n_sel, NSA_SEL_BLOCK, hd)
    v_blk = v_slc.reshape(B, G, n_sel, NSA_SEL_BLOCK, hd)
    C = NSA_QUERY_CHUNK
    nq = S // C
    q_ch = qg.reshape(B, G, Hg, nq, C, hd).transpose(3, 0, 1, 2, 4, 5)
    i_ch = sel_idx.reshape(B, G, nq, C, k_top).transpose(2, 0, 1, 3, 4)
    t_ch = t_pos.reshape(nq, C)
    b_ix = jnp.arange(B)[:, None, None, None]
    g_ix = jnp.arange(G)[None, :, None, None]
    n_keys = k_top * NSA_SEL_BLOCK

    def sel_chunk(args):
        qc, ic, tc = args
        kg = k_blk[b_ix, g_ix, ic].reshape(B, G, C, n_keys, hd)
        vg = v_blk[b_ix, g_ix, ic].reshape(B, G, C, n_keys, hd)
        kpos = (ic[..., None] * NSA_SEL_BLOCK + jnp.arange(NSA_SEL_BLOCK)).reshape(B, G, C, n_keys)
        dist = tc[:, None] - kpos
        s = (jnp.einsum('bghcd,bgcnd->bghcn', qc, kg).astype(jnp.float32) * scale
             - sl[None, :, :, None, None] * dist.astype(jnp.float32)[:, :, None])
        p = _masked_softmax(s, (dist >= 0)[:, :, None])
        return jnp.einsum('bghcn,bgcnd->bghcd', p.astype(vg.dtype), vg)

    o_sel = lax.map(sel_chunk, (q_ch, i_ch, t_ch))
    o_sel = o_sel.transpose(1, 2, 3, 0, 4, 5).reshape(B, G, Hg, S, hd)

    WB = WIN_QBLOCK
    nwb = S // WB
    nw = NSA_WINDOW // WB

    def windows(t):
        tp = jnp.pad(t, ((0, 0), (0, 0), (nw * WB, 0), (0, 0))).reshape(B, G, nwb + nw, WB, hd)
        return jnp.concatenate([tp[:, :, i:i + nwb] for i in range(nw + 1)], axis=3)

    kw = windows(k_win).transpose(2, 0, 1, 3, 4)
    vw = windows(v_win).transpose(2, 0, 1, 3, 4)
    q_blocks = qg.reshape(B, G, Hg, nwb, WB, hd).transpose(3, 0, 1, 2, 4, 5)
    qpos = jnp.arange(nwb)[:, None] * WB + jnp.arange(WB)[None, :]
    kpos_w = jnp.arange(nwb)[:, None] * WB - nw * WB + jnp.arange((nw + 1) * WB)[None, :]

    def win_block(args):
        qb, kb, vb, qp, kp = args
        dist = qp[:, None] - kp[None, :]
        mask = (dist >= 0) & (dist < NSA_WINDOW) & (kp[None, :] >= 0)
        s = (jnp.einsum('bghqd,bgkd->bghqk', qb, kb).astype(jnp.float32) * scale
             - sl[None, :, :, None, None] * dist.astype(jnp.float32))
        p = _masked_softmax(s, mask)
        return jnp.einsum('bghqk,bgkd->bghqd', p.astype(vb.dtype), vb)

    o_win = lax.map(win_block, (q_blocks, kw, vw, qpos, kpos_w))
    o_win = o_win.transpose(1, 2, 3, 0, 4, 5).reshape(B, G, Hg, S, hd)

    g = jax.nn.sigmoid(gate_logits.astype(jnp.float32)).transpose(0, 2, 3, 1, 4)
    o = g[..., 0:1] * o_cmp + g[..., 1:2] * o_sel + g[..., 2:3] * o_win
    return o.transpose(0, 3, 1, 2, 4).reshape(B, S, NSA_WIDTH).astype(q.dtype)


def _moba_attention(q, k, v, slopes):
    B, S, H, hd = q.shape
    L = MOBA_BLOCK
    nb = -(-S // L)
    S_pad = nb * L
    scale = hd ** -0.5
    q, k, v = [jnp.pad(t.transpose(0, 2, 1, 3), ((0, 0), (0, 0), (0, S_pad - S), (0, 0)))
               for t in (q, k, v)]
    t_pos = jnp.arange(S_pad)
    k_blk = k.reshape(B, H, nb, L, hd)
    v_blk = v.reshape(B, H, nb, L, hd)
    q_blk = q.reshape(B, H, nb, L, hd)

    rel = jnp.arange(L)[:, None] - jnp.arange(L)[None, :]
    s_own = (jnp.einsum('bhnqd,bhnkd->bhnqk', q_blk, k_blk).astype(jnp.float32) * scale
             - slopes[None, :, None, None, None] * rel.astype(jnp.float32))
    e_own, m_own, l_own = _softmax_stats(s_own, rel >= 0)
    o_own = jnp.einsum('bhnqk,bhnkd->bhnqd', e_own, v_blk.astype(jnp.float32)).reshape(B, H, S_pad, hd)
    m_own = m_own.reshape(B, H, S_pad, 1)
    l_own = l_own.reshape(B, H, S_pad, 1)

    k_sel = min(MOBA_TOPK, nb - 1)
    if k_sel > 0:
        k_mean = jnp.mean(k_blk.astype(jnp.float32), axis=3)
        gs = jnp.einsum('bhtd,bhnd->bhtn', q.astype(jnp.float32), k_mean)
        cur = t_pos // L
        past = jnp.arange(nb)[None, :] < cur[:, None]
        gs = jnp.where(past, gs, NEG_INF)
        _, idx = lax.top_k(gs, k_sel)
        valid = idx < cur[:, None]
        C = MOBA_QUERY_CHUNK
        nq = S_pad // C
        q_ch = q.reshape(B, H, nq, C, hd).transpose(2, 0, 1, 3, 4)
        i_ch = idx.reshape(B, H, nq, C, k_sel).transpose(2, 0, 1, 3, 4)
        v_ch = valid.reshape(B, H, nq, C, k_sel).transpose(2, 0, 1, 3, 4)
        t_ch = t_pos.reshape(nq, C)
        b_ix = jnp.arange(B)[:, None, None, None]
        h_ix = jnp.arange(H)[None, :, None, None]
        n_keys = k_sel * L

        def sel_chunk(args):
            qc, ic, vc, tc = args
            kg = k_blk[b_ix, h_ix, ic].reshape(B, H, C, n_keys, hd)
            vg = v_blk[b_ix, h_ix, ic].reshape(B, H, C, n_keys, hd)
            kpos = (ic[..., None] * L + jnp.arange(L)).reshape(B, H, C, n_keys)
            mask = jnp.broadcast_to(vc[..., None], (B, H, C, k_sel, L)).reshape(B, H, C, n_keys)
            dist = (tc[:, None] - kpos).astype(jnp.float32)
            s = (jnp.einsum('bhcd,bhcnd->bhcn', qc, kg).astype(jnp.float32) * scale
                 - slopes[None, :, None, None] * dist)
            e, m, l = _softmax_stats(s, mask)
            return jnp.einsum('bhcn,bhcnd->bhcd', e, vg.astype(jnp.float32)), m, l

        o_s, m_s, l_s = lax.map(sel_chunk, (q_ch, i_ch, v_ch, t_ch))
        o_s, m_s, l_s = [a.transpose(1, 2, 0, 3, 4).reshape(B, H, S_pad, a.shape[-1]) for a in (o_s, m_s, l_s)]
        m_tot = jnp.maximum(m_own, m_s)
        a_own = jnp.exp(m_own - m_tot)
        a_s = jnp.exp(m_s - m_tot)
        o = (o_own * a_own + o_s * a_s) / (l_own * a_own + l_s * a_s)
    else:
        o = o_own / l_own
    return o[:, :, :S].transpose(0, 2, 1, 3).reshape(B, S, H * hd).astype(k.dtype)


def _layer(h, p_i, ffn1_norm, ffn1_w1, ffn1_w3, ffn1_w2, mix_norm, w_in,
           cmp_pos_k, cmp_w1_k, cmp_w2_k, cmp_pos_v, cmp_w1_v, cmp_w2_v,
           w_up_nsa, w_up_moba, w_out, ffn2_norm, ffn2_w1, ffn2_w3, ffn2_w2,
           ple_norm, w_ple_gate, w_ple):
    B, S = h.shape[0], h.shape[1]
    h = h + 0.5 * _swiglu(_rmsnorm(h, ffn1_norm), ffn1_w1, ffn1_w3, ffn1_w2)
    u = _rmsnorm(h, mix_norm)
    z = u @ w_in
    offs = np.cumsum((0,) + IN_SPLITS)
    parts = [z[..., int(offs[i]):int(offs[i + 1])] for i in range(len(IN_SPLITS))]
    q_n = parts[0].reshape(B, S, NSA_HEADS, HEAD_DIM)
    kv_n = [t.reshape(B, S, NSA_KV_GROUPS, HEAD_DIM) for t in parts[1:7]]
    g_n = parts[7].reshape(B, S, NSA_KV_GROUPS, NSA_HEADS // NSA_KV_GROUPS, 3)
    q_m, k_m, v_m = [t.reshape(B, S, MOBA_HEADS, HEAD_DIM) for t in parts[8:11]]
    gate_a, gate_b = parts[11], parts[12]
    slopes_n, slopes_m = _alibi_slopes()
    y_n = _nsa_attention(q_n, kv_n[0], kv_n[1], kv_n[2], kv_n[3], kv_n[4], kv_n[5], g_n,
                         cmp_pos_k, cmp_w1_k, cmp_w2_k, cmp_pos_v, cmp_w1_v, cmp_w2_v, slopes_n) @ w_up_nsa
    y_m = _moba_attention(q_m, k_m, v_m, slopes_m) @ w_up_moba
    mixed = jax.nn.sigmoid(gate_a) * y_n + jax.nn.sigmoid(gate_b) * y_m
    h = h + mixed @ w_out
    h = h + 0.5 * _swiglu(_rmsnorm(h, ffn2_norm), ffn2_w1, ffn2_w3, ffn2_w2)
    gate_p = jax.nn.sigmoid(_rmsnorm(h, ple_norm) @ w_ple_gate)
    h = h + gate_p * (p_i @ w_ple).astype(h.dtype)
    return h


def setup_inputs(seed: int = 0) -> dict:
    key = jax.random.key(seed)
    ks = iter(jax.random.split(key, 40))

    def nrm(shape, scale):
        return jax.random.normal(next(ks), shape, jnp.float32) * scale

    def gain(shape):
        return 1.0 + nrm(shape, 0.02)

    L, D, F = DEPTH, D_MODEL, D_FF
    cmp_in = NSA_CMP_BLOCK * HEAD_DIM
    return {
        "x": nrm((BATCH, SEQ, D), 1.0),
        "p": nrm((L, BATCH, SEQ, PLE_DIM), 1.0),
        "ffn1_norm": gain((L, D)),
        "ffn1_w1": nrm((L, D, F), D ** -0.5),
        "ffn1_w3": nrm((L, D, F), D ** -0.5),
        "ffn1_w2": nrm((L, F, D), F ** -0.5),
        "mix_norm": gain((L, D)),
        "w_in": nrm((L, D, IN_WIDTH), D ** -0.5),
        "cmp_pos_k": nrm((L, NSA_CMP_BLOCK, HEAD_DIM), 0.1),
        "cmp_w1_k": nrm((L, cmp_in, NSA_CMP_HIDDEN), cmp_in ** -0.5),
        "cmp_w2_k": nrm((L, NSA_CMP_HIDDEN, HEAD_DIM), NSA_CMP_HIDDEN ** -0.5),
        "cmp_pos_v": nrm((L, NSA_CMP_BLOCK, HEAD_DIM), 0.1),
        "cmp_w1_v": nrm((L, cmp_in, NSA_CMP_HIDDEN), cmp_in ** -0.5),
        "cmp_w2_v": nrm((L, NSA_CMP_HIDDEN, HEAD_DIM), NSA_CMP_HIDDEN ** -0.5),
        "w_up_nsa": nrm((L, NSA_WIDTH, D), NSA_WIDTH ** -0.5),
        "w_up_moba": nrm((L, MOBA_WIDTH, D), MOBA_WIDTH ** -0.5),
        "w_out": nrm((L, D, D), D ** -0.5),
        "ffn2_norm": gain((L, D)),
        "ffn2_w1": nrm((L, D, F), D ** -0.5),
        "ffn2_w3": nrm((L, D, F), D ** -0.5),
        "ffn2_w2": nrm((L, F, D), F ** -0.5),
        "ple_norm": gain((L, D)),
        "w_ple_gate": nrm((L, D, D), D ** -0.5),
        "w_ple": nrm((L, PLE_DIM, D), PLE_DIM ** -0.5),
        "final_norm": gain((D,)),
    }


def reference(x, p, ffn1_norm, ffn1_w1, ffn1_w3, ffn1_w2, mix_norm, w_in,
              cmp_pos_k, cmp_w1_k, cmp_w2_k, cmp_pos_v, cmp_w1_v, cmp_w2_v,
              w_up_nsa, w_up_moba, w_out, ffn2_norm, ffn2_w1, ffn2_w3, ffn2_w2,
              ple_norm, w_ple_gate, w_ple, final_norm):
    h = x
    for i in range(DEPTH):
        h = _layer(h, p[i], ffn1_norm[i], ffn1_w1[i], ffn1_w3[i], ffn1_w2[i], mix_norm[i], w_in[i],
                   cmp_pos_k[i], cmp_w1_k[i], cmp_w2_k[i], cmp_pos_v[i], cmp_w1_v[i], cmp_w2_v[i],
                   w_up_nsa[i], w_up_moba[i], w_out[i], ffn2_norm[i], ffn2_w1[i], ffn2_w3[i], ffn2_w2[i],
                   ple_norm[i], w_ple_gate[i], w_ple[i])
    return _rmsnorm(h, final_norm)
```

```python
import functools

import numpy as np
import jax
import jax.numpy as jnp
from jax import lax
from jax.experimental import pallas as pl
from jax.experimental.pallas import tpu as pltpu

F32 = jnp.float32
BF16 = jnp.bfloat16

D_MODEL = 1024
HEAD_DIM = 64
NSA_HEADS = 8
NSA_KV_GROUPS = 2
NSA_HEADS_PER_GROUP = NSA_HEADS // NSA_KV_GROUPS
NSA_CMP_BLOCK = 32
NSA_CMP_STRIDE = 16
NSA_CMP_HIDDEN = 128
NSA_SEL_BLOCK = 64
NSA_SEL_TOPK = 16
NSA_WINDOW = 512
MOBA_HEADS = 8
MOBA_BLOCK = 256
MOBA_TOPK = 3
D_FF = 2816
PLE_DIM = 256
RMS_EPS = 1e-6
NEG_INF = -1e30
TINY = 1e-30
FORCE_SCORE = 1e9
SCALE = HEAD_DIM ** -0.5

NSA_WIDTH = NSA_HEADS * HEAD_DIM
NSA_KV_WIDTH = NSA_KV_GROUPS * HEAD_DIM
MOBA_WIDTH = MOBA_HEADS * HEAD_DIM
N_GATE_LOGITS = 3 * NSA_HEADS
OFF_GATE_LOGITS = NSA_WIDTH + 6 * NSA_KV_WIDTH
OFF_MOBA = OFF_GATE_LOGITS + N_GATE_LOGITS
OFF_MERGE = OFF_MOBA + 3 * MOBA_WIDTH
QKV_WIDTH = OFF_GATE_LOGITS + 3 * MOBA_WIDTH
N_QKV_HEADS = QKV_WIDTH // HEAD_DIM
HEAD_NSA_Q = 0
HEAD_NSA_KCMP = NSA_HEADS
HEAD_NSA_VCMP = HEAD_NSA_KCMP + NSA_KV_GROUPS
HEAD_NSA_KSLC = HEAD_NSA_VCMP + NSA_KV_GROUPS
HEAD_NSA_VSLC = HEAD_NSA_KSLC + NSA_KV_GROUPS
HEAD_NSA_KWIN = HEAD_NSA_VSLC + NSA_KV_GROUPS
HEAD_NSA_VWIN = HEAD_NSA_KWIN + NSA_KV_GROUPS
HEAD_MOBA_Q = HEAD_NSA_VWIN + NSA_KV_GROUPS
HEAD_MOBA_K = HEAD_MOBA_Q + MOBA_HEADS
HEAD_MOBA_V = HEAD_MOBA_K + MOBA_HEADS

LANES = 128
VMEM_LIMIT = 56 * 1024 * 1024

TOKEN_TILE = 512
FF_CHUNK = 256
ATTN_TQ = 256
ATTN_TK = 256

_NT = (((1,), (1,)), ((), ()))


def _cparams(*sem):
    return pltpu.CompilerParams(dimension_semantics=sem, vmem_limit_bytes=VMEM_LIMIT)


def _resident(shape):
    nd = len(shape)
    return pl.BlockSpec(shape, lambda *_: (0,) * nd, pipeline_mode=pl.Buffered(1))


def _rms(x, g):
    ms = jnp.mean(x * x, axis=-1, keepdims=True)
    return x * lax.rsqrt(ms + RMS_EPS) * g


def _sigmoid(x):
    return 1.0 / (1.0 + jnp.exp(-x))


def _dot(a, b):
    return jnp.dot(a, b, preferred_element_type=F32)


def _dot_f32(a, b):
    return jnp.dot(a, b, preferred_element_type=F32, precision=lax.Precision.HIGHEST)


def _dot_nt(a, b, precision=None):
    return lax.dot_general(a, b, _NT, preferred_element_type=F32, precision=precision)


def _ffn_kernel(*refs, tail, final):
    x_ref, g_ref, w1_ref, w3_ref, w2_ref = refs[:5]
    o_ref, acc_ref = refs[-2:]
    if tail:
        p_ref, gp_ref, wpg_ref, wp_ref = refs[5:9]
    if final:
        gf_ref = refs[-3]
    x = x_ref[...]
    xn = _rms(x, g_ref[...]).astype(BF16)
    for c in range(D_FF // FF_CHUNK):
        cols = slice(c * FF_CHUNK, (c + 1) * FF_CHUNK)
        a = _dot(xn, w1_ref[:, cols])
        b = _dot(xn, w3_ref[:, cols])
        hid = (a * _sigmoid(a) * b).astype(BF16)
        y = _dot(hid, w2_ref[cols, :])
        if c == 0:
            acc_ref[...] = y
        else:
            acc_ref[...] += y
    h = x + 0.5 * acc_ref[...]
    if tail:
        gate = _sigmoid(_dot(_rms(h, gp_ref[...]).astype(BF16), wpg_ref[...]))
        h = h + gate * _dot(p_ref[...].astype(BF16), wp_ref[...])
    if final:
        h = _rms(h, gf_ref[...])
    o_ref[...] = h


def _ffn(x, g, w1, w3, w2, tail=None, final_norm=None):
    n = x.shape[0]
    row = lambda w: pl.BlockSpec((TOKEN_TILE, w), lambda i: (i, 0))
    args = [x, g, w1, w3, w2]
    specs = [row(D_MODEL), _resident((1, D_MODEL)), _resident(w1.shape), _resident(w3.shape),
             _resident(w2.shape)]
    if tail is not None:
        p, gp, wpg, wp = tail
        args += [p, gp, wpg, wp]
        specs += [row(PLE_DIM), _resident((1, D_MODEL)), _resident(wpg.shape), _resident(wp.shape)]
    if final_norm is not None:
        args.append(final_norm)
        specs.append(_resident((1, D_MODEL)))
    return pl.pallas_call(
        functools.partial(_ffn_kernel, tail=tail is not None, final=final_norm is not None),
        grid=(n // TOKEN_TILE,),
        in_specs=specs,
        out_specs=row(D_MODEL),
        out_shape=jax.ShapeDtypeStruct((n, D_MODEL), F32),
        scratch_shapes=[pltpu.VMEM((TOKEN_TILE, D_MODEL), F32)],
        compiler_params=_cparams("parallel"),
        name="ffn_tail" if tail is not None else "ffn",
    )(*args)


def _inproj_kernel(h_ref, g_ref, w_ref, qkv_ref, zg_ref, merge_ref):
    u = _rms(h_ref[...], g_ref[...]).astype(BF16)
    step = 512
    for lo in range(0, QKV_WIDTH, step):
        hi = min(lo + step, QKV_WIDTH)
        qkv_ref[:, lo:hi] = _dot(u, w_ref[:, lo:hi]).astype(BF16)
    zg_ref[...] = _dot(u, w_ref[:, QKV_WIDTH:QKV_WIDTH + LANES])
    base = QKV_WIDTH + LANES
    for lo in range(0, 2 * D_MODEL, step):
        merge_ref[:, lo:lo + step] = _dot(u, w_ref[:, base + lo:base + lo + step])


def _inproj(h, g, w):
    n = h.shape[0]
    row = lambda width: pl.BlockSpec((TOKEN_TILE, width), lambda i: (i, 0))
    return pl.pallas_call(
        _inproj_kernel,
        grid=(n // TOKEN_TILE,),
        in_specs=[row(D_MODEL), _resident((1, D_MODEL)), _resident(w.shape)],
        out_specs=[row(QKV_WIDTH), row(LANES), row(2 * D_MODEL)],
        out_shape=[jax.ShapeDtypeStruct((n, QKV_WIDTH), BF16),
                   jax.ShapeDtypeStruct((n, LANES), F32),
                   jax.ShapeDtypeStruct((n, 2 * D_MODEL), F32)],
        compiler_params=_cparams("parallel"),
        name="inproj",
    )(h, g, w)


def _compress_kernel(kv_ref, pos_ref, w1_ref, w2_ref, o_ref):
    half = NSA_CMP_STRIDE * HEAD_DIM
    rows = kv_ref[...].astype(F32)
    first = _dot_f32(rows + pos_ref[:, :half], w1_ref[:half, :])
    second = _dot_f32(rows + pos_ref[:, half:], w1_ref[half:, :])
    n_rows = rows.shape[0]
    pre = first + pltpu.roll(second, n_rows - 1, 0)
    cdf = 0.5 * (1.0 + jnp.tanh(np.sqrt(2.0 / np.pi).astype(np.float32) * (pre + 0.044715 * (pre * pre * pre))))
    o_ref[...] = _dot_f32(pre * cdf, w2_ref[...])


def _compress(qkv_rows, pos, w1, w2):
    b, _, n_rows, width = qkv_rows.shape
    n_out = 2 * NSA_KV_GROUPS
    return pl.pallas_call(
        _compress_kernel,
        grid=(b, n_out),
        in_specs=[
            pl.BlockSpec((None, None, n_rows, width), lambda i, j: (i, HEAD_NSA_KCMP + j, 0, 0)),
            pl.BlockSpec((None, 1, 2 * width), lambda i, j: (j // NSA_KV_GROUPS, 0, 0)),
            pl.BlockSpec((None, 2 * width, NSA_CMP_HIDDEN), lambda i, j: (j // NSA_KV_GROUPS, 0, 0)),
            pl.BlockSpec((None, NSA_CMP_HIDDEN, HEAD_DIM), lambda i, j: (j // NSA_KV_GROUPS, 0, 0)),
        ],
        out_specs=pl.BlockSpec((None, None, n_rows, HEAD_DIM), lambda i, j: (i, j, 0, 0)),
        out_shape=jax.ShapeDtypeStruct((b, n_out, n_rows, HEAD_DIM), F32),
        compiler_params=_cparams("parallel", "parallel"),
        name="compress",
    )(qkv_rows, pos, w1, w2)


def _count_rank(score, lane, n_candidates):
    rank = jnp.zeros(score.shape, jnp.int32)
    for j in range(n_candidates):
        col = score[:, j:j + 1]
        ahead = (col > score) | ((col == score) & (lane > j))
        rank = rank + ahead.astype(jnp.int32)
    return rank


def _cmp_select_kernel(slopes_ref, q_ref, kc_ref, vc_ref, ov_ref, o_ref, sel_ref, *, n_sel):
    g = pl.program_id(1)
    qi = pl.program_id(2)
    tq = q_ref.shape[1]
    t = qi * tq + lax.broadcasted_iota(jnp.int32, (tq, LANES), 0)
    lane = lax.broadcasted_iota(jnp.int32, (tq, LANES), 1)
    dist = t - (lane * NSA_CMP_STRIDE + (NSA_CMP_BLOCK - 1))
    visible = dist >= 0
    dist_f = dist.astype(F32)
    kc = kc_ref[...]
    vc = vc_ref[...]
    p_sum = jnp.zeros((tq, LANES), F32)
    for hh in range(NSA_HEADS_PER_GROUP):
        slope = slopes_ref[g * NSA_HEADS_PER_GROUP + hh]
        q = q_ref[hh].astype(F32)
        s = _dot_nt(q, kc, lax.Precision.HIGHEST) * SCALE - slope * dist_f
        s = jnp.where(visible, s, NEG_INF)
        m = jnp.max(s, axis=-1, keepdims=True)
        e = jnp.where(visible, jnp.exp(s - m), 0.0)
        p = e / jnp.maximum(jnp.sum(e, axis=-1, keepdims=True), TINY)
        p_sum = p_sum + p
        o_ref[hh] = _dot_f32(p, vc)
    imp = _dot_f32(p_sum, ov_ref[...])
    cur = t // NSA_SEL_BLOCK
    causal = lane * NSA_SEL_BLOCK <= t
    forced = (lane == 0) | (lane == cur) | (lane == cur - 1)
    imp = jnp.where(causal, jnp.where(forced, FORCE_SCORE, imp), NEG_INF)
    rank = _count_rank(imp, lane, n_sel)
    chosen = (rank < min(NSA_SEL_TOPK, n_sel)) & causal
    sel_ref[...] = chosen.astype(BF16)


def _cmp_select(slopes, qkv_hm, kcvc, overlap):
    b, _, s, _ = qkv_hm.shape
    tq = ATTN_TQ
    hg = NSA_HEADS_PER_GROUP
    return pl.pallas_call(
        functools.partial(_cmp_select_kernel, n_sel=s // NSA_SEL_BLOCK),
        grid=(b, NSA_KV_GROUPS, s // tq),
        in_specs=[
            pl.BlockSpec(memory_space=pltpu.SMEM),
            pl.BlockSpec((None, hg, tq, HEAD_DIM), lambda i, g, q: (i, g, q, 0)),
            pl.BlockSpec((None, None, LANES, HEAD_DIM), lambda i, g, q: (i, g, 0, 0)),
            pl.BlockSpec((None, None, LANES, HEAD_DIM), lambda i, g, q: (i, NSA_KV_GROUPS + g, 0, 0)),
            pl.BlockSpec((LANES, LANES), lambda i, g, q: (0, 0)),
        ],
        out_specs=[
            pl.BlockSpec((None, hg, tq, HEAD_DIM), lambda i, g, q: (i, g, q, 0)),
            pl.BlockSpec((None, None, tq, LANES), lambda i, g, q: (i, g, q, 0)),
        ],
        out_shape=[jax.ShapeDtypeStruct((b, NSA_HEADS, s, HEAD_DIM), F32),
                   jax.ShapeDtypeStruct((b, NSA_KV_GROUPS, s, LANES), BF16)],
        compiler_params=_cparams("parallel", "parallel", "parallel"),
        name="cmp_select",
    )(slopes, qkv_hm, kcvc, kcvc, overlap)


def _moba_select_kernel(q_ref, k_ref, sel_ref, kmean_ref):
    s = q_ref.shape[0]
    nb = s // MOBA_BLOCK
    kf = k_ref[...].astype(F32)
    kmean_ref[...] = jnp.zeros(kmean_ref.shape, F32)
    kmean_ref[0:nb, :] = jnp.mean(kf.reshape(nb, MOBA_BLOCK, HEAD_DIM), axis=1)
    gs = _dot_nt(q_ref[...].astype(F32), kmean_ref[...], lax.Precision.HIGHEST)
    t = lax.broadcasted_iota(jnp.int32, (s, LANES), 0)
    lane = lax.broadcasted_iota(jnp.int32, (s, LANES), 1)
    cur = t // MOBA_BLOCK
    past = lane < cur
    gs = jnp.where(past, gs, NEG_INF)
    rank = _count_rank(gs, lane, nb)
    chosen = ((rank < min(MOBA_TOPK, nb - 1)) & past) | (lane == cur)
    sel_ref[...] = chosen.astype(BF16)


def _moba_select(qkv_hm):
    b, _, s, _ = qkv_hm.shape
    head = lambda off: pl.BlockSpec((None, None, s, HEAD_DIM), lambda i, h: (i, off + h, 0, 0))
    return pl.pallas_call(
        _moba_select_kernel,
        grid=(b, MOBA_HEADS),
        in_specs=[head(HEAD_MOBA_Q), head(HEAD_MOBA_K)],
        out_specs=pl.BlockSpec((None, None, s, LANES), lambda i, h: (i, h, 0, 0)),
        out_shape=jax.ShapeDtypeStruct((b, MOBA_HEADS, s, LANES), BF16),
        scratch_shapes=[pltpu.VMEM((LANES, HEAD_DIM), F32)],
        compiler_params=_cparams("parallel", "parallel"),
        name="moba_select",
    )(qkv_hm, qkv_hm)


def _attn_kernel(*refs, window, clamp_sum):
    if window is None:
        slopes_ref, q_ref, k_ref, v_ref, sel_ref, expand_ref, o_ref = refs
    else:
        slopes_ref, q_ref, k_ref, v_ref, o_ref = refs
    h = pl.program_id(1)
    qi = pl.program_id(2)
    tq, tk = ATTN_TQ, ATTN_TK
    slope = slopes_ref[h]
    q = q_ref[...] * SCALE
    row = lax.broadcasted_iota(jnp.int32, (tq, tk), 0)
    col = lax.broadcasted_iota(jnp.int32, (tq, tk), 1)
    first = 0 if window is None else jnp.maximum(qi - window // tk, 0)

    def body(c, carry):
        m, l, acc = carry
        keys = pl.ds(pl.multiple_of(c * tk, tk), tk)
        dist = (qi * tq - c * tk) + row - col
        valid = dist >= 0
        if window is None:
            picked = _dot(sel_ref[...], expand_ref[:, keys])
            valid = valid & (picked > 0.5)
        else:
            valid = valid & (dist < window)
        s = _dot_nt(q, k_ref[keys, :]) - slope * dist.astype(F32)
        s = jnp.where(valid, s, NEG_INF)
        m_new = jnp.maximum(m, jnp.max(s, axis=-1, keepdims=True))
        alpha = jnp.exp(m - m_new)
        e = jnp.where(valid, jnp.exp(s - m_new), 0.0)
        l = alpha * l + jnp.sum(e, axis=-1, keepdims=True)
        acc = alpha * acc + _dot(e.astype(BF16), v_ref[keys, :])
        return m_new, l, acc

    init = (jnp.full((tq, 1), NEG_INF, F32), jnp.zeros((tq, 1), F32), jnp.zeros((tq, HEAD_DIM), F32))
    _, l, acc = lax.fori_loop(first, qi + 1, body, init)
    o_ref[...] = acc / (jnp.maximum(l, TINY) if clamp_sum else l)


def _attention(slopes, qkv_hm, *, q_head, k_head, v_head, n_heads, heads_per_kv, window=None,
               sel=None, heads_per_sel=1, expand=None, clamp_sum):
    b, _, s, _ = qkv_hm.shape
    tq = ATTN_TQ
    in_specs = [
        pl.BlockSpec(memory_space=pltpu.SMEM),
        pl.BlockSpec((None, None, tq, HEAD_DIM), lambda i, h, q: (i, q_head + h, q, 0)),
        pl.BlockSpec((None, None, s, HEAD_DIM), lambda i, h, q: (i, k_head + h // heads_per_kv, 0, 0)),
        pl.BlockSpec((None, None, s, HEAD_DIM), lambda i, h, q: (i, v_head + h // heads_per_kv, 0, 0)),
    ]
    args = [slopes, qkv_hm, qkv_hm, qkv_hm]
    if window is None:
        in_specs += [
            pl.BlockSpec((None, None, tq, LANES), lambda i, h, q: (i, h // heads_per_sel, q, 0)),
            pl.BlockSpec((LANES, s), lambda i, h, q: (0, 0)),
        ]
        args += [sel, expand]
    return pl.pallas_call(
        functools.partial(_attn_kernel, window=window, clamp_sum=clamp_sum),
        grid=(b, n_heads, s // tq),
        in_specs=in_specs,
        out_specs=pl.BlockSpec((None, None, tq, HEAD_DIM), lambda i, h, q: (i, h, q, 0)),
        out_shape=jax.ShapeDtypeStruct((b, n_heads, s, HEAD_DIM), F32),
        compiler_params=_cparams("parallel", "parallel", "parallel"),
        name="attn_window" if window is not None else "attn_blocks",
    )(*args)


def _mix_kernel(h_ref, zg_ref, merge_ref, ocmp_ref, osel_ref, owin_ref, omoba_ref, spread_ref,
                wn_ref, wm_ref, wo_ref, o_ref):
    gates = _sigmoid(zg_ref[...])
    o_nsa = None
    for r, branch in enumerate((ocmp_ref, osel_ref, owin_ref)):
        term = _dot_f32(gates, spread_ref[r]) * branch[...]
        o_nsa = term if o_nsa is None else o_nsa + term
    y_n = _dot(o_nsa.astype(BF16), wn_ref[...])
    y_m = _dot(omoba_ref[...].astype(BF16), wm_ref[...])
    mixed = _sigmoid(merge_ref[:, :D_MODEL]) * y_n + _sigmoid(merge_ref[:, D_MODEL:]) * y_m
    o_ref[...] = h_ref[...] + _dot(mixed.astype(BF16), wo_ref[...])


def _mix(h, zg, merge, ocmp, osel, owin, omoba, spread, wn, wm, wo):
    n = h.shape[0]
    row = lambda width: pl.BlockSpec((TOKEN_TILE, width), lambda i: (i, 0))
    return pl.pallas_call(
        _mix_kernel,
        grid=(n // TOKEN_TILE,),
        in_specs=[row(D_MODEL), row(LANES), row(2 * D_MODEL), row(NSA_WIDTH), row(NSA_WIDTH),
                  row(NSA_WIDTH), row(MOBA_WIDTH), _resident(spread.shape), _resident(wn.shape),
                  _resident(wm.shape), _resident(wo.shape)],
        out_specs=row(D_MODEL),
        out_shape=jax.ShapeDtypeStruct((n, D_MODEL), F32),
        compiler_params=_cparams("parallel"),
        name="mix",
    )(h, zg, merge, ocmp, osel, owin, omoba, spread, wn, wm, wo)


def _block_expander(block, s):
    return jnp.asarray(np.arange(LANES)[:, None] == (np.arange(s)[None, :] // block), BF16)


def _token_major(o):
    b, h, s, d = o.shape
    return o.transpose(0, 2, 1, 3).reshape(b * s, h * d)


def kernel(x, p, ffn1_norm, ffn1_w1, ffn1_w3, ffn1_w2, mix_norm, w_in, cmp_pos_k, cmp_w1_k, cmp_w2_k, cmp_pos_v, cmp_w1_v, cmp_w2_v, w_up_nsa, w_up_moba, w_out, ffn2_norm, ffn2_w1, ffn2_w3, ffn2_w2, ple_norm, w_ple_gate, w_ple, final_norm):
    b, s, d = x.shape
    n = b * s
    depth = p.shape[0]
    n_all = NSA_HEADS + MOBA_HEADS
    slopes = jnp.exp2(-8.0 * (jnp.arange(n_all, dtype=F32) + 1.0) / n_all)
    slopes_n, slopes_m = slopes[0::2], slopes[1::2]

    n_cmp = (s - NSA_CMP_BLOCK) // NSA_CMP_STRIDE + 1
    n_sel = s // NSA_SEL_BLOCK
    c_start = np.arange(LANES) * NSA_CMP_STRIDE
    s_start = np.arange(LANES) * NSA_SEL_BLOCK
    overlap = ((c_start[:, None] <= s_start[None, :] + NSA_SEL_BLOCK - 1)
               & (c_start[:, None] + NSA_CMP_BLOCK - 1 >= s_start[None, :])
               & (np.arange(LANES)[:, None] < n_cmp) & (np.arange(LANES)[None, :] < n_sel))
    overlap = jnp.asarray(overlap, F32)
    spread_np = np.zeros((3, LANES, NSA_WIDTH), np.float32)
    for r in range(3):
        for hh in range(NSA_HEADS):
            spread_np[r, 3 * hh + r, hh * HEAD_DIM:(hh + 1) * HEAD_DIM] = 1.0
    spread = jnp.asarray(spread_np)
    expand_sel = _block_expander(NSA_SEL_BLOCK, s)
    expand_moba = _block_expander(MOBA_BLOCK, s)

    h = x.reshape(n, d)
    for i in range(depth):
        vec = lambda a: a[i].reshape(1, -1)
        h = _ffn(h, vec(ffn1_norm), ffn1_w1[i].astype(BF16), ffn1_w3[i].astype(BF16),
                 ffn1_w2[i].astype(BF16))

        wi = w_in[i]
        w_cat = jnp.concatenate([
            wi[:, :OFF_GATE_LOGITS], wi[:, OFF_MOBA:OFF_MERGE],
            jnp.pad(wi[:, OFF_GATE_LOGITS:OFF_MOBA], ((0, 0), (0, LANES - N_GATE_LOGITS))),
            wi[:, OFF_MERGE:]], axis=1).astype(BF16)
        qkv, zg, merge = _inproj(h, vec(mix_norm), w_cat)
        qkv_hm = qkv.reshape(b, s, N_QKV_HEADS, HEAD_DIM).transpose(0, 2, 1, 3)

        pos = jnp.stack([cmp_pos_k[i].reshape(1, -1), cmp_pos_v[i].reshape(1, -1)])
        kcvc = _compress(qkv_hm.reshape(b, N_QKV_HEADS, s // NSA_CMP_STRIDE, NSA_CMP_STRIDE * HEAD_DIM),
                         pos, jnp.stack([cmp_w1_k[i], cmp_w1_v[i]]), jnp.stack([cmp_w2_k[i], cmp_w2_v[i]]))
        o_cmp, sel_nsa = _cmp_select(slopes_n, qkv_hm, kcvc, overlap)
        o_sel = _attention(slopes_n, qkv_hm, q_head=HEAD_NSA_Q, k_head=HEAD_NSA_KSLC, v_head=HEAD_NSA_VSLC,
                           n_heads=NSA_HEADS, heads_per_kv=NSA_HEADS_PER_GROUP, sel=sel_nsa,
                           heads_per_sel=NSA_HEADS_PER_GROUP, expand=expand_sel, clamp_sum=True)
        o_win = _attention(slopes_n, qkv_hm, q_head=HEAD_NSA_Q, k_head=HEAD_NSA_KWIN, v_head=HEAD_NSA_VWIN,
                           n_heads=NSA_HEADS, heads_per_kv=NSA_HEADS_PER_GROUP, window=NSA_WINDOW,
                           clamp_sum=True)
        sel_moba = _moba_select(qkv_hm)
        o_moba = _attention(slopes_m, qkv_hm, q_head=HEAD_MOBA_Q, k_head=HEAD_MOBA_K, v_head=HEAD_MOBA_V,
                            n_heads=MOBA_HEADS, heads_per_kv=1, sel=sel_moba, expand=expand_moba,
                            clamp_sum=False)

        h = _mix(h, zg, merge, _token_major(o_cmp), _token_major(o_sel), _token_major(o_win),
                 _token_major(o_moba), spread, w_up_nsa[i].astype(BF16), w_up_moba[i].astype(BF16),
                 w_out[i].astype(BF16))

        h = _ffn(h, vec(ffn2_norm), ffn2_w1[i].astype(BF16), ffn2_w3[i].astype(BF16), ffn2_w2[i].astype(BF16),
                 tail=(p[i].reshape(n, PLE_DIM), vec(ple_norm), w_ple_gate[i].astype(BF16),
                       w_ple[i].astype(BF16)),
                 final_norm=final_norm.reshape(1, -1) if i + 1 == depth else None)
    return h.reshape(b, s, d)
```

```python
import functools

import numpy as np
import jax
import jax.numpy as jnp
from jax import lax
from jax.experimental import pallas as pl
from jax.experimental.pallas import tpu as pltpu

F32 = jnp.float32
BF16 = jnp.bfloat16

D_MODEL = 1024
HEAD_DIM = 64
NSA_HEADS = 8
NSA_KV_GROUPS = 2
NSA_HEADS_PER_GROUP = NSA_HEADS // NSA_KV_GROUPS
NSA_CMP_BLOCK = 32
NSA_CMP_STRIDE = 16
NSA_CMP_HIDDEN = 128
NSA_SEL_BLOCK = 64
NSA_SEL_TOPK = 16
NSA_WINDOW = 512
MOBA_HEADS = 8
MOBA_BLOCK = 256
MOBA_TOPK = 3
D_FF = 2816
PLE_DIM = 256
RMS_EPS = 1e-6
NEG_INF = -1e30
TINY = 1e-30
FORCE_SCORE = 1e9
LOG2E = float(np.log2(np.e))
Q_SCALE = HEAD_DIM ** -0.5 * LOG2E

NSA_WIDTH = NSA_HEADS * HEAD_DIM
NSA_KV_WIDTH = NSA_KV_GROUPS * HEAD_DIM
MOBA_WIDTH = MOBA_HEADS * HEAD_DIM
N_GATE_LOGITS = 3 * NSA_HEADS
OFF_KCMP = NSA_WIDTH
OFF_VCMP = OFF_KCMP + NSA_KV_WIDTH
OFF_KSLC = OFF_VCMP + NSA_KV_WIDTH
OFF_VSLC = OFF_KSLC + NSA_KV_WIDTH
OFF_KWIN = OFF_VSLC + NSA_KV_WIDTH
OFF_VWIN = OFF_KWIN + NSA_KV_WIDTH
OFF_GATE_LOGITS = OFF_VWIN + NSA_KV_WIDTH
OFF_MOBA_Q = OFF_GATE_LOGITS + N_GATE_LOGITS
OFF_MOBA_K = OFF_MOBA_Q + MOBA_WIDTH
OFF_MOBA_V = OFF_MOBA_K + MOBA_WIDTH
OFF_MERGE = OFF_MOBA_V + MOBA_WIDTH

LANES = 128
MXU_DIM = 256
VMEM_LIMIT = 56 * 1024 * 1024

TOKEN_TILE = 512
FF_CHUNK = 256
ATTN_TQ = 256
ATTN_TK = 256
BIAS_ROW = 2 * HEAD_DIM
SEL_ROW = BIAS_ROW + 16
N_SLOPE_PIECES = 3
MOBA_SEL_ROWS = 16

_NT = (((1,), (1,)), ((), ()))
_TN = (((0,), (0,)), ((), ()))


def _cparams(*sem):
    return pltpu.CompilerParams(dimension_semantics=sem, vmem_limit_bytes=VMEM_LIMIT)


def _resident(shape):
    nd = len(shape)
    return pl.BlockSpec(shape, lambda *_: (0,) * nd, pipeline_mode=pl.Buffered(1))


def _rms(x, g):
    ms = jnp.mean(x * x, axis=-1, keepdims=True)
    return x * lax.rsqrt(ms + RMS_EPS) * g


def _sigmoid(x):
    return 1.0 / (1.0 + jnp.exp(-x))


def _dot(a, b):
    return jnp.dot(a, b, preferred_element_type=F32)


def _dot_f32(a, b):
    return jnp.dot(a, b, preferred_element_type=F32, precision=lax.Precision.HIGHEST)


def _dot_nt(a, b, precision=None):
    return lax.dot_general(a, b, _NT, preferred_element_type=F32, precision=precision)


def _dot_tn(a, b):
    return lax.dot_general(a, b, _TN, preferred_element_type=F32)


def _ffn_kernel(*refs, tail, final):
    x_ref, g_ref, w1_ref, w3_ref, w2_ref = refs[:5]
    o_ref, acc_ref = refs[-2:]
    if tail:
        p_ref, gp_ref, wpg_ref, wp_ref = refs[5:9]
    if final:
        gf_ref = refs[-3]
    x = x_ref[...]
    xn = _rms(x, g_ref[...]).astype(BF16)
    for c in range(D_FF // FF_CHUNK):
        cols = slice(c * FF_CHUNK, (c + 1) * FF_CHUNK)
        a = _dot(xn, w1_ref[:, cols])
        b = _dot(xn, w3_ref[:, cols])
        hid = (a * _sigmoid(a) * b).astype(BF16)
        y = _dot(hid, w2_ref[cols, :])
        if c == 0:
            acc_ref[...] = y
        else:
            acc_ref[...] += y
    h = x + 0.5 * acc_ref[...]
    if tail:
        gate = _sigmoid(_dot(_rms(h, gp_ref[...]).astype(BF16), wpg_ref[...]))
        h = h + gate * _dot(p_ref[...].astype(BF16), wp_ref[...])
    if final:
        h = _rms(h, gf_ref[...])
    o_ref[...] = h


def _ffn(x, g, w1, w3, w2, tail=None, final_norm=None):
    n = x.shape[0]
    row = lambda w: pl.BlockSpec((TOKEN_TILE, w), lambda i: (i, 0))
    args = [x, g, w1, w3, w2]
    specs = [row(D_MODEL), _resident((1, D_MODEL)), _resident(w1.shape), _resident(w3.shape),
             _resident(w2.shape)]
    if tail is not None:
        p, gp, wpg, wp = tail
        args += [p, gp, wpg, wp]
        specs += [row(PLE_DIM), _resident((1, D_MODEL)), _resident(wpg.shape), _resident(wp.shape)]
    if final_norm is not None:
        args.append(final_norm)
        specs.append(_resident((1, D_MODEL)))
    return pl.pallas_call(
        functools.partial(_ffn_kernel, tail=tail is not None, final=final_norm is not None),
        grid=(n // TOKEN_TILE,),
        in_specs=specs,
        out_specs=row(D_MODEL),
        out_shape=jax.ShapeDtypeStruct((n, D_MODEL), F32),
        scratch_shapes=[pltpu.VMEM((TOKEN_TILE, D_MODEL), F32)],
        compiler_params=_cparams("parallel"),
        name="ffn_tail" if tail is not None else "ffn",
    )(*args)


def _inproj_kernel(h_ref, g_ref, wq_ref, wk_ref, wc_ref, wv_ref, wg_ref, wm_ref,
                   qT_ref, k_ref, cmp_ref, vT_ref, zgT_ref, merge_ref):
    u = _rms(h_ref[...], g_ref[...]).astype(BF16)
    qT_ref[...] = (_dot_nt(wq_ref[...], u) * Q_SCALE).astype(BF16)
    k_ref[...] = _dot(u, wk_ref[...]).astype(BF16)
    for kind in range(2):
        cmp_ref[kind] = _dot(u, wc_ref[kind]).astype(BF16)
    vT_ref[...] = _dot_nt(wv_ref[...], u).astype(BF16)
    zgT_ref[...] = _dot_nt(wg_ref[...], u)
    for lo in range(0, 2 * D_MODEL, 512):
        merge_ref[:, lo:lo + 512] = _dot(u, wm_ref[:, lo:lo + 512])


def _inproj(h, g, wq_t, wk, wc, wv_t, wg_t, wm):
    n = h.shape[0]
    tm = TOKEN_TILE
    rows = lambda width: pl.BlockSpec((tm, width), lambda i: (i, 0))
    cols = lambda height: pl.BlockSpec((height, tm), lambda i: (0, i))
    return pl.pallas_call(
        _inproj_kernel,
        grid=(n // tm,),
        in_specs=[rows(D_MODEL), _resident((1, D_MODEL)), _resident(wq_t.shape), _resident(wk.shape),
                  _resident(wc.shape), _resident(wv_t.shape), _resident(wg_t.shape), _resident(wm.shape)],
        out_specs=[cols(wq_t.shape[0]), rows(wk.shape[1]),
                   pl.BlockSpec((2, tm, NSA_KV_WIDTH), lambda i: (0, i, 0)),
                   cols(wv_t.shape[0]), cols(LANES), rows(2 * D_MODEL)],
        out_shape=[jax.ShapeDtypeStruct((wq_t.shape[0], n), BF16),
                   jax.ShapeDtypeStruct((n, wk.shape[1]), BF16),
                   jax.ShapeDtypeStruct((2, n, NSA_KV_WIDTH), BF16),
                   jax.ShapeDtypeStruct((wv_t.shape[0], n), BF16),
                   jax.ShapeDtypeStruct((LANES, n), F32),
                   jax.ShapeDtypeStruct((n, 2 * D_MODEL), F32)],
        compiler_params=_cparams("parallel"),
        name="inproj",
    )(h, g, wq_t, wk, wc, wv_t, wg_t, wm)


def _compress_kernel(rows_ref, pos_ref, w1_ref, w2_ref, w2t_ref, o_ref, ot_ref):
    rows = rows_ref[...].astype(F32)
    first = _dot((rows + pos_ref[0]).astype(BF16), w1_ref[0])
    second = _dot((rows + pos_ref[1]).astype(BF16), w1_ref[1])
    n_rows = rows.shape[0]
    pre = first + pltpu.roll(second, n_rows - 1, 0)
    c0 = float(np.sqrt(2.0 / np.pi))
    hid = pre * (0.5 * (1.0 + jnp.tanh(c0 * (pre + 0.044715 * (pre * pre * pre)))))
    o_ref[...] = _dot_f32(hid, w2_ref[...])
    ot_ref[...] = _dot_nt(w2t_ref[...], hid, lax.Precision.HIGHEST)


def _compress(cmp_rows, pos, w1, w2, w2t):
    _, b, n_rows, width = cmp_rows.shape
    g = NSA_KV_GROUPS
    per_kind_group = lambda *tail: pl.BlockSpec((None, None) + tail, lambda i, k, j: (k, j) + (0,) * len(tail))
    per_kind = lambda *tail: pl.BlockSpec((None,) + tail, lambda i, k, j: (k,) + (0,) * len(tail))
    return pl.pallas_call(
        _compress_kernel,
        grid=(b, 2, g),
        in_specs=[
            pl.BlockSpec((None, None, n_rows, width), lambda i, k, j: (k, i, 0, 0)),
            per_kind_group(2, 1, width),
            per_kind_group(2, width, NSA_CMP_HIDDEN),
            per_kind(NSA_CMP_HIDDEN, HEAD_DIM),
            per_kind(HEAD_DIM, NSA_CMP_HIDDEN),
        ],
        out_specs=[
            pl.BlockSpec((None, None, n_rows, HEAD_DIM), lambda i, k, j: (i, k * g + j, 0, 0)),
            pl.BlockSpec((None, None, HEAD_DIM, n_rows), lambda i, k, j: (i, k * g + j, 0, 0)),
        ],
        out_shape=[jax.ShapeDtypeStruct((b, 2 * g, n_rows, HEAD_DIM), F32),
                   jax.ShapeDtypeStruct((b, 2 * g, HEAD_DIM, n_rows), F32)],
        compiler_params=_cparams("parallel", "parallel", "parallel"),
        name="compress",
    )(cmp_rows, pos, w1, w2, w2t)


def _count_rank(score, idx, n_candidates):
    rank = jnp.zeros(score.shape, jnp.int32)
    for j in range(n_candidates):
        row = score[j:j + 1, :]
        ahead = (row > score) | ((row == score) & (idx > j))
        rank = rank + ahead.astype(jnp.int32)
    return rank


def _cmp_select_kernel(slopes_ref, qT_ref, kc_ref, vcT_ref, ovT_ref, oT_ref, negsel_ref):
    g = pl.program_id(1)
    qi = pl.program_id(2)
    tq = qT_ref.shape[1]
    n_cmp = kc_ref.shape[0]
    n_sel = ovT_ref.shape[0]
    hd = HEAD_DIM
    t = qi * tq + lax.broadcasted_iota(jnp.int32, (n_cmp, tq), 1)
    blk_c = lax.broadcasted_iota(jnp.int32, (n_cmp, tq), 0)
    dist = t - (blk_c * NSA_CMP_STRIDE + (NSA_CMP_BLOCK - 1))
    visible = dist >= 0
    dist_f = dist.astype(F32)
    kc = kc_ref[...]
    vcT = vcT_ref[...]
    p_sum = jnp.zeros((n_cmp, tq), F32)
    for hh in range(NSA_HEADS_PER_GROUP):
        slope2 = slopes_ref[g * NSA_HEADS_PER_GROUP + hh]
        q = qT_ref[hh * hd:(hh + 1) * hd, :].astype(F32)
        s = _dot_f32(kc, q) - slope2 * dist_f
        s = jnp.where(visible, s, NEG_INF)
        m = jnp.max(s, axis=0, keepdims=True)
        e = jnp.where(visible, jnp.exp2(s - m), 0.0)
        p = e / jnp.maximum(jnp.sum(e, axis=0, keepdims=True), TINY)
        p_sum = p_sum + p
        oT_ref[hh * hd:(hh + 1) * hd, :] = _dot_f32(vcT, p)
    imp = _dot_f32(ovT_ref[...], p_sum)
    ts = qi * tq + lax.broadcasted_iota(jnp.int32, (n_sel, tq), 1)
    blk = lax.broadcasted_iota(jnp.int32, (n_sel, tq), 0)
    cur = ts // NSA_SEL_BLOCK
    causal = blk * NSA_SEL_BLOCK <= ts
    forced = (blk == 0) | (blk == cur) | (blk == cur - 1)
    imp = jnp.where(causal, jnp.where(forced, FORCE_SCORE, imp), NEG_INF)
    rank = _count_rank(imp, blk, n_sel)
    chosen = (rank < min(NSA_SEL_TOPK, n_sel)) & causal
    negsel_ref[...] = jnp.where(chosen, 0.0, NEG_INF).astype(BF16)


def _cmp_select(slopes2, qT, kc, vcT, overlap_t, batch):
    n = qT.shape[1]
    tq = ATTN_TQ
    nq = n // batch // tq
    rows = NSA_HEADS_PER_GROUP * HEAD_DIM
    n_cmp = kc.shape[2]
    n_sel = overlap_t.shape[0]
    return pl.pallas_call(
        _cmp_select_kernel,
        grid=(batch, NSA_KV_GROUPS, nq),
        in_specs=[
            pl.BlockSpec(memory_space=pltpu.SMEM),
            pl.BlockSpec((rows, tq), lambda i, g, q: (g, i * nq + q)),
            pl.BlockSpec((None, None, n_cmp, HEAD_DIM), lambda i, g, q: (i, g, 0, 0)),
            pl.BlockSpec((None, None, HEAD_DIM, n_cmp), lambda i, g, q: (i, NSA_KV_GROUPS + g, 0, 0)),
            pl.BlockSpec((n_sel, n_cmp), lambda i, g, q: (0, 0)),
        ],
        out_specs=[
            pl.BlockSpec((rows, tq), lambda i, g, q: (g, i * nq + q)),
            pl.BlockSpec((None, n_sel, tq), lambda i, g, q: (g, 0, i * nq + q)),
        ],
        out_shape=[jax.ShapeDtypeStruct((NSA_WIDTH, n), F32),
                   jax.ShapeDtypeStruct((NSA_KV_GROUPS, n_sel, n), BF16)],
        compiler_params=_cparams("parallel", "parallel", "parallel"),
        name="cmp_select",
    )(slopes2, qT, kc, vcT, overlap_t)


def _moba_select_kernel(qT_ref, k_ref, negsel_ref):
    s = k_ref.shape[0]
    nb = s // MOBA_BLOCK
    hd = HEAD_DIM
    kmean = jnp.mean(k_ref[...].astype(F32).reshape(nb, MOBA_BLOCK, 2 * hd), axis=1)
    t = lax.broadcasted_iota(jnp.int32, (nb, s), 1)
    blk = lax.broadcasted_iota(jnp.int32, (nb, s), 0)
    cur = t // MOBA_BLOCK
    past = blk < cur
    for j in range(2):
        gs = _dot_f32(kmean[:, j * hd:(j + 1) * hd], qT_ref[j * hd:(j + 1) * hd, :].astype(F32))
        gs = jnp.where(past, gs, NEG_INF)
        rank = _count_rank(gs, blk, nb)
        chosen = ((rank < min(MOBA_TOPK, nb - 1)) & past) | (blk == cur)
        negsel_ref[j, 0:nb, :] = jnp.where(chosen, 0.0, NEG_INF).astype(BF16)
        negsel_ref[j, nb:, :] = jnp.zeros((MOBA_SEL_ROWS - nb, s), BF16)


def _moba_select(qT, k_tok, batch, q_row_block, k_col_block):
    n = qT.shape[1]
    s = n // batch
    pair = 2 * HEAD_DIM
    return pl.pallas_call(
        _moba_select_kernel,
        grid=(batch, MOBA_HEADS // 2),
        in_specs=[pl.BlockSpec((pair, s), lambda i, j: (q_row_block + j, i)),
                  pl.BlockSpec((s, pair), lambda i, j: (i, k_col_block + j))],
        out_specs=pl.BlockSpec((2, MOBA_SEL_ROWS, s), lambda i, j: (j, 0, i)),
        out_shape=jax.ShapeDtypeStruct((MOBA_HEADS, MOBA_SEL_ROWS, n), BF16),
        compiler_params=_cparams("parallel", "parallel"),
        name="moba_select",
    )(qT, k_tok)


def _attn_kernel(*refs, n_pairs, heads_per_pair, v_heads_per_pair, has_sel, slot_of, window, clamp_sum):
    n_in = 6 if has_sel else 5
    qT_ref, k_ref, ktab_ref, vT_ref, qbias_ref = refs[:5]
    negsel_ref = refs[5] if has_sel else None
    oT_ref = refs[n_in]
    qaug_ref = refs[n_in + 1]
    s_bufs = refs[n_in + 2:n_in + 4]
    p_bufs = refs[n_in + 4:n_in + 6]
    grp = pl.program_id(1)
    qi = pl.program_id(2)
    tq, tk, hd = ATTN_TQ, ATTN_TK, HEAD_DIM
    hpp = heads_per_pair
    pcol = hpp * tq
    ncol = n_pairs * pcol
    vrows = v_heads_per_pair * hd

    for i in range(n_pairs * hpp):
        cols = slice(i * tq, (i + 1) * tq)
        slot = slot_of(i % hpp, grp)
        q = qT_ref[i * hd:(i + 1) * hd, :]
        zero = jnp.zeros_like(q)
        qaug_ref[0:hd, cols] = jnp.where(slot == 0, q, zero)
        qaug_ref[hd:2 * hd, cols] = jnp.where(slot == 1, q, zero)
        qaug_ref[BIAS_ROW:SEL_ROW, cols] = jnp.concatenate([qbias_ref[i]] * (tq // LANES), axis=1)
        fill = SEL_ROW
        if has_sel:
            n_rows = negsel_ref.shape[1]
            qaug_ref[SEL_ROW:SEL_ROW + n_rows, cols] = negsel_ref[i % negsel_ref.shape[0]]
            fill = SEL_ROW + n_rows
        qaug_ref[fill:, cols] = jnp.zeros((MXU_DIM - fill, tq), BF16)

    key_i = lax.broadcasted_iota(jnp.int32, (tk, ncol), 0)
    qry_i = lax.broadcasted_iota(jnp.int32, (tk, ncol), 1) % tq

    def key_slice(c):
        return pl.ds(pl.multiple_of(c * tk, tk), tk)

    def scores(c, buf):
        keys = key_slice(c)
        for pr in range(n_pairs):
            kaug = jnp.concatenate([k_ref[keys, pr * LANES:(pr + 1) * LANES], ktab_ref[keys, :]], axis=1)
            cols = slice(pr * pcol, (pr + 1) * pcol)
            s_bufs[buf][:, cols] = _dot(kaug, qaug_ref[:, cols])

    def softmax(buf, m, l, mask):
        s = s_bufs[buf][...]
        if mask is not None:
            s = jnp.where(mask, s, NEG_INF)
        m_new = jnp.maximum(m, jnp.max(s, axis=0, keepdims=True))
        alpha = jnp.exp2(m - m_new)
        p = jnp.exp2(s - m_new)
        l = alpha * l + jnp.sum(p, axis=0, keepdims=True)
        p_bufs[buf][...] = p.astype(BF16)
        return m_new, l, alpha

    def weighted(c, buf, alpha, acc):
        keys = key_slice(c)
        parts = [_dot(vT_ref[pr * vrows:(pr + 1) * vrows, keys], p_bufs[buf][:, pr * pcol:(pr + 1) * pcol])
                 for pr in range(n_pairs)]
        return alpha * acc + (parts[0] if n_pairs == 1 else jnp.concatenate(parts, axis=1))

    m0 = jnp.full((1, ncol), NEG_INF, F32)
    l0 = jnp.zeros((1, ncol), F32)
    acc0 = jnp.zeros((vrows, ncol), F32)

    if window is None:
        def step(c, buf, carry):
            m, l, alpha_prev, acc = carry
            scores(c + 1, 1 - buf)
            acc = weighted(jnp.maximum(c - 1, 0), 1 - buf, alpha_prev, acc)
            m, l, alpha = softmax(buf, m, l, None)
            return m, l, alpha, acc

        def finish(buf, carry):
            m, l, alpha_prev, acc = carry
            acc = weighted(jnp.maximum(qi - 1, 0), 1 - buf, alpha_prev, acc)
            m, l, alpha = softmax(buf, m, l, key_i <= qry_i)
            return l, weighted(qi, buf, alpha, acc)

        p_bufs[1][...] = jnp.zeros(p_bufs[1].shape, BF16)
        scores(0, 0)
        carry = (m0, l0, jnp.ones((1, ncol), F32), acc0)
        carry = lax.fori_loop(0, qi // 2, lambda j, cr: step(2 * j + 1, 1, step(2 * j, 0, cr)), carry)
        l, acc = lax.cond(qi % 2 == 1, lambda cr: finish(1, step(qi - 1, 0, cr)),
                          lambda cr: finish(0, cr), carry)
    else:
        assert window == 2 * tk and tq == tk
        big = jnp.int32(2 ** 30)
        lower = key_i > jnp.where(qi >= 2, qry_i, big)
        middle = key_i >= jnp.where(qi >= 1, 0, big)
        c0, c1 = jnp.maximum(qi - 2, 0), jnp.maximum(qi - 1, 0)
        scores(c0, 0)
        scores(c1, 1)
        m, l, alpha = softmax(0, m0, l0, lower)
        scores(qi, 0)
        acc = weighted(c0, 0, alpha, acc0)
        m, l, alpha = softmax(1, m, l, middle)
        acc = weighted(c1, 1, alpha, acc)
        m, l, alpha = softmax(0, m, l, key_i <= qry_i)
        acc = weighted(qi, 0, alpha, acc)

    o = acc / (jnp.maximum(l, TINY) if clamp_sum else l)
    for i in range(n_pairs * hpp):
        vr = (i % v_heads_per_pair) * hd
        oT_ref[i * hd:(i + 1) * hd, :] = o[vr:vr + hd, i * tq:(i + 1) * tq]


def _attention(qT, k_tok, key_table, vT, qbias, negsel, *, batch, n_steps, n_pairs, heads_per_pair,
               v_heads_per_pair, q_block, k_block, v_block, slot_of, window=None, clamp_sum, name):
    n = qT.shape[1]
    s = n // batch
    tq, tk, hd = ATTN_TQ, ATTN_TK, HEAD_DIM
    nq = s // tq
    nh = n_pairs * heads_per_pair
    ncol = nh * tq
    in_specs = [
        pl.BlockSpec((nh * hd, tq), lambda b, g, q: (q_block(g), b * nq + q)),
        pl.BlockSpec((s, n_pairs * LANES), lambda b, g, q: (b, k_block(g))),
        pl.BlockSpec((s, LANES), lambda b, g, q: (0, 0)),
        pl.BlockSpec((n_pairs * v_heads_per_pair * hd, s), lambda b, g, q: (v_block(g), b)),
        pl.BlockSpec((nh, 16, LANES), lambda b, g, q: (g, 0, 0)),
    ]
    args = [qT, k_tok, key_table, vT, qbias]
    if negsel is not None:
        per_step = negsel.shape[0] // n_steps
        in_specs.append(pl.BlockSpec((per_step, negsel.shape[1], tq), lambda b, g, q: (g, 0, b * nq + q)))
        args.append(negsel)
    return pl.pallas_call(
        functools.partial(_attn_kernel, n_pairs=n_pairs, heads_per_pair=heads_per_pair,
                          v_heads_per_pair=v_heads_per_pair, has_sel=negsel is not None, slot_of=slot_of,
                          window=window, clamp_sum=clamp_sum),
        grid=(batch, n_steps, nq),
        in_specs=in_specs,
        out_specs=pl.BlockSpec((nh * hd, tq), lambda b, g, q: (g, b * nq + q)),
        out_shape=jax.ShapeDtypeStruct((n_steps * nh * hd, n), F32),
        scratch_shapes=[pltpu.VMEM((MXU_DIM, ncol), BF16),
                        pltpu.VMEM((tk, ncol), F32), pltpu.VMEM((tk, ncol), F32),
                        pltpu.VMEM((tk, ncol), BF16), pltpu.VMEM((tk, ncol), BF16)],
        compiler_params=_cparams("parallel", "parallel", "arbitrary"),
        name=name,
    )(*args)


def _mix_kernel(h_ref, zgT_ref, merge_ref, ocmp_ref, osel_ref, owin_ref, omoba_ref, spread_ref,
                wn_ref, wm_ref, wo_ref, o_ref):
    gates = _sigmoid(zgT_ref[...])
    o_nsa = None
    for r, branch in enumerate((ocmp_ref, osel_ref, owin_ref)):
        term = _dot_f32(spread_ref[r], gates) * branch[...]
        o_nsa = term if o_nsa is None else o_nsa + term
    y_n = _dot_tn(o_nsa.astype(BF16), wn_ref[...])
    y_m = _dot_tn(omoba_ref[...].astype(BF16), wm_ref[...])
    mixed = _sigmoid(merge_ref[:, :D_MODEL]) * y_n + _sigmoid(merge_ref[:, D_MODEL:]) * y_m
    o_ref[...] = h_ref[...] + _dot(mixed.astype(BF16), wo_ref[...])


def _mix(h, zgT, merge, ocmpT, oselT, owinT, omobaT, spread_t, wn, wm, wo):
    n = h.shape[0]
    tm = TOKEN_TILE
    rows = lambda width: pl.BlockSpec((tm, width), lambda i: (i, 0))
    cols = lambda height: pl.BlockSpec((height, tm), lambda i: (0, i))
    return pl.pallas_call(
        _mix_kernel,
        grid=(n // tm,),
        in_specs=[rows(D_MODEL), cols(LANES), rows(2 * D_MODEL), cols(NSA_WIDTH), cols(NSA_WIDTH),
                  cols(NSA_WIDTH), cols(MOBA_WIDTH), _resident(spread_t.shape), _resident(wn.shape),
                  _resident(wm.shape), _resident(wo.shape)],
        out_specs=rows(D_MODEL),
        out_shape=jax.ShapeDtypeStruct((n, D_MODEL), F32),
        compiler_params=_cparams("parallel"),
        name="mix",
    )(h, zgT, merge, ocmpT, oselT, owinT, omobaT, spread_t, wn, wm, wo)


def _key_table(s, block):
    pos = np.arange(s)
    tab = np.zeros((s, LANES), np.float32)
    tab[:, 0:N_SLOPE_PIECES] = ((pos // LANES) * LANES)[:, None]
    tab[:, N_SLOPE_PIECES:2 * N_SLOPE_PIECES] = (pos % LANES)[:, None]
    if block is not None:
        tab[pos, (SEL_ROW - BIAS_ROW) + pos // block] = 1.0
    return jnp.asarray(tab, BF16)


def _slope_rows(slopes2):
    s1 = slopes2.astype(BF16)
    r1 = slopes2 - s1.astype(F32)
    s2 = r1.astype(BF16)
    s3 = (r1 - s2.astype(F32)).astype(BF16)
    rows = jnp.stack([s1, s2, s3, s1, s2, s3], axis=1)
    rows = jnp.pad(rows, ((0, 0), (0, 16 - 2 * N_SLOPE_PIECES)))
    return jnp.broadcast_to(rows[:, :, None], rows.shape + (LANES,))


def _compress_weights(pos, w1, group):
    st, hd, gw = NSA_CMP_STRIDE, HEAD_DIM, NSA_KV_WIDTH
    lo = group * hd
    w = jnp.zeros((2, st, gw, NSA_CMP_HIDDEN), F32).at[:, :, lo:lo + hd, :].set(
        w1.reshape(2, st, hd, NSA_CMP_HIDDEN))
    ps = jnp.zeros((2, st, gw), F32).at[:, :, lo:lo + hd].set(pos.reshape(2, st, hd))
    return ps.reshape(2, 1, st * gw), w.reshape(2, st * gw, NSA_CMP_HIDDEN).astype(BF16)


def kernel(x, p, ffn1_norm, ffn1_w1, ffn1_w3, ffn1_w2, mix_norm, w_in, cmp_pos_k, cmp_w1_k, cmp_w2_k, cmp_pos_v, cmp_w1_v, cmp_w2_v, w_up_nsa, w_up_moba, w_out, ffn2_norm, ffn2_w1, ffn2_w3, ffn2_w2, ple_norm, w_ple_gate, w_ple, final_norm):
    b, s, d = x.shape
    n = b * s
    depth = p.shape[0]
    n_all = NSA_HEADS + MOBA_HEADS
    slopes = jnp.exp2(-8.0 * (jnp.arange(n_all, dtype=F32) + 1.0) / n_all)
    slopes2_n, slopes2_m = slopes[0::2] * LOG2E, slopes[1::2] * LOG2E
    qbias_n, qbias_m = _slope_rows(slopes2_n), _slope_rows(slopes2_m)

    n_cmp_rows = s // NSA_CMP_STRIDE
    n_cmp = (s - NSA_CMP_BLOCK) // NSA_CMP_STRIDE + 1
    n_sel = s // NSA_SEL_BLOCK
    c_start = np.arange(n_cmp_rows) * NSA_CMP_STRIDE
    s_start = np.arange(n_sel) * NSA_SEL_BLOCK
    overlap_t = ((c_start[None, :] <= s_start[:, None] + NSA_SEL_BLOCK - 1)
                 & (c_start[None, :] + NSA_CMP_BLOCK - 1 >= s_start[:, None])
                 & (np.arange(n_cmp_rows)[None, :] < n_cmp))
    overlap_t = jnp.asarray(overlap_t, F32)
    spread_np = np.zeros((3, NSA_WIDTH, LANES), np.float32)
    for r in range(3):
        for hh in range(NSA_HEADS):
            spread_np[r, hh * HEAD_DIM:(hh + 1) * HEAD_DIM, 3 * hh + r] = 1.0
    spread_t = jnp.asarray(spread_np)
    table_sel = _key_table(s, NSA_SEL_BLOCK)
    table_win = _key_table(s, None)
    table_moba = _key_table(s, MOBA_BLOCK)

    h = x.reshape(n, d)
    for i in range(depth):
        vec = lambda a: a[i].reshape(1, -1)
        h = _ffn(h, vec(ffn1_norm), ffn1_w1[i].astype(BF16), ffn1_w3[i].astype(BF16),
                 ffn1_w2[i].astype(BF16))

        wi = w_in[i]
        col = lambda lo, width: wi[:, lo:lo + width]
        wq_t = jnp.concatenate([col(0, NSA_WIDTH), col(OFF_MOBA_Q, MOBA_WIDTH)], axis=1).T.astype(BF16)
        wk = jnp.concatenate([col(OFF_KSLC, NSA_KV_WIDTH), col(OFF_KWIN, NSA_KV_WIDTH),
                              col(OFF_MOBA_K, MOBA_WIDTH)], axis=1).astype(BF16)
        wc = jnp.stack([col(OFF_KCMP, NSA_KV_WIDTH), col(OFF_VCMP, NSA_KV_WIDTH)]).astype(BF16)
        wv_t = jnp.concatenate([col(OFF_VSLC, NSA_KV_WIDTH), col(OFF_VWIN, NSA_KV_WIDTH),
                                col(OFF_MOBA_V, MOBA_WIDTH)], axis=1).T.astype(BF16)
        wg_t = jnp.pad(col(OFF_GATE_LOGITS, N_GATE_LOGITS), ((0, 0), (0, LANES - N_GATE_LOGITS))).T.astype(BF16)
        wm = col(OFF_MERGE, 2 * D_MODEL).astype(BF16)
        qT, k_tok, cmp_tok, vT, zgT, merge = _inproj(h, vec(mix_norm), wq_t, wk, wc, wv_t, wg_t, wm)

        cmp_w = [[_compress_weights(ps[i], w1[i], g) for g in range(NSA_KV_GROUPS)]
                 for ps, w1 in ((cmp_pos_k, cmp_w1_k), (cmp_pos_v, cmp_w1_v))]
        pos = jnp.stack([jnp.stack([pw[0] for pw in kind]) for kind in cmp_w])
        w1c = jnp.stack([jnp.stack([pw[1] for pw in kind]) for kind in cmp_w])
        w2 = jnp.stack([cmp_w2_k[i], cmp_w2_v[i]])
        cmp_rows = cmp_tok.reshape(2, b, n_cmp_rows, NSA_CMP_STRIDE * NSA_KV_WIDTH)
        kc, kcT = _compress(cmp_rows, pos, w1c, w2, w2.transpose(0, 2, 1))

        ocmpT, negsel_n = _cmp_select(slopes2_n, qT, kc, kcT, overlap_t, b)
        nsa = dict(batch=b, n_steps=NSA_KV_GROUPS, n_pairs=1, heads_per_pair=NSA_HEADS_PER_GROUP,
                   v_heads_per_pair=1, q_block=lambda g: g, slot_of=lambda j, g: g, clamp_sum=True)
        oselT = _attention(qT, k_tok, table_sel, vT, qbias_n, negsel_n, k_block=lambda g: 0,
                           v_block=lambda g: g, name="attn_select", **nsa)
        owinT = _attention(qT, k_tok, table_win, vT, qbias_n, None, k_block=lambda g: 1,
                           v_block=lambda g: NSA_KV_GROUPS + g, window=NSA_WINDOW, name="attn_window", **nsa)
        negsel_m = _moba_select(qT, k_tok, b, q_row_block=NSA_WIDTH // LANES,
                                k_col_block=2 * NSA_KV_WIDTH // LANES)
        omobaT = _attention(qT, k_tok, table_moba, vT, qbias_m, negsel_m, batch=b, n_steps=2, n_pairs=2,
                            heads_per_pair=2, v_heads_per_pair=2, q_block=lambda g: 2 + g,
                            k_block=lambda g: 1 + g, v_block=lambda g: 1 + g, slot_of=lambda j, g: j,
                            clamp_sum=False, name="attn_moba")

        h = _mix(h, zgT, merge, ocmpT, oselT, owinT, omobaT, spread_t, w_up_nsa[i].astype(BF16),
                 w_up_moba[i].astype(BF16), w_out[i].astype(BF16))

        h = _ffn(h, vec(ffn2_norm), ffn2_w1[i].astype(BF16), ffn2_w3[i].astype(BF16), ffn2_w2[i].astype(BF16),
                 tail=(p[i].reshape(n, PLE_DIM), vec(ple_norm), w_ple_gate[i].astype(BF16),
                       w_ple[i].astype(BF16)),
                 final_norm=final_norm.reshape(1, -1) if i + 1 == depth else None)
    return h.reshape(b, s, d)
```

```python
import functools

import numpy as np
import jax
import jax.numpy as jnp
from jax import lax
from jax.experimental import pallas as pl
from jax.experimental.pallas import tpu as pltpu

F32 = jnp.float32
BF16 = jnp.bfloat16

D_MODEL = 1024
HEAD_DIM = 64
NSA_HEADS = 8
NSA_KV_GROUPS = 2
NSA_HEADS_PER_GROUP = NSA_HEADS // NSA_KV_GROUPS
NSA_CMP_BLOCK = 32
NSA_CMP_STRIDE = 16
NSA_CMP_HIDDEN = 128
NSA_SEL_BLOCK = 64
NSA_SEL_TOPK = 16
NSA_WINDOW = 512
MOBA_HEADS = 8
MOBA_BLOCK = 256
MOBA_TOPK = 3
D_FF = 2816
PLE_DIM = 256
RMS_EPS = 1e-6
NEG_INF = -1e30
TINY = 1e-30
FORCE_SCORE = 1e9
LOG2E = float(np.log2(np.e))
Q_SCALE = HEAD_DIM ** -0.5 * LOG2E

NSA_WIDTH = NSA_HEADS * HEAD_DIM
NSA_KV_WIDTH = NSA_KV_GROUPS * HEAD_DIM
MOBA_WIDTH = MOBA_HEADS * HEAD_DIM
N_GATE_LOGITS = 3 * NSA_HEADS
OFF_KCMP = NSA_WIDTH
OFF_VCMP = OFF_KCMP + NSA_KV_WIDTH
OFF_KSLC = OFF_VCMP + NSA_KV_WIDTH
OFF_VSLC = OFF_KSLC + NSA_KV_WIDTH
OFF_KWIN = OFF_VSLC + NSA_KV_WIDTH
OFF_VWIN = OFF_KWIN + NSA_KV_WIDTH
OFF_GATE_LOGITS = OFF_VWIN + NSA_KV_WIDTH
OFF_MOBA_Q = OFF_GATE_LOGITS + N_GATE_LOGITS
OFF_MOBA_K = OFF_MOBA_Q + MOBA_WIDTH
OFF_MOBA_V = OFF_MOBA_K + MOBA_WIDTH
OFF_MERGE = OFF_MOBA_V + MOBA_WIDTH

LANES = 128
MXU_DIM = 256
VMEM_LIMIT = 56 * 1024 * 1024

TOKEN_TILE = 512
FF_CHUNK = 256
ATTN_TQ = 256
ATTN_TK = 256
BIAS_ROW = 2 * HEAD_DIM
SEL_ROW = BIAS_ROW + 16
N_SLOPE_PIECES = 3
MOBA_SEL_ROWS = 16
ONES_ROWS = 16

_NT = (((1,), (1,)), ((), ()))
_TN = (((0,), (0,)), ((), ()))


def _cparams(*sem):
    return pltpu.CompilerParams(dimension_semantics=sem, vmem_limit_bytes=VMEM_LIMIT)


def _resident(shape):
    nd = len(shape)
    return pl.BlockSpec(shape, lambda *_: (0,) * nd, pipeline_mode=pl.Buffered(1))


def _rms(x, g):
    ms = jnp.mean(x * x, axis=-1, keepdims=True)
    return x * lax.rsqrt(ms + RMS_EPS) * g


def _sigmoid(x):
    return 1.0 / (1.0 + jnp.exp(-x))


def _dot(a, b):
    return jnp.dot(a, b, preferred_element_type=F32)


def _dot_f32(a, b):
    return jnp.dot(a, b, preferred_element_type=F32, precision=lax.Precision.HIGHEST)


def _dot_nt(a, b, precision=None):
    return lax.dot_general(a, b, _NT, preferred_element_type=F32, precision=precision)


def _dot_tn(a, b):
    return lax.dot_general(a, b, _TN, preferred_element_type=F32)


def _ffn_kernel(*refs, tail, final):
    x_ref, g_ref, w1_ref, w3_ref, w2_ref = refs[:5]
    o_ref, acc_ref = refs[-2:]
    if tail:
        p_ref, gp_ref, wpg_ref, wp_ref = refs[5:9]
    if final:
        gf_ref = refs[-3]
    x = x_ref[...]
    xn = _rms(x, g_ref[...]).astype(BF16)
    for c in range(D_FF // FF_CHUNK):
        cols = slice(c * FF_CHUNK, (c + 1) * FF_CHUNK)
        a = _dot(xn, w1_ref[:, cols])
        b = _dot(xn, w3_ref[:, cols])
        hid = (a * _sigmoid(a) * b).astype(BF16)
        y = _dot(hid, w2_ref[cols, :])
        if c == 0:
            acc_ref[...] = y
        else:
            acc_ref[...] += y
    h = x + 0.5 * acc_ref[...]
    if tail:
        gate = _sigmoid(_dot(_rms(h, gp_ref[...]).astype(BF16), wpg_ref[...]))
        h = h + gate * _dot(p_ref[...].astype(BF16), wp_ref[...])
    if final:
        h = _rms(h, gf_ref[...])
    o_ref[...] = h


def _ffn(x, g, w1, w3, w2, tail=None, final_norm=None):
    n = x.shape[0]
    row = lambda w: pl.BlockSpec((TOKEN_TILE, w), lambda i: (i, 0))
    args = [x, g, w1, w3, w2]
    specs = [row(D_MODEL), _resident((1, D_MODEL)), _resident(w1.shape), _resident(w3.shape),
             _resident(w2.shape)]
    if tail is not None:
        p, gp, wpg, wp = tail
        args += [p, gp, wpg, wp]
        specs += [row(PLE_DIM), _resident((1, D_MODEL)), _resident(wpg.shape), _resident(wp.shape)]
    if final_norm is not None:
        args.append(final_norm)
        specs.append(_resident((1, D_MODEL)))
    return pl.pallas_call(
        functools.partial(_ffn_kernel, tail=tail is not None, final=final_norm is not None),
        grid=(n // TOKEN_TILE,),
        in_specs=specs,
        out_specs=row(D_MODEL),
        out_shape=jax.ShapeDtypeStruct((n, D_MODEL), F32),
        scratch_shapes=[pltpu.VMEM((TOKEN_TILE, D_MODEL), F32)],
        compiler_params=_cparams("parallel"),
        name="ffn_tail" if tail is not None else "ffn",
    )(*args)


def _inproj_kernel(h_ref, g_ref, wq_ref, wk_ref, wc_ref, wv_ref, wg_ref,
                   qT_ref, k_ref, cmp_ref, vT_ref, zgT_ref):
    u = _rms(h_ref[...], g_ref[...]).astype(BF16)
    qT_ref[...] = (_dot_nt(wq_ref[...], u) * Q_SCALE).astype(BF16)
    k_ref[...] = _dot(u, wk_ref[...]).astype(BF16)
    for kind in range(2):
        cmp_ref[kind] = _dot(u, wc_ref[kind]).astype(BF16)
    vT_ref[...] = _dot_nt(wv_ref[...], u).astype(BF16)
    zgT_ref[...] = _dot_nt(wg_ref[...], u)


def _inproj(h, g, wq_t, wk, wc, wv_t, wg_t):
    n = h.shape[0]
    tm = TOKEN_TILE
    rows = lambda width: pl.BlockSpec((tm, width), lambda i: (i, 0))
    cols = lambda height: pl.BlockSpec((height, tm), lambda i: (0, i))
    return pl.pallas_call(
        _inproj_kernel,
        grid=(n // tm,),
        in_specs=[rows(D_MODEL), _resident((1, D_MODEL)), _resident(wq_t.shape), _resident(wk.shape),
                  _resident(wc.shape), _resident(wv_t.shape), _resident(wg_t.shape)],
        out_specs=[cols(wq_t.shape[0]), rows(wk.shape[1]),
                   pl.BlockSpec((2, tm, NSA_KV_WIDTH), lambda i: (0, i, 0)),
                   cols(wv_t.shape[0]), cols(LANES)],
        out_shape=[jax.ShapeDtypeStruct((wq_t.shape[0], n), BF16),
                   jax.ShapeDtypeStruct((n, wk.shape[1]), BF16),
                   jax.ShapeDtypeStruct((2, n, NSA_KV_WIDTH), BF16),
                   jax.ShapeDtypeStruct((wv_t.shape[0], n), BF16),
                   jax.ShapeDtypeStruct((LANES, n), F32)],
        compiler_params=_cparams("parallel"),
        name="inproj",
    )(h, g, wq_t, wk, wc, wv_t, wg_t)


def _split3(x):
    hi = x.astype(BF16)
    r = x - hi.astype(F32)
    mid = r.astype(BF16)
    return hi, mid, (r - mid.astype(F32)).astype(BF16)


def _compress_kernel(rows_ref, pos_ref, w1_ref, w2_ref, w2t_ref, o_ref, ot_ref):
    rows = rows_ref[...].astype(F32)
    first = _dot((rows + pos_ref[0]).astype(BF16), w1_ref[0])
    second = _dot((rows + pos_ref[1]).astype(BF16), w1_ref[1])
    n_rows = rows.shape[0]
    pre = first + pltpu.roll(second, n_rows - 1, 0)
    c0 = float(np.sqrt(2.0 / np.pi))
    hid = pre * (0.5 * (1.0 + jnp.tanh(c0 * (pre + 0.044715 * (pre * pre * pre)))))
    out = _dot_f32(hid, w2_ref[...])
    o_ref[...] = jnp.concatenate(_split3(out) + (jnp.zeros(out.shape, BF16),), axis=1)
    ot_ref[...] = _dot_nt(w2t_ref[...], hid, lax.Precision.HIGHEST).astype(BF16)


def _compress(cmp_rows, pos, w1, w2, w2t):
    _, b, n_rows, width = cmp_rows.shape
    g = NSA_KV_GROUPS
    per_kind_group = lambda *tail: pl.BlockSpec((None, None) + tail, lambda i, k, j: (k, j) + (0,) * len(tail))
    per_kind = lambda *tail: pl.BlockSpec((None,) + tail, lambda i, k, j: (k,) + (0,) * len(tail))
    return pl.pallas_call(
        _compress_kernel,
        grid=(b, 2, g),
        in_specs=[
            pl.BlockSpec((None, None, n_rows, width), lambda i, k, j: (k, i, 0, 0)),
            per_kind_group(2, 1, width),
            per_kind_group(2, width, NSA_CMP_HIDDEN),
            per_kind(NSA_CMP_HIDDEN, HEAD_DIM),
            per_kind(HEAD_DIM, NSA_CMP_HIDDEN),
        ],
        out_specs=[
            pl.BlockSpec((None, None, n_rows, 4 * HEAD_DIM), lambda i, k, j: (i, k * g + j, 0, 0)),
            pl.BlockSpec((None, None, HEAD_DIM, n_rows), lambda i, k, j: (i, k * g + j, 0, 0)),
        ],
        out_shape=[jax.ShapeDtypeStruct((b, 2 * g, n_rows, 4 * HEAD_DIM), BF16),
                   jax.ShapeDtypeStruct((b, 2 * g, HEAD_DIM, n_rows), BF16)],
        compiler_params=_cparams("parallel", "parallel", "parallel"),
        name="compress",
    )(cmp_rows, pos, w1, w2, w2t)


def _count_rank(score, idx, n_candidates):
    rank = jnp.zeros(score.shape, jnp.int32)
    for j in range(n_candidates):
        row = score[j:j + 1, :]
        ahead = (row > score) | ((row == score) & (idx > j))
        rank = rank + ahead.astype(jnp.int32)
    return rank


def _cmp_select_kernel(slopes_ref, qT_ref, kc_ref, vcT_ref, ovT_ref, oT_ref, negsel_ref):
    g = pl.program_id(1)
    qi = pl.program_id(2)
    tq = qT_ref.shape[1]
    n_cmp = kc_ref.shape[0]
    n_sel = ovT_ref.shape[0]
    hd = HEAD_DIM
    t = qi * tq + lax.broadcasted_iota(jnp.int32, (n_cmp, tq), 1)
    blk_c = lax.broadcasted_iota(jnp.int32, (n_cmp, tq), 0)
    dist = t - (blk_c * NSA_CMP_STRIDE + (NSA_CMP_BLOCK - 1))
    visible = dist >= 0
    dist_f = dist.astype(F32)
    kc3 = kc_ref[...]
    vcT = vcT_ref[...]
    p_sum = jnp.zeros((n_cmp, tq), F32)
    for hh in range(NSA_HEADS_PER_GROUP):
        slope2 = slopes_ref[g * NSA_HEADS_PER_GROUP + hh]
        q = qT_ref[hh * hd:(hh + 1) * hd, :]
        s = _dot(kc3, jnp.concatenate([q, q, q, jnp.zeros_like(q)], axis=0)) - slope2 * dist_f
        s = jnp.where(visible, s, NEG_INF)
        m = jnp.max(s, axis=0, keepdims=True)
        e = jnp.where(visible, jnp.exp2(s - m), 0.0)
        p = e * (1.0 / jnp.maximum(jnp.sum(e, axis=0, keepdims=True), TINY))
        p_sum = p_sum + p
        oT_ref[hh * hd:(hh + 1) * hd, :] = _dot(vcT, p.astype(BF16))
    imp = _dot(ovT_ref[...], jnp.concatenate(_split3(p_sum), axis=0))
    ts = qi * tq + lax.broadcasted_iota(jnp.int32, (n_sel, tq), 1)
    blk = lax.broadcasted_iota(jnp.int32, (n_sel, tq), 0)
    cur = ts // NSA_SEL_BLOCK
    causal = blk * NSA_SEL_BLOCK <= ts
    forced = (blk == 0) | (blk == cur) | (blk == cur - 1)
    imp = jnp.where(causal, jnp.where(forced, FORCE_SCORE, imp), NEG_INF)
    rank = _count_rank(imp, blk, n_sel)
    chosen = (rank < min(NSA_SEL_TOPK, n_sel)) & causal
    negsel_ref[...] = jnp.where(chosen, 0.0, NEG_INF).astype(BF16)


def _cmp_select(slopes2, qT, kc, vcT, overlap_t, batch):
    n = qT.shape[1]
    tq = ATTN_TQ
    nq = n // batch // tq
    rows = NSA_HEADS_PER_GROUP * HEAD_DIM
    n_cmp = kc.shape[2]
    n_sel = overlap_t.shape[0]
    return pl.pallas_call(
        _cmp_select_kernel,
        grid=(batch, NSA_KV_GROUPS, nq),
        in_specs=[
            pl.BlockSpec(memory_space=pltpu.SMEM),
            pl.BlockSpec((rows, tq), lambda i, g, q: (g, i * nq + q)),
            pl.BlockSpec((None, None, n_cmp, 4 * HEAD_DIM), lambda i, g, q: (i, g, 0, 0)),
            pl.BlockSpec((None, None, HEAD_DIM, n_cmp), lambda i, g, q: (i, NSA_KV_GROUPS + g, 0, 0)),
            pl.BlockSpec((n_sel, 3 * n_cmp), lambda i, g, q: (0, 0)),
        ],
        out_specs=[
            pl.BlockSpec((rows, tq), lambda i, g, q: (g, i * nq + q)),
            pl.BlockSpec((None, n_sel, tq), lambda i, g, q: (g, 0, i * nq + q)),
        ],
        out_shape=[jax.ShapeDtypeStruct((NSA_WIDTH, n), F32),
                   jax.ShapeDtypeStruct((NSA_KV_GROUPS, n_sel, n), BF16)],
        compiler_params=_cparams("parallel", "parallel", "parallel"),
        name="cmp_select",
    )(slopes2, qT, kc, vcT, overlap_t)


def _moba_select_kernel(qT_ref, k_ref, negsel_ref):
    s = k_ref.shape[0]
    nb = s // MOBA_BLOCK
    hd = HEAD_DIM
    kmean = jnp.mean(k_ref[...].astype(F32).reshape(nb, MOBA_BLOCK, 2 * hd), axis=1)
    t = lax.broadcasted_iota(jnp.int32, (nb, s), 1)
    blk = lax.broadcasted_iota(jnp.int32, (nb, s), 0)
    cur = t // MOBA_BLOCK
    past = blk < cur
    for j in range(2):
        gs = _dot_f32(kmean[:, j * hd:(j + 1) * hd], qT_ref[j * hd:(j + 1) * hd, :].astype(F32))
        gs = jnp.where(past, gs, NEG_INF)
        rank = _count_rank(gs, blk, nb)
        chosen = ((rank < min(MOBA_TOPK, nb - 1)) & past) | (blk == cur)
        negsel_ref[j, 0:nb, :] = jnp.where(chosen, 0.0, NEG_INF).astype(BF16)
        negsel_ref[j, nb:, :] = jnp.zeros((MOBA_SEL_ROWS - nb, s), BF16)


def _moba_select(qT, k_tok, batch, q_row_block, k_col_block):
    n = qT.shape[1]
    s = n // batch
    pair = 2 * HEAD_DIM
    return pl.pallas_call(
        _moba_select_kernel,
        grid=(batch, MOBA_HEADS // 2),
        in_specs=[pl.BlockSpec((pair, s), lambda i, j: (q_row_block + j, i)),
                  pl.BlockSpec((s, pair), lambda i, j: (i, k_col_block + j))],
        out_specs=pl.BlockSpec((2, MOBA_SEL_ROWS, s), lambda i, j: (j, 0, i)),
        out_shape=jax.ShapeDtypeStruct((MOBA_HEADS, MOBA_SEL_ROWS, n), BF16),
        compiler_params=_cparams("parallel", "parallel"),
        name="moba_select",
    )(qT, k_tok)


def _attn_kernel(*refs, n_pairs, heads_per_pair, v_heads_per_pair, has_sel, slot_of, window, clamp_sum):
    n_in = 6 if has_sel else 5
    qT_ref, k_ref, ktab_ref, vT_ref, qbias_ref = refs[:5]
    negsel_ref = refs[5] if has_sel else None
    oT_ref = refs[n_in]
    qaug_ref, vaug_ref, m_ref, acc_ref, s_ref, cmax_ref, alpha_ref, p_ref = refs[n_in + 1:n_in + 9]
    grp = pl.program_id(1)
    qi = pl.program_id(2)
    tq, tk, hd = ATTN_TQ, ATTN_TK, HEAD_DIM
    hpp = heads_per_pair
    pcol = hpp * tq
    ncol = n_pairs * pcol
    vrows = v_heads_per_pair * hd
    arows = vrows + ONES_ROWS

    @pl.when(qi == 0)
    def _():
        for pr in range(n_pairs):
            vaug_ref[pr * arows:pr * arows + vrows, :] = vT_ref[pr * vrows:(pr + 1) * vrows, :]
            vaug_ref[pr * arows + vrows:(pr + 1) * arows, :] = jnp.ones((ONES_ROWS, vaug_ref.shape[1]), BF16)

    for i in range(n_pairs * hpp):
        cols = slice(i * tq, (i + 1) * tq)
        slot = slot_of(i % hpp, grp)
        q = qT_ref[i * hd:(i + 1) * hd, :]
        zero = jnp.zeros_like(q)
        qaug_ref[0:hd, cols] = jnp.where(slot == 0, q, zero)
        qaug_ref[hd:2 * hd, cols] = jnp.where(slot == 1, q, zero)
        qaug_ref[BIAS_ROW:SEL_ROW, cols] = jnp.concatenate([qbias_ref[i]] * (tq // LANES), axis=1)
        fill = SEL_ROW
        if has_sel:
            n_rows = negsel_ref.shape[1]
            qaug_ref[SEL_ROW:SEL_ROW + n_rows, cols] = negsel_ref[i % negsel_ref.shape[0]]
            fill = SEL_ROW + n_rows
        qaug_ref[fill:, cols] = jnp.zeros((MXU_DIM - fill, tq), BF16)

    key_i = lax.broadcasted_iota(jnp.int32, (tk, ncol), 0)
    qry_i = lax.broadcasted_iota(jnp.int32, (tk, ncol), 1) % tq

    def key_slice(c):
        return pl.ds(pl.multiple_of(c * tk, tk), tk)

    def scores(c, mask):
        keys = key_slice(c)
        for pr in range(n_pairs):
            kaug = jnp.concatenate([k_ref[keys, pr * LANES:(pr + 1) * LANES], ktab_ref[keys, :]], axis=1)
            cols = slice(pr * pcol, (pr + 1) * pcol)
            s = _dot(kaug, qaug_ref[:, cols])
            if mask is not None:
                s = jnp.where(mask[:, cols], s, NEG_INF)
            s_ref[:, cols] = s
            cmax_ref[:, cols] = jnp.max(s, axis=0, keepdims=True)

    def softmax():
        m = m_ref[...]
        m_new = jnp.maximum(m, cmax_ref[...])
        m_ref[...] = m_new
        alpha_ref[...] = jnp.exp2(m - m_new)
        p_ref[...] = jnp.exp2(s_ref[...] - m_new).astype(BF16)

    def weighted(c):
        keys = key_slice(c)
        for pr in range(n_pairs):
            cols = slice(pr * pcol, (pr + 1) * pcol)
            pv = _dot(vaug_ref[pr * arows:(pr + 1) * arows, keys], p_ref[:, cols])
            acc_ref[:, cols] = alpha_ref[:, cols] * acc_ref[:, cols] + pv

    def step(c_prev, c_next, next_mask):
        weighted(c_prev)
        softmax()
        scores(c_next, next_mask)

    m_ref[...] = jnp.full((1, ncol), NEG_INF, F32)
    acc_ref[...] = jnp.zeros((arows, ncol), F32)
    big = jnp.int32(2 ** 30)
    diagonal = key_i <= qry_i

    if window is None:
        p_ref[...] = jnp.zeros(p_ref.shape, BF16)
        alpha_ref[...] = jnp.ones((1, ncol), F32)
        scores(0, key_i <= jnp.where(qi == 0, qry_i, big))

        @pl.loop(0, jnp.maximum(qi - 1, 0))
        def _(c):
            step(jnp.maximum(c - 1, 0), c + 1, None)

        @pl.when(qi > 0)
        def _():
            step(jnp.maximum(qi - 2, 0), qi, diagonal)

        weighted(jnp.maximum(qi - 1, 0))
        softmax()
        weighted(qi)
    else:
        assert window == 2 * tk and tq == tk
        lower = key_i > jnp.where(qi >= 2, qry_i, big)
        middle = key_i >= jnp.where(qi >= 1, 0, big)
        c0, c1 = jnp.maximum(qi - 2, 0), jnp.maximum(qi - 1, 0)
        scores(c0, lower)
        softmax()
        scores(c1, middle)
        step(c0, qi, diagonal)
        weighted(c1)
        softmax()
        weighted(qi)

    for i in range(n_pairs * hpp):
        vr = (i % v_heads_per_pair) * hd
        cols = slice(i * tq, (i + 1) * tq)
        l = acc_ref[vrows:vrows + 1, cols]
        oT_ref[i * hd:(i + 1) * hd, :] = acc_ref[vr:vr + hd, cols] / (jnp.maximum(l, TINY) if clamp_sum else l)


def _attention(qT, k_tok, key_table, vT, qbias, negsel, *, batch, n_steps, n_pairs, heads_per_pair,
               v_heads_per_pair, q_block, k_block, v_block, slot_of, window=None, clamp_sum, name):
    n = qT.shape[1]
    s = n // batch
    tq, tk, hd = ATTN_TQ, ATTN_TK, HEAD_DIM
    nq = s // tq
    nh = n_pairs * heads_per_pair
    ncol = nh * tq
    in_specs = [
        pl.BlockSpec((nh * hd, tq), lambda b, g, q: (q_block(g), b * nq + q)),
        pl.BlockSpec((s, n_pairs * LANES), lambda b, g, q: (b, k_block(g))),
        pl.BlockSpec((s, LANES), lambda b, g, q: (0, 0)),
        pl.BlockSpec((n_pairs * v_heads_per_pair * hd, s), lambda b, g, q: (v_block(g), b)),
        pl.BlockSpec((nh, 16, LANES), lambda b, g, q: (g, 0, 0)),
    ]
    args = [qT, k_tok, key_table, vT, qbias]
    if negsel is not None:
        per_step = negsel.shape[0] // n_steps
        in_specs.append(pl.BlockSpec((per_step, negsel.shape[1], tq), lambda b, g, q: (g, 0, b * nq + q)))
        args.append(negsel)
    return pl.pallas_call(
        functools.partial(_attn_kernel, n_pairs=n_pairs, heads_per_pair=heads_per_pair,
                          v_heads_per_pair=v_heads_per_pair, has_sel=negsel is not None, slot_of=slot_of,
                          window=window, clamp_sum=clamp_sum),
        grid=(batch, n_steps, nq),
        in_specs=in_specs,
        out_specs=pl.BlockSpec((nh * hd, tq), lambda b, g, q: (g, b * nq + q)),
        out_shape=jax.ShapeDtypeStruct((n_steps * nh * hd, n), F32),
        scratch_shapes=[pltpu.VMEM((MXU_DIM, ncol), BF16),
                        pltpu.VMEM((n_pairs * (v_heads_per_pair * hd + ONES_ROWS), s), BF16),
                        pltpu.VMEM((1, ncol), F32),
                        pltpu.VMEM((v_heads_per_pair * hd + ONES_ROWS, ncol), F32),
                        pltpu.VMEM((tk, ncol), F32), pltpu.VMEM((1, ncol), F32),
                        pltpu.VMEM((1, ncol), F32), pltpu.VMEM((tk, ncol), BF16)],
        compiler_params=_cparams("parallel", "parallel", "arbitrary"),
        name=name,
    )(*args)


def _mix_kernel(h_ref, g_ref, zgT_ref, ocmp_ref, osel_ref, owin_ref, omoba_ref, spread_ref,
                wmerge_ref, wn_ref, wm_ref, wo_ref, o_ref):
    h = h_ref[...]
    u = _rms(h, g_ref[...]).astype(BF16)
    gates = _sigmoid(zgT_ref[...])
    o_nsa = None
    for r, branch in enumerate((ocmp_ref, osel_ref, owin_ref)):
        term = _dot_f32(spread_ref[r], gates) * branch[...]
        o_nsa = term if o_nsa is None else o_nsa + term
    y_n = _dot_tn(o_nsa.astype(BF16), wn_ref[...])
    y_m = _dot_tn(omoba_ref[...].astype(BF16), wm_ref[...])
    mixed = (_sigmoid(_dot(u, wmerge_ref[:, :D_MODEL])) * y_n
             + _sigmoid(_dot(u, wmerge_ref[:, D_MODEL:])) * y_m)
    o_ref[...] = h + _dot(mixed.astype(BF16), wo_ref[...])


def _mix(h, g, zgT, ocmpT, oselT, owinT, omobaT, spread_t, wmerge, wn, wm, wo):
    n = h.shape[0]
    tm = TOKEN_TILE
    rows = lambda width: pl.BlockSpec((tm, width), lambda i: (i, 0))
    cols = lambda height: pl.BlockSpec((height, tm), lambda i: (0, i))
    return pl.pallas_call(
        _mix_kernel,
        grid=(n // tm,),
        in_specs=[rows(D_MODEL), _resident((1, D_MODEL)), cols(LANES), cols(NSA_WIDTH), cols(NSA_WIDTH),
                  cols(NSA_WIDTH), cols(MOBA_WIDTH), _resident(spread_t.shape), _resident(wmerge.shape),
                  _resident(wn.shape), _resident(wm.shape), _resident(wo.shape)],
        out_specs=rows(D_MODEL),
        out_shape=jax.ShapeDtypeStruct((n, D_MODEL), F32),
        compiler_params=_cparams("parallel"),
        name="mix",
    )(h, g, zgT, ocmpT, oselT, owinT, omobaT, spread_t, wmerge, wn, wm, wo)


def _key_table(s, block):
    pos = np.arange(s)
    tab = np.zeros((s, LANES), np.float32)
    tab[:, 0:N_SLOPE_PIECES] = ((pos // LANES) * LANES)[:, None]
    tab[:, N_SLOPE_PIECES:2 * N_SLOPE_PIECES] = (pos % LANES)[:, None]
    if block is not None:
        tab[pos, (SEL_ROW - BIAS_ROW) + pos // block] = 1.0
    return jnp.asarray(tab, BF16)


def _slope_rows(slopes2):
    s1 = slopes2.astype(BF16)
    r1 = slopes2 - s1.astype(F32)
    s2 = r1.astype(BF16)
    s3 = (r1 - s2.astype(F32)).astype(BF16)
    rows = jnp.stack([s1, s2, s3, s1, s2, s3], axis=1)
    rows = jnp.pad(rows, ((0, 0), (0, 16 - 2 * N_SLOPE_PIECES)))
    return jnp.broadcast_to(rows[:, :, None], rows.shape + (LANES,))


def _compress_weights(pos, w1, group):
    st, hd, gw = NSA_CMP_STRIDE, HEAD_DIM, NSA_KV_WIDTH
    lo = group * hd
    w = jnp.zeros((2, st, gw, NSA_CMP_HIDDEN), F32).at[:, :, lo:lo + hd, :].set(
        w1.reshape(2, st, hd, NSA_CMP_HIDDEN))
    ps = jnp.zeros((2, st, gw), F32).at[:, :, lo:lo + hd].set(pos.reshape(2, st, hd))
    return ps.reshape(2, 1, st * gw), w.reshape(2, st * gw, NSA_CMP_HIDDEN).astype(BF16)


def kernel(x, p, ffn1_norm, ffn1_w1, ffn1_w3, ffn1_w2, mix_norm, w_in, cmp_pos_k, cmp_w1_k, cmp_w2_k, cmp_pos_v, cmp_w1_v, cmp_w2_v, w_up_nsa, w_up_moba, w_out, ffn2_norm, ffn2_w1, ffn2_w3, ffn2_w2, ple_norm, w_ple_gate, w_ple, final_norm):
    b, s, d = x.shape
    n = b * s
    depth = p.shape[0]
    n_all = NSA_HEADS + MOBA_HEADS
    slopes = jnp.exp2(-8.0 * (jnp.arange(n_all, dtype=F32) + 1.0) / n_all)
    slopes2_n, slopes2_m = slopes[0::2] * LOG2E, slopes[1::2] * LOG2E
    qbias_n, qbias_m = _slope_rows(slopes2_n), _slope_rows(slopes2_m)

    n_cmp_rows = s // NSA_CMP_STRIDE
    n_cmp = (s - NSA_CMP_BLOCK) // NSA_CMP_STRIDE + 1
    n_sel = s // NSA_SEL_BLOCK
    c_start = np.arange(n_cmp_rows) * NSA_CMP_STRIDE
    s_start = np.arange(n_sel) * NSA_SEL_BLOCK
    overlap_t = ((c_start[None, :] <= s_start[:, None] + NSA_SEL_BLOCK - 1)
                 & (c_start[None, :] + NSA_CMP_BLOCK - 1 >= s_start[:, None])
                 & (np.arange(n_cmp_rows)[None, :] < n_cmp))
    overlap_t = jnp.asarray(np.tile(overlap_t, (1, 3)), BF16)
    spread_np = np.zeros((3, NSA_WIDTH, LANES), np.float32)
    for r in range(3):
        for hh in range(NSA_HEADS):
            spread_np[r, hh * HEAD_DIM:(hh + 1) * HEAD_DIM, 3 * hh + r] = 1.0
    spread_t = jnp.asarray(spread_np)
    table_sel = _key_table(s, NSA_SEL_BLOCK)
    table_win = _key_table(s, None)
    table_moba = _key_table(s, MOBA_BLOCK)

    h = x.reshape(n, d)
    for i in range(depth):
        vec = lambda a: a[i].reshape(1, -1)
        h = _ffn(h, vec(ffn1_norm), ffn1_w1[i].astype(BF16), ffn1_w3[i].astype(BF16),
                 ffn1_w2[i].astype(BF16))

        wi = w_in[i]
        col = lambda lo, width: wi[:, lo:lo + width]
        wq_t = jnp.concatenate([col(0, NSA_WIDTH), col(OFF_MOBA_Q, MOBA_WIDTH)], axis=1).T.astype(BF16)
        wk = jnp.concatenate([col(OFF_KSLC, NSA_KV_WIDTH), col(OFF_KWIN, NSA_KV_WIDTH),
                              col(OFF_MOBA_K, MOBA_WIDTH)], axis=1).astype(BF16)
        wc = jnp.stack([col(OFF_KCMP, NSA_KV_WIDTH), col(OFF_VCMP, NSA_KV_WIDTH)]).astype(BF16)
        wv_t = jnp.concatenate([col(OFF_VSLC, NSA_KV_WIDTH), col(OFF_VWIN, NSA_KV_WIDTH),
                                col(OFF_MOBA_V, MOBA_WIDTH)], axis=1).T.astype(BF16)
        wg_t = jnp.pad(col(OFF_GATE_LOGITS, N_GATE_LOGITS), ((0, 0), (0, LANES - N_GATE_LOGITS))).T.astype(BF16)
        w_merge = col(OFF_MERGE, 2 * D_MODEL).astype(BF16)
        qT, k_tok, cmp_tok, vT, zgT = _inproj(h, vec(mix_norm), wq_t, wk, wc, wv_t, wg_t)

        cmp_w = [[_compress_weights(ps[i], w1[i], g) for g in range(NSA_KV_GROUPS)]
                 for ps, w1 in ((cmp_pos_k, cmp_w1_k), (cmp_pos_v, cmp_w1_v))]
        pos = jnp.stack([jnp.stack([pw[0] for pw in kind]) for kind in cmp_w])
        w1c = jnp.stack([jnp.stack([pw[1] for pw in kind]) for kind in cmp_w])
        w2 = jnp.stack([cmp_w2_k[i], cmp_w2_v[i]])
        cmp_rows = cmp_tok.reshape(2, b, n_cmp_rows, NSA_CMP_STRIDE * NSA_KV_WIDTH)
        kc, kcT = _compress(cmp_rows, pos, w1c, w2, w2.transpose(0, 2, 1))

        ocmpT, negsel_n = _cmp_select(slopes2_n, qT, kc, kcT, overlap_t, b)
        nsa = dict(batch=b, n_steps=NSA_KV_GROUPS, n_pairs=1, heads_per_pair=NSA_HEADS_PER_GROUP,
                   v_heads_per_pair=1, q_block=lambda g: g, slot_of=lambda j, g: g, clamp_sum=True)
        oselT = _attention(qT, k_tok, table_sel, vT, qbias_n, negsel_n, k_block=lambda g: 0,
                           v_block=lambda g: g, name="attn_select", **nsa)
        owinT = _attention(qT, k_tok, table_win, vT, qbias_n, None, k_block=lambda g: 1,
                           v_block=lambda g: NSA_KV_GROUPS + g, window=NSA_WINDOW, name="attn_window", **nsa)
        negsel_m = _moba_select(qT, k_tok, b, q_row_block=NSA_WIDTH // LANES,
                                k_col_block=2 * NSA_KV_WIDTH // LANES)
        omobaT = _attention(qT, k_tok, table_moba, vT, qbias_m, negsel_m, batch=b, n_steps=2, n_pairs=2,
                            heads_per_pair=2, v_heads_per_pair=2, q_block=lambda g: 2 + g,
                            k_block=lambda g: 1 + g, v_block=lambda g: 1 + g, slot_of=lambda j, g: j,
                            clamp_sum=False, name="attn_moba")

        h = _mix(h, vec(mix_norm), zgT, ocmpT, oselT, owinT, omobaT, spread_t, w_merge,
                 w_up_nsa[i].astype(BF16), w_up_moba[i].astype(BF16), w_out[i].astype(BF16))

        h = _ffn(h, vec(ffn2_norm), ffn2_w1[i].astype(BF16), ffn2_w3[i].astype(BF16), ffn2_w2[i].astype(BF16),
                 tail=(p[i].reshape(n, PLE_DIM), vec(ple_norm), w_ple_gate[i].astype(BF16),
                       w_ple[i].astype(BF16)),
                 final_norm=final_norm.reshape(1, -1) if i + 1 == depth else None)
    return h.reshape(b, s, d)
```

```python
import functools

import numpy as np
import jax
import jax.numpy as jnp
from jax import lax
from jax.experimental import pallas as pl
from jax.experimental.pallas import tpu as pltpu

F32 = jnp.float32
BF16 = jnp.bfloat16

D_MODEL = 1024
HEAD_DIM = 64
NSA_HEADS = 8
NSA_KV_GROUPS = 2
NSA_HEADS_PER_GROUP = NSA_HEADS // NSA_KV_GROUPS
NSA_CMP_BLOCK = 32
NSA_CMP_STRIDE = 16
NSA_CMP_HIDDEN = 128
NSA_SEL_BLOCK = 64
NSA_SEL_TOPK = 16
NSA_WINDOW = 512
MOBA_HEADS = 8
MOBA_BLOCK = 256
MOBA_TOPK = 3
D_FF = 2816
PLE_DIM = 256
RMS_EPS = 1e-6
NEG_INF = -1e30
TINY = 1e-30
FORCE_SCORE = 1e9
LOG2E = float(np.log2(np.e))
Q_SCALE = HEAD_DIM ** -0.5 * LOG2E

NSA_WIDTH = NSA_HEADS * HEAD_DIM
NSA_KV_WIDTH = NSA_KV_GROUPS * HEAD_DIM
MOBA_WIDTH = MOBA_HEADS * HEAD_DIM
N_GATE_LOGITS = 3 * NSA_HEADS
OFF_KCMP = NSA_WIDTH
OFF_VCMP = OFF_KCMP + NSA_KV_WIDTH
OFF_KSLC = OFF_VCMP + NSA_KV_WIDTH
OFF_VSLC = OFF_KSLC + NSA_KV_WIDTH
OFF_KWIN = OFF_VSLC + NSA_KV_WIDTH
OFF_VWIN = OFF_KWIN + NSA_KV_WIDTH
OFF_GATE_LOGITS = OFF_VWIN + NSA_KV_WIDTH
OFF_MOBA_Q = OFF_GATE_LOGITS + N_GATE_LOGITS
OFF_MOBA_K = OFF_MOBA_Q + MOBA_WIDTH
OFF_MOBA_V = OFF_MOBA_K + MOBA_WIDTH
OFF_MERGE = OFF_MOBA_V + MOBA_WIDTH

LANES = 128
MXU_DIM = 256
VMEM_LIMIT = 56 * 1024 * 1024

TOKEN_TILE = 512
FF_CHUNK = 256
ATTN_TQ = 256
ATTN_TK = 256
CMP_TQ = 2048
ATTN_UNROLL = 4
BIAS_ROW = 2 * HEAD_DIM
SEL_ROW = BIAS_ROW + 16
N_SLOPE_PIECES = 3
MOBA_SEL_ROWS = 16
ONES_ROWS = 16

_NT = (((1,), (1,)), ((), ()))
_TN = (((0,), (0,)), ((), ()))


def _cparams(*sem):
    return pltpu.CompilerParams(dimension_semantics=sem, vmem_limit_bytes=VMEM_LIMIT)


def _resident(shape):
    nd = len(shape)
    return pl.BlockSpec(shape, lambda *_: (0,) * nd, pipeline_mode=pl.Buffered(1))


def _rms(x, g):
    ms = jnp.mean(x * x, axis=-1, keepdims=True)
    return x * lax.rsqrt(ms + RMS_EPS) * g


def _sigmoid(x):
    return 1.0 / (1.0 + jnp.exp(-x))


def _dot(a, b):
    return jnp.dot(a, b, preferred_element_type=F32)


def _dot_f32(a, b):
    return jnp.dot(a, b, preferred_element_type=F32, precision=lax.Precision.HIGHEST)


def _dot_nt(a, b, precision=None):
    return lax.dot_general(a, b, _NT, preferred_element_type=F32, precision=precision)


def _dot_tn(a, b):
    return lax.dot_general(a, b, _TN, preferred_element_type=F32)


def _ffn_kernel(*refs, tail, final):
    x_ref, g_ref, w1_ref, w3_ref, w2_ref = refs[:5]
    o_ref, acc_ref = refs[-2:]
    if tail:
        p_ref, gp_ref, wpg_ref, wp_ref = refs[5:9]
    if final:
        gf_ref = refs[-3]
    x = x_ref[...]
    xn = _rms(x, g_ref[...]).astype(BF16)
    for c in range(D_FF // FF_CHUNK):
        cols = slice(c * FF_CHUNK, (c + 1) * FF_CHUNK)
        a = _dot(xn, w1_ref[:, cols])
        b = _dot(xn, w3_ref[:, cols])
        hid = (a * _sigmoid(a) * b).astype(BF16)
        y = _dot(hid, w2_ref[cols, :])
        if c == 0:
            acc_ref[...] = y
        else:
            acc_ref[...] += y
    h = x + 0.5 * acc_ref[...]
    if tail:
        gate = _sigmoid(_dot(_rms(h, gp_ref[...]).astype(BF16), wpg_ref[...]))
        h = h + gate * _dot(p_ref[...].astype(BF16), wp_ref[...])
    if final:
        h = _rms(h, gf_ref[...])
    o_ref[...] = h


def _ffn(x, g, w1, w3, w2, tail=None, final_norm=None):
    n = x.shape[0]
    row = lambda w: pl.BlockSpec((TOKEN_TILE, w), lambda i: (i, 0))
    args = [x, g, w1, w3, w2]
    specs = [row(D_MODEL), _resident((1, D_MODEL)), _resident(w1.shape), _resident(w3.shape),
             _resident(w2.shape)]
    if tail is not None:
        p, gp, wpg, wp = tail
        args += [p, gp, wpg, wp]
        specs += [row(PLE_DIM), _resident((1, D_MODEL)), _resident(wpg.shape), _resident(wp.shape)]
    if final_norm is not None:
        args.append(final_norm)
        specs.append(_resident((1, D_MODEL)))
    return pl.pallas_call(
        functools.partial(_ffn_kernel, tail=tail is not None, final=final_norm is not None),
        grid=(n // TOKEN_TILE,),
        in_specs=specs,
        out_specs=row(D_MODEL),
        out_shape=jax.ShapeDtypeStruct((n, D_MODEL), F32),
        scratch_shapes=[pltpu.VMEM((TOKEN_TILE, D_MODEL), F32)],
        compiler_params=_cparams("parallel"),
        name="ffn_tail" if tail is not None else "ffn",
    )(*args)


def _inproj_kernel(h_ref, g_ref, wq_ref, wk_ref, wc_ref, wv_ref, wg_ref,
                   qT_ref, k_ref, cmp_ref, vT_ref, zgT_ref):
    u = _rms(h_ref[...], g_ref[...]).astype(BF16)
    qT_ref[...] = (_dot_nt(wq_ref[...], u) * Q_SCALE).astype(BF16)
    k_ref[...] = _dot(u, wk_ref[...]).astype(BF16)
    for kind in range(2):
        cmp_ref[kind] = _dot(u, wc_ref[kind]).astype(BF16)
    vT_ref[...] = _dot_nt(wv_ref[...], u).astype(BF16)
    zgT_ref[...] = _dot_nt(wg_ref[...], u)


def _inproj(h, g, wq_t, wk, wc, wv_t, wg_t):
    n = h.shape[0]
    tm = TOKEN_TILE
    rows = lambda width: pl.BlockSpec((tm, width), lambda i: (i, 0))
    cols = lambda height: pl.BlockSpec((height, tm), lambda i: (0, i))
    return pl.pallas_call(
        _inproj_kernel,
        grid=(n // tm,),
        in_specs=[rows(D_MODEL), _resident((1, D_MODEL)), _resident(wq_t.shape), _resident(wk.shape),
                  _resident(wc.shape), _resident(wv_t.shape), _resident(wg_t.shape)],
        out_specs=[cols(wq_t.shape[0]), rows(wk.shape[1]),
                   pl.BlockSpec((2, tm, NSA_KV_WIDTH), lambda i: (0, i, 0)),
                   cols(wv_t.shape[0]), cols(LANES)],
        out_shape=[jax.ShapeDtypeStruct((wq_t.shape[0], n), BF16),
                   jax.ShapeDtypeStruct((n, wk.shape[1]), BF16),
                   jax.ShapeDtypeStruct((2, n, NSA_KV_WIDTH), BF16),
                   jax.ShapeDtypeStruct((wv_t.shape[0], n), BF16),
                   jax.ShapeDtypeStruct((LANES, n), F32)],
        compiler_params=_cparams("parallel"),
        name="inproj",
    )(h, g, wq_t, wk, wc, wv_t, wg_t)


def _split3(x):
    hi = x.astype(BF16)
    r = x - hi.astype(F32)
    mid = r.astype(BF16)
    return hi, mid, (r - mid.astype(F32)).astype(BF16)


def _compress_kernel(rows_ref, pos_ref, w1_ref, w2_ref, w2t_ref, o_ref, ot_ref):
    rows = rows_ref[...].astype(F32)
    first = _dot((rows + pos_ref[0]).astype(BF16), w1_ref[0])
    second = _dot((rows + pos_ref[1]).astype(BF16), w1_ref[1])
    n_rows = rows.shape[0]
    pre = first + pltpu.roll(second, n_rows - 1, 0)
    c0 = float(np.sqrt(2.0 / np.pi))
    hid = pre * (0.5 * (1.0 + jnp.tanh(c0 * (pre + 0.044715 * (pre * pre * pre)))))
    out = _dot_f32(hid, w2_ref[...])
    o_ref[...] = jnp.concatenate(_split3(out) + (jnp.zeros(out.shape, BF16),), axis=1)
    ot_ref[...] = _dot_nt(w2t_ref[...], hid, lax.Precision.HIGHEST).astype(BF16)


def _compress(cmp_rows, pos, w1, w2, w2t):
    _, b, n_rows, width = cmp_rows.shape
    g = NSA_KV_GROUPS
    per_kind_group = lambda *tail: pl.BlockSpec((None, None) + tail, lambda i, k, j: (k, j) + (0,) * len(tail))
    per_kind = lambda *tail: pl.BlockSpec((None,) + tail, lambda i, k, j: (k,) + (0,) * len(tail))
    return pl.pallas_call(
        _compress_kernel,
        grid=(b, 2, g),
        in_specs=[
            pl.BlockSpec((None, None, n_rows, width), lambda i, k, j: (k, i, 0, 0)),
            per_kind_group(2, 1, width),
            per_kind_group(2, width, NSA_CMP_HIDDEN),
            per_kind(NSA_CMP_HIDDEN, HEAD_DIM),
            per_kind(HEAD_DIM, NSA_CMP_HIDDEN),
        ],
        out_specs=[
            pl.BlockSpec((None, None, n_rows, 4 * HEAD_DIM), lambda i, k, j: (i, k * g + j, 0, 0)),
            pl.BlockSpec((None, None, HEAD_DIM, n_rows), lambda i, k, j: (i, k * g + j, 0, 0)),
        ],
        out_shape=[jax.ShapeDtypeStruct((b, 2 * g, n_rows, 4 * HEAD_DIM), BF16),
                   jax.ShapeDtypeStruct((b, 2 * g, HEAD_DIM, n_rows), BF16)],
        compiler_params=_cparams("parallel", "parallel", "parallel"),
        name="compress",
    )(cmp_rows, pos, w1, w2, w2t)


def _count_rank(score, idx, n_candidates):
    rank = jnp.zeros(score.shape, jnp.int32)
    for j in range(n_candidates):
        row = score[j:j + 1, :]
        ahead = (row > score) | ((row == score) & (idx > j))
        rank = rank + ahead.astype(jnp.int32)
    return rank


def _cmp_select_kernel(slopes_ref, qT_ref, kc_ref, vcT_ref, ovT_ref, oT_ref, negsel_ref):
    g = pl.program_id(1)
    qi = pl.program_id(2)
    tq = qT_ref.shape[1]
    n_cmp = kc_ref.shape[0]
    n_sel = ovT_ref.shape[0]
    hd = HEAD_DIM
    t = qi * tq + lax.broadcasted_iota(jnp.int32, (n_cmp, tq), 1)
    blk_c = lax.broadcasted_iota(jnp.int32, (n_cmp, tq), 0)
    dist = t - (blk_c * NSA_CMP_STRIDE + (NSA_CMP_BLOCK - 1))
    visible = dist >= 0
    dist_f = dist.astype(F32)
    kc3 = kc_ref[...]
    vcT = vcT_ref[...]
    p_sum = jnp.zeros((n_cmp, tq), F32)
    for hh in range(NSA_HEADS_PER_GROUP):
        slope2 = slopes_ref[g * NSA_HEADS_PER_GROUP + hh]
        q = qT_ref[hh * hd:(hh + 1) * hd, :]
        s = _dot(kc3, jnp.concatenate([q, q, q, jnp.zeros_like(q)], axis=0)) - slope2 * dist_f
        s = jnp.where(visible, s, NEG_INF)
        m = jnp.max(s, axis=0, keepdims=True)
        e = jnp.exp2(s - m)
        any_visible = m > 0.5 * NEG_INF
        p = e * jnp.where(any_visible, 1.0 / jnp.maximum(jnp.sum(e, axis=0, keepdims=True), TINY), 0.0)
        p_sum = p_sum + p
        oT_ref[hh * hd:(hh + 1) * hd, :] = _dot(vcT, p.astype(BF16))
    imp = _dot(ovT_ref[...], jnp.concatenate(_split3(p_sum), axis=0))
    ts = qi * tq + lax.broadcasted_iota(jnp.int32, (n_sel, tq), 1)
    blk = lax.broadcasted_iota(jnp.int32, (n_sel, tq), 0)
    cur = ts // NSA_SEL_BLOCK
    causal = blk * NSA_SEL_BLOCK <= ts
    forced = (blk == 0) | (blk == cur) | (blk == cur - 1)
    imp = jnp.where(causal, jnp.where(forced, FORCE_SCORE, imp), NEG_INF)
    rank = _count_rank(imp, blk, n_sel)
    chosen = (rank < min(NSA_SEL_TOPK, n_sel)) & causal
    negsel_ref[...] = jnp.where(chosen, 0.0, NEG_INF).astype(BF16)


def _cmp_select(slopes2, qT, kc, vcT, overlap_t, batch):
    n = qT.shape[1]
    tq = min(CMP_TQ, n // batch)
    nq = n // batch // tq
    rows = NSA_HEADS_PER_GROUP * HEAD_DIM
    n_cmp = kc.shape[2]
    n_sel = overlap_t.shape[0]
    return pl.pallas_call(
        _cmp_select_kernel,
        grid=(batch, NSA_KV_GROUPS, nq),
        in_specs=[
            pl.BlockSpec(memory_space=pltpu.SMEM),
            pl.BlockSpec((rows, tq), lambda i, g, q: (g, i * nq + q)),
            pl.BlockSpec((None, None, n_cmp, 4 * HEAD_DIM), lambda i, g, q: (i, g, 0, 0)),
            pl.BlockSpec((None, None, HEAD_DIM, n_cmp), lambda i, g, q: (i, NSA_KV_GROUPS + g, 0, 0)),
            pl.BlockSpec((n_sel, 3 * n_cmp), lambda i, g, q: (0, 0)),
        ],
        out_specs=[
            pl.BlockSpec((rows, tq), lambda i, g, q: (g, i * nq + q)),
            pl.BlockSpec((None, n_sel, tq), lambda i, g, q: (g, 0, i * nq + q)),
        ],
        out_shape=[jax.ShapeDtypeStruct((NSA_WIDTH, n), F32),
                   jax.ShapeDtypeStruct((NSA_KV_GROUPS, n_sel, n), BF16)],
        compiler_params=_cparams("parallel", "parallel", "parallel"),
        name="cmp_select",
    )(slopes2, qT, kc, vcT, overlap_t)


def _moba_select_kernel(qT_ref, k_ref, negsel_ref):
    s = k_ref.shape[0]
    nb = s // MOBA_BLOCK
    hd = HEAD_DIM
    kmean = jnp.mean(k_ref[...].astype(F32).reshape(nb, MOBA_BLOCK, 2 * hd), axis=1)
    t = lax.broadcasted_iota(jnp.int32, (nb, s), 1)
    blk = lax.broadcasted_iota(jnp.int32, (nb, s), 0)
    cur = t // MOBA_BLOCK
    past = blk < cur
    for j in range(2):
        gs = _dot_f32(kmean[:, j * hd:(j + 1) * hd], qT_ref[j * hd:(j + 1) * hd, :].astype(F32))
        gs = jnp.where(past, gs, NEG_INF)
        rank = _count_rank(gs, blk, nb)
        chosen = ((rank < min(MOBA_TOPK, nb - 1)) & past) | (blk == cur)
        negsel_ref[j, 0:nb, :] = jnp.where(chosen, 0.0, NEG_INF).astype(BF16)
        negsel_ref[j, nb:, :] = jnp.zeros((MOBA_SEL_ROWS - nb, s), BF16)


def _moba_select(qT, k_tok, batch, q_row_block, k_col_block):
    n = qT.shape[1]
    s = n // batch
    pair = 2 * HEAD_DIM
    return pl.pallas_call(
        _moba_select_kernel,
        grid=(batch, MOBA_HEADS // 2),
        in_specs=[pl.BlockSpec((pair, s), lambda i, j: (q_row_block + j, i)),
                  pl.BlockSpec((s, pair), lambda i, j: (i, k_col_block + j))],
        out_specs=pl.BlockSpec((2, MOBA_SEL_ROWS, s), lambda i, j: (j, 0, i)),
        out_shape=jax.ShapeDtypeStruct((MOBA_HEADS, MOBA_SEL_ROWS, n), BF16),
        compiler_params=_cparams("parallel", "parallel"),
        name="moba_select",
    )(qT, k_tok)


def _attn_schedule(nq, window_chunks):
    def padded(pairs, start):
        n_pad = (-(start + len(pairs) - 1)) % ATTN_UNROLL
        return pairs + [(0, 0, 0)] * n_pad

    diag = [(q, q, 1) for q in range(nq)]
    groups = [("diag", padded(diag, 0))]
    if window_chunks is None:
        full = [(q, c, 1) for q in range(nq) for c in range(q)]
        groups.append((None, full))
    else:
        assert window_chunks == 2
        done = len(groups[0][1])
        lower = padded([(q, q - 2, 1) for q in range(2, nq)], done)
        groups.append(("lower", lower))
        groups.append(("middle", [(q, q - 1, 1) for q in range(1, nq)]))
    pairs, kinds = [], []
    for kind, group in groups:
        pairs += group
        kinds += [kind] * len(group)
    n_pad = (-(len(pairs) - 1)) % ATTN_UNROLL
    pairs += [(0, 0, 0)] * n_pad
    kinds += [kinds[-1] if kinds[-1] is not None else "diag"] * n_pad
    body_kinds = []
    for j in range((len(pairs) - 1) // ATTN_UNROLL):
        ks = set(kinds[1 + j * ATTN_UNROLL:1 + (j + 1) * ATTN_UNROLL])
        assert len(ks) == 1, "a loop body must issue scores of one mask kind"
        body_kinds.append(ks.pop())
    table = np.asarray(pairs, np.int32)
    return table[:, 0], table[:, 1], table[:, 2], tuple(body_kinds)


def _attn_kernel(*refs, n_pairs, heads_per_pair, v_heads_per_pair, has_sel, slot_of, body_kinds, clamp_sum):
    n_in = 9 if has_sel else 8
    qtab_ref, ctab_ref, vtab_ref, qT_ref, k_ref, ktab_ref, vT_ref, qbias_ref = refs[:8]
    negsel_ref = refs[8] if has_sel else None
    oT_ref = refs[n_in]
    qaug_ref, vaug_ref, m_ref, acc_ref = refs[n_in + 1:n_in + 5]
    s_bufs = refs[n_in + 5:n_in + 7]
    cmax_bufs = refs[n_in + 7:n_in + 9]
    alpha_bufs = refs[n_in + 9:n_in + 11]
    p_bufs = refs[n_in + 11:n_in + 13]
    grp = pl.program_id(1)
    tq, tk, hd = ATTN_TQ, ATTN_TK, HEAD_DIM
    nq = qaug_ref.shape[0]
    hpp = heads_per_pair
    nh = n_pairs * hpp
    pcol = hpp * tq
    ncol = n_pairs * pcol
    vrows = v_heads_per_pair * hd
    arows = vrows + ONES_ROWS
    n_steps = qtab_ref.shape[0]
    unroll = ATTN_UNROLL

    for pr in range(n_pairs):
        vaug_ref[pr * arows:pr * arows + vrows, :] = vT_ref[pr * vrows:(pr + 1) * vrows, :]
        vaug_ref[pr * arows + vrows:(pr + 1) * arows, :] = jnp.ones((ONES_ROWS, vaug_ref.shape[1]), BF16)

    for qi in range(nq):
        toks = slice(qi * tq, (qi + 1) * tq)
        for i in range(nh):
            cols = slice(i * tq, (i + 1) * tq)
            slot = slot_of(i % hpp, grp)
            q = qT_ref[i * hd:(i + 1) * hd, toks]
            zero = jnp.zeros_like(q)
            qaug_ref[qi, 0:hd, cols] = jnp.where(slot == 0, q, zero)
            qaug_ref[qi, hd:2 * hd, cols] = jnp.where(slot == 1, q, zero)
            qaug_ref[qi, BIAS_ROW:SEL_ROW, cols] = jnp.concatenate([qbias_ref[i]] * (tq // LANES), axis=1)
            fill = SEL_ROW
            if has_sel:
                n_rows = negsel_ref.shape[1]
                qaug_ref[qi, SEL_ROW:SEL_ROW + n_rows, cols] = negsel_ref[i % negsel_ref.shape[0], :, toks]
                fill = SEL_ROW + n_rows
            qaug_ref[qi, fill:, cols] = jnp.zeros((MXU_DIM - fill, tq), BF16)

    m_ref[...] = jnp.full(m_ref.shape, NEG_INF, F32)
    acc_ref[...] = jnp.zeros(acc_ref.shape, F32)
    p_bufs[1][...] = jnp.zeros(p_bufs[1].shape, BF16)
    alpha_bufs[1][...] = jnp.ones((1, ncol), F32)

    key_i = lax.broadcasted_iota(jnp.int32, (tk, ncol), 0)
    qry_row = lax.broadcasted_iota(jnp.int32, (1, ncol), 1) % tq
    big = jnp.int32(2 ** 30)

    def visible(kind, valid):
        if kind == "diag":
            return key_i <= jnp.where(valid, qry_row, -1)
        if kind == "lower":
            return key_i > jnp.where(valid, qry_row, big)
        if kind == "middle":
            return key_i >= jnp.where(valid, 0, big)
        return None

    def key_slice(t):
        return pl.ds(pl.multiple_of(ctab_ref[t] * tk, tk), tk)

    def scores(t, buf, kind):
        qi = qtab_ref[t]
        keys = key_slice(t)
        mask = visible(kind, vtab_ref[t] > 0)
        for pr in range(n_pairs):
            kaug = jnp.concatenate([k_ref[keys, pr * LANES:(pr + 1) * LANES], ktab_ref[keys, :]], axis=1)
            cols = slice(pr * pcol, (pr + 1) * pcol)
            s = _dot(kaug, qaug_ref[qi, :, cols])
            if mask is not None:
                s = jnp.where(mask[:, cols], s, NEG_INF)
            s_bufs[buf][:, cols] = s
            cmax_bufs[buf][:, cols] = jnp.max(s, axis=0, keepdims=True)

    def softmax(t, buf):
        qi = qtab_ref[t]
        m = m_ref[qi]
        m_new = jnp.maximum(m, cmax_bufs[buf][...])
        m_ref[qi] = m_new
        alpha_bufs[buf][...] = jnp.exp2(m - m_new)
        p_bufs[buf][...] = jnp.exp2(s_bufs[buf][...] - m_new).astype(BF16)

    def weighted(t, buf):
        qi = qtab_ref[t]
        keys = key_slice(t)
        for pr in range(n_pairs):
            cols = slice(pr * pcol, (pr + 1) * pcol)
            pv = _dot(vaug_ref[pr * arows:(pr + 1) * arows, keys], p_bufs[buf][:, cols])
            acc_ref[qi, :, cols] = alpha_bufs[buf][:, cols] * acc_ref[qi, :, cols] + pv

    scores(0, 0, "diag")
    first = 0
    for kind in sorted(set(body_kinds), key=body_kinds.index):
        count = body_kinds.count(kind)
        assert body_kinds[first:first + count] == (kind,) * count

        @pl.loop(first, first + count)
        def _(j, kind=kind):
            for u in range(unroll):
                t = j * unroll + u
                scores(t + 1, (u + 1) % 2, kind)
                weighted(jnp.maximum(t - 1, 0), (u + 1) % 2)
                softmax(t, u % 2)

        first += count
    last = n_steps - 1
    weighted(last - 1, (last - 1) % 2)
    softmax(last, last % 2)
    weighted(last, last % 2)

    for qi in range(nq):
        for i in range(nh):
            vr = (i % v_heads_per_pair) * hd
            cols = slice(i * tq, (i + 1) * tq)
            l = acc_ref[qi, vrows:vrows + 1, cols]
            oT_ref[i * hd:(i + 1) * hd, qi * tq:(qi + 1) * tq] = (
                acc_ref[qi, vr:vr + hd, cols] / (jnp.maximum(l, TINY) if clamp_sum else l))


def _attention(qT, k_tok, key_table, vT, qbias, negsel, *, batch, n_steps, n_pairs, heads_per_pair,
               v_heads_per_pair, q_block, k_block, v_block, slot_of, window=None, clamp_sum, name):
    n = qT.shape[1]
    s = n // batch
    tq, tk, hd = ATTN_TQ, ATTN_TK, HEAD_DIM
    assert tq == tk and s % tq == 0
    nq = s // tq
    nh = n_pairs * heads_per_pair
    ncol = nh * tq
    arows = v_heads_per_pair * hd + ONES_ROWS
    if window is not None:
        assert window % tk == 0
    q_tab, c_tab, v_tab, body_kinds = _attn_schedule(nq, None if window is None else window // tk)
    smem = pl.BlockSpec(memory_space=pltpu.SMEM)
    in_specs = [
        smem, smem, smem,
        pl.BlockSpec((nh * hd, s), lambda b, g: (q_block(g), b)),
        pl.BlockSpec((s, n_pairs * LANES), lambda b, g: (b, k_block(g))),
        pl.BlockSpec((s, LANES), lambda b, g: (0, 0)),
        pl.BlockSpec((n_pairs * v_heads_per_pair * hd, s), lambda b, g: (v_block(g), b)),
        pl.BlockSpec((nh, 16, LANES), lambda b, g: (g, 0, 0)),
    ]
    args = [jnp.asarray(q_tab), jnp.asarray(c_tab), jnp.asarray(v_tab), qT, k_tok, key_table, vT, qbias]
    if negsel is not None:
        per_step = negsel.shape[0] // n_steps
        in_specs.append(pl.BlockSpec((per_step, negsel.shape[1], s), lambda b, g: (g, 0, b)))
        args.append(negsel)
    vec = lambda: pltpu.VMEM((1, ncol), F32)
    return pl.pallas_call(
        functools.partial(_attn_kernel, n_pairs=n_pairs, heads_per_pair=heads_per_pair,
                          v_heads_per_pair=v_heads_per_pair, has_sel=negsel is not None, slot_of=slot_of,
                          body_kinds=body_kinds, clamp_sum=clamp_sum),
        grid=(batch, n_steps),
        in_specs=in_specs,
        out_specs=pl.BlockSpec((nh * hd, s), lambda b, g: (g, b)),
        out_shape=jax.ShapeDtypeStruct((n_steps * nh * hd, n), F32),
        scratch_shapes=[pltpu.VMEM((nq, MXU_DIM, ncol), BF16),
                        pltpu.VMEM((n_pairs * arows, s), BF16),
                        pltpu.VMEM((nq, 1, ncol), F32),
                        pltpu.VMEM((nq, arows, ncol), F32),
                        pltpu.VMEM((tk, ncol), F32), pltpu.VMEM((tk, ncol), F32),
                        vec(), vec(), vec(), vec(),
                        pltpu.VMEM((tk, ncol), BF16), pltpu.VMEM((tk, ncol), BF16)],
        compiler_params=_cparams("parallel", "parallel"),
        name=name,
    )(*args)


def _mix_kernel(h_ref, g_ref, zgT_ref, ocmp_ref, osel_ref, owin_ref, omoba_ref,
                wmerge_ref, wn_ref, wm_ref, wo_ref, o_ref, gate_ref, onsa_ref):
    h = h_ref[...]
    u = _rms(h, g_ref[...]).astype(BF16)
    gate_ref[...] = _sigmoid(zgT_ref[0:gate_ref.shape[0], :])
    hd = HEAD_DIM
    for hh in range(NSA_HEADS):
        rows = slice(hh * hd, (hh + 1) * hd)
        o = None
        for r, branch in enumerate((ocmp_ref, osel_ref, owin_ref)):
            gate = gate_ref[3 * hh + r:3 * hh + r + 1, :]
            term = gate * branch[rows, :]
            o = term if o is None else o + term
        onsa_ref[rows, :] = o.astype(BF16)
    y_n = _dot_tn(onsa_ref[...], wn_ref[...])
    y_m = _dot_tn(omoba_ref[...].astype(BF16), wm_ref[...])
    mixed = (_sigmoid(_dot(u, wmerge_ref[:, :D_MODEL])) * y_n
             + _sigmoid(_dot(u, wmerge_ref[:, D_MODEL:])) * y_m)
    o_ref[...] = h + _dot(mixed.astype(BF16), wo_ref[...])


def _mix(h, g, zgT, ocmpT, oselT, owinT, omobaT, wmerge, wn, wm, wo):
    n = h.shape[0]
    tm = TOKEN_TILE
    rows = lambda width: pl.BlockSpec((tm, width), lambda i: (i, 0))
    cols = lambda height: pl.BlockSpec((height, tm), lambda i: (0, i))
    gate_rows = -(-N_GATE_LOGITS // 8) * 8
    return pl.pallas_call(
        _mix_kernel,
        grid=(n // tm,),
        in_specs=[rows(D_MODEL), _resident((1, D_MODEL)), cols(LANES), cols(NSA_WIDTH), cols(NSA_WIDTH),
                  cols(NSA_WIDTH), cols(MOBA_WIDTH), _resident(wmerge.shape),
                  _resident(wn.shape), _resident(wm.shape), _resident(wo.shape)],
        out_specs=rows(D_MODEL),
        out_shape=jax.ShapeDtypeStruct((n, D_MODEL), F32),
        scratch_shapes=[pltpu.VMEM((gate_rows, tm), F32), pltpu.VMEM((NSA_WIDTH, tm), BF16)],
        compiler_params=_cparams("parallel"),
        name="mix",
    )(h, g, zgT, ocmpT, oselT, owinT, omobaT, wmerge, wn, wm, wo)


def _key_table(s, block):
    pos = np.arange(s)
    tab = np.zeros((s, LANES), np.float32)
    tab[:, 0:N_SLOPE_PIECES] = ((pos // LANES) * LANES)[:, None]
    tab[:, N_SLOPE_PIECES:2 * N_SLOPE_PIECES] = (pos % LANES)[:, None]
    if block is not None:
        tab[pos, (SEL_ROW - BIAS_ROW) + pos // block] = 1.0
    return jnp.asarray(tab, BF16)


def _slope_rows(slopes2):
    s1 = slopes2.astype(BF16)
    r1 = slopes2 - s1.astype(F32)
    s2 = r1.astype(BF16)
    s3 = (r1 - s2.astype(F32)).astype(BF16)
    rows = jnp.stack([s1, s2, s3, s1, s2, s3], axis=1)
    rows = jnp.pad(rows, ((0, 0), (0, 16 - 2 * N_SLOPE_PIECES)))
    return jnp.broadcast_to(rows[:, :, None], rows.shape + (LANES,))


def _compress_weights(pos, w1, group):
    st, hd, gw = NSA_CMP_STRIDE, HEAD_DIM, NSA_KV_WIDTH
    lo = group * hd
    w = jnp.zeros((2, st, gw, NSA_CMP_HIDDEN), F32).at[:, :, lo:lo + hd, :].set(
        w1.reshape(2, st, hd, NSA_CMP_HIDDEN))
    ps = jnp.zeros((2, st, gw), F32).at[:, :, lo:lo + hd].set(pos.reshape(2, st, hd))
    return ps.reshape(2, 1, st * gw), w.reshape(2, st * gw, NSA_CMP_HIDDEN).astype(BF16)


def kernel(x, p, ffn1_norm, ffn1_w1, ffn1_w3, ffn1_w2, mix_norm, w_in, cmp_pos_k, cmp_w1_k, cmp_w2_k, cmp_pos_v, cmp_w1_v, cmp_w2_v, w_up_nsa, w_up_moba, w_out, ffn2_norm, ffn2_w1, ffn2_w3, ffn2_w2, ple_norm, w_ple_gate, w_ple, final_norm):
    b, s, d = x.shape
    n = b * s
    depth = p.shape[0]
    n_all = NSA_HEADS + MOBA_HEADS
    slopes = jnp.exp2(-8.0 * (jnp.arange(n_all, dtype=F32) + 1.0) / n_all)
    slopes2_n, slopes2_m = slopes[0::2] * LOG2E, slopes[1::2] * LOG2E
    qbias_n, qbias_m = _slope_rows(slopes2_n), _slope_rows(slopes2_m)

    n_cmp_rows = s // NSA_CMP_STRIDE
    n_cmp = (s - NSA_CMP_BLOCK) // NSA_CMP_STRIDE + 1
    n_sel = s // NSA_SEL_BLOCK
    c_start = np.arange(n_cmp_rows) * NSA_CMP_STRIDE
    s_start = np.arange(n_sel) * NSA_SEL_BLOCK
    overlap_t = ((c_start[None, :] <= s_start[:, None] + NSA_SEL_BLOCK - 1)
                 & (c_start[None, :] + NSA_CMP_BLOCK - 1 >= s_start[:, None])
                 & (np.arange(n_cmp_rows)[None, :] < n_cmp))
    overlap_t = jnp.asarray(np.tile(overlap_t, (1, 3)), BF16)
    table_sel = _key_table(s, NSA_SEL_BLOCK)
    table_win = _key_table(s, None)
    table_moba = _key_table(s, MOBA_BLOCK)

    h = x.reshape(n, d)
    for i in range(depth):
        vec = lambda a: a[i].reshape(1, -1)
        h = _ffn(h, vec(ffn1_norm), ffn1_w1[i].astype(BF16), ffn1_w3[i].astype(BF16),
                 ffn1_w2[i].astype(BF16))

        wi = w_in[i]
        col = lambda lo, width: wi[:, lo:lo + width]
        wq_t = jnp.concatenate([col(0, NSA_WIDTH), col(OFF_MOBA_Q, MOBA_WIDTH)], axis=1).T.astype(BF16)
        wk = jnp.concatenate([col(OFF_KSLC, NSA_KV_WIDTH), col(OFF_KWIN, NSA_KV_WIDTH),
                              col(OFF_MOBA_K, MOBA_WIDTH)], axis=1).astype(BF16)
        wc = jnp.stack([col(OFF_KCMP, NSA_KV_WIDTH), col(OFF_VCMP, NSA_KV_WIDTH)]).astype(BF16)
        wv_t = jnp.concatenate([col(OFF_VSLC, NSA_KV_WIDTH), col(OFF_VWIN, NSA_KV_WIDTH),
                                col(OFF_MOBA_V, MOBA_WIDTH)], axis=1).T.astype(BF16)
        wg_t = jnp.pad(col(OFF_GATE_LOGITS, N_GATE_LOGITS), ((0, 0), (0, LANES - N_GATE_LOGITS))).T.astype(BF16)
        w_merge = col(OFF_MERGE, 2 * D_MODEL).astype(BF16)
        qT, k_tok, cmp_tok, vT, zgT = _inproj(h, vec(mix_norm), wq_t, wk, wc, wv_t, wg_t)

        cmp_w = [[_compress_weights(ps[i], w1[i], g) for g in range(NSA_KV_GROUPS)]
                 for ps, w1 in ((cmp_pos_k, cmp_w1_k), (cmp_pos_v, cmp_w1_v))]
        pos = jnp.stack([jnp.stack([pw[0] for pw in kind]) for kind in cmp_w])
        w1c = jnp.stack([jnp.stack([pw[1] for pw in kind]) for kind in cmp_w])
        w2 = jnp.stack([cmp_w2_k[i], cmp_w2_v[i]])
        cmp_rows = cmp_tok.reshape(2, b, n_cmp_rows, NSA_CMP_STRIDE * NSA_KV_WIDTH)
        kc, kcT = _compress(cmp_rows, pos, w1c, w2, w2.transpose(0, 2, 1))

        ocmpT, negsel_n = _cmp_select(slopes2_n, qT, kc, kcT, overlap_t, b)
        nsa = dict(batch=b, n_steps=NSA_KV_GROUPS, n_pairs=1, heads_per_pair=NSA_HEADS_PER_GROUP,
                   v_heads_per_pair=1, q_block=lambda g: g, slot_of=lambda j, g: g, clamp_sum=True)
        oselT = _attention(qT, k_tok, table_sel, vT, qbias_n, negsel_n, k_block=lambda g: 0,
                           v_block=lambda g: g, name="attn_select", **nsa)
        owinT = _attention(qT, k_tok, table_win, vT, qbias_n, None, k_block=lambda g: 1,
                           v_block=lambda g: NSA_KV_GROUPS + g, window=NSA_WINDOW, name="attn_window", **nsa)
        negsel_m = _moba_select(qT, k_tok, b, q_row_block=NSA_WIDTH // LANES,
                                k_col_block=2 * NSA_KV_WIDTH // LANES)
        omobaT = _attention(qT, k_tok, table_moba, vT, qbias_m, negsel_m, batch=b, n_steps=2, n_pairs=2,
                            heads_per_pair=2, v_heads_per_pair=2, q_block=lambda g: 2 + g,
                            k_block=lambda g: 1 + g, v_block=lambda g: 1 + g, slot_of=lambda j, g: j,
                            clamp_sum=False, name="attn_moba")

        h = _mix(h, vec(mix_norm), zgT, ocmpT, oselT, owinT, omobaT, w_merge,
                 w_up_nsa[i].astype(BF16), w_up_moba[i].astype(BF16), w_out[i].astype(BF16))

        h = _ffn(h, vec(ffn2_norm), ffn2_w1[i].astype(BF16), ffn2_w3[i].astype(BF16), ffn2_w2[i].astype(BF16),
                 tail=(p[i].reshape(n, PLE_DIM), vec(ple_norm), w_ple_gate[i].astype(BF16),
                       w_ple[i].astype(BF16)),
                 final_norm=final_norm.reshape(1, -1) if i + 1 == depth else None)
    return h.reshape(b, s, d)
```

```python
import functools

import numpy as np
import jax
import jax.numpy as jnp
from jax import lax
from jax.experimental import pallas as pl
from jax.experimental.pallas import tpu as pltpu

F32 = jnp.float32
BF16 = jnp.bfloat16

D_MODEL = 1024
HEAD_DIM = 64
NSA_HEADS = 8
NSA_KV_GROUPS = 2
NSA_HEADS_PER_GROUP = NSA_HEADS // NSA_KV_GROUPS
NSA_CMP_BLOCK = 32
NSA_CMP_STRIDE = 16
NSA_CMP_HIDDEN = 128
NSA_SEL_BLOCK = 64
NSA_SEL_TOPK = 16
NSA_WINDOW = 512
MOBA_HEADS = 8
MOBA_BLOCK = 256
MOBA_TOPK = 3
D_FF = 2816
PLE_DIM = 256
RMS_EPS = 1e-6
NEG_INF = -1e30
TINY = 1e-30
FORCE_SCORE = 1e9
LOG2E = float(np.log2(np.e))
Q_SCALE = HEAD_DIM ** -0.5 * LOG2E

NSA_WIDTH = NSA_HEADS * HEAD_DIM
NSA_KV_WIDTH = NSA_KV_GROUPS * HEAD_DIM
MOBA_WIDTH = MOBA_HEADS * HEAD_DIM
N_GATE_LOGITS = 3 * NSA_HEADS
OFF_KCMP = NSA_WIDTH
OFF_VCMP = OFF_KCMP + NSA_KV_WIDTH
OFF_KSLC = OFF_VCMP + NSA_KV_WIDTH
OFF_VSLC = OFF_KSLC + NSA_KV_WIDTH
OFF_KWIN = OFF_VSLC + NSA_KV_WIDTH
OFF_VWIN = OFF_KWIN + NSA_KV_WIDTH
OFF_GATE_LOGITS = OFF_VWIN + NSA_KV_WIDTH
OFF_MOBA_Q = OFF_GATE_LOGITS + N_GATE_LOGITS
OFF_MOBA_K = OFF_MOBA_Q + MOBA_WIDTH
OFF_MOBA_V = OFF_MOBA_K + MOBA_WIDTH
OFF_MERGE = OFF_MOBA_V + MOBA_WIDTH

LANES = 128
MXU_DIM = 256
VMEM_LIMIT = 56 * 1024 * 1024

TOKEN_TILE = 512
FF_CHUNK = 256
ATTN_TQ = 256
ATTN_TK = 256
CMP_TQ = 2048
ATTN_UNROLL = 4
BIAS_ROW = 2 * HEAD_DIM
SEL_ROW = BIAS_ROW + 16
N_SLOPE_PIECES = 3
MOBA_SEL_ROWS = 16
ONES_ROWS = 16
SAFE_EXCESS = 60.0

QT_ROW_NSA, QT_ROW_MOBA = 0, NSA_WIDTH
K_COL_SLC, K_COL_WIN, K_COL_MOBA = 0, NSA_KV_WIDTH, 2 * NSA_KV_WIDTH
VT_ROW_SLC, VT_ROW_WIN, VT_ROW_MOBA = 0, NSA_KV_WIDTH, 2 * NSA_KV_WIDTH
ATTN_HEADS_PER_STEP = 4
SELF_SCORE_LAYOUT = ((QT_ROW_NSA, K_COL_SLC, NSA_HEADS_PER_GROUP), (QT_ROW_NSA, K_COL_WIN, NSA_HEADS_PER_GROUP),
                     (QT_ROW_MOBA, K_COL_MOBA, 1))
SELF_BLOCKS_PER_KIND = NSA_HEADS // ATTN_HEADS_PER_STEP
SELF_ROWS = 8

_NT = (((1,), (1,)), ((), ()))
_TN = (((0,), (0,)), ((), ()))


def _cparams(*sem):
    return pltpu.CompilerParams(dimension_semantics=sem, vmem_limit_bytes=VMEM_LIMIT)


def _resident(shape):
    nd = len(shape)
    return pl.BlockSpec(shape, lambda *_: (0,) * nd, pipeline_mode=pl.Buffered(1))


def _rms(x, g):
    ms = jnp.mean(x * x, axis=-1, keepdims=True)
    return x * lax.rsqrt(ms + RMS_EPS) * g


def _sigmoid(x):
    return 1.0 / (1.0 + jnp.exp(-x))


def _dot(a, b):
    return jnp.dot(a, b, preferred_element_type=F32)


def _dot_f32(a, b):
    return jnp.dot(a, b, preferred_element_type=F32, precision=lax.Precision.HIGHEST)


def _dot_nt(a, b, precision=None):
    return lax.dot_general(a, b, _NT, preferred_element_type=F32, precision=precision)


def _dot_tn(a, b):
    return lax.dot_general(a, b, _TN, preferred_element_type=F32)


def _ffn_kernel(*refs, tail, final):
    x_ref, g_ref, w1_ref, w3_ref, w2_ref = refs[:5]
    o_ref, acc_ref = refs[-2:]
    if tail:
        p_ref, gp_ref, wpg_ref, wp_ref = refs[5:9]
    if final:
        gf_ref = refs[-3]
    x = x_ref[...]
    xn = _rms(x, g_ref[...]).astype(BF16)
    for c in range(D_FF // FF_CHUNK):
        cols = slice(c * FF_CHUNK, (c + 1) * FF_CHUNK)
        a = _dot(xn, w1_ref[:, cols])
        b = _dot(xn, w3_ref[:, cols])
        hid = (a * _sigmoid(a) * b).astype(BF16)
        y = _dot(hid, w2_ref[cols, :])
        if c == 0:
            acc_ref[...] = y
        else:
            acc_ref[...] += y
    h = x + 0.5 * acc_ref[...]
    if tail:
        gate = _sigmoid(_dot(_rms(h, gp_ref[...]).astype(BF16), wpg_ref[...]))
        h = h + gate * _dot(p_ref[...].astype(BF16), wp_ref[...])
    if final:
        h = _rms(h, gf_ref[...])
    o_ref[...] = h


def _ffn(x, g, w1, w3, w2, tail=None, final_norm=None):
    n = x.shape[0]
    row = lambda w: pl.BlockSpec((TOKEN_TILE, w), lambda i: (i, 0))
    args = [x, g, w1, w3, w2]
    specs = [row(D_MODEL), _resident((1, D_MODEL)), _resident(w1.shape), _resident(w3.shape),
             _resident(w2.shape)]
    if tail is not None:
        p, gp, wpg, wp = tail
        args += [p, gp, wpg, wp]
        specs += [row(PLE_DIM), _resident((1, D_MODEL)), _resident(wpg.shape), _resident(wp.shape)]
    if final_norm is not None:
        args.append(final_norm)
        specs.append(_resident((1, D_MODEL)))
    return pl.pallas_call(
        functools.partial(_ffn_kernel, tail=tail is not None, final=final_norm is not None),
        grid=(n // TOKEN_TILE,),
        in_specs=specs,
        out_specs=row(D_MODEL),
        out_shape=jax.ShapeDtypeStruct((n, D_MODEL), F32),
        scratch_shapes=[pltpu.VMEM((TOKEN_TILE, D_MODEL), F32)],
        compiler_params=_cparams("parallel"),
        name="ffn_tail" if tail is not None else "ffn",
    )(*args)


def _inproj_kernel(h_ref, g_ref, wq_ref, wk_ref, wkt_ref, wc_ref, wv_ref, wg_ref,
                   qT_ref, k_ref, cmp_ref, vT_ref, zgT_ref, self_ref):
    u = _rms(h_ref[...], g_ref[...]).astype(BF16)
    q_t = _dot_nt(wq_ref[...], u) * Q_SCALE
    qT_ref[...] = q_t.astype(BF16)
    k_ref[...] = _dot(u, wk_ref[...]).astype(BF16)
    for kind in range(2):
        cmp_ref[kind] = _dot(u, wc_ref[kind]).astype(BF16)
    vT_ref[...] = _dot_nt(wv_ref[...], u).astype(BF16)
    zgT_ref[...] = _dot_nt(wg_ref[...], u)
    k_t = _dot_nt(wkt_ref[...], u)
    hd = HEAD_DIM
    rows_per_block = self_ref.shape[1]
    for kind, (q_row, k_row, heads_per_k) in enumerate(SELF_SCORE_LAYOUT):
        for blk in range(SELF_BLOCKS_PER_KIND):
            rows = []
            for j in range(ATTN_HEADS_PER_STEP):
                h = blk * ATTN_HEADS_PER_STEP + j
                q = q_t[q_row + h * hd:q_row + (h + 1) * hd]
                kk = k_t[k_row + (h // heads_per_k) * hd:k_row + (h // heads_per_k + 1) * hd]
                rows.append(jnp.sum(q * kk, axis=0, keepdims=True))
            rows.append(jnp.zeros((rows_per_block - ATTN_HEADS_PER_STEP, q_t.shape[1]), F32))
            self_ref[kind * SELF_BLOCKS_PER_KIND + blk] = jnp.concatenate(rows, axis=0)


def _inproj(h, g, wq_t, wk, wk_t, wc, wv_t, wg_t):
    n = h.shape[0]
    tm = TOKEN_TILE
    rows = lambda width: pl.BlockSpec((tm, width), lambda i: (i, 0))
    cols = lambda height: pl.BlockSpec((height, tm), lambda i: (0, i))
    n_self = len(SELF_SCORE_LAYOUT) * SELF_BLOCKS_PER_KIND
    return pl.pallas_call(
        _inproj_kernel,
        grid=(n // tm,),
        in_specs=[rows(D_MODEL), _resident((1, D_MODEL)), _resident(wq_t.shape), _resident(wk.shape),
                  _resident(wk_t.shape), _resident(wc.shape), _resident(wv_t.shape), _resident(wg_t.shape)],
        out_specs=[cols(wq_t.shape[0]), rows(wk.shape[1]),
                   pl.BlockSpec((2, tm, NSA_KV_WIDTH), lambda i: (0, i, 0)),
                   cols(wv_t.shape[0]), cols(LANES),
                   pl.BlockSpec((n_self, SELF_ROWS, tm), lambda i: (0, 0, i))],
        out_shape=[jax.ShapeDtypeStruct((wq_t.shape[0], n), BF16),
                   jax.ShapeDtypeStruct((n, wk.shape[1]), BF16),
                   jax.ShapeDtypeStruct((2, n, NSA_KV_WIDTH), BF16),
                   jax.ShapeDtypeStruct((wv_t.shape[0], n), BF16),
                   jax.ShapeDtypeStruct((LANES, n), F32),
                   jax.ShapeDtypeStruct((n_self, SELF_ROWS, n), F32)],
        compiler_params=_cparams("parallel"),
        name="inproj",
    )(h, g, wq_t, wk, wk_t, wc, wv_t, wg_t)


def _split3(x):
    hi = x.astype(BF16)
    r = x - hi.astype(F32)
    mid = r.astype(BF16)
    return hi, mid, (r - mid.astype(F32)).astype(BF16)


def _compress_kernel(rows_ref, pos_ref, w1_ref, w2_ref, w2t_ref, o_ref, ot_ref):
    rows = rows_ref[...].astype(F32)
    first = _dot((rows + pos_ref[0]).astype(BF16), w1_ref[0])
    second = _dot((rows + pos_ref[1]).astype(BF16), w1_ref[1])
    n_rows = rows.shape[0]
    pre = first + pltpu.roll(second, n_rows - 1, 0)
    c0 = float(np.sqrt(2.0 / np.pi))
    hid = pre * (0.5 * (1.0 + jnp.tanh(c0 * (pre + 0.044715 * (pre * pre * pre)))))
    out = _dot_f32(hid, w2_ref[...])
    o_ref[...] = jnp.concatenate(_split3(out) + (jnp.zeros(out.shape, BF16),), axis=1)
    ot_ref[...] = _dot_nt(w2t_ref[...], hid, lax.Precision.HIGHEST).astype(BF16)


def _compress(cmp_rows, pos, w1, w2, w2t):
    _, b, n_rows, width = cmp_rows.shape
    g = NSA_KV_GROUPS
    per_kind_group = lambda *tail: pl.BlockSpec((None, None) + tail, lambda i, k, j: (k, j) + (0,) * len(tail))
    per_kind = lambda *tail: pl.BlockSpec((None,) + tail, lambda i, k, j: (k,) + (0,) * len(tail))
    return pl.pallas_call(
        _compress_kernel,
        grid=(b, 2, g),
        in_specs=[
            pl.BlockSpec((None, None, n_rows, width), lambda i, k, j: (k, i, 0, 0)),
            per_kind_group(2, 1, width),
            per_kind_group(2, width, NSA_CMP_HIDDEN),
            per_kind(NSA_CMP_HIDDEN, HEAD_DIM),
            per_kind(HEAD_DIM, NSA_CMP_HIDDEN),
        ],
        out_specs=[
            pl.BlockSpec((None, None, n_rows, 4 * HEAD_DIM), lambda i, k, j: (i, k * g + j, 0, 0)),
            pl.BlockSpec((None, None, HEAD_DIM, n_rows), lambda i, k, j: (i, k * g + j, 0, 0)),
        ],
        out_shape=[jax.ShapeDtypeStruct((b, 2 * g, n_rows, 4 * HEAD_DIM), BF16),
                   jax.ShapeDtypeStruct((b, 2 * g, HEAD_DIM, n_rows), BF16)],
        compiler_params=_cparams("parallel", "parallel", "parallel"),
        name="compress",
    )(cmp_rows, pos, w1, w2, w2t)


def _count_rank(score, idx, n_candidates):
    rank = jnp.zeros(score.shape, jnp.int32)
    for j in range(n_candidates):
        row = score[j:j + 1, :]
        ahead = (row > score) | ((row == score) & (idx > j))
        rank = rank + ahead.astype(jnp.int32)
    return rank


def _cmp_select_kernel(slopes_ref, qT_ref, kc_ref, vcT_ref, ovT_ref, oT_ref, negsel_ref):
    g = pl.program_id(1)
    qi = pl.program_id(2)
    tq = qT_ref.shape[1]
    n_cmp = kc_ref.shape[0]
    n_sel = ovT_ref.shape[0]
    hd = HEAD_DIM
    t = qi * tq + lax.broadcasted_iota(jnp.int32, (n_cmp, tq), 1)
    blk_c = lax.broadcasted_iota(jnp.int32, (n_cmp, tq), 0)
    dist = t - (blk_c * NSA_CMP_STRIDE + (NSA_CMP_BLOCK - 1))
    visible = dist >= 0
    dist_f = dist.astype(F32)
    kc3 = kc_ref[...]
    vcT = vcT_ref[...]
    p_sum = jnp.zeros((n_cmp, tq), F32)
    for hh in range(NSA_HEADS_PER_GROUP):
        slope2 = slopes_ref[g * NSA_HEADS_PER_GROUP + hh]
        q = qT_ref[hh * hd:(hh + 1) * hd, :]
        s = _dot(kc3, jnp.concatenate([q, q, q, jnp.zeros_like(q)], axis=0)) - slope2 * dist_f
        s = jnp.where(visible, s, NEG_INF)
        m = jnp.max(s, axis=0, keepdims=True)
        e = jnp.exp2(s - m)
        any_visible = m > 0.5 * NEG_INF
        p = e * jnp.where(any_visible, 1.0 / jnp.maximum(jnp.sum(e, axis=0, keepdims=True), TINY), 0.0)
        p_sum = p_sum + p
        oT_ref[hh * hd:(hh + 1) * hd, :] = _dot(vcT, p.astype(BF16))
    imp = _dot(ovT_ref[...], jnp.concatenate(_split3(p_sum), axis=0))
    ts = qi * tq + lax.broadcasted_iota(jnp.int32, (n_sel, tq), 1)
    blk = lax.broadcasted_iota(jnp.int32, (n_sel, tq), 0)
    cur = ts // NSA_SEL_BLOCK
    causal = blk * NSA_SEL_BLOCK <= ts
    forced = (blk == 0) | (blk == cur) | (blk == cur - 1)
    imp = jnp.where(causal, jnp.where(forced, FORCE_SCORE, imp), NEG_INF)
    rank = _count_rank(imp, blk, n_sel)
    chosen = (rank < min(NSA_SEL_TOPK, n_sel)) & causal
    negsel_ref[...] = jnp.where(chosen, 0.0, NEG_INF).astype(BF16)


def _cmp_select(slopes2, qT, kc, vcT, overlap_t, batch):
    n = qT.shape[1]
    tq = min(CMP_TQ, n // batch)
    nq = n // batch // tq
    rows = NSA_HEADS_PER_GROUP * HEAD_DIM
    n_cmp = kc.shape[2]
    n_sel = overlap_t.shape[0]
    return pl.pallas_call(
        _cmp_select_kernel,
        grid=(batch, NSA_KV_GROUPS, nq),
        in_specs=[
            pl.BlockSpec(memory_space=pltpu.SMEM),
            pl.BlockSpec((rows, tq), lambda i, g, q: (g, i * nq + q)),
            pl.BlockSpec((None, None, n_cmp, 4 * HEAD_DIM), lambda i, g, q: (i, g, 0, 0)),
            pl.BlockSpec((None, None, HEAD_DIM, n_cmp), lambda i, g, q: (i, NSA_KV_GROUPS + g, 0, 0)),
            pl.BlockSpec((n_sel, 3 * n_cmp), lambda i, g, q: (0, 0)),
        ],
        out_specs=[
            pl.BlockSpec((rows, tq), lambda i, g, q: (g, i * nq + q)),
            pl.BlockSpec((None, n_sel, tq), lambda i, g, q: (g, 0, i * nq + q)),
        ],
        out_shape=[jax.ShapeDtypeStruct((NSA_WIDTH, n), F32),
                   jax.ShapeDtypeStruct((NSA_KV_GROUPS, n_sel, n), BF16)],
        compiler_params=_cparams("parallel", "parallel", "parallel"),
        name="cmp_select",
    )(slopes2, qT, kc, vcT, overlap_t)


def _moba_select_kernel(qT_ref, k_ref, negsel_ref):
    s = k_ref.shape[0]
    nb = s // MOBA_BLOCK
    hd = HEAD_DIM
    kmean = jnp.mean(k_ref[...].astype(F32).reshape(nb, MOBA_BLOCK, 2 * hd), axis=1)
    t = lax.broadcasted_iota(jnp.int32, (nb, s), 1)
    blk = lax.broadcasted_iota(jnp.int32, (nb, s), 0)
    cur = t // MOBA_BLOCK
    past = blk < cur
    for j in range(2):
        gs = _dot_f32(kmean[:, j * hd:(j + 1) * hd], qT_ref[j * hd:(j + 1) * hd, :].astype(F32))
        gs = jnp.where(past, gs, NEG_INF)
        rank = _count_rank(gs, blk, nb)
        chosen = ((rank < min(MOBA_TOPK, nb - 1)) & past) | (blk == cur)
        negsel_ref[j, 0:nb, :] = jnp.where(chosen, 0.0, NEG_INF).astype(BF16)
        negsel_ref[j, nb:, :] = jnp.zeros((MOBA_SEL_ROWS - nb, s), BF16)


def _moba_select(qT, k_tok, batch, q_row_block, k_col_block):
    n = qT.shape[1]
    s = n // batch
    pair = 2 * HEAD_DIM
    return pl.pallas_call(
        _moba_select_kernel,
        grid=(batch, MOBA_HEADS // 2),
        in_specs=[pl.BlockSpec((pair, s), lambda i, j: (q_row_block + j, i)),
                  pl.BlockSpec((s, pair), lambda i, j: (i, k_col_block + j))],
        out_specs=pl.BlockSpec((2, MOBA_SEL_ROWS, s), lambda i, j: (j, 0, i)),
        out_shape=jax.ShapeDtypeStruct((MOBA_HEADS, MOBA_SEL_ROWS, n), BF16),
        compiler_params=_cparams("parallel", "parallel"),
        name="moba_select",
    )(qT, k_tok)


def _attn_schedule(nq, window_chunks):
    def padded(pairs, start):
        n_pad = (-(start + len(pairs) - 1)) % ATTN_UNROLL
        return pairs + [(0, 0, 0)] * n_pad

    diag = [(q, q, 1) for q in range(nq)]
    groups = [("diag", padded(diag, 0))]
    if window_chunks is None:
        full = [(q, c, 1) for q in range(nq) for c in range(q)]
        groups.append((None, full))
    else:
        assert window_chunks == 2
        done = len(groups[0][1])
        lower = padded([(q, q - 2, 1) for q in range(2, nq)], done)
        groups.append(("lower", lower))
        groups.append(("middle", [(q, q - 1, 1) for q in range(1, nq)]))
    pairs, kinds = [], []
    for kind, group in groups:
        pairs += group
        kinds += [kind] * len(group)
    n_pad = (-(len(pairs) - 1)) % ATTN_UNROLL
    pairs += [(0, 0, 0)] * n_pad
    kinds += [kinds[-1] if kinds[-1] is not None else "diag"] * n_pad
    body_kinds = []
    for j in range((len(pairs) - 1) // ATTN_UNROLL):
        ks = set(kinds[1 + j * ATTN_UNROLL:1 + (j + 1) * ATTN_UNROLL])
        assert len(ks) == 1, "a loop body must issue scores of one mask kind"
        body_kinds.append(ks.pop())
    table = np.asarray(pairs, np.int32)
    return table[:, 0], table[:, 1], table[:, 2], tuple(body_kinds)


def _attn_kernel(*refs, n_pairs, heads_per_pair, heads_per_v, has_sel, slot_of, body_kinds, clamp_sum):
    n_in = 11 if has_sel else 10
    (qtab_ref, ctab_ref, vtab_ref, slopes_ref, qT_ref, k_ref, ktab_ref, vT_ref, qbias_ref,
     sself_ref) = refs[:10]
    negsel_ref = refs[10] if has_sel else None
    oT_ref = refs[n_in]
    qaug_ref, vaug_ref, m_ref, acc_ref, excess_ref = refs[n_in + 1:n_in + 6]
    s_bufs = refs[n_in + 6:n_in + 8]
    cmax_bufs = refs[n_in + 8:n_in + 10]
    alpha_bufs = refs[n_in + 10:n_in + 12]
    p_bufs = refs[n_in + 12:n_in + 14]
    grp = pl.program_id(1)
    tq, tk, hd = ATTN_TQ, ATTN_TK, HEAD_DIM
    nq = qaug_ref.shape[0]
    hpp = heads_per_pair
    nh = n_pairs * hpp
    pcol = hpp * tq
    ncol = n_pairs * pcol
    n_vgroups = nh // heads_per_v
    vcol = heads_per_v * tq
    arows = hd + ONES_ROWS
    n_steps = qtab_ref.shape[0]
    unroll = ATTN_UNROLL

    for vg in range(n_vgroups):
        vaug_ref[vg * arows:vg * arows + hd, :] = vT_ref[vg * hd:(vg + 1) * hd, :]
        vaug_ref[vg * arows + hd:(vg + 1) * arows, :] = jnp.ones((ONES_ROWS, vaug_ref.shape[1]), BF16)

    for qi in range(nq):
        toks = slice(qi * tq, (qi + 1) * tq)
        for i in range(nh):
            cols = slice(i * tq, (i + 1) * tq)
            slot = slot_of(i % hpp, grp)
            q = qT_ref[i * hd:(i + 1) * hd, toks]
            zero = jnp.zeros_like(q)
            qaug_ref[qi, 0:hd, cols] = jnp.where(slot == 0, q, zero)
            qaug_ref[qi, hd:2 * hd, cols] = jnp.where(slot == 1, q, zero)
            qaug_ref[qi, BIAS_ROW:SEL_ROW, cols] = jnp.concatenate([qbias_ref[i]] * (tq // LANES), axis=1)
            fill = SEL_ROW
            if has_sel:
                n_rows = negsel_ref.shape[1]
                qaug_ref[qi, SEL_ROW:SEL_ROW + n_rows, cols] = negsel_ref[i % negsel_ref.shape[0], :, toks]
                fill = SEL_ROW + n_rows
            qaug_ref[qi, fill:, cols] = jnp.zeros((MXU_DIM - fill, tq), BF16)

    key_i = lax.broadcasted_iota(jnp.int32, (tk, ncol), 0)
    qry_row = lax.broadcasted_iota(jnp.int32, (1, ncol), 1) % tq
    big = jnp.int32(2 ** 30)

    def visible(kind, valid):
        if kind == "diag":
            return key_i <= jnp.where(valid, qry_row, -1)
        if kind == "lower":
            return key_i > jnp.where(valid, qry_row, big)
        if kind == "middle":
            return key_i >= jnp.where(valid, 0, big)
        return None

    def key_slice(t):
        return pl.ds(pl.multiple_of(ctab_ref[t] * tk, tk), tk)

    def scores(t, buf, kind):
        qi = qtab_ref[t]
        keys = key_slice(t)
        mask = visible(kind, vtab_ref[t] > 0)
        for pr in range(n_pairs):
            kaug = jnp.concatenate([k_ref[keys, pr * LANES:(pr + 1) * LANES], ktab_ref[keys, :]], axis=1)
            cols = slice(pr * pcol, (pr + 1) * pcol)
            s = _dot(kaug, qaug_ref[qi, :, cols])
            if mask is not None:
                s = jnp.where(mask[:, cols], s, NEG_INF)
            s_bufs[buf][:, cols] = s
            cmax_bufs[buf][:, cols] = jnp.max(s, axis=0, keepdims=True)

    def softmax(t, buf):
        qi = qtab_ref[t]
        m = m_ref[qi]
        m_new = jnp.maximum(m, cmax_bufs[buf][...])
        m_ref[qi] = m_new
        alpha_bufs[buf][...] = jnp.exp2(m - m_new)
        p_bufs[buf][...] = jnp.exp2(s_bufs[buf][...] - m_new).astype(BF16)

    def weighted(t, buf, rescale):
        qi = qtab_ref[t]
        keys = key_slice(t)
        for vg in range(n_vgroups):
            cols = slice(vg * vcol, (vg + 1) * vcol)
            pv = _dot(vaug_ref[vg * arows:(vg + 1) * arows, keys], p_bufs[buf][:, cols])
            old = acc_ref[qi, :, cols]
            acc_ref[qi, :, cols] = (alpha_bufs[buf][:, cols] * old if rescale else old) + pv

    def fast_scores(t, buf, kind):
        qi = qtab_ref[t]
        keys = key_slice(t)
        mask = visible(kind, vtab_ref[t] > 0)
        ref = m_ref[qi]
        for pr in range(n_pairs):
            kaug = jnp.concatenate([k_ref[keys, pr * LANES:(pr + 1) * LANES], ktab_ref[keys, :]], axis=1)
            cols = slice(pr * pcol, (pr + 1) * pcol)
            s = _dot(kaug, qaug_ref[qi, :, cols])
            if mask is not None:
                s = jnp.where(mask[:, cols], s, NEG_INF)
            excess_ref[:, cols] = jnp.maximum(excess_ref[:, cols], jnp.max(s, axis=0, keepdims=True) - ref[:, cols])
            p_bufs[buf][:, cols] = jnp.exp2(s - ref[:, cols]).astype(BF16)

    def pipeline(step):
        first = 0
        for kind in sorted(set(body_kinds), key=body_kinds.index):
            count = body_kinds.count(kind)
            assert body_kinds[first:first + count] == (kind,) * count

            @pl.loop(first, first + count)
            def _(j, kind=kind):
                for u in range(unroll):
                    step(j * unroll + u, u, kind)

            first += count

    last = n_steps - 1

    for qi in range(nq):
        pos = (qi * tq + lax.broadcasted_iota(jnp.int32, (1, tq), 1)).astype(F32)
        for i in range(nh):
            m_ref[qi, :, i * tq:(i + 1) * tq] = (
                sself_ref[i:i + 1, qi * tq:(qi + 1) * tq] + slopes_ref[grp * nh + i] * pos)
    acc_ref[...] = jnp.zeros(acc_ref.shape, F32)
    excess_ref[...] = jnp.zeros((1, ncol), F32)
    fast_scores(0, 0, "diag")

    def fast_step(t, u, kind):
        fast_scores(t + 1, (u + 1) % 2, kind)
        weighted(t, u % 2, rescale=False)

    pipeline(fast_step)
    weighted(last, last % 2, rescale=False)

    @pl.when(jnp.max(excess_ref[...]) > SAFE_EXCESS)
    def _():
        m_ref[...] = jnp.full(m_ref.shape, NEG_INF, F32)
        acc_ref[...] = jnp.zeros(acc_ref.shape, F32)
        p_bufs[1][...] = jnp.zeros(p_bufs[1].shape, BF16)
        alpha_bufs[1][...] = jnp.ones((1, ncol), F32)
        scores(0, 0, "diag")

        def safe_step(t, u, kind):
            scores(t + 1, (u + 1) % 2, kind)
            weighted(jnp.maximum(t - 1, 0), (u + 1) % 2, rescale=True)
            softmax(t, u % 2)

        pipeline(safe_step)
        weighted(last - 1, (last - 1) % 2, rescale=True)
        softmax(last, last % 2)
        weighted(last, last % 2, rescale=True)

    for qi in range(nq):
        for i in range(nh):
            cols = slice(i * tq, (i + 1) * tq)
            l = acc_ref[qi, hd:hd + 1, cols]
            oT_ref[i * hd:(i + 1) * hd, qi * tq:(qi + 1) * tq] = (
                acc_ref[qi, 0:hd, cols] / (jnp.maximum(l, TINY) if clamp_sum else l))


def _attention(slopes2, qT, k_tok, key_table, vT, qbias, self_scores, negsel, *, batch, n_steps, n_pairs,
               heads_per_pair, heads_per_v, q_block, k_block, v_block, self_block, slot_of, window=None,
               clamp_sum, name):
    n = qT.shape[1]
    s = n // batch
    tq, tk, hd = ATTN_TQ, ATTN_TK, HEAD_DIM
    assert tq == tk and s % tq == 0
    nq = s // tq
    nh = n_pairs * heads_per_pair
    assert nh <= self_scores.shape[1] and nh % heads_per_v == 0
    ncol = nh * tq
    arows = hd + ONES_ROWS
    n_vgroups = nh // heads_per_v
    if window is not None:
        assert window % tk == 0
    q_tab, c_tab, v_tab, body_kinds = _attn_schedule(nq, None if window is None else window // tk)
    smem = pl.BlockSpec(memory_space=pltpu.SMEM)
    in_specs = [
        smem, smem, smem, smem,
        pl.BlockSpec((nh * hd, s), lambda b, g: (q_block(g), b)),
        pl.BlockSpec((s, n_pairs * LANES), lambda b, g: (b, k_block(g))),
        pl.BlockSpec((s, LANES), lambda b, g: (0, 0)),
        pl.BlockSpec((n_vgroups * hd, s), lambda b, g: (v_block(g), b)),
        pl.BlockSpec((nh, 16, LANES), lambda b, g: (g, 0, 0)),
        pl.BlockSpec((None, self_scores.shape[1], s), lambda b, g: (self_block(g), 0, b)),
    ]
    args = [jnp.asarray(q_tab), jnp.asarray(c_tab), jnp.asarray(v_tab), slopes2, qT, k_tok, key_table, vT,
            qbias, self_scores]
    if negsel is not None:
        per_step = negsel.shape[0] // n_steps
        in_specs.append(pl.BlockSpec((per_step, negsel.shape[1], s), lambda b, g: (g, 0, b)))
        args.append(negsel)
    vec = lambda: pltpu.VMEM((1, ncol), F32)
    return pl.pallas_call(
        functools.partial(_attn_kernel, n_pairs=n_pairs, heads_per_pair=heads_per_pair,
                          heads_per_v=heads_per_v, has_sel=negsel is not None, slot_of=slot_of,
                          body_kinds=body_kinds, clamp_sum=clamp_sum),
        grid=(batch, n_steps),
        in_specs=in_specs,
        out_specs=pl.BlockSpec((nh * hd, s), lambda b, g: (g, b)),
        out_shape=jax.ShapeDtypeStruct((n_steps * nh * hd, n), F32),
        scratch_shapes=[pltpu.VMEM((nq, MXU_DIM, ncol), BF16),
                        pltpu.VMEM((n_vgroups * arows, s), BF16),
                        pltpu.VMEM((nq, 1, ncol), F32),
                        pltpu.VMEM((nq, arows, ncol), F32),
                        vec(),
                        pltpu.VMEM((tk, ncol), F32), pltpu.VMEM((tk, ncol), F32),
                        vec(), vec(), vec(), vec(),
                        pltpu.VMEM((tk, ncol), BF16), pltpu.VMEM((tk, ncol), BF16)],
        compiler_params=_cparams("parallel", "parallel"),
        name=name,
    )(*args)


def _mix_kernel(h_ref, g_ref, zgT_ref, ocmp_ref, osel_ref, owin_ref, omoba_ref,
                wmerge_ref, wn_ref, wm_ref, wo_ref, o_ref, gate_ref, onsa_ref):
    h = h_ref[...]
    u = _rms(h, g_ref[...]).astype(BF16)
    gate_ref[...] = _sigmoid(zgT_ref[0:gate_ref.shape[0], :])
    hd = HEAD_DIM
    for hh in range(NSA_HEADS):
        rows = slice(hh * hd, (hh + 1) * hd)
        o = None
        for r, branch in enumerate((ocmp_ref, osel_ref, owin_ref)):
            gate = gate_ref[3 * hh + r:3 * hh + r + 1, :]
            term = gate * branch[rows, :]
            o = term if o is None else o + term
        onsa_ref[rows, :] = o.astype(BF16)
    y_n = _dot_tn(onsa_ref[...], wn_ref[...])
    y_m = _dot_tn(omoba_ref[...].astype(BF16), wm_ref[...])
    mixed = (_sigmoid(_dot(u, wmerge_ref[:, :D_MODEL])) * y_n
             + _sigmoid(_dot(u, wmerge_ref[:, D_MODEL:])) * y_m)
    o_ref[...] = h + _dot(mixed.astype(BF16), wo_ref[...])


def _mix(h, g, zgT, ocmpT, oselT, owinT, omobaT, wmerge, wn, wm, wo):
    n = h.shape[0]
    tm = TOKEN_TILE
    rows = lambda width: pl.BlockSpec((tm, width), lambda i: (i, 0))
    cols = lambda height: pl.BlockSpec((height, tm), lambda i: (0, i))
    gate_rows = -(-N_GATE_LOGITS // 8) * 8
    return pl.pallas_call(
        _mix_kernel,
        grid=(n // tm,),
        in_specs=[rows(D_MODEL), _resident((1, D_MODEL)), cols(LANES), cols(NSA_WIDTH), cols(NSA_WIDTH),
                  cols(NSA_WIDTH), cols(MOBA_WIDTH), _resident(wmerge.shape),
                  _resident(wn.shape), _resident(wm.shape), _resident(wo.shape)],
        out_specs=rows(D_MODEL),
        out_shape=jax.ShapeDtypeStruct((n, D_MODEL), F32),
        scratch_shapes=[pltpu.VMEM((gate_rows, tm), F32), pltpu.VMEM((NSA_WIDTH, tm), BF16)],
        compiler_params=_cparams("parallel"),
        name="mix",
    )(h, g, zgT, ocmpT, oselT, owinT, omobaT, wmerge, wn, wm, wo)


def _key_table(s, block):
    pos = np.arange(s)
    tab = np.zeros((s, LANES), np.float32)
    tab[:, 0:N_SLOPE_PIECES] = ((pos // LANES) * LANES)[:, None]
    tab[:, N_SLOPE_PIECES:2 * N_SLOPE_PIECES] = (pos % LANES)[:, None]
    if block is not None:
        tab[pos, (SEL_ROW - BIAS_ROW) + pos // block] = 1.0
    return jnp.asarray(tab, BF16)


def _slope_rows(slopes2):
    s1 = slopes2.astype(BF16)
    r1 = slopes2 - s1.astype(F32)
    s2 = r1.astype(BF16)
    s3 = (r1 - s2.astype(F32)).astype(BF16)
    rows = jnp.stack([s1, s2, s3, s1, s2, s3], axis=1)
    rows = jnp.pad(rows, ((0, 0), (0, 16 - 2 * N_SLOPE_PIECES)))
    return jnp.broadcast_to(rows[:, :, None], rows.shape + (LANES,))


def _compress_weights(pos, w1, group):
    st, hd, gw = NSA_CMP_STRIDE, HEAD_DIM, NSA_KV_WIDTH
    lo = group * hd
    w = jnp.zeros((2, st, gw, NSA_CMP_HIDDEN), F32).at[:, :, lo:lo + hd, :].set(
        w1.reshape(2, st, hd, NSA_CMP_HIDDEN))
    ps = jnp.zeros((2, st, gw), F32).at[:, :, lo:lo + hd].set(pos.reshape(2, st, hd))
    return ps.reshape(2, 1, st * gw), w.reshape(2, st * gw, NSA_CMP_HIDDEN).astype(BF16)


def kernel(x, p, ffn1_norm, ffn1_w1, ffn1_w3, ffn1_w2, mix_norm, w_in, cmp_pos_k, cmp_w1_k, cmp_w2_k, cmp_pos_v, cmp_w1_v, cmp_w2_v, w_up_nsa, w_up_moba, w_out, ffn2_norm, ffn2_w1, ffn2_w3, ffn2_w2, ple_norm, w_ple_gate, w_ple, final_norm):
    b, s, d = x.shape
    n = b * s
    depth = p.shape[0]
    n_all = NSA_HEADS + MOBA_HEADS
    slopes = jnp.exp2(-8.0 * (jnp.arange(n_all, dtype=F32) + 1.0) / n_all)
    slopes2_n, slopes2_m = slopes[0::2] * LOG2E, slopes[1::2] * LOG2E
    qbias_n, qbias_m = _slope_rows(slopes2_n), _slope_rows(slopes2_m)

    n_cmp_rows = s // NSA_CMP_STRIDE
    n_cmp = (s - NSA_CMP_BLOCK) // NSA_CMP_STRIDE + 1
    n_sel = s // NSA_SEL_BLOCK
    c_start = np.arange(n_cmp_rows) * NSA_CMP_STRIDE
    s_start = np.arange(n_sel) * NSA_SEL_BLOCK
    overlap_t = ((c_start[None, :] <= s_start[:, None] + NSA_SEL_BLOCK - 1)
                 & (c_start[None, :] + NSA_CMP_BLOCK - 1 >= s_start[:, None])
                 & (np.arange(n_cmp_rows)[None, :] < n_cmp))
    overlap_t = jnp.asarray(np.tile(overlap_t, (1, 3)), BF16)
    table_sel = _key_table(s, NSA_SEL_BLOCK)
    table_win = _key_table(s, None)
    table_moba = _key_table(s, MOBA_BLOCK)

    h = x.reshape(n, d)
    for i in range(depth):
        vec = lambda a: a[i].reshape(1, -1)
        h = _ffn(h, vec(ffn1_norm), ffn1_w1[i].astype(BF16), ffn1_w3[i].astype(BF16),
                 ffn1_w2[i].astype(BF16))

        wi = w_in[i]
        col = lambda lo, width: wi[:, lo:lo + width]
        wq_t = jnp.concatenate([col(0, NSA_WIDTH), col(OFF_MOBA_Q, MOBA_WIDTH)], axis=1).T.astype(BF16)
        wk = jnp.concatenate([col(OFF_KSLC, NSA_KV_WIDTH), col(OFF_KWIN, NSA_KV_WIDTH),
                              col(OFF_MOBA_K, MOBA_WIDTH)], axis=1).astype(BF16)
        wc = jnp.stack([col(OFF_KCMP, NSA_KV_WIDTH), col(OFF_VCMP, NSA_KV_WIDTH)]).astype(BF16)
        wv_t = jnp.concatenate([col(OFF_VSLC, NSA_KV_WIDTH), col(OFF_VWIN, NSA_KV_WIDTH),
                                col(OFF_MOBA_V, MOBA_WIDTH)], axis=1).T.astype(BF16)
        wg_t = jnp.pad(col(OFF_GATE_LOGITS, N_GATE_LOGITS), ((0, 0), (0, LANES - N_GATE_LOGITS))).T.astype(BF16)
        w_merge = col(OFF_MERGE, 2 * D_MODEL).astype(BF16)
        wk_t = wk.T
        qT, k_tok, cmp_tok, vT, zgT, self_scores = _inproj(h, vec(mix_norm), wq_t, wk, wk_t, wc, wv_t, wg_t)

        cmp_w = [[_compress_weights(ps[i], w1[i], g) for g in range(NSA_KV_GROUPS)]
                 for ps, w1 in ((cmp_pos_k, cmp_w1_k), (cmp_pos_v, cmp_w1_v))]
        pos = jnp.stack([jnp.stack([pw[0] for pw in kind]) for kind in cmp_w])
        w1c = jnp.stack([jnp.stack([pw[1] for pw in kind]) for kind in cmp_w])
        w2 = jnp.stack([cmp_w2_k[i], cmp_w2_v[i]])
        cmp_rows = cmp_tok.reshape(2, b, n_cmp_rows, NSA_CMP_STRIDE * NSA_KV_WIDTH)
        kc, kcT = _compress(cmp_rows, pos, w1c, w2, w2.transpose(0, 2, 1))

        ocmpT, negsel_n = _cmp_select(slopes2_n, qT, kc, kcT, overlap_t, b)
        hs = ATTN_HEADS_PER_STEP
        q_rows, pair = hs * HEAD_DIM, 2 * HEAD_DIM
        nsa = dict(batch=b, n_steps=NSA_KV_GROUPS, n_pairs=1, heads_per_pair=hs, heads_per_v=hs,
                   q_block=lambda g: QT_ROW_NSA // q_rows + g, slot_of=lambda j, g: g, clamp_sum=True)
        oselT = _attention(slopes2_n, qT, k_tok, table_sel, vT, qbias_n, self_scores, negsel_n,
                           k_block=lambda g: K_COL_SLC // pair, v_block=lambda g: VT_ROW_SLC // HEAD_DIM + g,
                           self_block=lambda g: g, name="attn_select", **nsa)
        owinT = _attention(slopes2_n, qT, k_tok, table_win, vT, qbias_n, self_scores, None,
                           k_block=lambda g: K_COL_WIN // pair, v_block=lambda g: VT_ROW_WIN // HEAD_DIM + g,
                           self_block=lambda g: SELF_BLOCKS_PER_KIND + g, window=NSA_WINDOW,
                           name="attn_window", **nsa)
        negsel_m = _moba_select(qT, k_tok, b, q_row_block=QT_ROW_MOBA // pair, k_col_block=K_COL_MOBA // pair)
        omobaT = _attention(slopes2_m, qT, k_tok, table_moba, vT, qbias_m, self_scores, negsel_m, batch=b,
                            n_steps=MOBA_HEADS // hs, n_pairs=hs // 2, heads_per_pair=2, heads_per_v=1,
                            q_block=lambda g: QT_ROW_MOBA // q_rows + g,
                            k_block=lambda g: K_COL_MOBA // (hs // 2 * pair) + g,
                            v_block=lambda g: VT_ROW_MOBA // q_rows + g,
                            self_block=lambda g: 2 * SELF_BLOCKS_PER_KIND + g, slot_of=lambda j, g: j,
                            clamp_sum=False, name="attn_moba")

        h = _mix(h, vec(mix_norm), zgT, ocmpT, oselT, owinT, omobaT, w_merge,
                 w_up_nsa[i].astype(BF16), w_up_moba[i].astype(BF16), w_out[i].astype(BF16))

        h = _ffn(h, vec(ffn2_norm), ffn2_w1[i].astype(BF16), ffn2_w3[i].astype(BF16), ffn2_w2[i].astype(BF16),
                 tail=(p[i].reshape(n, PLE_DIM), vec(ple_norm), w_ple_gate[i].astype(BF16),
                       w_ple[i].astype(BF16)),
                 final_norm=final_norm.reshape(1, -1) if i + 1 == depth else None)
    return h.reshape(b, s, d)
```

```python
import functools

import numpy as np
import jax
import jax.numpy as jnp
from jax import lax
from jax.experimental import pallas as pl
from jax.experimental.pallas import tpu as pltpu

F32 = jnp.float32
BF16 = jnp.bfloat16

D_MODEL = 1024
HEAD_DIM = 64
NSA_HEADS = 8
NSA_KV_GROUPS = 2
NSA_HEADS_PER_GROUP = NSA_HEADS // NSA_KV_GROUPS
NSA_CMP_BLOCK = 32
NSA_CMP_STRIDE = 16
NSA_CMP_HIDDEN = 128
NSA_SEL_BLOCK = 64
NSA_SEL_TOPK = 16
NSA_WINDOW = 512
MOBA_HEADS = 8
MOBA_BLOCK = 256
MOBA_TOPK = 3
D_FF = 2816
PLE_DIM = 256
RMS_EPS = 1e-6
NEG_INF = -1e30
TINY = 1e-30
FORCE_SCORE = 1e9
LOG2E = float(np.log2(np.e))
Q_SCALE = HEAD_DIM ** -0.5 * LOG2E

NSA_WIDTH = NSA_HEADS * HEAD_DIM
NSA_KV_WIDTH = NSA_KV_GROUPS * HEAD_DIM
MOBA_WIDTH = MOBA_HEADS * HEAD_DIM
N_GATE_LOGITS = 3 * NSA_HEADS
OFF_KCMP = NSA_WIDTH
OFF_VCMP = OFF_KCMP + NSA_KV_WIDTH
OFF_KSLC = OFF_VCMP + NSA_KV_WIDTH
OFF_VSLC = OFF_KSLC + NSA_KV_WIDTH
OFF_KWIN = OFF_VSLC + NSA_KV_WIDTH
OFF_VWIN = OFF_KWIN + NSA_KV_WIDTH
OFF_GATE_LOGITS = OFF_VWIN + NSA_KV_WIDTH
OFF_MOBA_Q = OFF_GATE_LOGITS + N_GATE_LOGITS
OFF_MOBA_K = OFF_MOBA_Q + MOBA_WIDTH
OFF_MOBA_V = OFF_MOBA_K + MOBA_WIDTH
OFF_MERGE = OFF_MOBA_V + MOBA_WIDTH

LANES = 128
MXU_DIM = 256
VMEM_LIMIT = 56 * 1024 * 1024

TOKEN_TILE = 512
FF_CHUNK = 256
ATTN_TQ = 256
ATTN_TK = 256
CMP_TQ = 2048
ATTN_UNROLL = 4
BIAS_ROW = 2 * HEAD_DIM
SEL_ROW = BIAS_ROW + 16
N_SLOPE_PIECES = 3
MOBA_SEL_ROWS = 16
ONES_ROWS = 16
SAFE_EXCESS = 60.0

QT_ROW_NSA, QT_ROW_MOBA = 0, NSA_WIDTH
K_COL_SLC, K_COL_WIN, K_COL_MOBA = 0, NSA_KV_WIDTH, 2 * NSA_KV_WIDTH
VT_ROW_SLC, VT_ROW_WIN, VT_ROW_MOBA = 0, NSA_KV_WIDTH, 2 * NSA_KV_WIDTH
ATTN_HEADS_PER_STEP = 4

_NT = (((1,), (1,)), ((), ()))
_TN = (((0,), (0,)), ((), ()))


def _cparams(*sem):
    return pltpu.CompilerParams(dimension_semantics=sem, vmem_limit_bytes=VMEM_LIMIT)


def _resident(shape):
    nd = len(shape)
    return pl.BlockSpec(shape, lambda *_: (0,) * nd, pipeline_mode=pl.Buffered(1))


def _rms(x, g):
    ms = jnp.mean(x * x, axis=-1, keepdims=True)
    return x * lax.rsqrt(ms + RMS_EPS) * g


def _sigmoid(x):
    return 1.0 / (1.0 + jnp.exp(-x))


def _dot(a, b):
    return jnp.dot(a, b, preferred_element_type=F32)


def _dot_f32(a, b):
    return jnp.dot(a, b, preferred_element_type=F32, precision=lax.Precision.HIGHEST)


def _dot_nt(a, b, precision=None):
    return lax.dot_general(a, b, _NT, preferred_element_type=F32, precision=precision)


def _dot_tn(a, b):
    return lax.dot_general(a, b, _TN, preferred_element_type=F32)


def _ffn_kernel(*refs, tail, final):
    x_ref, g_ref, w1_ref, w3_ref, w2_ref = refs[:5]
    o_ref, acc_ref = refs[-2:]
    if tail:
        p_ref, gp_ref, wpg_ref, wp_ref = refs[5:9]
    if final:
        gf_ref = refs[-3]
    x = x_ref[...]
    xn = _rms(x, g_ref[...]).astype(BF16)
    for c in range(D_FF // FF_CHUNK):
        cols = slice(c * FF_CHUNK, (c + 1) * FF_CHUNK)
        a = _dot(xn, w1_ref[:, cols])
        b = _dot(xn, w3_ref[:, cols])
        hid = (a * _sigmoid(a) * b).astype(BF16)
        y = _dot(hid, w2_ref[cols, :])
        if c == 0:
            acc_ref[...] = y
        else:
            acc_ref[...] += y
    h = x + 0.5 * acc_ref[...]
    if tail:
        gate = _sigmoid(_dot(_rms(h, gp_ref[...]).astype(BF16), wpg_ref[...]))
        h = h + gate * _dot(p_ref[...].astype(BF16), wp_ref[...])
    if final:
        h = _rms(h, gf_ref[...])
    o_ref[...] = h


def _ffn(x, g, w1, w3, w2, tail=None, final_norm=None):
    n = x.shape[0]
    row = lambda w: pl.BlockSpec((TOKEN_TILE, w), lambda i: (i, 0))
    args = [x, g, w1, w3, w2]
    specs = [row(D_MODEL), _resident((1, D_MODEL)), _resident(w1.shape), _resident(w3.shape),
             _resident(w2.shape)]
    if tail is not None:
        p, gp, wpg, wp = tail
        args += [p, gp, wpg, wp]
        specs += [row(PLE_DIM), _resident((1, D_MODEL)), _resident(wpg.shape), _resident(wp.shape)]
    if final_norm is not None:
        args.append(final_norm)
        specs.append(_resident((1, D_MODEL)))
    return pl.pallas_call(
        functools.partial(_ffn_kernel, tail=tail is not None, final=final_norm is not None),
        grid=(n // TOKEN_TILE,),
        in_specs=specs,
        out_specs=row(D_MODEL),
        out_shape=jax.ShapeDtypeStruct((n, D_MODEL), F32),
        scratch_shapes=[pltpu.VMEM((TOKEN_TILE, D_MODEL), F32)],
        compiler_params=_cparams("parallel"),
        name="ffn_tail" if tail is not None else "ffn",
    )(*args)


def _inproj_kernel(h_ref, g_ref, wq_ref, wk_ref, wc_ref, wv_ref, wg_ref,
                   qT_ref, k_ref, cmp_ref, vT_ref, zgT_ref):
    u = _rms(h_ref[...], g_ref[...]).astype(BF16)
    qT_ref[...] = (_dot_nt(wq_ref[...], u) * Q_SCALE).astype(BF16)
    k_ref[...] = _dot(u, wk_ref[...]).astype(BF16)
    for kind in range(2):
        cmp_ref[kind] = _dot(u, wc_ref[kind]).astype(BF16)
    vT_ref[...] = _dot_nt(wv_ref[...], u).astype(BF16)
    zgT_ref[...] = _dot_nt(wg_ref[...], u)


def _inproj(h, g, wq_t, wk, wc, wv_t, wg_t):
    n = h.shape[0]
    tm = TOKEN_TILE
    rows = lambda width: pl.BlockSpec((tm, width), lambda i: (i, 0))
    cols = lambda height: pl.BlockSpec((height, tm), lambda i: (0, i))
    return pl.pallas_call(
        _inproj_kernel,
        grid=(n // tm,),
        in_specs=[rows(D_MODEL), _resident((1, D_MODEL)), _resident(wq_t.shape), _resident(wk.shape),
                  _resident(wc.shape), _resident(wv_t.shape), _resident(wg_t.shape)],
        out_specs=[cols(wq_t.shape[0]), rows(wk.shape[1]),
                   pl.BlockSpec((2, tm, NSA_KV_WIDTH), lambda i: (0, i, 0)),
                   cols(wv_t.shape[0]), cols(LANES)],
        out_shape=[jax.ShapeDtypeStruct((wq_t.shape[0], n), BF16),
                   jax.ShapeDtypeStruct((n, wk.shape[1]), BF16),
                   jax.ShapeDtypeStruct((2, n, NSA_KV_WIDTH), BF16),
                   jax.ShapeDtypeStruct((wv_t.shape[0], n), BF16),
                   jax.ShapeDtypeStruct((LANES, n), F32)],
        compiler_params=_cparams("parallel"),
        name="inproj",
    )(h, g, wq_t, wk, wc, wv_t, wg_t)


def _split3(x):
    hi = x.astype(BF16)
    r = x - hi.astype(F32)
    mid = r.astype(BF16)
    return hi, mid, (r - mid.astype(F32)).astype(BF16)


def _compress_kernel(rows_ref, pos_ref, w1_ref, w2_ref, w2t_ref, o_ref, ot_ref):
    c0 = float(np.sqrt(2.0 / np.pi))
    hidden = NSA_CMP_HIDDEN
    for kind in range(2):
        rows = rows_ref[kind].astype(F32)
        first = _dot((rows + pos_ref[kind, 0]).astype(BF16), w1_ref[kind, 0])
        second = _dot((rows + pos_ref[kind, 1]).astype(BF16), w1_ref[kind, 1])
        n_rows = rows.shape[0]
        pre = first + pltpu.roll(second, n_rows - 1, 0)
        hid = pre * (0.5 * (1.0 + jnp.tanh(c0 * (pre + 0.044715 * (pre * pre * pre)))))
        for g in range(NSA_KV_GROUPS):
            hid_g = hid[:, g * hidden:(g + 1) * hidden]
            out = _dot_f32(hid_g, w2_ref[kind])
            o_ref[kind * NSA_KV_GROUPS + g] = jnp.concatenate(_split3(out) + (jnp.zeros(out.shape, BF16),), axis=1)
            ot_ref[kind * NSA_KV_GROUPS + g] = _dot_nt(w2t_ref[kind], hid_g, lax.Precision.HIGHEST).astype(BF16)


def _compress(cmp_rows, pos, w1, w2, w2t):
    _, b, n_rows, width = cmp_rows.shape
    n_out = 2 * NSA_KV_GROUPS
    return pl.pallas_call(
        _compress_kernel,
        grid=(b,),
        in_specs=[
            pl.BlockSpec((2, None, n_rows, width), lambda i: (0, i, 0, 0)),
            _resident(pos.shape), _resident(w1.shape), _resident(w2.shape), _resident(w2t.shape),
        ],
        out_specs=[
            pl.BlockSpec((None, n_out, n_rows, 4 * HEAD_DIM), lambda i: (i, 0, 0, 0)),
            pl.BlockSpec((None, n_out, HEAD_DIM, n_rows), lambda i: (i, 0, 0, 0)),
        ],
        out_shape=[jax.ShapeDtypeStruct((b, n_out, n_rows, 4 * HEAD_DIM), BF16),
                   jax.ShapeDtypeStruct((b, n_out, HEAD_DIM, n_rows), BF16)],
        compiler_params=_cparams("parallel"),
        name="compress",
    )(cmp_rows, pos, w1, w2, w2t)


def _count_rank(score, idx, n_candidates):
    rank = jnp.zeros(score.shape, jnp.int32)
    for j in range(n_candidates):
        row = score[j:j + 1, :]
        ahead = (row > score) | ((row == score) & (idx > j))
        rank = rank + ahead.astype(jnp.int32)
    return rank


def _cmp_select_kernel(slopes_ref, qT_ref, kc_ref, vcT_ref, ovT_ref, oT_ref, negsel_ref):
    g = pl.program_id(1)
    qi = pl.program_id(2)
    tq = qT_ref.shape[1]
    n_cmp = kc_ref.shape[0]
    n_sel = ovT_ref.shape[0]
    hd = HEAD_DIM
    t = qi * tq + lax.broadcasted_iota(jnp.int32, (n_cmp, tq), 1)
    blk_c = lax.broadcasted_iota(jnp.int32, (n_cmp, tq), 0)
    dist = t - (blk_c * NSA_CMP_STRIDE + (NSA_CMP_BLOCK - 1))
    visible = dist >= 0
    dist_f = dist.astype(F32)
    kc3 = kc_ref[...]
    vcT = vcT_ref[...]
    p_sum = jnp.zeros((n_cmp, tq), F32)
    for hh in range(NSA_HEADS_PER_GROUP):
        slope2 = slopes_ref[g * NSA_HEADS_PER_GROUP + hh]
        q = qT_ref[hh * hd:(hh + 1) * hd, :]
        s = _dot(kc3, jnp.concatenate([q, q, q, jnp.zeros_like(q)], axis=0)) - slope2 * dist_f
        s = jnp.where(visible, s, NEG_INF)
        m = jnp.max(s, axis=0, keepdims=True)
        e = jnp.exp2(s - m)
        any_visible = m > 0.5 * NEG_INF
        p = e * jnp.where(any_visible, 1.0 / jnp.maximum(jnp.sum(e, axis=0, keepdims=True), TINY), 0.0)
        p_sum = p_sum + p
        oT_ref[hh * hd:(hh + 1) * hd, :] = _dot(vcT, p.astype(BF16))
    imp = _dot(ovT_ref[...], jnp.concatenate(_split3(p_sum), axis=0))
    ts = qi * tq + lax.broadcasted_iota(jnp.int32, (n_sel, tq), 1)
    blk = lax.broadcasted_iota(jnp.int32, (n_sel, tq), 0)
    cur = ts // NSA_SEL_BLOCK
    causal = blk * NSA_SEL_BLOCK <= ts
    forced = (blk == 0) | (blk == cur) | (blk == cur - 1)
    imp = jnp.where(causal, jnp.where(forced, FORCE_SCORE, imp), NEG_INF)
    rank = _count_rank(imp, blk, n_sel)
    chosen = (rank < min(NSA_SEL_TOPK, n_sel)) & causal
    negsel_ref[...] = jnp.where(chosen, 0.0, NEG_INF).astype(BF16)


def _cmp_select(slopes2, qT, kc, vcT, overlap_t, batch):
    n = qT.shape[1]
    tq = min(CMP_TQ, n // batch)
    nq = n // batch // tq
    rows = NSA_HEADS_PER_GROUP * HEAD_DIM
    n_cmp = kc.shape[2]
    n_sel = overlap_t.shape[0]
    return pl.pallas_call(
        _cmp_select_kernel,
        grid=(batch, NSA_KV_GROUPS, nq),
        in_specs=[
            pl.BlockSpec(memory_space=pltpu.SMEM),
            pl.BlockSpec((rows, tq), lambda i, g, q: (g, i * nq + q)),
            pl.BlockSpec((None, None, n_cmp, 4 * HEAD_DIM), lambda i, g, q: (i, g, 0, 0)),
            pl.BlockSpec((None, None, HEAD_DIM, n_cmp), lambda i, g, q: (i, NSA_KV_GROUPS + g, 0, 0)),
            pl.BlockSpec((n_sel, 3 * n_cmp), lambda i, g, q: (0, 0)),
        ],
        out_specs=[
            pl.BlockSpec((rows, tq), lambda i, g, q: (g, i * nq + q)),
            pl.BlockSpec((None, n_sel, tq), lambda i, g, q: (g, 0, i * nq + q)),
        ],
        out_shape=[jax.ShapeDtypeStruct((NSA_WIDTH, n), F32),
                   jax.ShapeDtypeStruct((NSA_KV_GROUPS, n_sel, n), BF16)],
        compiler_params=_cparams("parallel", "parallel", "parallel"),
        name="cmp_select",
    )(slopes2, qT, kc, vcT, overlap_t)


def _moba_select_kernel(qT_ref, k_ref, negsel_ref):
    s = k_ref.shape[0]
    nb = s // MOBA_BLOCK
    hd = HEAD_DIM
    kmean = jnp.mean(k_ref[...].astype(F32).reshape(nb, MOBA_BLOCK, 2 * hd), axis=1)
    t = lax.broadcasted_iota(jnp.int32, (nb, s), 1)
    blk = lax.broadcasted_iota(jnp.int32, (nb, s), 0)
    cur = t // MOBA_BLOCK
    past = blk < cur
    for j in range(2):
        gs = _dot_f32(kmean[:, j * hd:(j + 1) * hd], qT_ref[j * hd:(j + 1) * hd, :].astype(F32))
        gs = jnp.where(past, gs, NEG_INF)
        rank = _count_rank(gs, blk, nb)
        chosen = ((rank < min(MOBA_TOPK, nb - 1)) & past) | (blk == cur)
        negsel_ref[j, 0:nb, :] = jnp.where(chosen, 0.0, NEG_INF).astype(BF16)
        negsel_ref[j, nb:, :] = jnp.zeros((MOBA_SEL_ROWS - nb, s), BF16)


def _moba_select(qT, k_tok, batch, q_row_block, k_col_block):
    n = qT.shape[1]
    s = n // batch
    pair = 2 * HEAD_DIM
    return pl.pallas_call(
        _moba_select_kernel,
        grid=(batch, MOBA_HEADS // 2),
        in_specs=[pl.BlockSpec((pair, s), lambda i, j: (q_row_block + j, i)),
                  pl.BlockSpec((s, pair), lambda i, j: (i, k_col_block + j))],
        out_specs=pl.BlockSpec((2, MOBA_SEL_ROWS, s), lambda i, j: (j, 0, i)),
        out_shape=jax.ShapeDtypeStruct((MOBA_HEADS, MOBA_SEL_ROWS, n), BF16),
        compiler_params=_cparams("parallel", "parallel"),
        name="moba_select",
    )(qT, k_tok)


def _attn_schedule(nq, window_chunks):
    def padded(pairs, start):
        n_pad = (-(start + len(pairs) - 1)) % ATTN_UNROLL
        return pairs + [(0, 0, 0)] * n_pad

    diag = [(q, q, 1) for q in range(nq)]
    groups = [("diag", padded(diag, 0))]
    if window_chunks is None:
        full = [(q, c, 1) for q in range(nq) for c in range(q)]
        groups.append((None, full))
    else:
        assert window_chunks == 2
        done = len(groups[0][1])
        lower = padded([(q, q - 2, 1) for q in range(2, nq)], done)
        groups.append(("lower", lower))
        groups.append(("middle", [(q, q - 1, 1) for q in range(1, nq)]))
    pairs, kinds = [], []
    for kind, group in groups:
        pairs += group
        kinds += [kind] * len(group)
    n_pad = (-(len(pairs) - 1)) % ATTN_UNROLL
    pairs += [(0, 0, 0)] * n_pad
    kinds += [kinds[-1] if kinds[-1] is not None else "diag"] * n_pad
    body_kinds = []
    for j in range((len(pairs) - 1) // ATTN_UNROLL):
        ks = set(kinds[1 + j * ATTN_UNROLL:1 + (j + 1) * ATTN_UNROLL])
        assert len(ks) == 1, "a loop body must issue scores of one mask kind"
        body_kinds.append(ks.pop())
    table = np.asarray(pairs, np.int32)
    return table[:, 0], table[:, 1], table[:, 2], tuple(body_kinds)


def _attn_kernel(*refs, n_pairs, heads_per_pair, heads_per_v, has_sel, slot_of, body_kinds, clamp_sum):
    n_in = 10 if has_sel else 9
    qtab_ref, ctab_ref, vtab_ref, slopes_ref, qT_ref, k_ref, ktab_ref, vT_ref, qbias_ref = refs[:9]
    negsel_ref = refs[9] if has_sel else None
    oT_ref = refs[n_in]
    qaug_ref, vaug_ref, m_ref, acc_ref, excess_ref = refs[n_in + 1:n_in + 6]
    s_bufs = refs[n_in + 6:n_in + 8]
    cmax_bufs = refs[n_in + 8:n_in + 10]
    alpha_bufs = refs[n_in + 10:n_in + 12]
    p_bufs = refs[n_in + 12:n_in + 14]
    grp = pl.program_id(1)
    tq, tk, hd = ATTN_TQ, ATTN_TK, HEAD_DIM
    nq = qaug_ref.shape[0]
    hpp = heads_per_pair
    nh = n_pairs * hpp
    pcol = hpp * tq
    ncol = n_pairs * pcol
    n_vgroups = nh // heads_per_v
    vcol = heads_per_v * tq
    arows = hd + ONES_ROWS
    n_steps = qtab_ref.shape[0]
    unroll = ATTN_UNROLL

    for vg in range(n_vgroups):
        vaug_ref[vg * arows:vg * arows + hd, :] = vT_ref[vg * hd:(vg + 1) * hd, :]
        vaug_ref[vg * arows + hd:(vg + 1) * arows, :] = jnp.ones((ONES_ROWS, vaug_ref.shape[1]), BF16)

    for qi in range(nq):
        toks = slice(qi * tq, (qi + 1) * tq)
        for i in range(nh):
            cols = slice(i * tq, (i + 1) * tq)
            slot = slot_of(i % hpp, grp)
            q = qT_ref[i * hd:(i + 1) * hd, toks]
            zero = jnp.zeros_like(q)
            qaug_ref[qi, 0:hd, cols] = jnp.where(slot == 0, q, zero)
            qaug_ref[qi, hd:2 * hd, cols] = jnp.where(slot == 1, q, zero)
            qaug_ref[qi, BIAS_ROW:SEL_ROW, cols] = jnp.concatenate([qbias_ref[i]] * (tq // LANES), axis=1)
            fill = SEL_ROW
            if has_sel:
                n_rows = negsel_ref.shape[1]
                qaug_ref[qi, SEL_ROW:SEL_ROW + n_rows, cols] = negsel_ref[i % negsel_ref.shape[0], :, toks]
                fill = SEL_ROW + n_rows
            qaug_ref[qi, fill:, cols] = jnp.zeros((MXU_DIM - fill, tq), BF16)

    key_i = lax.broadcasted_iota(jnp.int32, (tk, ncol), 0)
    qry_row = lax.broadcasted_iota(jnp.int32, (1, ncol), 1) % tq
    big = jnp.int32(2 ** 30)

    def visible(kind, valid):
        if kind == "diag":
            return key_i <= jnp.where(valid, qry_row, -1)
        if kind == "lower":
            return key_i > jnp.where(valid, qry_row, big)
        if kind == "middle":
            return key_i >= jnp.where(valid, 0, big)
        return None

    def key_slice(t):
        return pl.ds(pl.multiple_of(ctab_ref[t] * tk, tk), tk)

    def scores(t, buf, kind):
        qi = qtab_ref[t]
        keys = key_slice(t)
        mask = visible(kind, vtab_ref[t] > 0)
        for pr in range(n_pairs):
            kaug = jnp.concatenate([k_ref[keys, pr * LANES:(pr + 1) * LANES], ktab_ref[keys, :]], axis=1)
            cols = slice(pr * pcol, (pr + 1) * pcol)
            s = _dot(kaug, qaug_ref[qi, :, cols])
            if mask is not None:
                s = jnp.where(mask[:, cols], s, NEG_INF)
            s_bufs[buf][:, cols] = s
            cmax_bufs[buf][:, cols] = jnp.max(s, axis=0, keepdims=True)

    def softmax(t, buf):
        qi = qtab_ref[t]
        m = m_ref[qi]
        m_new = jnp.maximum(m, cmax_bufs[buf][...])
        m_ref[qi] = m_new
        alpha_bufs[buf][...] = jnp.exp2(m - m_new)
        p_bufs[buf][...] = jnp.exp2(s_bufs[buf][...] - m_new).astype(BF16)

    def weighted(t, buf, rescale):
        qi = qtab_ref[t]
        keys = key_slice(t)
        for vg in range(n_vgroups):
            cols = slice(vg * vcol, (vg + 1) * vcol)
            pv = _dot(vaug_ref[vg * arows:(vg + 1) * arows, keys], p_bufs[buf][:, cols])
            old = acc_ref[qi, :, cols]
            acc_ref[qi, :, cols] = (alpha_bufs[buf][:, cols] * old if rescale else old) + pv

    def fast_scores(t, buf, kind):
        qi = qtab_ref[t]
        keys = key_slice(t)
        mask = visible(kind, vtab_ref[t] > 0)
        ref = m_ref[qi]
        for pr in range(n_pairs):
            kaug = jnp.concatenate([k_ref[keys, pr * LANES:(pr + 1) * LANES], ktab_ref[keys, :]], axis=1)
            cols = slice(pr * pcol, (pr + 1) * pcol)
            s = _dot(kaug, qaug_ref[qi, :, cols])
            if mask is not None:
                s = jnp.where(mask[:, cols], s, NEG_INF)
            excess_ref[:, cols] = jnp.maximum(excess_ref[:, cols], jnp.max(s, axis=0, keepdims=True) - ref[:, cols])
            p_bufs[buf][:, cols] = jnp.exp2(s - ref[:, cols]).astype(BF16)

    def pipeline(step):
        first = 0
        for kind in sorted(set(body_kinds), key=body_kinds.index):
            count = body_kinds.count(kind)
            assert body_kinds[first:first + count] == (kind,) * count

            @pl.loop(first, first + count)
            def _(j, kind=kind):
                for u in range(unroll):
                    step(j * unroll + u, u, kind)

            first += count

    last = n_steps - 1

    for qi in range(nq):
        pos = (qi * tq + lax.broadcasted_iota(jnp.int32, (1, tq), 1)).astype(F32)
        for i in range(nh):
            m_ref[qi, :, i * tq:(i + 1) * tq] = slopes_ref[grp * nh + i] * pos
    acc_ref[...] = jnp.zeros(acc_ref.shape, F32)
    excess_ref[...] = jnp.full((1, ncol), NEG_INF, F32)
    fast_scores(0, 0, "diag")

    def fast_step(t, u, kind):
        fast_scores(t + 1, (u + 1) % 2, kind)
        weighted(t, u % 2, rescale=False)

    pipeline(fast_step)
    weighted(last, last % 2, rescale=False)

    excess = excess_ref[...]
    @pl.when((jnp.max(excess) > SAFE_EXCESS) | (jnp.min(excess) < -SAFE_EXCESS))
    def _():
        m_ref[...] = jnp.full(m_ref.shape, NEG_INF, F32)
        acc_ref[...] = jnp.zeros(acc_ref.shape, F32)
        p_bufs[1][...] = jnp.zeros(p_bufs[1].shape, BF16)
        alpha_bufs[1][...] = jnp.ones((1, ncol), F32)
        scores(0, 0, "diag")

        def safe_step(t, u, kind):
            scores(t + 1, (u + 1) % 2, kind)
            weighted(jnp.maximum(t - 1, 0), (u + 1) % 2, rescale=True)
            softmax(t, u % 2)

        pipeline(safe_step)
        weighted(last - 1, (last - 1) % 2, rescale=True)
        softmax(last, last % 2)
        weighted(last, last % 2, rescale=True)

    for qi in range(nq):
        for i in range(nh):
            cols = slice(i * tq, (i + 1) * tq)
            l = acc_ref[qi, hd:hd + 1, cols]
            oT_ref[i * hd:(i + 1) * hd, qi * tq:(qi + 1) * tq] = (
                acc_ref[qi, 0:hd, cols] / (jnp.maximum(l, TINY) if clamp_sum else l))


def _attention(slopes2, qT, k_tok, key_table, vT, qbias, negsel, *, batch, n_steps, n_pairs,
               heads_per_pair, heads_per_v, q_block, k_block, v_block, slot_of, window=None,
               clamp_sum, name):
    n = qT.shape[1]
    s = n // batch
    tq, tk, hd = ATTN_TQ, ATTN_TK, HEAD_DIM
    assert tq == tk and s % tq == 0
    nq = s // tq
    nh = n_pairs * heads_per_pair
    assert nh % heads_per_v == 0
    ncol = nh * tq
    arows = hd + ONES_ROWS
    n_vgroups = nh // heads_per_v
    if window is not None:
        assert window % tk == 0
    q_tab, c_tab, v_tab, body_kinds = _attn_schedule(nq, None if window is None else window // tk)
    smem = pl.BlockSpec(memory_space=pltpu.SMEM)
    in_specs = [
        smem, smem, smem, smem,
        pl.BlockSpec((nh * hd, s), lambda b, g: (q_block(g), b)),
        pl.BlockSpec((s, n_pairs * LANES), lambda b, g: (b, k_block(g))),
        pl.BlockSpec((s, LANES), lambda b, g: (0, 0)),
        pl.BlockSpec((n_vgroups * hd, s), lambda b, g: (v_block(g), b)),
        pl.BlockSpec((nh, 16, LANES), lambda b, g: (g, 0, 0)),
    ]
    args = [jnp.asarray(q_tab), jnp.asarray(c_tab), jnp.asarray(v_tab), slopes2, qT, k_tok, key_table, vT,
            qbias]
    if negsel is not None:
        per_step = negsel.shape[0] // n_steps
        in_specs.append(pl.BlockSpec((per_step, negsel.shape[1], s), lambda b, g: (g, 0, b)))
        args.append(negsel)
    vec = lambda: pltpu.VMEM((1, ncol), F32)
    return pl.pallas_call(
        functools.partial(_attn_kernel, n_pairs=n_pairs, heads_per_pair=heads_per_pair,
                          heads_per_v=heads_per_v, has_sel=negsel is not None, slot_of=slot_of,
                          body_kinds=body_kinds, clamp_sum=clamp_sum),
        grid=(batch, n_steps),
        in_specs=in_specs,
        out_specs=pl.BlockSpec((nh * hd, s), lambda b, g: (g, b)),
        out_shape=jax.ShapeDtypeStruct((n_steps * nh * hd, n), F32),
        scratch_shapes=[pltpu.VMEM((nq, MXU_DIM, ncol), BF16),
                        pltpu.VMEM((n_vgroups * arows, s), BF16),
                        pltpu.VMEM((nq, 1, ncol), F32),
                        pltpu.VMEM((nq, arows, ncol), F32),
                        vec(),
                        pltpu.VMEM((tk, ncol), F32), pltpu.VMEM((tk, ncol), F32),
                        vec(), vec(), vec(), vec(),
                        pltpu.VMEM((tk, ncol), BF16), pltpu.VMEM((tk, ncol), BF16)],
        compiler_params=_cparams("parallel", "parallel"),
        name=name,
    )(*args)


def _mix_kernel(h_ref, g_ref, zgT_ref, ocmp_ref, osel_ref, owin_ref, omoba_ref,
                wmerge_ref, wn_ref, wm_ref, wo_ref, o_ref, gate_ref, onsa_ref):
    h = h_ref[...]
    u = _rms(h, g_ref[...]).astype(BF16)
    gate_ref[...] = _sigmoid(zgT_ref[0:gate_ref.shape[0], :])
    hd = HEAD_DIM
    for hh in range(NSA_HEADS):
        rows = slice(hh * hd, (hh + 1) * hd)
        o = None
        for r, branch in enumerate((ocmp_ref, osel_ref, owin_ref)):
            gate = gate_ref[3 * hh + r:3 * hh + r + 1, :]
            term = gate * branch[rows, :]
            o = term if o is None else o + term
        onsa_ref[rows, :] = o.astype(BF16)
    y_n = _dot_tn(onsa_ref[...], wn_ref[...])
    y_m = _dot_tn(omoba_ref[...].astype(BF16), wm_ref[...])
    mixed = (_sigmoid(_dot(u, wmerge_ref[:, :D_MODEL])) * y_n
             + _sigmoid(_dot(u, wmerge_ref[:, D_MODEL:])) * y_m)
    o_ref[...] = h + _dot(mixed.astype(BF16), wo_ref[...])


def _mix(h, g, zgT, ocmpT, oselT, owinT, omobaT, wmerge, wn, wm, wo):
    n = h.shape[0]
    tm = TOKEN_TILE
    rows = lambda width: pl.BlockSpec((tm, width), lambda i: (i, 0))
    cols = lambda height: pl.BlockSpec((height, tm), lambda i: (0, i))
    gate_rows = -(-N_GATE_LOGITS // 8) * 8
    return pl.pallas_call(
        _mix_kernel,
        grid=(n // tm,),
        in_specs=[rows(D_MODEL), _resident((1, D_MODEL)), cols(LANES), cols(NSA_WIDTH), cols(NSA_WIDTH),
                  cols(NSA_WIDTH), cols(MOBA_WIDTH), _resident(wmerge.shape),
                  _resident(wn.shape), _resident(wm.shape), _resident(wo.shape)],
        out_specs=rows(D_MODEL),
        out_shape=jax.ShapeDtypeStruct((n, D_MODEL), F32),
        scratch_shapes=[pltpu.VMEM((gate_rows, tm), F32), pltpu.VMEM((NSA_WIDTH, tm), BF16)],
        compiler_params=_cparams("parallel"),
        name="mix",
    )(h, g, zgT, ocmpT, oselT, owinT, omobaT, wmerge, wn, wm, wo)


def _key_table(s, block):
    pos = np.arange(s)
    tab = np.zeros((s, LANES), np.float32)
    tab[:, 0:N_SLOPE_PIECES] = ((pos // LANES) * LANES)[:, None]
    tab[:, N_SLOPE_PIECES:2 * N_SLOPE_PIECES] = (pos % LANES)[:, None]
    if block is not None:
        tab[pos, (SEL_ROW - BIAS_ROW) + pos // block] = 1.0
    return jnp.asarray(tab, BF16)


def _slope_rows(slopes2):
    s1 = slopes2.astype(BF16)
    r1 = slopes2 - s1.astype(F32)
    s2 = r1.astype(BF16)
    s3 = (r1 - s2.astype(F32)).astype(BF16)
    rows = jnp.stack([s1, s2, s3, s1, s2, s3], axis=1)
    rows = jnp.pad(rows, ((0, 0), (0, 16 - 2 * N_SLOPE_PIECES)))
    return jnp.broadcast_to(rows[:, :, None], rows.shape + (LANES,))


def _compress_weights(pos, w1):
    st, hd, gw, hidden, groups = NSA_CMP_STRIDE, HEAD_DIM, NSA_KV_WIDTH, NSA_CMP_HIDDEN, NSA_KV_GROUPS
    w = jnp.zeros((2, st, gw, groups * hidden), F32)
    for g in range(groups):
        w = w.at[:, :, g * hd:(g + 1) * hd, g * hidden:(g + 1) * hidden].set(w1.reshape(2, st, hd, hidden))
    ps = jnp.tile(pos.reshape(2, st, 1, hd), (1, 1, groups, 1))
    return ps.reshape(2, 1, st * gw), w.reshape(2, st * gw, groups * hidden).astype(BF16)


def kernel(x, p, ffn1_norm, ffn1_w1, ffn1_w3, ffn1_w2, mix_norm, w_in, cmp_pos_k, cmp_w1_k, cmp_w2_k, cmp_pos_v, cmp_w1_v, cmp_w2_v, w_up_nsa, w_up_moba, w_out, ffn2_norm, ffn2_w1, ffn2_w3, ffn2_w2, ple_norm, w_ple_gate, w_ple, final_norm):
    b, s, d = x.shape
    n = b * s
    depth = p.shape[0]
    n_all = NSA_HEADS + MOBA_HEADS
    slopes = jnp.exp2(-8.0 * (jnp.arange(n_all, dtype=F32) + 1.0) / n_all)
    slopes2_n, slopes2_m = slopes[0::2] * LOG2E, slopes[1::2] * LOG2E
    qbias_n, qbias_m = _slope_rows(slopes2_n), _slope_rows(slopes2_m)

    n_cmp_rows = s // NSA_CMP_STRIDE
    n_cmp = (s - NSA_CMP_BLOCK) // NSA_CMP_STRIDE + 1
    n_sel = s // NSA_SEL_BLOCK
    c_start = np.arange(n_cmp_rows) * NSA_CMP_STRIDE
    s_start = np.arange(n_sel) * NSA_SEL_BLOCK
    overlap_t = ((c_start[None, :] <= s_start[:, None] + NSA_SEL_BLOCK - 1)
                 & (c_start[None, :] + NSA_CMP_BLOCK - 1 >= s_start[:, None])
                 & (np.arange(n_cmp_rows)[None, :] < n_cmp))
    overlap_t = jnp.asarray(np.tile(overlap_t, (1, 3)), BF16)
    table_sel = _key_table(s, NSA_SEL_BLOCK)
    table_win = _key_table(s, None)
    table_moba = _key_table(s, MOBA_BLOCK)

    h = x.reshape(n, d)
    for i in range(depth):
        vec = lambda a: a[i].reshape(1, -1)
        h = _ffn(h, vec(ffn1_norm), ffn1_w1[i].astype(BF16), ffn1_w3[i].astype(BF16),
                 ffn1_w2[i].astype(BF16))

        wi = w_in[i]
        col = lambda lo, width: wi[:, lo:lo + width]
        wq_t = jnp.concatenate([col(0, NSA_WIDTH), col(OFF_MOBA_Q, MOBA_WIDTH)], axis=1).T.astype(BF16)
        wk = jnp.concatenate([col(OFF_KSLC, NSA_KV_WIDTH), col(OFF_KWIN, NSA_KV_WIDTH),
                              col(OFF_MOBA_K, MOBA_WIDTH)], axis=1).astype(BF16)
        wc = jnp.stack([col(OFF_KCMP, NSA_KV_WIDTH), col(OFF_VCMP, NSA_KV_WIDTH)]).astype(BF16)
        wv_t = jnp.concatenate([col(OFF_VSLC, NSA_KV_WIDTH), col(OFF_VWIN, NSA_KV_WIDTH),
                                col(OFF_MOBA_V, MOBA_WIDTH)], axis=1).T.astype(BF16)
        wg_t = jnp.pad(col(OFF_GATE_LOGITS, N_GATE_LOGITS), ((0, 0), (0, LANES - N_GATE_LOGITS))).T.astype(BF16)
        w_merge = col(OFF_MERGE, 2 * D_MODEL).astype(BF16)
        qT, k_tok, cmp_tok, vT, zgT = _inproj(h, vec(mix_norm), wq_t, wk, wc, wv_t, wg_t)

        cmp_w = [_compress_weights(ps[i], w1[i]) for ps, w1 in ((cmp_pos_k, cmp_w1_k), (cmp_pos_v, cmp_w1_v))]
        pos = jnp.stack([pw[0] for pw in cmp_w])
        w1c = jnp.stack([pw[1] for pw in cmp_w])
        w2 = jnp.stack([cmp_w2_k[i], cmp_w2_v[i]])
        cmp_rows = cmp_tok.reshape(2, b, n_cmp_rows, NSA_CMP_STRIDE * NSA_KV_WIDTH)
        kc, kcT = _compress(cmp_rows, pos, w1c, w2, w2.transpose(0, 2, 1))

        ocmpT, negsel_n = _cmp_select(slopes2_n, qT, kc, kcT, overlap_t, b)
        hs = ATTN_HEADS_PER_STEP
        q_rows, pair = hs * HEAD_DIM, 2 * HEAD_DIM
        nsa = dict(batch=b, n_steps=NSA_KV_GROUPS, n_pairs=1, heads_per_pair=hs, heads_per_v=hs,
                   q_block=lambda g: QT_ROW_NSA // q_rows + g, slot_of=lambda j, g: g, clamp_sum=True)
        oselT = _attention(slopes2_n, qT, k_tok, table_sel, vT, qbias_n, negsel_n,
                           k_block=lambda g: K_COL_SLC // pair, v_block=lambda g: VT_ROW_SLC // HEAD_DIM + g,
                           name="attn_select", **nsa)
        owinT = _attention(slopes2_n, qT, k_tok, table_win, vT, qbias_n, None,
                           k_block=lambda g: K_COL_WIN // pair, v_block=lambda g: VT_ROW_WIN // HEAD_DIM + g,
                           window=NSA_WINDOW, name="attn_window", **nsa)
        negsel_m = _moba_select(qT, k_tok, b, q_row_block=QT_ROW_MOBA // pair, k_col_block=K_COL_MOBA // pair)
        omobaT = _attention(slopes2_m, qT, k_tok, table_moba, vT, qbias_m, negsel_m, batch=b,
                            n_steps=MOBA_HEADS // hs, n_pairs=hs // 2, heads_per_pair=2, heads_per_v=1,
                            q_block=lambda g: QT_ROW_MOBA // q_rows + g,
                            k_block=lambda g: K_COL_MOBA // (hs // 2 * pair) + g,
                            v_block=lambda g: VT_ROW_MOBA // q_rows + g, slot_of=lambda j, g: j,
                            clamp_sum=False, name="attn_moba")

        h = _mix(h, vec(mix_norm), zgT, ocmpT, oselT, owinT, omobaT, w_merge,
                 w_up_nsa[i].astype(BF16), w_up_moba[i].astype(BF16), w_out[i].astype(BF16))

        h = _ffn(h, vec(ffn2_norm), ffn2_w1[i].astype(BF16), ffn2_w3[i].astype(BF16), ffn2_w2[i].astype(BF16),
                 tail=(p[i].reshape(n, PLE_DIM), vec(ple_norm), w_ple_gate[i].astype(BF16),
                       w_ple[i].astype(BF16)),
                 final_norm=final_norm.reshape(1, -1) if i + 1 == depth else None)
    return h.reshape(b, s, d)
```

```python
import functools

import numpy as np
import jax
import jax.numpy as jnp
from jax import lax
from jax.experimental import pallas as pl
from jax.experimental.pallas import tpu as pltpu

F32 = jnp.float32
BF16 = jnp.bfloat16

D_MODEL = 1024
HEAD_DIM = 64
NSA_HEADS = 8
NSA_KV_GROUPS = 2
NSA_HEADS_PER_GROUP = NSA_HEADS // NSA_KV_GROUPS
NSA_CMP_BLOCK = 32
NSA_CMP_STRIDE = 16
NSA_CMP_HIDDEN = 128
NSA_SEL_BLOCK = 64
NSA_SEL_TOPK = 16
NSA_WINDOW = 512
MOBA_HEADS = 8
MOBA_BLOCK = 256
MOBA_TOPK = 3
D_FF = 2816
PLE_DIM = 256
RMS_EPS = 1e-6
NEG_INF = -1e30
TINY = 1e-30
FORCE_SCORE = 1e9
LOG2E = float(np.log2(np.e))
Q_SCALE = HEAD_DIM ** -0.5 * LOG2E

NSA_WIDTH = NSA_HEADS * HEAD_DIM
NSA_KV_WIDTH = NSA_KV_GROUPS * HEAD_DIM
MOBA_WIDTH = MOBA_HEADS * HEAD_DIM
N_GATE_LOGITS = 3 * NSA_HEADS
OFF_KCMP = NSA_WIDTH
OFF_VCMP = OFF_KCMP + NSA_KV_WIDTH
OFF_KSLC = OFF_VCMP + NSA_KV_WIDTH
OFF_VSLC = OFF_KSLC + NSA_KV_WIDTH
OFF_KWIN = OFF_VSLC + NSA_KV_WIDTH
OFF_VWIN = OFF_KWIN + NSA_KV_WIDTH
OFF_GATE_LOGITS = OFF_VWIN + NSA_KV_WIDTH
OFF_MOBA_Q = OFF_GATE_LOGITS + N_GATE_LOGITS
OFF_MOBA_K = OFF_MOBA_Q + MOBA_WIDTH
OFF_MOBA_V = OFF_MOBA_K + MOBA_WIDTH
OFF_MERGE = OFF_MOBA_V + MOBA_WIDTH

LANES = 128
MXU_DIM = 256
VMEM_LIMIT = 56 * 1024 * 1024

TOKEN_TILE = 512
FF_CHUNK = 256
ATTN_TQ = 256
ATTN_TK = 256
CMP_TQ = 2048
ATTN_UNROLL = 4
ATTN_FAST_BODY = 28
BIAS_ROW = 2 * HEAD_DIM
SEL_ROW = BIAS_ROW + 16
N_SLOPE_PIECES = 3
MOBA_SEL_ROWS = 16
ONES_ROWS = 16
SAFE_EXCESS = 60.0

QT_ROW_NSA, QT_ROW_MOBA = 0, NSA_WIDTH
K_COL_MOBA, K_COL_SLC, K_COL_WIN = 0, MOBA_WIDTH, MOBA_WIDTH + NSA_KV_WIDTH
VT_ROW_SLC, VT_ROW_WIN, VT_ROW_MOBA = 0, NSA_KV_WIDTH, 2 * NSA_KV_WIDTH
ATTN_HEADS_PER_STEP = 4

_NT = (((1,), (1,)), ((), ()))
_TN = (((0,), (0,)), ((), ()))


def _cparams(*sem):
    return pltpu.CompilerParams(dimension_semantics=sem, vmem_limit_bytes=VMEM_LIMIT)


def _resident(shape):
    nd = len(shape)
    return pl.BlockSpec(shape, lambda *_: (0,) * nd, pipeline_mode=pl.Buffered(1))


def _rms(x, g):
    ms = jnp.mean(x * x, axis=-1, keepdims=True)
    return x * lax.rsqrt(ms + RMS_EPS) * g


def _sigmoid(x):
    return 1.0 / (1.0 + jnp.exp(-x))


def _dot(a, b):
    return jnp.dot(a, b, preferred_element_type=F32)


def _dot_f32(a, b):
    return jnp.dot(a, b, preferred_element_type=F32, precision=lax.Precision.HIGHEST)


def _dot_nt(a, b, precision=None):
    return lax.dot_general(a, b, _NT, preferred_element_type=F32, precision=precision)


def _dot_tn(a, b):
    return lax.dot_general(a, b, _TN, preferred_element_type=F32)


def _ffn_kernel(*refs, tail, final):
    x_ref, g_ref, w1_ref, w3_ref, w2_ref = refs[:5]
    o_ref, acc_ref = refs[-2:]
    if tail:
        p_ref, gp_ref, wpg_ref, wp_ref = refs[5:9]
    if final:
        gf_ref = refs[-3]
    x = x_ref[...]
    xn = _rms(x, g_ref[...]).astype(BF16)
    for c in range(D_FF // FF_CHUNK):
        cols = slice(c * FF_CHUNK, (c + 1) * FF_CHUNK)
        a = _dot(xn, w1_ref[:, cols])
        b = _dot(xn, w3_ref[:, cols])
        hid = (a * _sigmoid(a) * b).astype(BF16)
        y = _dot(hid, w2_ref[cols, :])
        if c == 0:
            acc_ref[...] = y
        else:
            acc_ref[...] += y
    h = x + 0.5 * acc_ref[...]
    if tail:
        gate = _sigmoid(_dot(_rms(h, gp_ref[...]).astype(BF16), wpg_ref[...]))
        h = h + gate * _dot(p_ref[...].astype(BF16), wp_ref[...])
    if final:
        h = _rms(h, gf_ref[...])
    o_ref[...] = h


def _ffn(x, g, w1, w3, w2, tail=None, final_norm=None):
    n = x.shape[0]
    row = lambda w: pl.BlockSpec((TOKEN_TILE, w), lambda i: (i, 0))
    args = [x, g, w1, w3, w2]
    specs = [row(D_MODEL), _resident((1, D_MODEL)), _resident(w1.shape), _resident(w3.shape),
             _resident(w2.shape)]
    if tail is not None:
        p, gp, wpg, wp = tail
        args += [p, gp, wpg, wp]
        specs += [row(PLE_DIM), _resident((1, D_MODEL)), _resident(wpg.shape), _resident(wp.shape)]
    if final_norm is not None:
        args.append(final_norm)
        specs.append(_resident((1, D_MODEL)))
    return pl.pallas_call(
        functools.partial(_ffn_kernel, tail=tail is not None, final=final_norm is not None),
        grid=(n // TOKEN_TILE,),
        in_specs=specs,
        out_specs=row(D_MODEL),
        out_shape=jax.ShapeDtypeStruct((n, D_MODEL), F32),
        scratch_shapes=[pltpu.VMEM((TOKEN_TILE, D_MODEL), F32)],
        compiler_params=_cparams("parallel"),
        name="ffn_tail" if tail is not None else "ffn",
    )(*args)


def _inproj_kernel(h_ref, g_ref, wq_ref, wk_ref, wc_ref, wv_ref, wg_ref,
                   qT_ref, k_ref, cmp_ref, vT_ref, zgT_ref):
    u = _rms(h_ref[...], g_ref[...]).astype(BF16)
    qT_ref[...] = (_dot_nt(wq_ref[...], u) * Q_SCALE).astype(BF16)
    k_ref[...] = _dot(u, wk_ref[...]).astype(BF16)
    for kind in range(2):
        cmp_ref[kind] = _dot(u, wc_ref[kind]).astype(BF16)
    vT_ref[...] = _dot_nt(wv_ref[...], u).astype(BF16)
    zgT_ref[...] = _dot_nt(wg_ref[...], u)


def _inproj(h, g, wq_t, wk, wc, wv_t, wg_t):
    n = h.shape[0]
    tm = TOKEN_TILE
    rows = lambda width: pl.BlockSpec((tm, width), lambda i: (i, 0))
    cols = lambda height: pl.BlockSpec((height, tm), lambda i: (0, i))
    return pl.pallas_call(
        _inproj_kernel,
        grid=(n // tm,),
        in_specs=[rows(D_MODEL), _resident((1, D_MODEL)), _resident(wq_t.shape), _resident(wk.shape),
                  _resident(wc.shape), _resident(wv_t.shape), _resident(wg_t.shape)],
        out_specs=[cols(wq_t.shape[0]), rows(wk.shape[1]),
                   pl.BlockSpec((2, tm, NSA_KV_WIDTH), lambda i: (0, i, 0)),
                   cols(wv_t.shape[0]), cols(LANES)],
        out_shape=[jax.ShapeDtypeStruct((wq_t.shape[0], n), BF16),
                   jax.ShapeDtypeStruct((n, wk.shape[1]), BF16),
                   jax.ShapeDtypeStruct((2, n, NSA_KV_WIDTH), BF16),
                   jax.ShapeDtypeStruct((wv_t.shape[0], n), BF16),
                   jax.ShapeDtypeStruct((LANES, n), F32)],
        compiler_params=_cparams("parallel"),
        name="inproj",
    )(h, g, wq_t, wk, wc, wv_t, wg_t)


def _split3(x):
    hi = x.astype(BF16)
    r = x - hi.astype(F32)
    mid = r.astype(BF16)
    return hi, mid, (r - mid.astype(F32)).astype(BF16)


def _compress_kernel(rows_ref, pos_ref, w1_ref, w2_ref, w2t_ref, o_ref, ot_ref):
    c0 = float(np.sqrt(2.0 / np.pi))
    hidden = NSA_CMP_HIDDEN
    for kind in range(2):
        rows = rows_ref[kind].astype(F32)
        first = _dot((rows + pos_ref[kind, 0]).astype(BF16), w1_ref[kind, 0])
        second = _dot((rows + pos_ref[kind, 1]).astype(BF16), w1_ref[kind, 1])
        n_rows = rows.shape[0]
        pre = first + pltpu.roll(second, n_rows - 1, 0)
        hid = pre * (0.5 * (1.0 + jnp.tanh(c0 * (pre + 0.044715 * (pre * pre * pre)))))
        for g in range(NSA_KV_GROUPS):
            hid_g = hid[:, g * hidden:(g + 1) * hidden]
            out = _dot_f32(hid_g, w2_ref[kind])
            o_ref[kind * NSA_KV_GROUPS + g] = jnp.concatenate(_split3(out) + (jnp.zeros(out.shape, BF16),), axis=1)
            ot_ref[kind * NSA_KV_GROUPS + g] = _dot_nt(w2t_ref[kind], hid_g, lax.Precision.HIGHEST).astype(BF16)


def _compress(cmp_rows, pos, w1, w2, w2t):
    _, b, n_rows, width = cmp_rows.shape
    n_out = 2 * NSA_KV_GROUPS
    return pl.pallas_call(
        _compress_kernel,
        grid=(b,),
        in_specs=[
            pl.BlockSpec((2, None, n_rows, width), lambda i: (0, i, 0, 0)),
            _resident(pos.shape), _resident(w1.shape), _resident(w2.shape), _resident(w2t.shape),
        ],
        out_specs=[
            pl.BlockSpec((None, n_out, n_rows, 4 * HEAD_DIM), lambda i: (i, 0, 0, 0)),
            pl.BlockSpec((None, n_out, HEAD_DIM, n_rows), lambda i: (i, 0, 0, 0)),
        ],
        out_shape=[jax.ShapeDtypeStruct((b, n_out, n_rows, 4 * HEAD_DIM), BF16),
                   jax.ShapeDtypeStruct((b, n_out, HEAD_DIM, n_rows), BF16)],
        compiler_params=_cparams("parallel"),
        name="compress",
    )(cmp_rows, pos, w1, w2, w2t)


def _count_rank(score, n_candidates):
    sub = 8
    tiles = [(lo, min(lo + sub, score.shape[0])) for lo in range(0, score.shape[0], sub)]
    ranks = [jnp.zeros((hi - lo,) + score.shape[1:], jnp.int32) for lo, hi in tiles]
    for j in range(n_candidates):
        row = score[j:j + 1, :]
        for n, (lo, hi) in enumerate(tiles):
            part = score[lo:hi]
            if lo > j:
                ahead = row >= part
            elif hi - 1 <= j:
                ahead = row > part
            else:
                after = lo + lax.broadcasted_iota(jnp.int32, part.shape, 0) > j
                ahead = (row > part) | ((row == part) & after)
            ranks[n] = ranks[n] + ahead.astype(jnp.int32)
    return ranks[0] if len(ranks) == 1 else jnp.concatenate(ranks, axis=0)


def _cmp_select_kernel(slopes_ref, qT_ref, kc_ref, vcT_ref, ovT_ref, oT_ref, negsel_ref):
    g = pl.program_id(1)
    qi = pl.program_id(2)
    tq = qT_ref.shape[1]
    n_cmp = kc_ref.shape[0]
    n_sel = ovT_ref.shape[0]
    hd = HEAD_DIM
    t = qi * tq + lax.broadcasted_iota(jnp.int32, (n_cmp, tq), 1)
    blk_c = lax.broadcasted_iota(jnp.int32, (n_cmp, tq), 0)
    dist = t - (blk_c * NSA_CMP_STRIDE + (NSA_CMP_BLOCK - 1))
    visible = dist >= 0
    dist_f = dist.astype(F32)
    kc3 = kc_ref[...]
    vcT = vcT_ref[...]
    p_sum = jnp.zeros((n_cmp, tq), F32)
    for hh in range(NSA_HEADS_PER_GROUP):
        slope2 = slopes_ref[g * NSA_HEADS_PER_GROUP + hh]
        q = qT_ref[hh * hd:(hh + 1) * hd, :]
        s = _dot(kc3, jnp.concatenate([q, q, q, jnp.zeros_like(q)], axis=0)) - slope2 * dist_f
        s = jnp.where(visible, s, NEG_INF)
        m = jnp.max(s, axis=0, keepdims=True)
        e = jnp.exp2(s - m)
        any_visible = m > 0.5 * NEG_INF
        p = e * jnp.where(any_visible, 1.0 / jnp.maximum(jnp.sum(e, axis=0, keepdims=True), TINY), 0.0)
        p_sum = p_sum + p
        oT_ref[hh * hd:(hh + 1) * hd, :] = _dot(vcT, p.astype(BF16))
    imp = _dot(ovT_ref[...], jnp.concatenate(_split3(p_sum), axis=0))
    ts = qi * tq + lax.broadcasted_iota(jnp.int32, (n_sel, tq), 1)
    blk = lax.broadcasted_iota(jnp.int32, (n_sel, tq), 0)
    cur = ts // NSA_SEL_BLOCK
    causal = blk * NSA_SEL_BLOCK <= ts
    forced = (blk == 0) | (blk == cur) | (blk == cur - 1)
    imp = jnp.where(causal, jnp.where(forced, FORCE_SCORE, imp), NEG_INF)
    rank = _count_rank(imp, n_sel)
    chosen = (rank < min(NSA_SEL_TOPK, n_sel)) & causal
    negsel_ref[...] = jnp.where(chosen, 0.0, NEG_INF).astype(BF16)


def _cmp_select(slopes2, qT, kc, vcT, overlap_t, batch):
    n = qT.shape[1]
    tq = min(CMP_TQ, n // batch)
    nq = n // batch // tq
    rows = NSA_HEADS_PER_GROUP * HEAD_DIM
    n_cmp = kc.shape[2]
    n_sel = overlap_t.shape[0]
    return pl.pallas_call(
        _cmp_select_kernel,
        grid=(batch, NSA_KV_GROUPS, nq),
        in_specs=[
            pl.BlockSpec(memory_space=pltpu.SMEM),
            pl.BlockSpec((rows, tq), lambda i, g, q: (g, i * nq + q)),
            pl.BlockSpec((None, None, n_cmp, 4 * HEAD_DIM), lambda i, g, q: (i, g, 0, 0)),
            pl.BlockSpec((None, None, HEAD_DIM, n_cmp), lambda i, g, q: (i, NSA_KV_GROUPS + g, 0, 0)),
            pl.BlockSpec((n_sel, 3 * n_cmp), lambda i, g, q: (0, 0)),
        ],
        out_specs=[
            pl.BlockSpec((rows, tq), lambda i, g, q: (g, i * nq + q)),
            pl.BlockSpec((None, n_sel, tq), lambda i, g, q: (g, 0, i * nq + q)),
        ],
        out_shape=[jax.ShapeDtypeStruct((NSA_WIDTH, n), F32),
                   jax.ShapeDtypeStruct((NSA_KV_GROUPS, n_sel, n), BF16)],
        compiler_params=_cparams("parallel", "parallel", "parallel"),
        name="cmp_select",
    )(slopes2, qT, kc, vcT, overlap_t)


def _moba_select_kernel(qT_ref, k_ref, negsel_ref):
    s = k_ref.shape[0]
    nb = s // MOBA_BLOCK
    hd = HEAD_DIM
    kmean = jnp.mean(k_ref[...].astype(F32).reshape(nb, MOBA_BLOCK, k_ref.shape[1]), axis=1)
    t = lax.broadcasted_iota(jnp.int32, (nb, s), 1)
    blk = lax.broadcasted_iota(jnp.int32, (nb, s), 0)
    cur = t // MOBA_BLOCK
    past = blk < cur
    for j in range(MOBA_HEADS):
        gs = _dot_f32(kmean[:, j * hd:(j + 1) * hd], qT_ref[j * hd:(j + 1) * hd, :].astype(F32))
        gs = jnp.where(past, gs, NEG_INF)
        rank = _count_rank(gs, nb)
        chosen = ((rank < min(MOBA_TOPK, nb - 1)) & past) | (blk == cur)
        negsel_ref[j, 0:nb, :] = jnp.where(chosen, 0.0, NEG_INF).astype(BF16)
        negsel_ref[j, nb:, :] = jnp.zeros((MOBA_SEL_ROWS - nb, s), BF16)


def _moba_select(qT, k_tok, batch, q_row_block, k_col_block):
    n = qT.shape[1]
    s = n // batch
    return pl.pallas_call(
        _moba_select_kernel,
        grid=(batch,),
        in_specs=[pl.BlockSpec((MOBA_WIDTH, s), lambda i: (q_row_block, i)),
                  pl.BlockSpec((s, MOBA_WIDTH), lambda i: (i, k_col_block))],
        out_specs=pl.BlockSpec((MOBA_HEADS, MOBA_SEL_ROWS, s), lambda i: (0, 0, i)),
        out_shape=jax.ShapeDtypeStruct((MOBA_HEADS, MOBA_SEL_ROWS, n), BF16),
        compiler_params=_cparams("parallel"),
        name="moba_select",
    )(qT, k_tok)


def _attn_schedule(nq, window_chunks):
    def padded(pairs, start):
        n_pad = (-(start + len(pairs) - 1)) % ATTN_UNROLL
        return pairs + [(0, 0, 0)] * n_pad

    diag = [(q, q, 1) for q in range(nq)]
    groups = [("diag", padded(diag, 0))]
    if window_chunks is None:
        full = [(q, c, 1) for q in range(nq) for c in range(q)]
        groups.append((None, full))
    else:
        assert window_chunks == 2
        done = len(groups[0][1])
        lower = padded([(q, q - 2, 1) for q in range(2, nq)], done)
        groups.append(("lower", lower))
        groups.append(("middle", [(q, q - 1, 1) for q in range(1, nq)]))
    pairs, kinds = [], []
    for kind, group in groups:
        pairs += group
        kinds += [kind] * len(group)
    n_pad = (-(len(pairs) - 1)) % ATTN_UNROLL
    pairs += [(0, 0, 0)] * n_pad
    kinds += [kinds[-1] if kinds[-1] is not None else "diag"] * n_pad
    body_kinds = []
    for j in range((len(pairs) - 1) // ATTN_UNROLL):
        ks = set(kinds[1 + j * ATTN_UNROLL:1 + (j + 1) * ATTN_UNROLL])
        assert len(ks) == 1, "a loop body must issue scores of one mask kind"
        body_kinds.append(ks.pop())
    table = np.asarray(pairs, np.int32)
    return table[:, 0], table[:, 1], table[:, 2], tuple(body_kinds)


def _attn_kernel(*refs, n_pairs, heads_per_pair, heads_per_v, has_sel, slot_of, body_kinds, clamp_sum):
    n_in = 10 if has_sel else 9
    qtab_ref, ctab_ref, vtab_ref, slopes_ref, qT_ref, k_ref, ktab_ref, vT_ref, qbias_ref = refs[:9]
    negsel_ref = refs[9] if has_sel else None
    oT_ref = refs[n_in]
    qaug_ref, vaug_ref, m_ref, acc_ref, excess_ref = refs[n_in + 1:n_in + 6]
    s_bufs = refs[n_in + 6:n_in + 8]
    cmax_bufs = refs[n_in + 8:n_in + 10]
    alpha_bufs = refs[n_in + 10:n_in + 12]
    p_bufs = refs[n_in + 12:n_in + 14]
    grp = pl.program_id(1)
    tq, tk, hd = ATTN_TQ, ATTN_TK, HEAD_DIM
    nq = qaug_ref.shape[0]
    hpp = heads_per_pair
    nh = n_pairs * hpp
    pcol = hpp * tq
    ncol = n_pairs * pcol
    n_vgroups = nh // heads_per_v
    vcol = heads_per_v * tq
    arows = hd + ONES_ROWS
    n_steps = qtab_ref.shape[0]
    unroll = ATTN_UNROLL

    for vg in range(n_vgroups):
        vaug_ref[vg * arows:vg * arows + hd, :] = vT_ref[vg * hd:(vg + 1) * hd, :]
        vaug_ref[vg * arows + hd:(vg + 1) * arows, :] = jnp.ones((ONES_ROWS, vaug_ref.shape[1]), BF16)

    for qi in range(nq):
        toks = slice(qi * tq, (qi + 1) * tq)
        for i in range(nh):
            cols = slice(i * tq, (i + 1) * tq)
            slot = slot_of(i % hpp, grp)
            q = qT_ref[i * hd:(i + 1) * hd, toks]
            zero = jnp.zeros_like(q)
            qaug_ref[qi, 0:hd, cols] = jnp.where(slot == 0, q, zero)
            qaug_ref[qi, hd:2 * hd, cols] = jnp.where(slot == 1, q, zero)
            qaug_ref[qi, BIAS_ROW:SEL_ROW, cols] = jnp.concatenate([qbias_ref[i]] * (tq // LANES), axis=1)
            fill = SEL_ROW
            if has_sel:
                n_rows = negsel_ref.shape[1]
                qaug_ref[qi, SEL_ROW:SEL_ROW + n_rows, cols] = negsel_ref[i % negsel_ref.shape[0], :, toks]
                fill = SEL_ROW + n_rows
            qaug_ref[qi, fill:, cols] = jnp.zeros((MXU_DIM - fill, tq), BF16)

    key_i = lax.broadcasted_iota(jnp.int32, (tk, ncol), 0)
    qry_row = lax.broadcasted_iota(jnp.int32, (1, ncol), 1) % tq
    big = jnp.int32(2 ** 30)

    def visible(kind, valid):
        if kind == "diag":
            return key_i <= jnp.where(valid, qry_row, -1)
        if kind == "lower":
            return key_i > jnp.where(valid, qry_row, big)
        if kind == "middle":
            return key_i >= jnp.where(valid, 0, big)
        return None

    def key_slice(t):
        return pl.ds(pl.multiple_of(ctab_ref[t] * tk, tk), tk)

    def scores(t, buf, kind):
        qi = qtab_ref[t]
        keys = key_slice(t)
        mask = visible(kind, vtab_ref[t] > 0)
        for pr in range(n_pairs):
            kaug = jnp.concatenate([k_ref[keys, pr * LANES:(pr + 1) * LANES], ktab_ref[keys, :]], axis=1)
            cols = slice(pr * pcol, (pr + 1) * pcol)
            s = _dot(kaug, qaug_ref[qi, :, cols])
            if mask is not None:
                s = jnp.where(mask[:, cols], s, NEG_INF)
            s_bufs[buf][:, cols] = s
            cmax_bufs[buf][:, cols] = jnp.max(s, axis=0, keepdims=True)

    def softmax(t, buf):
        qi = qtab_ref[t]
        m = m_ref[qi]
        m_new = jnp.maximum(m, cmax_bufs[buf][...])
        m_ref[qi] = m_new
        alpha_bufs[buf][...] = jnp.exp2(m - m_new)
        p_bufs[buf][...] = jnp.exp2(s_bufs[buf][...] - m_new).astype(BF16)

    def weighted(t, buf, rescale):
        qi = qtab_ref[t]
        keys = key_slice(t)
        for vg in range(n_vgroups):
            cols = slice(vg * vcol, (vg + 1) * vcol)
            pv = _dot(vaug_ref[vg * arows:(vg + 1) * arows, keys], p_bufs[buf][:, cols])
            old = acc_ref[qi, :, cols]
            acc_ref[qi, :, cols] = (alpha_bufs[buf][:, cols] * old if rescale else old) + pv

    def fast_scores(t, buf, kind):
        qi = qtab_ref[t]
        keys = key_slice(t)
        mask = visible(kind, vtab_ref[t] > 0)
        ref = m_ref[qi]
        for pr in range(n_pairs):
            kaug = jnp.concatenate([k_ref[keys, pr * LANES:(pr + 1) * LANES], ktab_ref[keys, :]], axis=1)
            cols = slice(pr * pcol, (pr + 1) * pcol)
            s = _dot(kaug, qaug_ref[qi, :, cols])
            if mask is not None:
                s = jnp.where(mask[:, cols], s, NEG_INF)
            excess_ref[:, cols] = jnp.maximum(excess_ref[:, cols], jnp.max(s, axis=0, keepdims=True) - ref[:, cols])
            p_bufs[buf][:, cols] = jnp.exp2(s - ref[:, cols]).astype(BF16)

    def pipeline(step, max_body):
        first = 0
        for kind in sorted(set(body_kinds), key=body_kinds.index):
            count = body_kinds.count(kind)
            assert body_kinds[first:first + count] == (kind,) * count
            merge = max(m for m in range(1, count + 1) if count % m == 0 and m * unroll <= max_body)
            steps = merge * unroll

            @pl.loop(0, count // merge)
            def _(j, kind=kind, base=first * unroll, steps=steps):
                for u in range(steps):
                    step(base + j * steps + u, u, kind)

            first += count

    last = n_steps - 1

    for qi in range(nq):
        pos = (qi * tq + lax.broadcasted_iota(jnp.int32, (1, tq), 1)).astype(F32)
        for i in range(nh):
            m_ref[qi, :, i * tq:(i + 1) * tq] = slopes_ref[grp * nh + i] * pos
    acc_ref[...] = jnp.zeros(acc_ref.shape, F32)
    excess_ref[...] = jnp.full((1, ncol), NEG_INF, F32)
    fast_scores(0, 0, "diag")

    def fast_step(t, u, kind):
        fast_scores(t + 1, (u + 1) % 2, kind)
        weighted(t, u % 2, rescale=False)

    pipeline(fast_step, ATTN_FAST_BODY)
    weighted(last, last % 2, rescale=False)

    excess = excess_ref[...]
    @pl.when((jnp.max(excess) > SAFE_EXCESS) | (jnp.min(excess) < -SAFE_EXCESS))
    def _():
        m_ref[...] = jnp.full(m_ref.shape, NEG_INF, F32)
        acc_ref[...] = jnp.zeros(acc_ref.shape, F32)
        p_bufs[1][...] = jnp.zeros(p_bufs[1].shape, BF16)
        alpha_bufs[1][...] = jnp.ones((1, ncol), F32)
        scores(0, 0, "diag")

        def safe_step(t, u, kind):
            scores(t + 1, (u + 1) % 2, kind)
            weighted(jnp.maximum(t - 1, 0), (u + 1) % 2, rescale=True)
            softmax(t, u % 2)

        pipeline(safe_step, unroll)
        weighted(last - 1, (last - 1) % 2, rescale=True)
        softmax(last, last % 2)
        weighted(last, last % 2, rescale=True)

    for qi in range(nq):
        for i in range(nh):
            cols = slice(i * tq, (i + 1) * tq)
            l = acc_ref[qi, hd:hd + 1, cols]
            oT_ref[i * hd:(i + 1) * hd, qi * tq:(qi + 1) * tq] = (
                acc_ref[qi, 0:hd, cols] / (jnp.maximum(l, TINY) if clamp_sum else l))


def _attention(slopes2, qT, k_tok, key_table, vT, qbias, negsel, *, batch, n_steps, n_pairs,
               heads_per_pair, heads_per_v, q_block, k_block, v_block, slot_of, window=None,
               clamp_sum, name):
    n = qT.shape[1]
    s = n // batch
    tq, tk, hd = ATTN_TQ, ATTN_TK, HEAD_DIM
    assert tq == tk and s % tq == 0
    nq = s // tq
    nh = n_pairs * heads_per_pair
    assert nh % heads_per_v == 0
    ncol = nh * tq
    arows = hd + ONES_ROWS
    n_vgroups = nh // heads_per_v
    if window is not None:
        assert window % tk == 0
    q_tab, c_tab, v_tab, body_kinds = _attn_schedule(nq, None if window is None else window // tk)
    smem = pl.BlockSpec(memory_space=pltpu.SMEM)
    in_specs = [
        smem, smem, smem, smem,
        pl.BlockSpec((nh * hd, s), lambda b, g: (q_block(g), b)),
        pl.BlockSpec((s, n_pairs * LANES), lambda b, g: (b, k_block(g))),
        pl.BlockSpec((s, LANES), lambda b, g: (0, 0)),
        pl.BlockSpec((n_vgroups * hd, s), lambda b, g: (v_block(g), b)),
        pl.BlockSpec((nh, 16, LANES), lambda b, g: (g, 0, 0)),
    ]
    args = [jnp.asarray(q_tab), jnp.asarray(c_tab), jnp.asarray(v_tab), slopes2, qT, k_tok, key_table, vT,
            qbias]
    if negsel is not None:
        per_step = negsel.shape[0] // n_steps
        in_specs.append(pl.BlockSpec((per_step, negsel.shape[1], s), lambda b, g: (g, 0, b)))
        args.append(negsel)
    vec = lambda: pltpu.VMEM((1, ncol), F32)
    return pl.pallas_call(
        functools.partial(_attn_kernel, n_pairs=n_pairs, heads_per_pair=heads_per_pair,
                          heads_per_v=heads_per_v, has_sel=negsel is not None, slot_of=slot_of,
                          body_kinds=body_kinds, clamp_sum=clamp_sum),
        grid=(batch, n_steps),
        in_specs=in_specs,
        out_specs=pl.BlockSpec((nh * hd, s), lambda b, g: (g, b)),
        out_shape=jax.ShapeDtypeStruct((n_steps * nh * hd, n), F32),
        scratch_shapes=[pltpu.VMEM((nq, MXU_DIM, ncol), BF16),
                        pltpu.VMEM((n_vgroups * arows, s), BF16),
                        pltpu.VMEM((nq, 1, ncol), F32),
                        pltpu.VMEM((nq, arows, ncol), F32),
                        vec(),
                        pltpu.VMEM((tk, ncol), F32), pltpu.VMEM((tk, ncol), F32),
                        vec(), vec(), vec(), vec(),
                        pltpu.VMEM((tk, ncol), BF16), pltpu.VMEM((tk, ncol), BF16)],
        compiler_params=_cparams("parallel", "parallel"),
        name=name,
    )(*args)


def _mix_kernel(h_ref, g_ref, zgT_ref, ocmp_ref, osel_ref, owin_ref, omoba_ref,
                wmerge_ref, wn_ref, wm_ref, wo_ref, o_ref, gate_ref, onsa_ref):
    h = h_ref[...]
    u = _rms(h, g_ref[...]).astype(BF16)
    gate_ref[...] = _sigmoid(zgT_ref[0:gate_ref.shape[0], :])
    hd = HEAD_DIM
    for hh in range(NSA_HEADS):
        rows = slice(hh * hd, (hh + 1) * hd)
        o = None
        for r, branch in enumerate((ocmp_ref, osel_ref, owin_ref)):
            gate = gate_ref[3 * hh + r:3 * hh + r + 1, :]
            term = gate * branch[rows, :]
            o = term if o is None else o + term
        onsa_ref[rows, :] = o.astype(BF16)
    y_n = _dot_tn(onsa_ref[...], wn_ref[...])
    y_m = _dot_tn(omoba_ref[...].astype(BF16), wm_ref[...])
    mixed = (_sigmoid(_dot(u, wmerge_ref[:, :D_MODEL])) * y_n
             + _sigmoid(_dot(u, wmerge_ref[:, D_MODEL:])) * y_m)
    o_ref[...] = h + _dot(mixed.astype(BF16), wo_ref[...])


def _mix(h, g, zgT, ocmpT, oselT, owinT, omobaT, wmerge, wn, wm, wo):
    n = h.shape[0]
    tm = TOKEN_TILE
    rows = lambda width: pl.BlockSpec((tm, width), lambda i: (i, 0))
    cols = lambda height: pl.BlockSpec((height, tm), lambda i: (0, i))
    gate_rows = -(-N_GATE_LOGITS // 8) * 8
    return pl.pallas_call(
        _mix_kernel,
        grid=(n // tm,),
        in_specs=[rows(D_MODEL), _resident((1, D_MODEL)), cols(LANES), cols(NSA_WIDTH), cols(NSA_WIDTH),
                  cols(NSA_WIDTH), cols(MOBA_WIDTH), _resident(wmerge.shape),
                  _resident(wn.shape), _resident(wm.shape), _resident(wo.shape)],
        out_specs=rows(D_MODEL),
        out_shape=jax.ShapeDtypeStruct((n, D_MODEL), F32),
        scratch_shapes=[pltpu.VMEM((gate_rows, tm), F32), pltpu.VMEM((NSA_WIDTH, tm), BF16)],
        compiler_params=_cparams("parallel"),
        name="mix",
    )(h, g, zgT, ocmpT, oselT, owinT, omobaT, wmerge, wn, wm, wo)


def _key_table(s, block):
    pos = np.arange(s)
    tab = np.zeros((s, LANES), np.float32)
    tab[:, 0:N_SLOPE_PIECES] = ((pos // LANES) * LANES)[:, None]
    tab[:, N_SLOPE_PIECES:2 * N_SLOPE_PIECES] = (pos % LANES)[:, None]
    if block is not None:
        tab[pos, (SEL_ROW - BIAS_ROW) + pos // block] = 1.0
    return jnp.asarray(tab, BF16)


def _slope_rows(slopes2):
    s1 = slopes2.astype(BF16)
    r1 = slopes2 - s1.astype(F32)
    s2 = r1.astype(BF16)
    s3 = (r1 - s2.astype(F32)).astype(BF16)
    rows = jnp.stack([s1, s2, s3, s1, s2, s3], axis=1)
    rows = jnp.pad(rows, ((0, 0), (0, 16 - 2 * N_SLOPE_PIECES)))
    return jnp.broadcast_to(rows[:, :, None], rows.shape + (LANES,))


def _compress_weights(pos, w1):
    st, hd, gw, hidden, groups = NSA_CMP_STRIDE, HEAD_DIM, NSA_KV_WIDTH, NSA_CMP_HIDDEN, NSA_KV_GROUPS
    w = jnp.zeros((2, st, gw, groups * hidden), F32)
    for g in range(groups):
        w = w.at[:, :, g * hd:(g + 1) * hd, g * hidden:(g + 1) * hidden].set(w1.reshape(2, st, hd, hidden))
    ps = jnp.tile(pos.reshape(2, st, 1, hd), (1, 1, groups, 1))
    return ps.reshape(2, 1, st * gw), w.reshape(2, st * gw, groups * hidden).astype(BF16)


def kernel(x, p, ffn1_norm, ffn1_w1, ffn1_w3, ffn1_w2, mix_norm, w_in, cmp_pos_k, cmp_w1_k, cmp_w2_k, cmp_pos_v, cmp_w1_v, cmp_w2_v, w_up_nsa, w_up_moba, w_out, ffn2_norm, ffn2_w1, ffn2_w3, ffn2_w2, ple_norm, w_ple_gate, w_ple, final_norm):
    b, s, d = x.shape
    n = b * s
    depth = p.shape[0]
    n_all = NSA_HEADS + MOBA_HEADS
    slopes = jnp.exp2(-8.0 * (jnp.arange(n_all, dtype=F32) + 1.0) / n_all)
    slopes2_n, slopes2_m = slopes[0::2] * LOG2E, slopes[1::2] * LOG2E
    qbias_n, qbias_m = _slope_rows(slopes2_n), _slope_rows(slopes2_m)

    n_cmp_rows = s // NSA_CMP_STRIDE
    n_cmp = (s - NSA_CMP_BLOCK) // NSA_CMP_STRIDE + 1
    n_sel = s // NSA_SEL_BLOCK
    c_start = np.arange(n_cmp_rows) * NSA_CMP_STRIDE
    s_start = np.arange(n_sel) * NSA_SEL_BLOCK
    overlap_t = ((c_start[None, :] <= s_start[:, None] + NSA_SEL_BLOCK - 1)
                 & (c_start[None, :] + NSA_CMP_BLOCK - 1 >= s_start[:, None])
                 & (np.arange(n_cmp_rows)[None, :] < n_cmp))
    overlap_t = jnp.asarray(np.tile(overlap_t, (1, 3)), BF16)
    table_sel = _key_table(s, NSA_SEL_BLOCK)
    table_win = _key_table(s, None)
    table_moba = _key_table(s, MOBA_BLOCK)

    h = x.reshape(n, d)
    for i in range(depth):
        vec = lambda a: a[i].reshape(1, -1)
        h = _ffn(h, vec(ffn1_norm), ffn1_w1[i].astype(BF16), ffn1_w3[i].astype(BF16),
                 ffn1_w2[i].astype(BF16))

        wi = w_in[i]
        col = lambda lo, width: wi[:, lo:lo + width]
        wq_t = jnp.concatenate([col(0, NSA_WIDTH), col(OFF_MOBA_Q, MOBA_WIDTH)], axis=1).T.astype(BF16)
        wk = jnp.concatenate([col(OFF_MOBA_K, MOBA_WIDTH), col(OFF_KSLC, NSA_KV_WIDTH),
                              col(OFF_KWIN, NSA_KV_WIDTH)], axis=1).astype(BF16)
        wc = jnp.stack([col(OFF_KCMP, NSA_KV_WIDTH), col(OFF_VCMP, NSA_KV_WIDTH)]).astype(BF16)
        wv_t = jnp.concatenate([col(OFF_VSLC, NSA_KV_WIDTH), col(OFF_VWIN, NSA_KV_WIDTH),
                                col(OFF_MOBA_V, MOBA_WIDTH)], axis=1).T.astype(BF16)
        wg_t = jnp.pad(col(OFF_GATE_LOGITS, N_GATE_LOGITS), ((0, 0), (0, LANES - N_GATE_LOGITS))).T.astype(BF16)
        w_merge = col(OFF_MERGE, 2 * D_MODEL).astype(BF16)
        qT, k_tok, cmp_tok, vT, zgT = _inproj(h, vec(mix_norm), wq_t, wk, wc, wv_t, wg_t)

        cmp_w = [_compress_weights(ps[i], w1[i]) for ps, w1 in ((cmp_pos_k, cmp_w1_k), (cmp_pos_v, cmp_w1_v))]
        pos = jnp.stack([pw[0] for pw in cmp_w])
        w1c = jnp.stack([pw[1] for pw in cmp_w])
        w2 = jnp.stack([cmp_w2_k[i], cmp_w2_v[i]])
        cmp_rows = cmp_tok.reshape(2, b, n_cmp_rows, NSA_CMP_STRIDE * NSA_KV_WIDTH)
        kc, kcT = _compress(cmp_rows, pos, w1c, w2, w2.transpose(0, 2, 1))

        ocmpT, negsel_n = _cmp_select(slopes2_n, qT, kc, kcT, overlap_t, b)
        hs = ATTN_HEADS_PER_STEP
        q_rows, pair = hs * HEAD_DIM, 2 * HEAD_DIM
        nsa = dict(batch=b, n_steps=NSA_KV_GROUPS, n_pairs=1, heads_per_pair=hs, heads_per_v=hs,
                   q_block=lambda g: QT_ROW_NSA // q_rows + g, slot_of=lambda j, g: g, clamp_sum=True)
        oselT = _attention(slopes2_n, qT, k_tok, table_sel, vT, qbias_n, negsel_n,
                           k_block=lambda g: K_COL_SLC // pair, v_block=lambda g: VT_ROW_SLC // HEAD_DIM + g,
                           name="attn_select", **nsa)
        owinT = _attention(slopes2_n, qT, k_tok, table_win, vT, qbias_n, None,
                           k_block=lambda g: K_COL_WIN // pair, v_block=lambda g: VT_ROW_WIN // HEAD_DIM + g,
                           window=NSA_WINDOW, name="attn_window", **nsa)
        negsel_m = _moba_select(qT, k_tok, b, q_row_block=QT_ROW_MOBA // MOBA_WIDTH,
                                k_col_block=K_COL_MOBA // MOBA_WIDTH)
        omobaT = _attention(slopes2_m, qT, k_tok, table_moba, vT, qbias_m, negsel_m, batch=b,
                            n_steps=MOBA_HEADS // hs, n_pairs=hs // 2, heads_per_pair=2, heads_per_v=1,
                            q_block=lambda g: QT_ROW_MOBA // q_rows + g,
                            k_block=lambda g: K_COL_MOBA // (hs // 2 * pair) + g,
                            v_block=lambda g: VT_ROW_MOBA // q_rows + g, slot_of=lambda j, g: j,
                            clamp_sum=False, name="attn_moba")

        h = _mix(h, vec(mix_norm), zgT, ocmpT, oselT, owinT, omobaT, w_merge,
                 w_up_nsa[i].astype(BF16), w_up_moba[i].astype(BF16), w_out[i].astype(BF16))

        h = _ffn(h, vec(ffn2_norm), ffn2_w1[i].astype(BF16), ffn2_w3[i].astype(BF16), ffn2_w2[i].astype(BF16),
                 tail=(p[i].reshape(n, PLE_DIM), vec(ple_norm), w_ple_gate[i].astype(BF16),
                       w_ple[i].astype(BF16)),
                 final_norm=final_norm.reshape(1, -1) if i + 1 == depth else None)
    return h.reshape(b, s, d)
```

```python
import functools

import numpy as np
import jax
import jax.numpy as jnp
from jax import lax
from jax.experimental import pallas as pl
from jax.experimental.pallas import tpu as pltpu

F32 = jnp.float32
BF16 = jnp.bfloat16

D_MODEL = 1024
HEAD_DIM = 64
NSA_HEADS = 8
NSA_KV_GROUPS = 2
NSA_HEADS_PER_GROUP = NSA_HEADS // NSA_KV_GROUPS
NSA_CMP_BLOCK = 32
NSA_CMP_STRIDE = 16
NSA_CMP_HIDDEN = 128
NSA_SEL_BLOCK = 64
NSA_SEL_TOPK = 16
NSA_WINDOW = 512
MOBA_HEADS = 8
MOBA_BLOCK = 256
MOBA_TOPK = 3
D_FF = 2816
PLE_DIM = 256
RMS_EPS = 1e-6
NEG_INF = -1e30
TINY = 1e-30
FORCE_SCORE = 1e9
LOG2E = float(np.log2(np.e))
Q_SCALE = HEAD_DIM ** -0.5 * LOG2E

NSA_WIDTH = NSA_HEADS * HEAD_DIM
NSA_KV_WIDTH = NSA_KV_GROUPS * HEAD_DIM
MOBA_WIDTH = MOBA_HEADS * HEAD_DIM
N_GATE_LOGITS = 3 * NSA_HEADS
OFF_KCMP = NSA_WIDTH
OFF_VCMP = OFF_KCMP + NSA_KV_WIDTH
OFF_KSLC = OFF_VCMP + NSA_KV_WIDTH
OFF_VSLC = OFF_KSLC + NSA_KV_WIDTH
OFF_KWIN = OFF_VSLC + NSA_KV_WIDTH
OFF_VWIN = OFF_KWIN + NSA_KV_WIDTH
OFF_GATE_LOGITS = OFF_VWIN + NSA_KV_WIDTH
OFF_MOBA_Q = OFF_GATE_LOGITS + N_GATE_LOGITS
OFF_MOBA_K = OFF_MOBA_Q + MOBA_WIDTH
OFF_MOBA_V = OFF_MOBA_K + MOBA_WIDTH
OFF_MERGE = OFF_MOBA_V + MOBA_WIDTH

LANES = 128
MXU_DIM = 256
VMEM_LIMIT = 56 * 1024 * 1024

TOKEN_TILE = 512
FF_CHUNK = 256
ATTN_TQ = 256
ATTN_TK = 256
CMP_TQ = 2048
ATTN_UNROLL = 4
ATTN_FAST_BODY = 28
BIAS_ROW = 2 * HEAD_DIM
SEL_ROW = BIAS_ROW + 16
N_SLOPE_PIECES = 3
MOBA_SEL_ROWS = 16
ONES_ROWS = 16
MASK_BIAS_ROW = {"diag": 0, "lower": 1}
SAFE_EXCESS = 60.0

QT_ROW_NSA, QT_ROW_MOBA = 0, NSA_WIDTH
K_COL_MOBA, K_COL_SLC, K_COL_WIN = 0, MOBA_WIDTH, MOBA_WIDTH + NSA_KV_WIDTH
VT_ROW_SLC, VT_ROW_WIN, VT_ROW_MOBA = 0, NSA_KV_WIDTH, 2 * NSA_KV_WIDTH
ATTN_HEADS_PER_STEP = 4

_NT = (((1,), (1,)), ((), ()))
_TN = (((0,), (0,)), ((), ()))


def _cparams(*sem):
    return pltpu.CompilerParams(dimension_semantics=sem, vmem_limit_bytes=VMEM_LIMIT)


def _resident(shape):
    nd = len(shape)
    return pl.BlockSpec(shape, lambda *_: (0,) * nd, pipeline_mode=pl.Buffered(1))


def _rms(x, g):
    ms = jnp.mean(x * x, axis=-1, keepdims=True)
    return x * lax.rsqrt(ms + RMS_EPS) * g


def _sigmoid(x):
    return 1.0 / (1.0 + jnp.exp(-x))


def _dot(a, b):
    return jnp.dot(a, b, preferred_element_type=F32)


def _dot_f32(a, b):
    return jnp.dot(a, b, preferred_element_type=F32, precision=lax.Precision.HIGHEST)


def _dot_nt(a, b, precision=None):
    return lax.dot_general(a, b, _NT, preferred_element_type=F32, precision=precision)


def _dot_tn(a, b):
    return lax.dot_general(a, b, _TN, preferred_element_type=F32)


def _ffn_kernel(*refs, tail, final, project):
    x_ref, g_ref, w1_ref, w3_ref, w2_ref = refs[:5]
    n_in = 5
    if tail:
        p_ref, gp_ref, wpg_ref, wp_ref = refs[n_in:n_in + 4]
        n_in += 4
    if final:
        gf_ref = refs[n_in]
        n_in += 1
    if project:
        gm_ref, wq_ref, wk_ref, wc_ref, wv_ref, wg_ref = refs[n_in:n_in + 6]
        n_in += 6
    o_ref = refs[n_in]
    acc_ref = refs[-1]
    x = x_ref[...]
    xn = _rms(x, g_ref[...]).astype(BF16)
    for c in range(D_FF // FF_CHUNK):
        cols = slice(c * FF_CHUNK, (c + 1) * FF_CHUNK)
        a = _dot(xn, w1_ref[:, cols])
        b = _dot(xn, w3_ref[:, cols])
        hid = (a * _sigmoid(a) * b).astype(BF16)
        y = _dot(hid, w2_ref[cols, :])
        if c == 0:
            acc_ref[...] = y
        else:
            acc_ref[...] += y
    h = x + 0.5 * acc_ref[...]
    if tail:
        gate = _sigmoid(_dot(_rms(h, gp_ref[...]).astype(BF16), wpg_ref[...]))
        h = h + gate * _dot(p_ref[...].astype(BF16), wp_ref[...])
    if final:
        h = _rms(h, gf_ref[...])
    o_ref[...] = h
    if project:
        qT_ref, k_ref, cmp_ref, vT_ref, zgT_ref = refs[n_in + 1:n_in + 6]
        u = _rms(h, gm_ref[...]).astype(BF16)
        qT_ref[...] = (_dot_nt(wq_ref[...], u) * Q_SCALE).astype(BF16)
        k_ref[...] = _dot(u, wk_ref[...]).astype(BF16)
        for kind in range(2):
            cmp_ref[kind] = _dot(u, wc_ref[kind]).astype(BF16)
        vT_ref[...] = _dot_nt(wv_ref[...], u).astype(BF16)
        zgT_ref[...] = _dot_nt(wg_ref[...], u)


def _ffn(x, g, w1, w3, w2, tail=None, final_norm=None, project=None):
    n = x.shape[0]
    tm = TOKEN_TILE
    row = lambda w: pl.BlockSpec((tm, w), lambda i: (i, 0))
    col = lambda height: pl.BlockSpec((height, tm), lambda i: (0, i))
    args = [x, g, w1, w3, w2]
    specs = [row(D_MODEL), _resident((1, D_MODEL)), _resident(w1.shape), _resident(w3.shape),
             _resident(w2.shape)]
    out_specs = [row(D_MODEL)]
    out_shape = [jax.ShapeDtypeStruct((n, D_MODEL), F32)]
    if tail is not None:
        p, gp, wpg, wp = tail
        args += [p, gp, wpg, wp]
        specs += [row(PLE_DIM), _resident((1, D_MODEL)), _resident(wpg.shape), _resident(wp.shape)]
    if final_norm is not None:
        args.append(final_norm)
        specs.append(_resident((1, D_MODEL)))
    if project is not None:
        gm, wq_t, wk, wc, wv_t, wg_t = project
        args += [gm, wq_t, wk, wc, wv_t, wg_t]
        specs += [_resident(a.shape) for a in project]
        out_specs += [col(wq_t.shape[0]), row(wk.shape[1]),
                      pl.BlockSpec((2, tm, NSA_KV_WIDTH), lambda i: (0, i, 0)),
                      col(wv_t.shape[0]), col(LANES)]
        out_shape += [jax.ShapeDtypeStruct((wq_t.shape[0], n), BF16),
                      jax.ShapeDtypeStruct((n, wk.shape[1]), BF16),
                      jax.ShapeDtypeStruct((2, n, NSA_KV_WIDTH), BF16),
                      jax.ShapeDtypeStruct((wv_t.shape[0], n), BF16),
                      jax.ShapeDtypeStruct((LANES, n), F32)]
    outs = pl.pallas_call(
        functools.partial(_ffn_kernel, tail=tail is not None, final=final_norm is not None,
                          project=project is not None),
        grid=(n // tm,),
        in_specs=specs,
        out_specs=out_specs,
        out_shape=out_shape,
        scratch_shapes=[pltpu.VMEM((tm, D_MODEL), F32)],
        compiler_params=_cparams("parallel"),
        name="ffn" + ("_tail" if tail is not None else "") + ("_project" if project is not None else ""),
    )(*args)
    return outs if project is not None else outs[0]


def _split3(x):
    hi = x.astype(BF16)
    r = x - hi.astype(F32)
    mid = r.astype(BF16)
    return hi, mid, (r - mid.astype(F32)).astype(BF16)


def _compress_kernel(rows_ref, pos_ref, w1_ref, w2_ref, w2t_ref, o_ref, ot_ref):
    c0 = float(np.sqrt(2.0 / np.pi))
    hidden = NSA_CMP_HIDDEN
    for kind in range(2):
        rows = rows_ref[kind].astype(F32)
        first = _dot((rows + pos_ref[kind, 0]).astype(BF16), w1_ref[kind, 0])
        second = _dot((rows + pos_ref[kind, 1]).astype(BF16), w1_ref[kind, 1])
        n_rows = rows.shape[0]
        pre = first + pltpu.roll(second, n_rows - 1, 0)
        hid = pre * (0.5 * (1.0 + jnp.tanh(c0 * (pre + 0.044715 * (pre * pre * pre)))))
        for g in range(NSA_KV_GROUPS):
            hid_g = hid[:, g * hidden:(g + 1) * hidden]
            out = _dot_f32(hid_g, w2_ref[kind])
            o_ref[kind * NSA_KV_GROUPS + g] = jnp.concatenate(_split3(out) + (jnp.zeros(out.shape, BF16),), axis=1)
            ot_ref[kind * NSA_KV_GROUPS + g] = _dot_nt(w2t_ref[kind], hid_g, lax.Precision.HIGHEST).astype(BF16)


def _compress(cmp_rows, pos, w1, w2, w2t):
    _, b, n_rows, width = cmp_rows.shape
    n_out = 2 * NSA_KV_GROUPS
    return pl.pallas_call(
        _compress_kernel,
        grid=(b,),
        in_specs=[
            pl.BlockSpec((2, None, n_rows, width), lambda i: (0, i, 0, 0)),
            _resident(pos.shape), _resident(w1.shape), _resident(w2.shape), _resident(w2t.shape),
        ],
        out_specs=[
            pl.BlockSpec((None, n_out, n_rows, 4 * HEAD_DIM), lambda i: (i, 0, 0, 0)),
            pl.BlockSpec((None, n_out, HEAD_DIM, n_rows), lambda i: (i, 0, 0, 0)),
        ],
        out_shape=[jax.ShapeDtypeStruct((b, n_out, n_rows, 4 * HEAD_DIM), BF16),
                   jax.ShapeDtypeStruct((b, n_out, HEAD_DIM, n_rows), BF16)],
        compiler_params=_cparams("parallel"),
        name="compress",
    )(cmp_rows, pos, w1, w2, w2t)


def _count_rank(score, n_candidates):
    sub = 8
    tiles = [(lo, min(lo + sub, score.shape[0])) for lo in range(0, score.shape[0], sub)]
    ranks = [jnp.zeros((hi - lo,) + score.shape[1:], jnp.int32) for lo, hi in tiles]
    for j in range(n_candidates):
        row = score[j:j + 1, :]
        for n, (lo, hi) in enumerate(tiles):
            part = score[lo:hi]
            if lo > j:
                ahead = row >= part
            elif hi - 1 <= j:
                ahead = row > part
            else:
                after = lo + lax.broadcasted_iota(jnp.int32, part.shape, 0) > j
                ahead = (row > part) | ((row == part) & after)
            ranks[n] = ranks[n] + ahead.astype(jnp.int32)
    return ranks[0] if len(ranks) == 1 else jnp.concatenate(ranks, axis=0)


def _cmp_select_kernel(slopes_ref, qT_ref, kc_ref, vcT_ref, ovT_ref, oT_ref, negsel_ref):
    g = pl.program_id(1)
    qi = pl.program_id(2)
    tq = qT_ref.shape[1]
    n_cmp = kc_ref.shape[0]
    n_sel = ovT_ref.shape[0]
    hd = HEAD_DIM
    t = qi * tq + lax.broadcasted_iota(jnp.int32, (n_cmp, tq), 1)
    blk_c = lax.broadcasted_iota(jnp.int32, (n_cmp, tq), 0)
    dist = t - (blk_c * NSA_CMP_STRIDE + (NSA_CMP_BLOCK - 1))
    visible = dist >= 0
    dist_f = dist.astype(F32)
    kc3 = kc_ref[...]
    vcT = vcT_ref[...]
    p_sum = jnp.zeros((n_cmp, tq), F32)
    for hh in range(NSA_HEADS_PER_GROUP):
        slope2 = slopes_ref[g * NSA_HEADS_PER_GROUP + hh]
        q = qT_ref[hh * hd:(hh + 1) * hd, :]
        s = _dot(kc3, jnp.concatenate([q, q, q, jnp.zeros_like(q)], axis=0)) - slope2 * dist_f
        s = jnp.where(visible, s, NEG_INF)
        m = jnp.max(s, axis=0, keepdims=True)
        e = jnp.exp2(s - m)
        any_visible = m > 0.5 * NEG_INF
        p = e * jnp.where(any_visible, 1.0 / jnp.maximum(jnp.sum(e, axis=0, keepdims=True), TINY), 0.0)
        p_sum = p_sum + p
        oT_ref[hh * hd:(hh + 1) * hd, :] = _dot(vcT, p.astype(BF16))
    imp = _dot(ovT_ref[...], jnp.concatenate(_split3(p_sum), axis=0))
    ts = qi * tq + lax.broadcasted_iota(jnp.int32, (n_sel, tq), 1)
    blk = lax.broadcasted_iota(jnp.int32, (n_sel, tq), 0)
    cur = ts // NSA_SEL_BLOCK
    causal = blk * NSA_SEL_BLOCK <= ts
    forced = (blk == 0) | (blk == cur) | (blk == cur - 1)
    imp = jnp.where(causal, jnp.where(forced, FORCE_SCORE, imp), NEG_INF)
    rank = _count_rank(imp, n_sel)
    chosen = (rank < min(NSA_SEL_TOPK, n_sel)) & causal
    negsel_ref[...] = jnp.where(chosen, 0.0, NEG_INF).astype(BF16)


def _cmp_select(slopes2, qT, kc, vcT, overlap_t, batch):
    n = qT.shape[1]
    tq = min(CMP_TQ, n // batch)
    nq = n // batch // tq
    rows = NSA_HEADS_PER_GROUP * HEAD_DIM
    n_cmp = kc.shape[2]
    n_sel = overlap_t.shape[0]
    return pl.pallas_call(
        _cmp_select_kernel,
        grid=(batch, NSA_KV_GROUPS, nq),
        in_specs=[
            pl.BlockSpec(memory_space=pltpu.SMEM),
            pl.BlockSpec((rows, tq), lambda i, g, q: (g, i * nq + q)),
            pl.BlockSpec((None, None, n_cmp, 4 * HEAD_DIM), lambda i, g, q: (i, g, 0, 0)),
            pl.BlockSpec((None, None, HEAD_DIM, n_cmp), lambda i, g, q: (i, NSA_KV_GROUPS + g, 0, 0)),
            pl.BlockSpec((n_sel, 3 * n_cmp), lambda i, g, q: (0, 0)),
        ],
        out_specs=[
            pl.BlockSpec((rows, tq), lambda i, g, q: (g, i * nq + q)),
            pl.BlockSpec((None, n_sel, tq), lambda i, g, q: (g, 0, i * nq + q)),
        ],
        out_shape=[jax.ShapeDtypeStruct((NSA_WIDTH, n), F32),
                   jax.ShapeDtypeStruct((NSA_KV_GROUPS, n_sel, n), BF16)],
        compiler_params=_cparams("parallel", "parallel", "parallel"),
        name="cmp_select",
    )(slopes2, qT, kc, vcT, overlap_t)


def _moba_select_kernel(qT_ref, k_ref, negsel_ref):
    s = k_ref.shape[0]
    nb = s // MOBA_BLOCK
    hd = HEAD_DIM
    kmean = jnp.mean(k_ref[...].astype(F32).reshape(nb, MOBA_BLOCK, k_ref.shape[1]), axis=1)
    t = lax.broadcasted_iota(jnp.int32, (nb, s), 1)
    blk = lax.broadcasted_iota(jnp.int32, (nb, s), 0)
    cur = t // MOBA_BLOCK
    past = blk < cur
    for j in range(MOBA_HEADS):
        gs = _dot_f32(kmean[:, j * hd:(j + 1) * hd], qT_ref[j * hd:(j + 1) * hd, :].astype(F32))
        gs = jnp.where(past, gs, NEG_INF)
        rank = _count_rank(gs, nb)
        chosen = ((rank < min(MOBA_TOPK, nb - 1)) & past) | (blk == cur)
        negsel_ref[j, 0:nb, :] = jnp.where(chosen, 0.0, NEG_INF).astype(BF16)
        negsel_ref[j, nb:, :] = jnp.zeros((MOBA_SEL_ROWS - nb, s), BF16)


def _moba_select(qT, k_tok, batch, q_row_block, k_col_block):
    n = qT.shape[1]
    s = n // batch
    return pl.pallas_call(
        _moba_select_kernel,
        grid=(batch,),
        in_specs=[pl.BlockSpec((MOBA_WIDTH, s), lambda i: (q_row_block, i)),
                  pl.BlockSpec((s, MOBA_WIDTH), lambda i: (i, k_col_block))],
        out_specs=pl.BlockSpec((MOBA_HEADS, MOBA_SEL_ROWS, s), lambda i: (0, 0, i)),
        out_shape=jax.ShapeDtypeStruct((MOBA_HEADS, MOBA_SEL_ROWS, n), BF16),
        compiler_params=_cparams("parallel"),
        name="moba_select",
    )(qT, k_tok)


def _attn_schedule(nq, window_chunks):
    def padded(pairs, start):
        n_pad = (-(start + len(pairs) - 1)) % ATTN_UNROLL
        return pairs + [(0, 0, 0)] * n_pad

    diag = [(q, q, 1) for q in range(nq)]
    groups = [("diag", padded(diag, 0))]
    if window_chunks is None:
        full = [(q, c, 1) for q in range(nq) for c in range(q)]
        groups.append((None, full))
    else:
        assert window_chunks == 2
        done = len(groups[0][1])
        lower = padded([(q, q - 2, 1) for q in range(2, nq)], done)
        groups.append(("lower", lower))
        groups.append(("middle", [(q, q - 1, 1) for q in range(1, nq)]))
    pairs, kinds = [], []
    for kind, group in groups:
        pairs += group
        kinds += [kind] * len(group)
    n_pad = (-(len(pairs) - 1)) % ATTN_UNROLL
    pairs += [(0, 0, 0)] * n_pad
    kinds += [kinds[-1] if kinds[-1] is not None else "diag"] * n_pad
    body_kinds = []
    for j in range((len(pairs) - 1) // ATTN_UNROLL):
        ks = set(kinds[1 + j * ATTN_UNROLL:1 + (j + 1) * ATTN_UNROLL])
        assert len(ks) == 1, "a loop body must issue scores of one mask kind"
        body_kinds.append(ks.pop())
    table = np.asarray(pairs, np.int32)
    return table[:, 0], table[:, 1], table[:, 2], tuple(body_kinds)


def _attn_kernel(*refs, n_pairs, heads_per_pair, heads_per_v, has_sel, slot_of, body_kinds, clamp_sum):
    n_in = 10 if has_sel else 9
    qtab_ref, ctab_ref, vtab_ref, slopes_ref, qT_ref, k_ref, ktab_ref, vT_ref, qbias_ref = refs[:9]
    negsel_ref = refs[9] if has_sel else None
    oT_ref = refs[n_in]
    qaug_ref, vaug_ref, m_ref, acc_ref, excess_ref = refs[n_in + 1:n_in + 6]
    s_bufs = refs[n_in + 6:n_in + 8]
    cmax_bufs = refs[n_in + 8:n_in + 10]
    alpha_bufs = refs[n_in + 10:n_in + 12]
    p_bufs = refs[n_in + 12:n_in + 14]
    maskbias_ref = refs[n_in + 14]
    grp = pl.program_id(1)
    tq, tk, hd = ATTN_TQ, ATTN_TK, HEAD_DIM
    nq = qaug_ref.shape[0]
    hpp = heads_per_pair
    nh = n_pairs * hpp
    pcol = hpp * tq
    ncol = n_pairs * pcol
    n_vgroups = nh // heads_per_v
    vcol = heads_per_v * tq
    arows = hd + ONES_ROWS
    n_steps = qtab_ref.shape[0]
    unroll = ATTN_UNROLL

    for vg in range(n_vgroups):
        vaug_ref[vg * arows:vg * arows + hd, :] = vT_ref[vg * hd:(vg + 1) * hd, :]
        vaug_ref[vg * arows + hd:(vg + 1) * arows, :] = jnp.ones((ONES_ROWS, vaug_ref.shape[1]), BF16)

    for qi in range(nq):
        toks = slice(qi * tq, (qi + 1) * tq)
        for i in range(nh):
            cols = slice(i * tq, (i + 1) * tq)
            slot = slot_of(i % hpp, grp)
            q = qT_ref[i * hd:(i + 1) * hd, toks]
            zero = jnp.zeros_like(q)
            qaug_ref[qi, 0:hd, cols] = jnp.where(slot == 0, q, zero)
            qaug_ref[qi, hd:2 * hd, cols] = jnp.where(slot == 1, q, zero)
            qaug_ref[qi, BIAS_ROW:SEL_ROW, cols] = jnp.concatenate([qbias_ref[i]] * (tq // LANES), axis=1)
            fill = SEL_ROW
            if has_sel:
                n_rows = negsel_ref.shape[1]
                qaug_ref[qi, SEL_ROW:SEL_ROW + n_rows, cols] = negsel_ref[i % negsel_ref.shape[0], :, toks]
                fill = SEL_ROW + n_rows
            qaug_ref[qi, fill:, cols] = jnp.zeros((MXU_DIM - fill, tq), BF16)

    key_i = lax.broadcasted_iota(jnp.int32, (tk, ncol), 0)
    qry_row = lax.broadcasted_iota(jnp.int32, (1, ncol), 1) % tq
    big = jnp.int32(2 ** 30)

    def visible(kind, valid):
        if kind == "diag":
            return key_i <= jnp.where(valid, qry_row, -1)
        if kind == "lower":
            return key_i > jnp.where(valid, qry_row, big)
        if kind == "middle":
            return key_i >= jnp.where(valid, 0, big)
        return None

    def key_slice(t):
        return pl.ds(pl.multiple_of(ctab_ref[t] * tk, tk), tk)

    def scores(t, buf, kind):
        qi = qtab_ref[t]
        keys = key_slice(t)
        mask = visible(kind, vtab_ref[t] > 0)
        for pr in range(n_pairs):
            kaug = jnp.concatenate([k_ref[keys, pr * LANES:(pr + 1) * LANES], ktab_ref[keys, :]], axis=1)
            cols = slice(pr * pcol, (pr + 1) * pcol)
            s = _dot(kaug, qaug_ref[qi, :, cols])
            if mask is not None:
                s = jnp.where(mask[:, cols], s, NEG_INF)
            s_bufs[buf][:, cols] = s
            cmax_bufs[buf][:, cols] = jnp.max(s, axis=0, keepdims=True)

    def softmax(t, buf):
        qi = qtab_ref[t]
        m = m_ref[qi]
        m_new = jnp.maximum(m, cmax_bufs[buf][...])
        m_ref[qi] = m_new
        alpha_bufs[buf][...] = jnp.exp2(m - m_new)
        p_bufs[buf][...] = jnp.exp2(s_bufs[buf][...] - m_new).astype(BF16)

    def weighted(t, buf, rescale):
        qi = qtab_ref[t]
        keys = key_slice(t)
        for vg in range(n_vgroups):
            cols = slice(vg * vcol, (vg + 1) * vcol)
            pv = _dot(vaug_ref[vg * arows:(vg + 1) * arows, keys], p_bufs[buf][:, cols])
            old = acc_ref[qi, :, cols]
            acc_ref[qi, :, cols] = (alpha_bufs[buf][:, cols] * old if rescale else old) + pv

    def fast_scores(t, buf, kind):
        qi = qtab_ref[t]
        keys = key_slice(t)
        ref = jnp.where(vtab_ref[t] > 0, m_ref[qi], -NEG_INF)
        for pr in range(n_pairs):
            kaug = jnp.concatenate([k_ref[keys, pr * LANES:(pr + 1) * LANES], ktab_ref[keys, :]], axis=1)
            cols = slice(pr * pcol, (pr + 1) * pcol)
            s = _dot(kaug, qaug_ref[qi, :, cols])
            if kind in MASK_BIAS_ROW:
                s = s + maskbias_ref[MASK_BIAS_ROW[kind], :, cols]
            excess_ref[:, cols] = jnp.maximum(excess_ref[:, cols], jnp.max(s, axis=0, keepdims=True) - ref[:, cols])
            p_bufs[buf][:, cols] = jnp.exp2(s - ref[:, cols]).astype(BF16)

    def pipeline(step, max_body):
        first = 0
        for kind in sorted(set(body_kinds), key=body_kinds.index):
            count = body_kinds.count(kind)
            assert body_kinds[first:first + count] == (kind,) * count
            merge = max(m for m in range(1, count + 1) if count % m == 0 and m * unroll <= max_body)
            steps = merge * unroll

            @pl.loop(0, count // merge)
            def _(j, kind=kind, base=first * unroll, steps=steps):
                for u in range(steps):
                    step(base + j * steps + u, u, kind)

            first += count

    last = n_steps - 1

    for qi in range(nq):
        pos = (qi * tq + lax.broadcasted_iota(jnp.int32, (1, tq), 1)).astype(F32)
        for i in range(nh):
            m_ref[qi, :, i * tq:(i + 1) * tq] = slopes_ref[grp * nh + i] * pos
    acc_ref[...] = jnp.zeros(acc_ref.shape, F32)
    excess_ref[...] = jnp.full((1, ncol), NEG_INF, F32)
    for kind, row in MASK_BIAS_ROW.items():
        if kind in body_kinds:
            maskbias_ref[row] = jnp.where(visible(kind, True), 0.0, NEG_INF)
    fast_scores(0, 0, "diag")

    def fast_step(t, u, kind):
        fast_scores(t + 1, (u + 1) % 2, kind)
        weighted(t, u % 2, rescale=False)

    pipeline(fast_step, ATTN_FAST_BODY)
    weighted(last, last % 2, rescale=False)

    excess = excess_ref[...]
    @pl.when((jnp.max(excess) > SAFE_EXCESS) | (jnp.min(excess) < -SAFE_EXCESS))
    def _():
        m_ref[...] = jnp.full(m_ref.shape, NEG_INF, F32)
        acc_ref[...] = jnp.zeros(acc_ref.shape, F32)
        p_bufs[1][...] = jnp.zeros(p_bufs[1].shape, BF16)
        alpha_bufs[1][...] = jnp.ones((1, ncol), F32)
        scores(0, 0, "diag")

        def safe_step(t, u, kind):
            scores(t + 1, (u + 1) % 2, kind)
            weighted(jnp.maximum(t - 1, 0), (u + 1) % 2, rescale=True)
            softmax(t, u % 2)

        pipeline(safe_step, unroll)
        weighted(last - 1, (last - 1) % 2, rescale=True)
        softmax(last, last % 2)
        weighted(last, last % 2, rescale=True)

    for qi in range(nq):
        for i in range(nh):
            cols = slice(i * tq, (i + 1) * tq)
            l = acc_ref[qi, hd:hd + 1, cols]
            oT_ref[i * hd:(i + 1) * hd, qi * tq:(qi + 1) * tq] = (
                acc_ref[qi, 0:hd, cols] / (jnp.maximum(l, TINY) if clamp_sum else l))


def _attention(slopes2, qT, k_tok, key_table, vT, qbias, negsel, *, batch, n_steps, n_pairs,
               heads_per_pair, heads_per_v, q_block, k_block, v_block, slot_of, window=None,
               clamp_sum, name):
    n = qT.shape[1]
    s = n // batch
    tq, tk, hd = ATTN_TQ, ATTN_TK, HEAD_DIM
    assert tq == tk and s % tq == 0
    nq = s // tq
    nh = n_pairs * heads_per_pair
    assert nh % heads_per_v == 0
    ncol = nh * tq
    arows = hd + ONES_ROWS
    n_vgroups = nh // heads_per_v
    if window is not None:
        assert window % tk == 0
    q_tab, c_tab, v_tab, body_kinds = _attn_schedule(nq, None if window is None else window // tk)
    smem = pl.BlockSpec(memory_space=pltpu.SMEM)
    in_specs = [
        smem, smem, smem, smem,
        pl.BlockSpec((nh * hd, s), lambda b, g: (q_block(g), b)),
        pl.BlockSpec((s, n_pairs * LANES), lambda b, g: (b, k_block(g))),
        pl.BlockSpec((s, LANES), lambda b, g: (0, 0)),
        pl.BlockSpec((n_vgroups * hd, s), lambda b, g: (v_block(g), b)),
        pl.BlockSpec((nh, 16, LANES), lambda b, g: (g, 0, 0)),
    ]
    args = [jnp.asarray(q_tab), jnp.asarray(c_tab), jnp.asarray(v_tab), slopes2, qT, k_tok, key_table, vT,
            qbias]
    if negsel is not None:
        per_step = negsel.shape[0] // n_steps
        in_specs.append(pl.BlockSpec((per_step, negsel.shape[1], s), lambda b, g: (g, 0, b)))
        args.append(negsel)
    vec = lambda: pltpu.VMEM((1, ncol), F32)
    return pl.pallas_call(
        functools.partial(_attn_kernel, n_pairs=n_pairs, heads_per_pair=heads_per_pair,
                          heads_per_v=heads_per_v, has_sel=negsel is not None, slot_of=slot_of,
                          body_kinds=body_kinds, clamp_sum=clamp_sum),
        grid=(batch, n_steps),
        in_specs=in_specs,
        out_specs=pl.BlockSpec((nh * hd, s), lambda b, g: (g, b)),
        out_shape=jax.ShapeDtypeStruct((n_steps * nh * hd, n), F32),
        scratch_shapes=[pltpu.VMEM((nq, MXU_DIM, ncol), BF16),
                        pltpu.VMEM((n_vgroups * arows, s), BF16),
                        pltpu.VMEM((nq, 1, ncol), F32),
                        pltpu.VMEM((nq, arows, ncol), F32),
                        vec(),
                        pltpu.VMEM((tk, ncol), F32), pltpu.VMEM((tk, ncol), F32),
                        vec(), vec(), vec(), vec(),
                        pltpu.VMEM((tk, ncol), BF16), pltpu.VMEM((tk, ncol), BF16),
                        pltpu.VMEM((sum(k in body_kinds for k in MASK_BIAS_ROW), tk, ncol), F32)],
        compiler_params=_cparams("parallel", "parallel"),
        name=name,
    )(*args)


def _mix_kernel(h_ref, g_ref, zgT_ref, ocmp_ref, osel_ref, owin_ref, omoba_ref,
                wmerge_ref, wn_ref, wm_ref, wo_ref, o_ref, gate_ref, onsa_ref):
    h = h_ref[...]
    u = _rms(h, g_ref[...]).astype(BF16)
    gate_ref[...] = _sigmoid(zgT_ref[0:gate_ref.shape[0], :])
    hd = HEAD_DIM
    for hh in range(NSA_HEADS):
        rows = slice(hh * hd, (hh + 1) * hd)
        o = None
        for r, branch in enumerate((ocmp_ref, osel_ref, owin_ref)):
            gate = gate_ref[3 * hh + r:3 * hh + r + 1, :]
            term = gate * branch[rows, :]
            o = term if o is None else o + term
        onsa_ref[rows, :] = o.astype(BF16)
    y_n = _dot_tn(onsa_ref[...], wn_ref[...])
    y_m = _dot_tn(omoba_ref[...].astype(BF16), wm_ref[...])
    mixed = (_sigmoid(_dot(u, wmerge_ref[:, :D_MODEL])) * y_n
             + _sigmoid(_dot(u, wmerge_ref[:, D_MODEL:])) * y_m)
    o_ref[...] = h + _dot(mixed.astype(BF16), wo_ref[...])


def _mix(h, g, zgT, ocmpT, oselT, owinT, omobaT, wmerge, wn, wm, wo):
    n = h.shape[0]
    tm = TOKEN_TILE
    rows = lambda width: pl.BlockSpec((tm, width), lambda i: (i, 0))
    cols = lambda height: pl.BlockSpec((height, tm), lambda i: (0, i))
    gate_rows = -(-N_GATE_LOGITS // 8) * 8
    return pl.pallas_call(
        _mix_kernel,
        grid=(n // tm,),
        in_specs=[rows(D_MODEL), _resident((1, D_MODEL)), cols(LANES), cols(NSA_WIDTH), cols(NSA_WIDTH),
                  cols(NSA_WIDTH), cols(MOBA_WIDTH), _resident(wmerge.shape),
                  _resident(wn.shape), _resident(wm.shape), _resident(wo.shape)],
        out_specs=rows(D_MODEL),
        out_shape=jax.ShapeDtypeStruct((n, D_MODEL), F32),
        scratch_shapes=[pltpu.VMEM((gate_rows, tm), F32), pltpu.VMEM((NSA_WIDTH, tm), BF16)],
        compiler_params=_cparams("parallel"),
        name="mix",
    )(h, g, zgT, ocmpT, oselT, owinT, omobaT, wmerge, wn, wm, wo)


def _key_table(s, block):
    pos = np.arange(s)
    tab = np.zeros((s, LANES), np.float32)
    tab[:, 0:N_SLOPE_PIECES] = ((pos // LANES) * LANES)[:, None]
    tab[:, N_SLOPE_PIECES:2 * N_SLOPE_PIECES] = (pos % LANES)[:, None]
    if block is not None:
        tab[pos, (SEL_ROW - BIAS_ROW) + pos // block] = 1.0
    return jnp.asarray(tab, BF16)


def _slope_rows(slopes2):
    s1 = slopes2.astype(BF16)
    r1 = slopes2 - s1.astype(F32)
    s2 = r1.astype(BF16)
    s3 = (r1 - s2.astype(F32)).astype(BF16)
    rows = jnp.stack([s1, s2, s3, s1, s2, s3], axis=1)
    rows = jnp.pad(rows, ((0, 0), (0, 16 - 2 * N_SLOPE_PIECES)))
    return jnp.broadcast_to(rows[:, :, None], rows.shape + (LANES,))


def _compress_weights(pos, w1):
    st, hd, gw, hidden, groups = NSA_CMP_STRIDE, HEAD_DIM, NSA_KV_WIDTH, NSA_CMP_HIDDEN, NSA_KV_GROUPS
    w = jnp.zeros((2, st, gw, groups * hidden), F32)
    for g in range(groups):
        w = w.at[:, :, g * hd:(g + 1) * hd, g * hidden:(g + 1) * hidden].set(w1.reshape(2, st, hd, hidden))
    ps = jnp.tile(pos.reshape(2, st, 1, hd), (1, 1, groups, 1))
    return ps.reshape(2, 1, st * gw), w.reshape(2, st * gw, groups * hidden).astype(BF16)


def kernel(x, p, ffn1_norm, ffn1_w1, ffn1_w3, ffn1_w2, mix_norm, w_in, cmp_pos_k, cmp_w1_k, cmp_w2_k, cmp_pos_v, cmp_w1_v, cmp_w2_v, w_up_nsa, w_up_moba, w_out, ffn2_norm, ffn2_w1, ffn2_w3, ffn2_w2, ple_norm, w_ple_gate, w_ple, final_norm):
    b, s, d = x.shape
    n = b * s
    depth = p.shape[0]
    n_all = NSA_HEADS + MOBA_HEADS
    slopes = jnp.exp2(-8.0 * (jnp.arange(n_all, dtype=F32) + 1.0) / n_all)
    slopes2_n, slopes2_m = slopes[0::2] * LOG2E, slopes[1::2] * LOG2E
    qbias_n, qbias_m = _slope_rows(slopes2_n), _slope_rows(slopes2_m)

    n_cmp_rows = s // NSA_CMP_STRIDE
    n_cmp = (s - NSA_CMP_BLOCK) // NSA_CMP_STRIDE + 1
    n_sel = s // NSA_SEL_BLOCK
    c_start = np.arange(n_cmp_rows) * NSA_CMP_STRIDE
    s_start = np.arange(n_sel) * NSA_SEL_BLOCK
    overlap_t = ((c_start[None, :] <= s_start[:, None] + NSA_SEL_BLOCK - 1)
                 & (c_start[None, :] + NSA_CMP_BLOCK - 1 >= s_start[:, None])
                 & (np.arange(n_cmp_rows)[None, :] < n_cmp))
    overlap_t = jnp.asarray(np.tile(overlap_t, (1, 3)), BF16)
    table_sel = _key_table(s, NSA_SEL_BLOCK)
    table_win = _key_table(s, None)
    table_moba = _key_table(s, MOBA_BLOCK)

    h = x.reshape(n, d)
    for i in range(depth):
        vec = lambda a: a[i].reshape(1, -1)
        wi = w_in[i]
        col = lambda lo, width: wi[:, lo:lo + width]
        wq_t = jnp.concatenate([col(0, NSA_WIDTH), col(OFF_MOBA_Q, MOBA_WIDTH)], axis=1).T.astype(BF16)
        wk = jnp.concatenate([col(OFF_MOBA_K, MOBA_WIDTH), col(OFF_KSLC, NSA_KV_WIDTH),
                              col(OFF_KWIN, NSA_KV_WIDTH)], axis=1).astype(BF16)
        wc = jnp.stack([col(OFF_KCMP, NSA_KV_WIDTH), col(OFF_VCMP, NSA_KV_WIDTH)]).astype(BF16)
        wv_t = jnp.concatenate([col(OFF_VSLC, NSA_KV_WIDTH), col(OFF_VWIN, NSA_KV_WIDTH),
                                col(OFF_MOBA_V, MOBA_WIDTH)], axis=1).T.astype(BF16)
        wg_t = jnp.pad(col(OFF_GATE_LOGITS, N_GATE_LOGITS), ((0, 0), (0, LANES - N_GATE_LOGITS))).T.astype(BF16)
        w_merge = col(OFF_MERGE, 2 * D_MODEL).astype(BF16)
        h, qT, k_tok, cmp_tok, vT, zgT = _ffn(
            h, vec(ffn1_norm), ffn1_w1[i].astype(BF16), ffn1_w3[i].astype(BF16), ffn1_w2[i].astype(BF16),
            project=(vec(mix_norm), wq_t, wk, wc, wv_t, wg_t))

        cmp_w = [_compress_weights(ps[i], w1[i]) for ps, w1 in ((cmp_pos_k, cmp_w1_k), (cmp_pos_v, cmp_w1_v))]
        pos = jnp.stack([pw[0] for pw in cmp_w])
        w1c = jnp.stack([pw[1] for pw in cmp_w])
        w2 = jnp.stack([cmp_w2_k[i], cmp_w2_v[i]])
        cmp_rows = cmp_tok.reshape(2, b, n_cmp_rows, NSA_CMP_STRIDE * NSA_KV_WIDTH)
        kc, kcT = _compress(cmp_rows, pos, w1c, w2, w2.transpose(0, 2, 1))

        ocmpT, negsel_n = _cmp_select(slopes2_n, qT, kc, kcT, overlap_t, b)
        hs = ATTN_HEADS_PER_STEP
        q_rows, pair = hs * HEAD_DIM, 2 * HEAD_DIM
        nsa = dict(batch=b, n_steps=NSA_KV_GROUPS, n_pairs=1, heads_per_pair=hs, heads_per_v=hs,
                   q_block=lambda g: QT_ROW_NSA // q_rows + g, slot_of=lambda j, g: g, clamp_sum=True)
        oselT = _attention(slopes2_n, qT, k_tok, table_sel, vT, qbias_n, negsel_n,
                           k_block=lambda g: K_COL_SLC // pair, v_block=lambda g: VT_ROW_SLC // HEAD_DIM + g,
                           name="attn_select", **nsa)
        owinT = _attention(slopes2_n, qT, k_tok, table_win, vT, qbias_n, None,
                           k_block=lambda g: K_COL_WIN // pair, v_block=lambda g: VT_ROW_WIN // HEAD_DIM + g,
                           window=NSA_WINDOW, name="attn_window", **nsa)
        negsel_m = _moba_select(qT, k_tok, b, q_row_block=QT_ROW_MOBA // MOBA_WIDTH,
                                k_col_block=K_COL_MOBA // MOBA_WIDTH)
        omobaT = _attention(slopes2_m, qT, k_tok, table_moba, vT, qbias_m, negsel_m, batch=b,
                            n_steps=MOBA_HEADS // hs, n_pairs=hs // 2, heads_per_pair=2, heads_per_v=1,
                            q_block=lambda g: QT_ROW_MOBA // q_rows + g,
                            k_block=lambda g: K_COL_MOBA // (hs // 2 * pair) + g,
                            v_block=lambda g: VT_ROW_MOBA // q_rows + g, slot_of=lambda j, g: j,
                            clamp_sum=False, name="attn_moba")

        h = _mix(h, vec(mix_norm), zgT, ocmpT, oselT, owinT, omobaT, w_merge,
                 w_up_nsa[i].astype(BF16), w_up_moba[i].astype(BF16), w_out[i].astype(BF16))

        h = _ffn(h, vec(ffn2_norm), ffn2_w1[i].astype(BF16), ffn2_w3[i].astype(BF16), ffn2_w2[i].astype(BF16),
                 tail=(p[i].reshape(n, PLE_DIM), vec(ple_norm), w_ple_gate[i].astype(BF16),
                       w_ple[i].astype(BF16)),
                 final_norm=final_norm.reshape(1, -1) if i + 1 == depth else None)
    return h.reshape(b, s, d)
```

```python
import functools

import numpy as np
import jax
import jax.numpy as jnp
from jax import lax
from jax.experimental import pallas as pl
from jax.experimental.pallas import tpu as pltpu

F32 = jnp.float32
BF16 = jnp.bfloat16

D_MODEL = 1024
HEAD_DIM = 64
NSA_HEADS = 8
NSA_KV_GROUPS = 2
NSA_HEADS_PER_GROUP = NSA_HEADS // NSA_KV_GROUPS
NSA_CMP_BLOCK = 32
NSA_CMP_STRIDE = 16
NSA_CMP_HIDDEN = 128
NSA_SEL_BLOCK = 64
NSA_SEL_TOPK = 16
NSA_WINDOW = 512
MOBA_HEADS = 8
MOBA_BLOCK = 256
MOBA_TOPK = 3
D_FF = 2816
PLE_DIM = 256
RMS_EPS = 1e-6
NEG_INF = -1e30
TINY = 1e-30
FORCE_SCORE = 1e9
LOG2E = float(np.log2(np.e))
Q_SCALE = HEAD_DIM ** -0.5 * LOG2E

NSA_WIDTH = NSA_HEADS * HEAD_DIM
NSA_KV_WIDTH = NSA_KV_GROUPS * HEAD_DIM
MOBA_WIDTH = MOBA_HEADS * HEAD_DIM
N_GATE_LOGITS = 3 * NSA_HEADS
OFF_KCMP = NSA_WIDTH
OFF_VCMP = OFF_KCMP + NSA_KV_WIDTH
OFF_KSLC = OFF_VCMP + NSA_KV_WIDTH
OFF_VSLC = OFF_KSLC + NSA_KV_WIDTH
OFF_KWIN = OFF_VSLC + NSA_KV_WIDTH
OFF_VWIN = OFF_KWIN + NSA_KV_WIDTH
OFF_GATE_LOGITS = OFF_VWIN + NSA_KV_WIDTH
OFF_MOBA_Q = OFF_GATE_LOGITS + N_GATE_LOGITS
OFF_MOBA_K = OFF_MOBA_Q + MOBA_WIDTH
OFF_MOBA_V = OFF_MOBA_K + MOBA_WIDTH
OFF_MERGE = OFF_MOBA_V + MOBA_WIDTH

LANES = 128
MXU_DIM = 256
VMEM_LIMIT = 56 * 1024 * 1024

TOKEN_TILE = 512
FF_CHUNK = 256
ATTN_TQ = 256
ATTN_TK = 256
CMP_TQ = 2048
ATTN_UNROLL = 2
ATTN_FAST_BODY = 14
ATTN_SAFE_BODY = 4
BIAS_ROW = 2 * HEAD_DIM
SEL_ROW = BIAS_ROW + 16
N_SLOPE_PIECES = 3
MOBA_SEL_ROWS = 16
ONES_ROWS = 16
MASK_BIAS_ROW = {"diag": 0, "lower": 1}
SAFE_EXCESS = 60.0

QT_ROW_NSA, QT_ROW_MOBA = 0, NSA_WIDTH
K_COL_MOBA, K_COL_SLC, K_COL_WIN = 0, MOBA_WIDTH, MOBA_WIDTH + NSA_KV_WIDTH
VT_ROW_SLC, VT_ROW_WIN, VT_ROW_MOBA = 0, NSA_KV_WIDTH, 2 * NSA_KV_WIDTH
ATTN_HEADS_PER_STEP = 4

_NT = (((1,), (1,)), ((), ()))
_TN = (((0,), (0,)), ((), ()))


def _cparams(*sem):
    return pltpu.CompilerParams(dimension_semantics=sem, vmem_limit_bytes=VMEM_LIMIT)


def _resident(shape):
    nd = len(shape)
    return pl.BlockSpec(shape, lambda *_: (0,) * nd, pipeline_mode=pl.Buffered(1))


def _rms(x, g):
    ms = jnp.mean(x * x, axis=-1, keepdims=True)
    return x * lax.rsqrt(ms + RMS_EPS) * g


def _sigmoid(x):
    return 1.0 / (1.0 + jnp.exp(-x))


def _dot(a, b):
    return jnp.dot(a, b, preferred_element_type=F32)


def _dot_f32(a, b):
    return jnp.dot(a, b, preferred_element_type=F32, precision=lax.Precision.HIGHEST)


def _dot_nt(a, b, precision=None):
    return lax.dot_general(a, b, _NT, preferred_element_type=F32, precision=precision)


def _dot_tn(a, b):
    return lax.dot_general(a, b, _TN, preferred_element_type=F32)


def _ffn_kernel(*refs, tail, final, project):
    x_ref, g_ref, w1_ref, w3_ref, w2_ref = refs[:5]
    n_in = 5
    if tail:
        p_ref, gp_ref, wpg_ref, wp_ref = refs[n_in:n_in + 4]
        n_in += 4
    if final:
        gf_ref = refs[n_in]
        n_in += 1
    if project:
        gm_ref, wq_ref, wk_ref, wc_ref, wv_ref, wg_ref = refs[n_in:n_in + 6]
        n_in += 6
    o_ref = refs[n_in]
    acc_ref = refs[-1]
    x = x_ref[...]
    xn = _rms(x, g_ref[...]).astype(BF16)
    for c in range(D_FF // FF_CHUNK):
        cols = slice(c * FF_CHUNK, (c + 1) * FF_CHUNK)
        a = _dot(xn, w1_ref[:, cols])
        b = _dot(xn, w3_ref[:, cols])
        hid = (a * _sigmoid(a) * b).astype(BF16)
        y = _dot(hid, w2_ref[cols, :])
        if c == 0:
            acc_ref[...] = y
        else:
            acc_ref[...] += y
    h = x + 0.5 * acc_ref[...]
    if tail:
        gate = _sigmoid(_dot(_rms(h, gp_ref[...]).astype(BF16), wpg_ref[...]))
        h = h + gate * _dot(p_ref[...].astype(BF16), wp_ref[...])
    if final:
        h = _rms(h, gf_ref[...])
    o_ref[...] = h
    if project:
        qT_ref, k_ref, cmp_ref, vT_ref, zgT_ref = refs[n_in + 1:n_in + 6]
        u = _rms(h, gm_ref[...]).astype(BF16)
        qT_ref[...] = (_dot_nt(wq_ref[...], u) * Q_SCALE).astype(BF16)
        k_ref[...] = _dot(u, wk_ref[...]).astype(BF16)
        for kind in range(2):
            cmp_ref[kind] = _dot(u, wc_ref[kind]).astype(BF16)
        vT_ref[...] = _dot_nt(wv_ref[...], u).astype(BF16)
        zgT_ref[...] = _dot_nt(wg_ref[...], u)


def _ffn(x, g, w1, w3, w2, tail=None, final_norm=None, project=None):
    n = x.shape[0]
    tm = TOKEN_TILE
    row = lambda w: pl.BlockSpec((tm, w), lambda i: (i, 0))
    col = lambda height: pl.BlockSpec((height, tm), lambda i: (0, i))
    args = [x, g, w1, w3, w2]
    specs = [row(D_MODEL), _resident((1, D_MODEL)), _resident(w1.shape), _resident(w3.shape),
             _resident(w2.shape)]
    out_specs = [row(D_MODEL)]
    out_shape = [jax.ShapeDtypeStruct((n, D_MODEL), F32)]
    if tail is not None:
        p, gp, wpg, wp = tail
        args += [p, gp, wpg, wp]
        specs += [row(PLE_DIM), _resident((1, D_MODEL)), _resident(wpg.shape), _resident(wp.shape)]
    if final_norm is not None:
        args.append(final_norm)
        specs.append(_resident((1, D_MODEL)))
    if project is not None:
        gm, wq_t, wk, wc, wv_t, wg_t = project
        args += [gm, wq_t, wk, wc, wv_t, wg_t]
        specs += [_resident(a.shape) for a in project]
        out_specs += [col(wq_t.shape[0]), row(wk.shape[1]),
                      pl.BlockSpec((2, tm, NSA_KV_WIDTH), lambda i: (0, i, 0)),
                      col(wv_t.shape[0]), col(LANES)]
        out_shape += [jax.ShapeDtypeStruct((wq_t.shape[0], n), BF16),
                      jax.ShapeDtypeStruct((n, wk.shape[1]), BF16),
                      jax.ShapeDtypeStruct((2, n, NSA_KV_WIDTH), BF16),
                      jax.ShapeDtypeStruct((wv_t.shape[0], n), BF16),
                      jax.ShapeDtypeStruct((LANES, n), F32)]
    outs = pl.pallas_call(
        functools.partial(_ffn_kernel, tail=tail is not None, final=final_norm is not None,
                          project=project is not None),
        grid=(n // tm,),
        in_specs=specs,
        out_specs=out_specs,
        out_shape=out_shape,
        scratch_shapes=[pltpu.VMEM((tm, D_MODEL), F32)],
        compiler_params=_cparams("parallel"),
        name="ffn" + ("_tail" if tail is not None else "") + ("_project" if project is not None else ""),
    )(*args)
    return outs if project is not None else outs[0]


def _split3(x):
    hi = x.astype(BF16)
    r = x - hi.astype(F32)
    mid = r.astype(BF16)
    return hi, mid, (r - mid.astype(F32)).astype(BF16)


def _compress_kernel(rows_ref, pos_ref, w1_ref, w2_ref, w2t_ref, o_ref, ot_ref):
    c0 = float(np.sqrt(2.0 / np.pi))
    hidden = NSA_CMP_HIDDEN
    for kind in range(2):
        rows = rows_ref[kind].astype(F32)
        first = _dot((rows + pos_ref[kind, 0]).astype(BF16), w1_ref[kind, 0])
        second = _dot((rows + pos_ref[kind, 1]).astype(BF16), w1_ref[kind, 1])
        n_rows = rows.shape[0]
        pre = first + pltpu.roll(second, n_rows - 1, 0)
        hid = pre * (0.5 * (1.0 + jnp.tanh(c0 * (pre + 0.044715 * (pre * pre * pre)))))
        for g in range(NSA_KV_GROUPS):
            hid_g = hid[:, g * hidden:(g + 1) * hidden]
            out = _dot_f32(hid_g, w2_ref[kind])
            o_ref[kind * NSA_KV_GROUPS + g] = jnp.concatenate(_split3(out) + (jnp.zeros(out.shape, BF16),), axis=1)
            ot_ref[kind * NSA_KV_GROUPS + g] = _dot_nt(w2t_ref[kind], hid_g, lax.Precision.HIGHEST).astype(BF16)


def _compress(cmp_rows, pos, w1, w2, w2t):
    _, b, n_rows, width = cmp_rows.shape
    n_out = 2 * NSA_KV_GROUPS
    return pl.pallas_call(
        _compress_kernel,
        grid=(b,),
        in_specs=[
            pl.BlockSpec((2, None, n_rows, width), lambda i: (0, i, 0, 0)),
            _resident(pos.shape), _resident(w1.shape), _resident(w2.shape), _resident(w2t.shape),
        ],
        out_specs=[
            pl.BlockSpec((None, n_out, n_rows, 4 * HEAD_DIM), lambda i: (i, 0, 0, 0)),
            pl.BlockSpec((None, n_out, HEAD_DIM, n_rows), lambda i: (i, 0, 0, 0)),
        ],
        out_shape=[jax.ShapeDtypeStruct((b, n_out, n_rows, 4 * HEAD_DIM), BF16),
                   jax.ShapeDtypeStruct((b, n_out, HEAD_DIM, n_rows), BF16)],
        compiler_params=_cparams("parallel"),
        name="compress",
    )(cmp_rows, pos, w1, w2, w2t)


def _count_rank(score, n_candidates):
    sub = 8
    tiles = [(lo, min(lo + sub, score.shape[0])) for lo in range(0, score.shape[0], sub)]
    ranks = [jnp.zeros((hi - lo,) + score.shape[1:], jnp.int32) for lo, hi in tiles]
    for j in range(n_candidates):
        row = score[j:j + 1, :]
        for n, (lo, hi) in enumerate(tiles):
            part = score[lo:hi]
            if lo > j:
                ahead = row >= part
            elif hi - 1 <= j:
                ahead = row > part
            else:
                after = lo + lax.broadcasted_iota(jnp.int32, part.shape, 0) > j
                ahead = (row > part) | ((row == part) & after)
            ranks[n] = ranks[n] + ahead.astype(jnp.int32)
    return ranks[0] if len(ranks) == 1 else jnp.concatenate(ranks, axis=0)


def _cmp_select_kernel(slopes_ref, qT_ref, kc_ref, vcT_ref, ovT_ref, oT_ref, negsel_ref):
    g = pl.program_id(1)
    qi = pl.program_id(2)
    tq = qT_ref.shape[1]
    n_cmp = kc_ref.shape[0]
    n_sel = ovT_ref.shape[0]
    hd = HEAD_DIM
    t = qi * tq + lax.broadcasted_iota(jnp.int32, (n_cmp, tq), 1)
    blk_c = lax.broadcasted_iota(jnp.int32, (n_cmp, tq), 0)
    dist = t - (blk_c * NSA_CMP_STRIDE + (NSA_CMP_BLOCK - 1))
    visible = dist >= 0
    dist_f = dist.astype(F32)
    kc3 = kc_ref[...]
    vcT = vcT_ref[...]
    p_sum = jnp.zeros((n_cmp, tq), F32)
    for hh in range(NSA_HEADS_PER_GROUP):
        slope2 = slopes_ref[g * NSA_HEADS_PER_GROUP + hh]
        q = qT_ref[hh * hd:(hh + 1) * hd, :]
        s = _dot(kc3, jnp.concatenate([q, q, q, jnp.zeros_like(q)], axis=0)) - slope2 * dist_f
        s = jnp.where(visible, s, NEG_INF)
        m = jnp.max(s, axis=0, keepdims=True)
        e = jnp.exp2(s - m)
        any_visible = m > 0.5 * NEG_INF
        p = e * jnp.where(any_visible, 1.0 / jnp.maximum(jnp.sum(e, axis=0, keepdims=True), TINY), 0.0)
        p_sum = p_sum + p
        oT_ref[hh * hd:(hh + 1) * hd, :] = _dot(vcT, p.astype(BF16))
    imp = _dot(ovT_ref[...], jnp.concatenate(_split3(p_sum), axis=0))
    ts = qi * tq + lax.broadcasted_iota(jnp.int32, (n_sel, tq), 1)
    blk = lax.broadcasted_iota(jnp.int32, (n_sel, tq), 0)
    cur = ts // NSA_SEL_BLOCK
    causal = blk * NSA_SEL_BLOCK <= ts
    forced = (blk == 0) | (blk == cur) | (blk == cur - 1)
    imp = jnp.where(causal, jnp.where(forced, FORCE_SCORE, imp), NEG_INF)
    rank = _count_rank(imp, n_sel)
    chosen = (rank < min(NSA_SEL_TOPK, n_sel)) & causal
    negsel_ref[...] = jnp.where(chosen, 0.0, NEG_INF).astype(BF16)


def _cmp_select(slopes2, qT, kc, vcT, overlap_t, batch):
    n = qT.shape[1]
    tq = min(CMP_TQ, n // batch)
    nq = n // batch // tq
    rows = NSA_HEADS_PER_GROUP * HEAD_DIM
    n_cmp = kc.shape[2]
    n_sel = overlap_t.shape[0]
    return pl.pallas_call(
        _cmp_select_kernel,
        grid=(batch, NSA_KV_GROUPS, nq),
        in_specs=[
            pl.BlockSpec(memory_space=pltpu.SMEM),
            pl.BlockSpec((rows, tq), lambda i, g, q: (g, i * nq + q)),
            pl.BlockSpec((None, None, n_cmp, 4 * HEAD_DIM), lambda i, g, q: (i, g, 0, 0)),
            pl.BlockSpec((None, None, HEAD_DIM, n_cmp), lambda i, g, q: (i, NSA_KV_GROUPS + g, 0, 0)),
            pl.BlockSpec((n_sel, 3 * n_cmp), lambda i, g, q: (0, 0)),
        ],
        out_specs=[
            pl.BlockSpec((rows, tq), lambda i, g, q: (g, i * nq + q)),
            pl.BlockSpec((None, n_sel, tq), lambda i, g, q: (g, 0, i * nq + q)),
        ],
        out_shape=[jax.ShapeDtypeStruct((NSA_WIDTH, n), F32),
                   jax.ShapeDtypeStruct((NSA_KV_GROUPS, n_sel, n), BF16)],
        compiler_params=_cparams("parallel", "parallel", "parallel"),
        name="cmp_select",
    )(slopes2, qT, kc, vcT, overlap_t)


def _moba_select_kernel(qT_ref, k_ref, negsel_ref):
    s = k_ref.shape[0]
    nb = s // MOBA_BLOCK
    hd = HEAD_DIM
    kmean = jnp.mean(k_ref[...].astype(F32).reshape(nb, MOBA_BLOCK, k_ref.shape[1]), axis=1)
    t = lax.broadcasted_iota(jnp.int32, (nb, s), 1)
    blk = lax.broadcasted_iota(jnp.int32, (nb, s), 0)
    cur = t // MOBA_BLOCK
    past = blk < cur
    for j in range(MOBA_HEADS):
        gs = _dot_f32(kmean[:, j * hd:(j + 1) * hd], qT_ref[j * hd:(j + 1) * hd, :].astype(F32))
        gs = jnp.where(past, gs, NEG_INF)
        rank = _count_rank(gs, nb)
        chosen = ((rank < min(MOBA_TOPK, nb - 1)) & past) | (blk == cur)
        negsel_ref[j, 0:nb, :] = jnp.where(chosen, 0.0, NEG_INF).astype(BF16)
        negsel_ref[j, nb:, :] = jnp.zeros((MOBA_SEL_ROWS - nb, s), BF16)


def _moba_select(qT, k_tok, batch, q_row_block, k_col_block):
    n = qT.shape[1]
    s = n // batch
    return pl.pallas_call(
        _moba_select_kernel,
        grid=(batch,),
        in_specs=[pl.BlockSpec((MOBA_WIDTH, s), lambda i: (q_row_block, i)),
                  pl.BlockSpec((s, MOBA_WIDTH), lambda i: (i, k_col_block))],
        out_specs=pl.BlockSpec((MOBA_HEADS, MOBA_SEL_ROWS, s), lambda i: (0, 0, i)),
        out_shape=jax.ShapeDtypeStruct((MOBA_HEADS, MOBA_SEL_ROWS, n), BF16),
        compiler_params=_cparams("parallel"),
        name="moba_select",
    )(qT, k_tok)


def _attn_schedule(nq, window_chunks):
    def padded(pairs, start):
        n_pad = (-(start + len(pairs) - 1)) % ATTN_UNROLL
        return pairs + [(0, 0, 0)] * n_pad

    diag = [(q, q, 1) for q in range(nq)]
    groups = [("diag", padded(diag, 0))]
    if window_chunks is None:
        full = [(q, c, 1) for q in range(nq) for c in range(q)]
        groups.append((None, full))
    else:
        assert window_chunks == 2
        done = len(groups[0][1])
        lower = padded([(q, q - 2, 1) for q in range(2, nq)], done)
        groups.append(("lower", lower))
        groups.append(("middle", [(q, q - 1, 1) for q in range(1, nq)]))
    pairs, kinds = [], []
    for kind, group in groups:
        pairs += group
        kinds += [kind] * len(group)
    n_pad = (-(len(pairs) - 1)) % ATTN_UNROLL
    pairs += [(0, 0, 0)] * n_pad
    kinds += [kinds[-1] if kinds[-1] is not None else "diag"] * n_pad
    body_kinds = []
    for j in range((len(pairs) - 1) // ATTN_UNROLL):
        ks = set(kinds[1 + j * ATTN_UNROLL:1 + (j + 1) * ATTN_UNROLL])
        assert len(ks) == 1, "a loop body must issue scores of one mask kind"
        body_kinds.append(ks.pop())
    table = np.asarray(pairs, np.int32)
    return table[:, 0], table[:, 1], table[:, 2], tuple(body_kinds)


def _attn_kernel(*refs, n_pairs, heads_per_pair, heads_per_v, has_sel, slot_of, body_kinds, clamp_sum):
    n_in = 11 if has_sel else 10
    (qtab_ref, ctab_ref, vtab_ref, slopes_ref, qT_ref, k_ref, ktab_ref, vT_ref, qbias_ref,
     maskbias_ref) = refs[:10]
    negsel_ref = refs[10] if has_sel else None
    oT_ref = refs[n_in]
    qaug_ref, vaug_ref, m_ref, acc_ref, excess_ref = refs[n_in + 1:n_in + 6]
    s_bufs = refs[n_in + 6:n_in + 8]
    cmax_bufs = refs[n_in + 8:n_in + 10]
    alpha_bufs = refs[n_in + 10:n_in + 12]
    p_bufs = refs[n_in + 12:n_in + 14]
    grp = pl.program_id(1)
    tq, tk, hd = ATTN_TQ, ATTN_TK, HEAD_DIM
    nq = qaug_ref.shape[0]
    hpp = heads_per_pair
    nh = n_pairs * hpp
    pcol = hpp * tq
    ncol = n_pairs * pcol
    n_vgroups = nh // heads_per_v
    vcol = heads_per_v * tq
    arows = hd + ONES_ROWS
    n_steps = qtab_ref.shape[0]
    unroll = ATTN_UNROLL

    for vg in range(n_vgroups):
        vaug_ref[vg * arows:vg * arows + hd, :] = vT_ref[vg * hd:(vg + 1) * hd, :]
        vaug_ref[vg * arows + hd:(vg + 1) * arows, :] = jnp.ones((ONES_ROWS, vaug_ref.shape[1]), BF16)

    for qi in range(nq):
        toks = slice(qi * tq, (qi + 1) * tq)
        for i in range(nh):
            cols = slice(i * tq, (i + 1) * tq)
            slot = slot_of(i % hpp, grp)
            q = qT_ref[i * hd:(i + 1) * hd, toks]
            zero = jnp.zeros_like(q)
            qaug_ref[qi, 0:hd, cols] = jnp.where(slot == 0, q, zero)
            qaug_ref[qi, hd:2 * hd, cols] = jnp.where(slot == 1, q, zero)
            qaug_ref[qi, BIAS_ROW:SEL_ROW, cols] = jnp.concatenate([qbias_ref[i]] * (tq // LANES), axis=1)
            fill = SEL_ROW
            if has_sel:
                n_rows = negsel_ref.shape[1]
                qaug_ref[qi, SEL_ROW:SEL_ROW + n_rows, cols] = negsel_ref[i % negsel_ref.shape[0], :, toks]
                fill = SEL_ROW + n_rows
            qaug_ref[qi, fill:, cols] = jnp.zeros((MXU_DIM - fill, tq), BF16)

    key_i = lax.broadcasted_iota(jnp.int32, (tk, ncol), 0)
    qry_row = lax.broadcasted_iota(jnp.int32, (1, ncol), 1) % tq
    big = jnp.int32(2 ** 30)

    def visible(kind, valid):
        if kind == "diag":
            return key_i <= jnp.where(valid, qry_row, -1)
        if kind == "lower":
            return key_i > jnp.where(valid, qry_row, big)
        if kind == "middle":
            return key_i >= jnp.where(valid, 0, big)
        return None

    def key_slice(t):
        return pl.ds(pl.multiple_of(ctab_ref[t] * tk, tk), tk)

    def scores(t, buf, kind):
        qi = qtab_ref[t]
        keys = key_slice(t)
        mask = visible(kind, vtab_ref[t] > 0)
        for pr in range(n_pairs):
            kaug = jnp.concatenate([k_ref[keys, pr * LANES:(pr + 1) * LANES], ktab_ref[keys, :]], axis=1)
            cols = slice(pr * pcol, (pr + 1) * pcol)
            s = _dot(kaug, qaug_ref[qi, :, cols])
            if mask is not None:
                s = jnp.where(mask[:, cols], s, NEG_INF)
            s_bufs[buf][:, cols] = s
            cmax_bufs[buf][:, cols] = jnp.max(s, axis=0, keepdims=True)

    def softmax(t, buf):
        qi = qtab_ref[t]
        m = m_ref[qi]
        m_new = jnp.maximum(m, cmax_bufs[buf][...])
        m_ref[qi] = m_new
        alpha_bufs[buf][...] = jnp.exp2(m - m_new)
        p_bufs[buf][...] = jnp.exp2(s_bufs[buf][...] - m_new).astype(BF16)

    def weighted(t, buf, rescale):
        qi = qtab_ref[t]
        keys = key_slice(t)
        for vg in range(n_vgroups):
            cols = slice(vg * vcol, (vg + 1) * vcol)
            pv = _dot(vaug_ref[vg * arows:(vg + 1) * arows, keys], p_bufs[buf][:, cols])
            old = acc_ref[qi, :, cols]
            acc_ref[qi, :, cols] = (alpha_bufs[buf][:, cols] * old if rescale else old) + pv

    def fast_scores(t, buf, kind):
        qi = qtab_ref[t]
        keys = key_slice(t)
        ref = jnp.where(vtab_ref[t] > 0, m_ref[qi], -NEG_INF)
        for pr in range(n_pairs):
            kaug = jnp.concatenate([k_ref[keys, pr * LANES:(pr + 1) * LANES], ktab_ref[keys, :]], axis=1)
            cols = slice(pr * pcol, (pr + 1) * pcol)
            s = _dot(kaug, qaug_ref[qi, :, cols])
            if kind in MASK_BIAS_ROW:
                s = s + maskbias_ref[MASK_BIAS_ROW[kind], :, cols]
            excess_ref[:, cols] = jnp.maximum(excess_ref[:, cols], jnp.max(s, axis=0, keepdims=True) - ref[:, cols])
            p_bufs[buf][:, cols] = jnp.exp2(s - ref[:, cols]).astype(BF16)

    def pipeline(step, max_body):
        first = 0
        for kind in sorted(set(body_kinds), key=body_kinds.index):
            count = body_kinds.count(kind)
            assert body_kinds[first:first + count] == (kind,) * count
            merge = max(m for m in range(1, count + 1) if count % m == 0 and m * unroll <= max_body)
            steps = merge * unroll

            @pl.loop(0, count // merge)
            def _(j, kind=kind, base=first * unroll, steps=steps):
                for u in range(steps):
                    step(base + j * steps + u, u, kind)

            first += count

    last = n_steps - 1

    for qi in range(nq):
        pos = (qi * tq + lax.broadcasted_iota(jnp.int32, (1, tq), 1)).astype(F32)
        for i in range(nh):
            m_ref[qi, :, i * tq:(i + 1) * tq] = slopes_ref[grp * nh + i] * pos
    acc_ref[...] = jnp.zeros(acc_ref.shape, F32)
    excess_ref[...] = jnp.full((1, ncol), NEG_INF, F32)
    fast_scores(0, 0, "diag")

    def fast_step(t, u, kind):
        fast_scores(t + 1, (u + 1) % 2, kind)
        weighted(t, u % 2, rescale=False)

    pipeline(fast_step, ATTN_FAST_BODY)
    weighted(last, last % 2, rescale=False)

    excess = excess_ref[...]
    @pl.when((jnp.max(excess) > SAFE_EXCESS) | (jnp.min(excess) < -SAFE_EXCESS))
    def _():
        m_ref[...] = jnp.full(m_ref.shape, NEG_INF, F32)
        acc_ref[...] = jnp.zeros(acc_ref.shape, F32)
        p_bufs[1][...] = jnp.zeros(p_bufs[1].shape, BF16)
        alpha_bufs[1][...] = jnp.ones((1, ncol), F32)
        scores(0, 0, "diag")

        def safe_step(t, u, kind):
            scores(t + 1, (u + 1) % 2, kind)
            weighted(jnp.maximum(t - 1, 0), (u + 1) % 2, rescale=True)
            softmax(t, u % 2)

        pipeline(safe_step, ATTN_SAFE_BODY)
        weighted(last - 1, (last - 1) % 2, rescale=True)
        softmax(last, last % 2)
        weighted(last, last % 2, rescale=True)

    for qi in range(nq):
        for i in range(nh):
            cols = slice(i * tq, (i + 1) * tq)
            l = acc_ref[qi, hd:hd + 1, cols]
            oT_ref[i * hd:(i + 1) * hd, qi * tq:(qi + 1) * tq] = (
                acc_ref[qi, 0:hd, cols] * (1.0 / (jnp.maximum(l, TINY) if clamp_sum else l)))


def _attention(slopes2, qT, k_tok, key_table, vT, qbias, negsel, *, batch, n_steps, n_pairs,
               heads_per_pair, heads_per_v, q_block, k_block, v_block, slot_of, window=None,
               clamp_sum, name):
    n = qT.shape[1]
    s = n // batch
    tq, tk, hd = ATTN_TQ, ATTN_TK, HEAD_DIM
    assert tq == tk and s % tq == 0
    nq = s // tq
    nh = n_pairs * heads_per_pair
    assert nh % heads_per_v == 0
    ncol = nh * tq
    arows = hd + ONES_ROWS
    n_vgroups = nh // heads_per_v
    if window is not None:
        assert window % tk == 0
    q_tab, c_tab, v_tab, body_kinds = _attn_schedule(nq, None if window is None else window // tk)
    key_i, qry_i = np.arange(tk)[:, None], (np.arange(ncol) % tq)[None, :]
    tables = {"diag": key_i <= qry_i, "lower": key_i > qry_i}
    mask_bias = jnp.asarray(np.stack([np.where(tables[k], 0.0, NEG_INF) for k in MASK_BIAS_ROW
                                      if k in body_kinds]), F32)
    smem = pl.BlockSpec(memory_space=pltpu.SMEM)
    in_specs = [
        smem, smem, smem, smem,
        pl.BlockSpec((nh * hd, s), lambda b, g: (q_block(g), b)),
        pl.BlockSpec((s, n_pairs * LANES), lambda b, g: (b, k_block(g))),
        pl.BlockSpec((s, LANES), lambda b, g: (0, 0)),
        pl.BlockSpec((n_vgroups * hd, s), lambda b, g: (v_block(g), b)),
        pl.BlockSpec((nh, 16, LANES), lambda b, g: (g, 0, 0)),
        _resident(mask_bias.shape),
    ]
    args = [jnp.asarray(q_tab), jnp.asarray(c_tab), jnp.asarray(v_tab), slopes2, qT, k_tok, key_table, vT,
            qbias, mask_bias]
    if negsel is not None:
        per_step = negsel.shape[0] // n_steps
        in_specs.append(pl.BlockSpec((per_step, negsel.shape[1], s), lambda b, g: (g, 0, b)))
        args.append(negsel)
    vec = lambda: pltpu.VMEM((1, ncol), F32)
    return pl.pallas_call(
        functools.partial(_attn_kernel, n_pairs=n_pairs, heads_per_pair=heads_per_pair,
                          heads_per_v=heads_per_v, has_sel=negsel is not None, slot_of=slot_of,
                          body_kinds=body_kinds, clamp_sum=clamp_sum),
        grid=(batch, n_steps),
        in_specs=in_specs,
        out_specs=pl.BlockSpec((nh * hd, s), lambda b, g: (g, b)),
        out_shape=jax.ShapeDtypeStruct((n_steps * nh * hd, n), F32),
        scratch_shapes=[pltpu.VMEM((nq, MXU_DIM, ncol), BF16),
                        pltpu.VMEM((n_vgroups * arows, s), BF16),
                        pltpu.VMEM((nq, 1, ncol), F32),
                        pltpu.VMEM((nq, arows, ncol), F32),
                        vec(),
                        pltpu.VMEM((tk, ncol), F32), pltpu.VMEM((tk, ncol), F32),
                        vec(), vec(), vec(), vec(),
                        pltpu.VMEM((tk, ncol), BF16), pltpu.VMEM((tk, ncol), BF16)],
        compiler_params=_cparams("parallel", "parallel"),
        name=name,
    )(*args)


def _mix_kernel(h_ref, g_ref, zgT_ref, ocmp_ref, osel_ref, owin_ref, omoba_ref,
                wmerge_ref, wn_ref, wm_ref, wo_ref, o_ref, gate_ref, onsa_ref):
    h = h_ref[...]
    u = _rms(h, g_ref[...]).astype(BF16)
    gate_ref[...] = _sigmoid(zgT_ref[0:gate_ref.shape[0], :])
    hd = HEAD_DIM
    for hh in range(NSA_HEADS):
        rows = slice(hh * hd, (hh + 1) * hd)
        o = None
        for r, branch in enumerate((ocmp_ref, osel_ref, owin_ref)):
            gate = gate_ref[3 * hh + r:3 * hh + r + 1, :]
            term = gate * branch[rows, :]
            o = term if o is None else o + term
        onsa_ref[rows, :] = o.astype(BF16)
    y_n = _dot_tn(onsa_ref[...], wn_ref[...])
    y_m = _dot_tn(omoba_ref[...].astype(BF16), wm_ref[...])
    mixed = (_sigmoid(_dot(u, wmerge_ref[:, :D_MODEL])) * y_n
             + _sigmoid(_dot(u, wmerge_ref[:, D_MODEL:])) * y_m)
    o_ref[...] = h + _dot(mixed.astype(BF16), wo_ref[...])


def _mix(h, g, zgT, ocmpT, oselT, owinT, omobaT, wmerge, wn, wm, wo):
    n = h.shape[0]
    tm = TOKEN_TILE
    rows = lambda width: pl.BlockSpec((tm, width), lambda i: (i, 0))
    cols = lambda height: pl.BlockSpec((height, tm), lambda i: (0, i))
    gate_rows = -(-N_GATE_LOGITS // 8) * 8
    return pl.pallas_call(
        _mix_kernel,
        grid=(n // tm,),
        in_specs=[rows(D_MODEL), _resident((1, D_MODEL)), cols(LANES), cols(NSA_WIDTH), cols(NSA_WIDTH),
                  cols(NSA_WIDTH), cols(MOBA_WIDTH), _resident(wmerge.shape),
                  _resident(wn.shape), _resident(wm.shape), _resident(wo.shape)],
        out_specs=rows(D_MODEL),
        out_shape=jax.ShapeDtypeStruct((n, D_MODEL), F32),
        scratch_shapes=[pltpu.VMEM((gate_rows, tm), F32), pltpu.VMEM((NSA_WIDTH, tm), BF16)],
        compiler_params=_cparams("parallel"),
        name="mix",
    )(h, g, zgT, ocmpT, oselT, owinT, omobaT, wmerge, wn, wm, wo)


def _key_table(s, block):
    pos = np.arange(s)
    tab = np.zeros((s, LANES), np.float32)
    tab[:, 0:N_SLOPE_PIECES] = ((pos // LANES) * LANES)[:, None]
    tab[:, N_SLOPE_PIECES:2 * N_SLOPE_PIECES] = (pos % LANES)[:, None]
    if block is not None:
        tab[pos, (SEL_ROW - BIAS_ROW) + pos // block] = 1.0
    return jnp.asarray(tab, BF16)


def _slope_rows(slopes2):
    s1 = slopes2.astype(BF16)
    r1 = slopes2 - s1.astype(F32)
    s2 = r1.astype(BF16)
    s3 = (r1 - s2.astype(F32)).astype(BF16)
    rows = jnp.stack([s1, s2, s3, s1, s2, s3], axis=1)
    rows = jnp.pad(rows, ((0, 0), (0, 16 - 2 * N_SLOPE_PIECES)))
    return jnp.broadcast_to(rows[:, :, None], rows.shape + (LANES,))


def _compress_weights(pos, w1):
    st, hd, gw, hidden, groups = NSA_CMP_STRIDE, HEAD_DIM, NSA_KV_WIDTH, NSA_CMP_HIDDEN, NSA_KV_GROUPS
    w = jnp.zeros((2, st, gw, groups * hidden), F32)
    for g in range(groups):
        w = w.at[:, :, g * hd:(g + 1) * hd, g * hidden:(g + 1) * hidden].set(w1.reshape(2, st, hd, hidden))
    ps = jnp.tile(pos.reshape(2, st, 1, hd), (1, 1, groups, 1))
    return ps.reshape(2, 1, st * gw), w.reshape(2, st * gw, groups * hidden).astype(BF16)


def kernel(x, p, ffn1_norm, ffn1_w1, ffn1_w3, ffn1_w2, mix_norm, w_in, cmp_pos_k, cmp_w1_k, cmp_w2_k, cmp_pos_v, cmp_w1_v, cmp_w2_v, w_up_nsa, w_up_moba, w_out, ffn2_norm, ffn2_w1, ffn2_w3, ffn2_w2, ple_norm, w_ple_gate, w_ple, final_norm):
    b, s, d = x.shape
    n = b * s
    depth = p.shape[0]
    n_all = NSA_HEADS + MOBA_HEADS
    slopes = jnp.exp2(-8.0 * (jnp.arange(n_all, dtype=F32) + 1.0) / n_all)
    slopes2_n, slopes2_m = slopes[0::2] * LOG2E, slopes[1::2] * LOG2E
    qbias_n, qbias_m = _slope_rows(slopes2_n), _slope_rows(slopes2_m)

    n_cmp_rows = s // NSA_CMP_STRIDE
    n_cmp = (s - NSA_CMP_BLOCK) // NSA_CMP_STRIDE + 1
    n_sel = s // NSA_SEL_BLOCK
    c_start = np.arange(n_cmp_rows) * NSA_CMP_STRIDE
    s_start = np.arange(n_sel) * NSA_SEL_BLOCK
    overlap_t = ((c_start[None, :] <= s_start[:, None] + NSA_SEL_BLOCK - 1)
                 & (c_start[None, :] + NSA_CMP_BLOCK - 1 >= s_start[:, None])
                 & (np.arange(n_cmp_rows)[None, :] < n_cmp))
    overlap_t = jnp.asarray(np.tile(overlap_t, (1, 3)), BF16)
    table_sel = _key_table(s, NSA_SEL_BLOCK)
    table_win = _key_table(s, None)
    table_moba = _key_table(s, MOBA_BLOCK)

    h = x.reshape(n, d)
    for i in range(depth):
        vec = lambda a: a[i].reshape(1, -1)
        wi = w_in[i]
        col = lambda lo, width: wi[:, lo:lo + width]
        wq_t = jnp.concatenate([col(0, NSA_WIDTH), col(OFF_MOBA_Q, MOBA_WIDTH)], axis=1).T.astype(BF16)
        wk = jnp.concatenate([col(OFF_MOBA_K, MOBA_WIDTH), col(OFF_KSLC, NSA_KV_WIDTH),
                              col(OFF_KWIN, NSA_KV_WIDTH)], axis=1).astype(BF16)
        wc = jnp.stack([col(OFF_KCMP, NSA_KV_WIDTH), col(OFF_VCMP, NSA_KV_WIDTH)]).astype(BF16)
        wv_t = jnp.concatenate([col(OFF_VSLC, NSA_KV_WIDTH), col(OFF_VWIN, NSA_KV_WIDTH),
                                col(OFF_MOBA_V, MOBA_WIDTH)], axis=1).T.astype(BF16)
        wg_t = jnp.pad(col(OFF_GATE_LOGITS, N_GATE_LOGITS), ((0, 0), (0, LANES - N_GATE_LOGITS))).T.astype(BF16)
        w_merge = col(OFF_MERGE, 2 * D_MODEL).astype(BF16)
        h, qT, k_tok, cmp_tok, vT, zgT = _ffn(
            h, vec(ffn1_norm), ffn1_w1[i].astype(BF16), ffn1_w3[i].astype(BF16), ffn1_w2[i].astype(BF16),
            project=(vec(mix_norm), wq_t, wk, wc, wv_t, wg_t))

        cmp_w = [_compress_weights(ps[i], w1[i]) for ps, w1 in ((cmp_pos_k, cmp_w1_k), (cmp_pos_v, cmp_w1_v))]
        pos = jnp.stack([pw[0] for pw in cmp_w])
        w1c = jnp.stack([pw[1] for pw in cmp_w])
        w2 = jnp.stack([cmp_w2_k[i], cmp_w2_v[i]])
        cmp_rows = cmp_tok.reshape(2, b, n_cmp_rows, NSA_CMP_STRIDE * NSA_KV_WIDTH)
        kc, kcT = _compress(cmp_rows, pos, w1c, w2, w2.transpose(0, 2, 1))

        ocmpT, negsel_n = _cmp_select(slopes2_n, qT, kc, kcT, overlap_t, b)
        hs = ATTN_HEADS_PER_STEP
        q_rows, pair = hs * HEAD_DIM, 2 * HEAD_DIM
        nsa = dict(batch=b, n_steps=NSA_KV_GROUPS, n_pairs=1, heads_per_pair=hs, heads_per_v=hs,
                   q_block=lambda g: QT_ROW_NSA // q_rows + g, slot_of=lambda j, g: g, clamp_sum=True)
        oselT = _attention(slopes2_n, qT, k_tok, table_sel, vT, qbias_n, negsel_n,
                           k_block=lambda g: K_COL_SLC // pair, v_block=lambda g: VT_ROW_SLC // HEAD_DIM + g,
                           name="attn_select", **nsa)
        owinT = _attention(slopes2_n, qT, k_tok, table_win, vT, qbias_n, None,
                           k_block=lambda g: K_COL_WIN // pair, v_block=lambda g: VT_ROW_WIN // HEAD_DIM + g,
                           window=NSA_WINDOW, name="attn_window", **nsa)
        negsel_m = _moba_select(qT, k_tok, b, q_row_block=QT_ROW_MOBA // MOBA_WIDTH,
                                k_col_block=K_COL_MOBA // MOBA_WIDTH)
        omobaT = _attention(slopes2_m, qT, k_tok, table_moba, vT, qbias_m, negsel_m, batch=b,
                            n_steps=MOBA_HEADS // hs, n_pairs=hs // 2, heads_per_pair=2, heads_per_v=1,
                            q_block=lambda g: QT_ROW_MOBA // q_rows + g,
                            k_block=lambda g: K_COL_MOBA // (hs // 2 * pair) + g,
                            v_block=lambda g: VT_ROW_MOBA // q_rows + g, slot_of=lambda j, g: j,
                            clamp_sum=False, name="attn_moba")

        h = _mix(h, vec(mix_norm), zgT, ocmpT, oselT, owinT, omobaT, w_merge,
                 w_up_nsa[i].astype(BF16), w_up_moba[i].astype(BF16), w_out[i].astype(BF16))

        h = _ffn(h, vec(ffn2_norm), ffn2_w1[i].astype(BF16), ffn2_w3[i].astype(BF16), ffn2_w2[i].astype(BF16),
                 tail=(p[i].reshape(n, PLE_DIM), vec(ple_norm), w_ple_gate[i].astype(BF16),
                       w_ple[i].astype(BF16)),
                 final_norm=final_norm.reshape(1, -1) if i + 1 == depth else None)
    return h.reshape(b, s, d)
```

```python
import functools

import numpy as np
import jax
import jax.numpy as jnp
from jax import lax
from jax.experimental import pallas as pl
from jax.experimental.pallas import tpu as pltpu

F32 = jnp.float32
BF16 = jnp.bfloat16

D_MODEL = 1024
HEAD_DIM = 64
NSA_HEADS = 8
NSA_KV_GROUPS = 2
NSA_HEADS_PER_GROUP = NSA_HEADS // NSA_KV_GROUPS
NSA_CMP_BLOCK = 32
NSA_CMP_STRIDE = 16
NSA_CMP_HIDDEN = 128
NSA_SEL_BLOCK = 64
NSA_SEL_TOPK = 16
NSA_WINDOW = 512
MOBA_HEADS = 8
MOBA_BLOCK = 256
MOBA_TOPK = 3
D_FF = 2816
PLE_DIM = 256
RMS_EPS = 1e-6
NEG_INF = -1e30
TINY = 1e-30
FORCE_SCORE = 1e9
LOG2E = float(np.log2(np.e))
Q_SCALE = HEAD_DIM ** -0.5 * LOG2E

NSA_WIDTH = NSA_HEADS * HEAD_DIM
NSA_KV_WIDTH = NSA_KV_GROUPS * HEAD_DIM
MOBA_WIDTH = MOBA_HEADS * HEAD_DIM
N_GATE_LOGITS = 3 * NSA_HEADS
OFF_KCMP = NSA_WIDTH
OFF_VCMP = OFF_KCMP + NSA_KV_WIDTH
OFF_KSLC = OFF_VCMP + NSA_KV_WIDTH
OFF_VSLC = OFF_KSLC + NSA_KV_WIDTH
OFF_KWIN = OFF_VSLC + NSA_KV_WIDTH
OFF_VWIN = OFF_KWIN + NSA_KV_WIDTH
OFF_GATE_LOGITS = OFF_VWIN + NSA_KV_WIDTH
OFF_MOBA_Q = OFF_GATE_LOGITS + N_GATE_LOGITS
OFF_MOBA_K = OFF_MOBA_Q + MOBA_WIDTH
OFF_MOBA_V = OFF_MOBA_K + MOBA_WIDTH
OFF_MERGE = OFF_MOBA_V + MOBA_WIDTH

LANES = 128
MXU_DIM = 256
VMEM_LIMIT = 56 * 1024 * 1024

TOKEN_TILE = 512
MIX_TILE = 1024
FF_CHUNK = 256
ATTN_TQ = 256
ATTN_TK = 256
CMP_TQ = 2048
ATTN_UNROLL = 2
ATTN_FAST_BODY = 14
ATTN_SAFE_BODY = 4
BIAS_ROW = 2 * HEAD_DIM
SEL_ROW = BIAS_ROW + 16
N_SLOPE_PIECES = 3
MOBA_SEL_ROWS = 16
ONES_ROWS = 16
MASK_BIAS_ROW = {"diag": 0, "lower": 1}
DEAD_QUADRANT = {"diag": (1, 0), "lower": (0, 1)}
SAFE_EXCESS = 60.0

QT_ROW_NSA, QT_ROW_MOBA = 0, NSA_WIDTH
K_COL_MOBA, K_COL_SLC, K_COL_WIN = 0, MOBA_WIDTH, MOBA_WIDTH + NSA_KV_WIDTH
VT_ROW_SLC, VT_ROW_WIN, VT_ROW_MOBA = 0, NSA_KV_WIDTH, 2 * NSA_KV_WIDTH
ATTN_HEADS_PER_STEP = 4

_NT = (((1,), (1,)), ((), ()))
_TN = (((0,), (0,)), ((), ()))


def _cparams(*sem):
    return pltpu.CompilerParams(dimension_semantics=sem, vmem_limit_bytes=VMEM_LIMIT)


def _resident(shape):
    nd = len(shape)
    return pl.BlockSpec(shape, lambda *_: (0,) * nd, pipeline_mode=pl.Buffered(1))


def _rms(x, g):
    ms = jnp.mean(x * x, axis=-1, keepdims=True)
    return x * lax.rsqrt(ms + RMS_EPS) * g


def _sigmoid(x):
    return 1.0 / (1.0 + jnp.exp(-x))


def _dot(a, b):
    return jnp.dot(a, b, preferred_element_type=F32)


def _dot_f32(a, b):
    return jnp.dot(a, b, preferred_element_type=F32, precision=lax.Precision.HIGHEST)


def _dot_nt(a, b, precision=None):
    return lax.dot_general(a, b, _NT, preferred_element_type=F32, precision=precision)


def _dot_tn(a, b):
    return lax.dot_general(a, b, _TN, preferred_element_type=F32)


def _ffn_kernel(*refs, tail, final, project):
    x_ref, g_ref, w1_ref, w3_ref, w2_ref = refs[:5]
    n_in = 5
    if tail:
        p_ref, gp_ref, wpg_ref, wp_ref = refs[n_in:n_in + 4]
        n_in += 4
    if final:
        gf_ref = refs[n_in]
        n_in += 1
    if project:
        gm_ref, wq_ref, wk_ref, wc_ref, wv_ref, wg_ref = refs[n_in:n_in + 6]
        n_in += 6
    o_ref = refs[n_in]
    acc_ref = refs[-1]
    x = x_ref[...]
    xn = _rms(x, g_ref[...]).astype(BF16)
    for c in range(D_FF // FF_CHUNK):
        cols = slice(c * FF_CHUNK, (c + 1) * FF_CHUNK)
        a = _dot(xn, w1_ref[:, cols])
        b = _dot(xn, w3_ref[:, cols])
        hid = (a * _sigmoid(a) * b).astype(BF16)
        y = _dot(hid, w2_ref[cols, :])
        if c == 0:
            acc_ref[...] = y
        else:
            acc_ref[...] += y
    h = x + 0.5 * acc_ref[...]
    if tail:
        gate = _sigmoid(_dot(_rms(h, gp_ref[...]).astype(BF16), wpg_ref[...]))
        h = h + gate * _dot(p_ref[...].astype(BF16), wp_ref[...])
    if final:
        h = _rms(h, gf_ref[...])
    o_ref[...] = h
    if project:
        qT_ref, k_ref, cmp_ref, vT_ref, zgT_ref = refs[n_in + 1:n_in + 6]
        u = _rms(h, gm_ref[...]).astype(BF16)
        qT_ref[...] = (_dot_nt(wq_ref[...], u) * Q_SCALE).astype(BF16)
        k_ref[...] = _dot(u, wk_ref[...]).astype(BF16)
        for kind in range(2):
            cmp_ref[kind] = _dot(u, wc_ref[kind]).astype(BF16)
        vT_ref[...] = _dot_nt(wv_ref[...], u).astype(BF16)
        zgT_ref[...] = _dot_nt(wg_ref[...], u)


def _ffn(x, g, w1, w3, w2, tail=None, final_norm=None, project=None):
    n = x.shape[0]
    tm = TOKEN_TILE
    row = lambda w: pl.BlockSpec((tm, w), lambda i: (i, 0))
    col = lambda height: pl.BlockSpec((height, tm), lambda i: (0, i))
    args = [x, g, w1, w3, w2]
    specs = [row(D_MODEL), _resident((1, D_MODEL)), _resident(w1.shape), _resident(w3.shape),
             _resident(w2.shape)]
    out_specs = [row(D_MODEL)]
    out_shape = [jax.ShapeDtypeStruct((n, D_MODEL), F32)]
    if tail is not None:
        p, gp, wpg, wp = tail
        args += [p, gp, wpg, wp]
        specs += [row(PLE_DIM), _resident((1, D_MODEL)), _resident(wpg.shape), _resident(wp.shape)]
    if final_norm is not None:
        args.append(final_norm)
        specs.append(_resident((1, D_MODEL)))
    if project is not None:
        gm, wq_t, wk, wc, wv_t, wg_t = project
        args += [gm, wq_t, wk, wc, wv_t, wg_t]
        specs += [_resident(a.shape) for a in project]
        out_specs += [col(wq_t.shape[0]), row(wk.shape[1]),
                      pl.BlockSpec((2, tm, NSA_KV_WIDTH), lambda i: (0, i, 0)),
                      col(wv_t.shape[0]), col(LANES)]
        out_shape += [jax.ShapeDtypeStruct((wq_t.shape[0], n), BF16),
                      jax.ShapeDtypeStruct((n, wk.shape[1]), BF16),
                      jax.ShapeDtypeStruct((2, n, NSA_KV_WIDTH), BF16),
                      jax.ShapeDtypeStruct((wv_t.shape[0], n), BF16),
                      jax.ShapeDtypeStruct((LANES, n), F32)]
    outs = pl.pallas_call(
        functools.partial(_ffn_kernel, tail=tail is not None, final=final_norm is not None,
                          project=project is not None),
        grid=(n // tm,),
        in_specs=specs,
        out_specs=out_specs,
        out_shape=out_shape,
        scratch_shapes=[pltpu.VMEM((tm, D_MODEL), F32)],
        compiler_params=_cparams("parallel"),
        name="ffn" + ("_tail" if tail is not None else "") + ("_project" if project is not None else ""),
    )(*args)
    return outs if project is not None else outs[0]


def _split3(x):
    hi = x.astype(BF16)
    r = x - hi.astype(F32)
    mid = r.astype(BF16)
    return hi, mid, (r - mid.astype(F32)).astype(BF16)


def _compress_kernel(rows_ref, pos_ref, w1_ref, w2_ref, w2t_ref, o_ref, ot_ref):
    c0 = float(np.sqrt(2.0 / np.pi))
    hidden = NSA_CMP_HIDDEN
    for kind in range(2):
        rows = rows_ref[kind].astype(F32)
        first = _dot((rows + pos_ref[kind, 0]).astype(BF16), w1_ref[kind, 0])
        second = _dot((rows + pos_ref[kind, 1]).astype(BF16), w1_ref[kind, 1])
        n_rows = rows.shape[0]
        pre = first + pltpu.roll(second, n_rows - 1, 0)
        hid = pre * (0.5 * (1.0 + jnp.tanh(c0 * (pre + 0.044715 * (pre * pre * pre)))))
        for g in range(NSA_KV_GROUPS):
            hid_g = hid[:, g * hidden:(g + 1) * hidden]
            out = _dot_f32(hid_g, w2_ref[kind])
            o_ref[kind * NSA_KV_GROUPS + g] = jnp.concatenate(_split3(out) + (jnp.zeros(out.shape, BF16),), axis=1)
            ot_ref[kind * NSA_KV_GROUPS + g] = _dot_nt(w2t_ref[kind], hid_g, lax.Precision.HIGHEST).astype(BF16)


def _compress(cmp_rows, pos, w1, w2, w2t):
    _, b, n_rows, width = cmp_rows.shape
    n_out = 2 * NSA_KV_GROUPS
    return pl.pallas_call(
        _compress_kernel,
        grid=(b,),
        in_specs=[
            pl.BlockSpec((2, None, n_rows, width), lambda i: (0, i, 0, 0)),
            _resident(pos.shape), _resident(w1.shape), _resident(w2.shape), _resident(w2t.shape),
        ],
        out_specs=[
            pl.BlockSpec((None, n_out, n_rows, 4 * HEAD_DIM), lambda i: (i, 0, 0, 0)),
            pl.BlockSpec((None, n_out, HEAD_DIM, n_rows), lambda i: (i, 0, 0, 0)),
        ],
        out_shape=[jax.ShapeDtypeStruct((b, n_out, n_rows, 4 * HEAD_DIM), BF16),
                   jax.ShapeDtypeStruct((b, n_out, HEAD_DIM, n_rows), BF16)],
        compiler_params=_cparams("parallel"),
        name="compress",
    )(cmp_rows, pos, w1, w2, w2t)


def _count_rank(score, n_candidates):
    sub = 8
    tiles = [(lo, min(lo + sub, score.shape[0])) for lo in range(0, score.shape[0], sub)]
    ranks = [jnp.zeros((hi - lo,) + score.shape[1:], jnp.int32) for lo, hi in tiles]
    for j in range(n_candidates):
        row = score[j:j + 1, :]
        for n, (lo, hi) in enumerate(tiles):
            part = score[lo:hi]
            if lo > j:
                ahead = row >= part
            elif hi - 1 <= j:
                ahead = row > part
            else:
                after = lo + lax.broadcasted_iota(jnp.int32, part.shape, 0) > j
                ahead = (row > part) | ((row == part) & after)
            ranks[n] = ranks[n] + ahead.astype(jnp.int32)
    return ranks[0] if len(ranks) == 1 else jnp.concatenate(ranks, axis=0)


def _cmp_select_kernel(slopes_ref, qT_ref, kc_ref, vcT_ref, ovT_ref, oT_ref, negsel_ref):
    g = pl.program_id(1)
    qi = pl.program_id(2)
    tq = qT_ref.shape[1]
    n_cmp = kc_ref.shape[0]
    n_sel = ovT_ref.shape[0]
    hd = HEAD_DIM
    t = qi * tq + lax.broadcasted_iota(jnp.int32, (n_cmp, tq), 1)
    blk_c = lax.broadcasted_iota(jnp.int32, (n_cmp, tq), 0)
    dist = t - (blk_c * NSA_CMP_STRIDE + (NSA_CMP_BLOCK - 1))
    visible = dist >= 0
    dist_f = dist.astype(F32)
    kc3 = kc_ref[...]
    vcT = vcT_ref[...]
    p_sum = jnp.zeros((n_cmp, tq), F32)
    for hh in range(NSA_HEADS_PER_GROUP):
        slope2 = slopes_ref[g * NSA_HEADS_PER_GROUP + hh]
        q = qT_ref[hh * hd:(hh + 1) * hd, :]
        s = _dot(kc3, jnp.concatenate([q, q, q, jnp.zeros_like(q)], axis=0)) - slope2 * dist_f
        s = jnp.where(visible, s, NEG_INF)
        m = jnp.max(s, axis=0, keepdims=True)
        e = jnp.exp2(s - m)
        any_visible = m > 0.5 * NEG_INF
        p = e * jnp.where(any_visible, 1.0 / jnp.maximum(jnp.sum(e, axis=0, keepdims=True), TINY), 0.0)
        p_sum = p_sum + p
        oT_ref[hh * hd:(hh + 1) * hd, :] = _dot(vcT, p.astype(BF16))
    imp = _dot(ovT_ref[...], jnp.concatenate(_split3(p_sum), axis=0))
    ts = qi * tq + lax.broadcasted_iota(jnp.int32, (n_sel, tq), 1)
    blk = lax.broadcasted_iota(jnp.int32, (n_sel, tq), 0)
    cur = ts // NSA_SEL_BLOCK
    causal = blk * NSA_SEL_BLOCK <= ts
    forced = (blk == 0) | (blk == cur) | (blk == cur - 1)
    imp = jnp.where(causal, jnp.where(forced, FORCE_SCORE, imp), NEG_INF)
    rank = _count_rank(imp, n_sel)
    chosen = (rank < min(NSA_SEL_TOPK, n_sel)) & causal
    negsel_ref[...] = jnp.where(chosen, 0.0, NEG_INF).astype(BF16)


def _cmp_select(slopes2, qT, kc, vcT, overlap_t, batch):
    n = qT.shape[1]
    tq = min(CMP_TQ, n // batch)
    nq = n // batch // tq
    rows = NSA_HEADS_PER_GROUP * HEAD_DIM
    n_cmp = kc.shape[2]
    n_sel = overlap_t.shape[0]
    return pl.pallas_call(
        _cmp_select_kernel,
        grid=(batch, NSA_KV_GROUPS, nq),
        in_specs=[
            pl.BlockSpec(memory_space=pltpu.SMEM),
            pl.BlockSpec((rows, tq), lambda i, g, q: (g, i * nq + q)),
            pl.BlockSpec((None, None, n_cmp, 4 * HEAD_DIM), lambda i, g, q: (i, g, 0, 0)),
            pl.BlockSpec((None, None, HEAD_DIM, n_cmp), lambda i, g, q: (i, NSA_KV_GROUPS + g, 0, 0)),
            pl.BlockSpec((n_sel, 3 * n_cmp), lambda i, g, q: (0, 0)),
        ],
        out_specs=[
            pl.BlockSpec((rows, tq), lambda i, g, q: (g, i * nq + q)),
            pl.BlockSpec((None, n_sel, tq), lambda i, g, q: (g, 0, i * nq + q)),
        ],
        out_shape=[jax.ShapeDtypeStruct((NSA_WIDTH, n), F32),
                   jax.ShapeDtypeStruct((NSA_KV_GROUPS, n_sel, n), BF16)],
        compiler_params=_cparams("parallel", "parallel", "parallel"),
        name="cmp_select",
    )(slopes2, qT, kc, vcT, overlap_t)


def _moba_select_kernel(qT_ref, k_ref, negsel_ref):
    s = k_ref.shape[0]
    nb = s // MOBA_BLOCK
    hd = HEAD_DIM
    kmean = jnp.mean(k_ref[...].astype(F32).reshape(nb, MOBA_BLOCK, k_ref.shape[1]), axis=1)
    t = lax.broadcasted_iota(jnp.int32, (nb, s), 1)
    blk = lax.broadcasted_iota(jnp.int32, (nb, s), 0)
    cur = t // MOBA_BLOCK
    past = blk < cur
    for j in range(MOBA_HEADS):
        gs = _dot_f32(kmean[:, j * hd:(j + 1) * hd], qT_ref[j * hd:(j + 1) * hd, :].astype(F32))
        gs = jnp.where(past, gs, NEG_INF)
        rank = _count_rank(gs, nb)
        chosen = ((rank < min(MOBA_TOPK, nb - 1)) & past) | (blk == cur)
        negsel_ref[j, 0:nb, :] = jnp.where(chosen, 0.0, NEG_INF).astype(BF16)
        negsel_ref[j, nb:, :] = jnp.zeros((MOBA_SEL_ROWS - nb, s), BF16)


def _moba_select(qT, k_tok, batch, q_row_block, k_col_block):
    n = qT.shape[1]
    s = n // batch
    return pl.pallas_call(
        _moba_select_kernel,
        grid=(batch,),
        in_specs=[pl.BlockSpec((MOBA_WIDTH, s), lambda i: (q_row_block, i)),
                  pl.BlockSpec((s, MOBA_WIDTH), lambda i: (i, k_col_block))],
        out_specs=pl.BlockSpec((MOBA_HEADS, MOBA_SEL_ROWS, s), lambda i: (0, 0, i)),
        out_shape=jax.ShapeDtypeStruct((MOBA_HEADS, MOBA_SEL_ROWS, n), BF16),
        compiler_params=_cparams("parallel"),
        name="moba_select",
    )(qT, k_tok)


def _attn_schedule(nq, window_chunks):
    def padded(pairs, start):
        n_pad = (-(start + len(pairs) - 1)) % ATTN_UNROLL
        return pairs + [(0, 0, 0)] * n_pad

    diag = [(q, q, 1) for q in range(nq)]
    groups = [("diag", padded(diag, 0))]
    if window_chunks is None:
        full = [(q, c, 1) for q in range(nq) for c in range(q)]
        groups.append((None, full))
    else:
        assert window_chunks == 2
        done = len(groups[0][1])
        lower = padded([(q, q - 2, 1) for q in range(2, nq)], done)
        groups.append(("lower", lower))
        groups.append(("middle", [(q, q - 1, 1) for q in range(1, nq)]))
    pairs, kinds = [], []
    for kind, group in groups:
        pairs += group
        kinds += [kind] * len(group)
    n_pad = (-(len(pairs) - 1)) % ATTN_UNROLL
    pairs += [(0, 0, 0)] * n_pad
    kinds += [kinds[-1] if kinds[-1] is not None else "diag"] * n_pad
    body_kinds = []
    for j in range((len(pairs) - 1) // ATTN_UNROLL):
        ks = set(kinds[1 + j * ATTN_UNROLL:1 + (j + 1) * ATTN_UNROLL])
        assert len(ks) == 1, "a loop body must issue scores of one mask kind"
        body_kinds.append(ks.pop())
    table = np.asarray(pairs, np.int32)
    return table[:, 0], table[:, 1], table[:, 2], tuple(body_kinds)


def _attn_kernel(*refs, n_pairs, heads_per_pair, heads_per_v, has_sel, slot_of, body_kinds, clamp_sum):
    n_in = 11 if has_sel else 10
    (qtab_ref, ctab_ref, vtab_ref, slopes_ref, qT_ref, k_ref, ktab_ref, vT_ref, qbias_ref,
     maskbias_ref) = refs[:10]
    negsel_ref = refs[10] if has_sel else None
    oT_ref = refs[n_in]
    qaug_ref, vaug_ref, m_ref, acc_ref, excess_ref = refs[n_in + 1:n_in + 6]
    s_bufs = refs[n_in + 6:n_in + 8]
    cmax_bufs = refs[n_in + 8:n_in + 10]
    alpha_bufs = refs[n_in + 10:n_in + 12]
    p_bufs = refs[n_in + 12:n_in + 14]
    grp = pl.program_id(1)
    tq, tk, hd = ATTN_TQ, ATTN_TK, HEAD_DIM
    nq = qaug_ref.shape[0]
    hpp = heads_per_pair
    nh = n_pairs * hpp
    pcol = hpp * tq
    ncol = n_pairs * pcol
    n_vgroups = nh // heads_per_v
    vcol = heads_per_v * tq
    arows = hd + ONES_ROWS
    n_steps = qtab_ref.shape[0]
    unroll = ATTN_UNROLL

    for vg in range(n_vgroups):
        vaug_ref[vg * arows:vg * arows + hd, :] = vT_ref[vg * hd:(vg + 1) * hd, :]
        vaug_ref[vg * arows + hd:(vg + 1) * arows, :] = jnp.ones((ONES_ROWS, vaug_ref.shape[1]), BF16)

    for qi in range(nq):
        toks = slice(qi * tq, (qi + 1) * tq)
        for i in range(nh):
            cols = slice(i * tq, (i + 1) * tq)
            slot = slot_of(i % hpp, grp)
            q = qT_ref[i * hd:(i + 1) * hd, toks]
            zero = jnp.zeros_like(q)
            qaug_ref[qi, 0:hd, cols] = jnp.where(slot == 0, q, zero)
            qaug_ref[qi, hd:2 * hd, cols] = jnp.where(slot == 1, q, zero)
            qaug_ref[qi, BIAS_ROW:SEL_ROW, cols] = jnp.concatenate([qbias_ref[i]] * (tq // LANES), axis=1)
            fill = SEL_ROW
            if has_sel:
                n_rows = negsel_ref.shape[1]
                qaug_ref[qi, SEL_ROW:SEL_ROW + n_rows, cols] = negsel_ref[i % negsel_ref.shape[0], :, toks]
                fill = SEL_ROW + n_rows
            qaug_ref[qi, fill:, cols] = jnp.zeros((MXU_DIM - fill, tq), BF16)

    key_i = lax.broadcasted_iota(jnp.int32, (tk, ncol), 0)
    qry_row = lax.broadcasted_iota(jnp.int32, (1, ncol), 1) % tq
    big = jnp.int32(2 ** 30)

    def visible(kind, valid):
        if kind == "diag":
            return key_i <= jnp.where(valid, qry_row, -1)
        if kind == "lower":
            return key_i > jnp.where(valid, qry_row, big)
        if kind == "middle":
            return key_i >= jnp.where(valid, 0, big)
        return None

    def key_slice(t):
        return pl.ds(pl.multiple_of(ctab_ref[t] * tk, tk), tk)

    def scores(t, buf, kind):
        qi = qtab_ref[t]
        keys = key_slice(t)
        mask = visible(kind, vtab_ref[t] > 0)
        for pr in range(n_pairs):
            kaug = jnp.concatenate([k_ref[keys, pr * LANES:(pr + 1) * LANES], ktab_ref[keys, :]], axis=1)
            cols = slice(pr * pcol, (pr + 1) * pcol)
            s = _dot(kaug, qaug_ref[qi, :, cols])
            if mask is not None:
                s = jnp.where(mask[:, cols], s, NEG_INF)
            s_bufs[buf][:, cols] = s
            cmax_bufs[buf][:, cols] = jnp.max(s, axis=0, keepdims=True)

    def softmax(t, buf):
        qi = qtab_ref[t]
        m = m_ref[qi]
        m_new = jnp.maximum(m, cmax_bufs[buf][...])
        m_ref[qi] = m_new
        alpha_bufs[buf][...] = jnp.exp2(m - m_new)
        p_bufs[buf][...] = jnp.exp2(s_bufs[buf][...] - m_new).astype(BF16)

    def weighted(t, buf, rescale):
        qi = qtab_ref[t]
        keys = key_slice(t)
        for vg in range(n_vgroups):
            cols = slice(vg * vcol, (vg + 1) * vcol)
            pv = _dot(vaug_ref[vg * arows:(vg + 1) * arows, keys], p_bufs[buf][:, cols])
            old = acc_ref[qi, :, cols]
            acc_ref[qi, :, cols] = (alpha_bufs[buf][:, cols] * old if rescale else old) + pv

    def fast_scores(t, buf, kind):
        qi = qtab_ref[t]
        keys = key_slice(t)
        ref = jnp.where(vtab_ref[t] > 0, m_ref[qi], -NEG_INF)
        for pr in range(n_pairs):
            kaug = jnp.concatenate([k_ref[keys, pr * LANES:(pr + 1) * LANES], ktab_ref[keys, :]], axis=1)
            cols = slice(pr * pcol, (pr + 1) * pcol)
            s = _dot(kaug, qaug_ref[qi, :, cols])
            if kind in MASK_BIAS_ROW:
                s = s + maskbias_ref[MASK_BIAS_ROW[kind], :, cols]
            excess_ref[:, cols] = jnp.maximum(excess_ref[:, cols], jnp.max(s, axis=0, keepdims=True) - ref[:, cols])
            dead = DEAD_QUADRANT.get(kind)
            if dead is None:
                p_bufs[buf][:, cols] = jnp.exp2(s - ref[:, cols]).astype(BF16)
                continue
            hk, hq = tk // 2, tq // 2
            for j in range(hpp):
                for kh in range(2):
                    for qh in range(2):
                        lo = j * tq + qh * hq
                        rows, sub = slice(kh * hk, (kh + 1) * hk), slice(lo, lo + hq)
                        out = slice(pr * pcol + lo, pr * pcol + lo + hq)
                        if (kh, qh) == dead:
                            p_bufs[buf][rows, out] = jnp.zeros((hk, hq), BF16)
                        else:
                            p_bufs[buf][rows, out] = jnp.exp2(s[rows, sub] - ref[:, out]).astype(BF16)

    def pipeline(step, max_body):
        first = 0
        for kind in sorted(set(body_kinds), key=body_kinds.index):
            count = body_kinds.count(kind)
            assert body_kinds[first:first + count] == (kind,) * count
            merge = max(m for m in range(1, count + 1) if count % m == 0 and m * unroll <= max_body)
            steps = merge * unroll

            @pl.loop(0, count // merge)
            def _(j, kind=kind, base=first * unroll, steps=steps):
                for u in range(steps):
                    step(base + j * steps + u, u, kind)

            first += count

    last = n_steps - 1

    for qi in range(nq):
        pos = (qi * tq + lax.broadcasted_iota(jnp.int32, (1, tq), 1)).astype(F32)
        for i in range(nh):
            m_ref[qi, :, i * tq:(i + 1) * tq] = slopes_ref[grp * nh + i] * pos
    acc_ref[...] = jnp.zeros(acc_ref.shape, F32)
    excess_ref[...] = jnp.full((1, ncol), NEG_INF, F32)
    fast_scores(0, 0, "diag")

    def fast_step(t, u, kind):
        fast_scores(t + 1, (u + 1) % 2, kind)
        weighted(t, u % 2, rescale=False)

    pipeline(fast_step, ATTN_FAST_BODY)
    weighted(last, last % 2, rescale=False)

    excess = excess_ref[...]
    @pl.when((jnp.max(excess) > SAFE_EXCESS) | (jnp.min(excess) < -SAFE_EXCESS))
    def _():
        m_ref[...] = jnp.full(m_ref.shape, NEG_INF, F32)
        acc_ref[...] = jnp.zeros(acc_ref.shape, F32)
        p_bufs[1][...] = jnp.zeros(p_bufs[1].shape, BF16)
        alpha_bufs[1][...] = jnp.ones((1, ncol), F32)
        scores(0, 0, "diag")

        def safe_step(t, u, kind):
            scores(t + 1, (u + 1) % 2, kind)
            weighted(jnp.maximum(t - 1, 0), (u + 1) % 2, rescale=True)
            softmax(t, u % 2)

        pipeline(safe_step, ATTN_SAFE_BODY)
        weighted(last - 1, (last - 1) % 2, rescale=True)
        softmax(last, last % 2)
        weighted(last, last % 2, rescale=True)

    for qi in range(nq):
        for i in range(nh):
            cols = slice(i * tq, (i + 1) * tq)
            l = acc_ref[qi, hd:hd + 1, cols]
            oT_ref[i * hd:(i + 1) * hd, qi * tq:(qi + 1) * tq] = (
                acc_ref[qi, 0:hd, cols] * (1.0 / (jnp.maximum(l, TINY) if clamp_sum else l)))


def _attention(slopes2, qT, k_tok, key_table, vT, qbias, negsel, *, batch, n_steps, n_pairs,
               heads_per_pair, heads_per_v, q_block, k_block, v_block, slot_of, window=None,
               clamp_sum, name):
    n = qT.shape[1]
    s = n // batch
    tq, tk, hd = ATTN_TQ, ATTN_TK, HEAD_DIM
    assert tq == tk and s % tq == 0
    nq = s // tq
    nh = n_pairs * heads_per_pair
    assert nh % heads_per_v == 0
    ncol = nh * tq
    arows = hd + ONES_ROWS
    n_vgroups = nh // heads_per_v
    if window is not None:
        assert window % tk == 0
    q_tab, c_tab, v_tab, body_kinds = _attn_schedule(nq, None if window is None else window // tk)
    key_i, qry_i = np.arange(tk)[:, None], (np.arange(ncol) % tq)[None, :]
    tables = {"diag": key_i <= qry_i, "lower": key_i > qry_i}
    mask_bias = jnp.asarray(np.stack([np.where(tables[k], 0.0, NEG_INF) for k in MASK_BIAS_ROW
                                      if k in body_kinds]), F32)
    smem = pl.BlockSpec(memory_space=pltpu.SMEM)
    in_specs = [
        smem, smem, smem, smem,
        pl.BlockSpec((nh * hd, s), lambda b, g: (q_block(g), b)),
        pl.BlockSpec((s, n_pairs * LANES), lambda b, g: (b, k_block(g))),
        pl.BlockSpec((s, LANES), lambda b, g: (0, 0)),
        pl.BlockSpec((n_vgroups * hd, s), lambda b, g: (v_block(g), b)),
        pl.BlockSpec((nh, 16, LANES), lambda b, g: (g, 0, 0)),
        _resident(mask_bias.shape),
    ]
    args = [jnp.asarray(q_tab), jnp.asarray(c_tab), jnp.asarray(v_tab), slopes2, qT, k_tok, key_table, vT,
            qbias, mask_bias]
    if negsel is not None:
        per_step = negsel.shape[0] // n_steps
        in_specs.append(pl.BlockSpec((per_step, negsel.shape[1], s), lambda b, g: (g, 0, b)))
        args.append(negsel)
    vec = lambda: pltpu.VMEM((1, ncol), F32)
    return pl.pallas_call(
        functools.partial(_attn_kernel, n_pairs=n_pairs, heads_per_pair=heads_per_pair,
                          heads_per_v=heads_per_v, has_sel=negsel is not None, slot_of=slot_of,
                          body_kinds=body_kinds, clamp_sum=clamp_sum),
        grid=(batch, n_steps),
        in_specs=in_specs,
        out_specs=pl.BlockSpec((nh * hd, s), lambda b, g: (g, b)),
        out_shape=jax.ShapeDtypeStruct((n_steps * nh * hd, n), F32),
        scratch_shapes=[pltpu.VMEM((nq, MXU_DIM, ncol), BF16),
                        pltpu.VMEM((n_vgroups * arows, s), BF16),
                        pltpu.VMEM((nq, 1, ncol), F32),
                        pltpu.VMEM((nq, arows, ncol), F32),
                        vec(),
                        pltpu.VMEM((tk, ncol), F32), pltpu.VMEM((tk, ncol), F32),
                        vec(), vec(), vec(), vec(),
                        pltpu.VMEM((tk, ncol), BF16), pltpu.VMEM((tk, ncol), BF16)],
        compiler_params=_cparams("parallel", "parallel"),
        name=name,
    )(*args)


def _mix_kernel(h_ref, g_ref, zgT_ref, ocmp_ref, osel_ref, owin_ref, omoba_ref,
                wmerge_ref, wn_ref, wm_ref, wo_ref, o_ref, gate_ref, onsa_ref):
    h = h_ref[...]
    u = _rms(h, g_ref[...]).astype(BF16)
    gate_ref[...] = _sigmoid(zgT_ref[0:gate_ref.shape[0], :])
    hd = HEAD_DIM
    for hh in range(NSA_HEADS):
        rows = slice(hh * hd, (hh + 1) * hd)
        o = None
        for r, branch in enumerate((ocmp_ref, osel_ref, owin_ref)):
            gate = gate_ref[3 * hh + r:3 * hh + r + 1, :]
            term = gate * branch[rows, :]
            o = term if o is None else o + term
        onsa_ref[rows, :] = o.astype(BF16)
    y_n = _dot_tn(onsa_ref[...], wn_ref[...])
    y_m = _dot_tn(omoba_ref[...].astype(BF16), wm_ref[...])
    mixed = (_sigmoid(_dot(u, wmerge_ref[:, :D_MODEL])) * y_n
             + _sigmoid(_dot(u, wmerge_ref[:, D_MODEL:])) * y_m)
    o_ref[...] = h + _dot(mixed.astype(BF16), wo_ref[...])


def _mix(h, g, zgT, ocmpT, oselT, owinT, omobaT, wmerge, wn, wm, wo):
    n = h.shape[0]
    tm = MIX_TILE
    rows = lambda width: pl.BlockSpec((tm, width), lambda i: (i, 0))
    cols = lambda height: pl.BlockSpec((height, tm), lambda i: (0, i))
    gate_rows = -(-N_GATE_LOGITS // 8) * 8
    return pl.pallas_call(
        _mix_kernel,
        grid=(n // tm,),
        in_specs=[rows(D_MODEL), _resident((1, D_MODEL)), cols(LANES), cols(NSA_WIDTH), cols(NSA_WIDTH),
                  cols(NSA_WIDTH), cols(MOBA_WIDTH), _resident(wmerge.shape),
                  _resident(wn.shape), _resident(wm.shape), _resident(wo.shape)],
        out_specs=rows(D_MODEL),
        out_shape=jax.ShapeDtypeStruct((n, D_MODEL), F32),
        scratch_shapes=[pltpu.VMEM((gate_rows, tm), F32), pltpu.VMEM((NSA_WIDTH, tm), BF16)],
        compiler_params=_cparams("parallel"),
        name="mix",
    )(h, g, zgT, ocmpT, oselT, owinT, omobaT, wmerge, wn, wm, wo)


def _key_table(s, block):
    pos = np.arange(s)
    tab = np.zeros((s, LANES), np.float32)
    tab[:, 0:N_SLOPE_PIECES] = ((pos // LANES) * LANES)[:, None]
    tab[:, N_SLOPE_PIECES:2 * N_SLOPE_PIECES] = (pos % LANES)[:, None]
    if block is not None:
        tab[pos, (SEL_ROW - BIAS_ROW) + pos // block] = 1.0
    return jnp.asarray(tab, BF16)


def _slope_rows(slopes2):
    s1 = slopes2.astype(BF16)
    r1 = slopes2 - s1.astype(F32)
    s2 = r1.astype(BF16)
    s3 = (r1 - s2.astype(F32)).astype(BF16)
    rows = jnp.stack([s1, s2, s3, s1, s2, s3], axis=1)
    rows = jnp.pad(rows, ((0, 0), (0, 16 - 2 * N_SLOPE_PIECES)))
    return jnp.broadcast_to(rows[:, :, None], rows.shape + (LANES,))


def _compress_weights(pos, w1):
    st, hd, gw, hidden, groups = NSA_CMP_STRIDE, HEAD_DIM, NSA_KV_WIDTH, NSA_CMP_HIDDEN, NSA_KV_GROUPS
    w = jnp.zeros((2, st, gw, groups * hidden), F32)
    for g in range(groups):
        w = w.at[:, :, g * hd:(g + 1) * hd, g * hidden:(g + 1) * hidden].set(w1.reshape(2, st, hd, hidden))
    ps = jnp.tile(pos.reshape(2, st, 1, hd), (1, 1, groups, 1))
    return ps.reshape(2, 1, st * gw), w.reshape(2, st * gw, groups * hidden).astype(BF16)


def kernel(x, p, ffn1_norm, ffn1_w1, ffn1_w3, ffn1_w2, mix_norm, w_in, cmp_pos_k, cmp_w1_k, cmp_w2_k, cmp_pos_v, cmp_w1_v, cmp_w2_v, w_up_nsa, w_up_moba, w_out, ffn2_norm, ffn2_w1, ffn2_w3, ffn2_w2, ple_norm, w_ple_gate, w_ple, final_norm):
    b, s, d = x.shape
    n = b * s
    depth = p.shape[0]
    n_all = NSA_HEADS + MOBA_HEADS
    slopes = jnp.exp2(-8.0 * (jnp.arange(n_all, dtype=F32) + 1.0) / n_all)
    slopes2_n, slopes2_m = slopes[0::2] * LOG2E, slopes[1::2] * LOG2E
    qbias_n, qbias_m = _slope_rows(slopes2_n), _slope_rows(slopes2_m)

    n_cmp_rows = s // NSA_CMP_STRIDE
    n_cmp = (s - NSA_CMP_BLOCK) // NSA_CMP_STRIDE + 1
    n_sel = s // NSA_SEL_BLOCK
    c_start = np.arange(n_cmp_rows) * NSA_CMP_STRIDE
    s_start = np.arange(n_sel) * NSA_SEL_BLOCK
    overlap_t = ((c_start[None, :] <= s_start[:, None] + NSA_SEL_BLOCK - 1)
                 & (c_start[None, :] + NSA_CMP_BLOCK - 1 >= s_start[:, None])
                 & (np.arange(n_cmp_rows)[None, :] < n_cmp))
    overlap_t = jnp.asarray(np.tile(overlap_t, (1, 3)), BF16)
    table_sel = _key_table(s, NSA_SEL_BLOCK)
    table_win = _key_table(s, None)
    table_moba = _key_table(s, MOBA_BLOCK)

    h = x.reshape(n, d)
    for i in range(depth):
        vec = lambda a: a[i].reshape(1, -1)
        wi = w_in[i]
        col = lambda lo, width: wi[:, lo:lo + width]
        wq_t = jnp.concatenate([col(0, NSA_WIDTH), col(OFF_MOBA_Q, MOBA_WIDTH)], axis=1).T.astype(BF16)
        wk = jnp.concatenate([col(OFF_MOBA_K, MOBA_WIDTH), col(OFF_KSLC, NSA_KV_WIDTH),
                              col(OFF_KWIN, NSA_KV_WIDTH)], axis=1).astype(BF16)
        wc = jnp.stack([col(OFF_KCMP, NSA_KV_WIDTH), col(OFF_VCMP, NSA_KV_WIDTH)]).astype(BF16)
        wv_t = jnp.concatenate([col(OFF_VSLC, NSA_KV_WIDTH), col(OFF_VWIN, NSA_KV_WIDTH),
                                col(OFF_MOBA_V, MOBA_WIDTH)], axis=1).T.astype(BF16)
        wg_t = jnp.pad(col(OFF_GATE_LOGITS, N_GATE_LOGITS), ((0, 0), (0, LANES - N_GATE_LOGITS))).T.astype(BF16)
        w_merge = col(OFF_MERGE, 2 * D_MODEL).astype(BF16)
        h, qT, k_tok, cmp_tok, vT, zgT = _ffn(
            h, vec(ffn1_norm), ffn1_w1[i].astype(BF16), ffn1_w3[i].astype(BF16), ffn1_w2[i].astype(BF16),
            project=(vec(mix_norm), wq_t, wk, wc, wv_t, wg_t))

        cmp_w = [_compress_weights(ps[i], w1[i]) for ps, w1 in ((cmp_pos_k, cmp_w1_k), (cmp_pos_v, cmp_w1_v))]
        pos = jnp.stack([pw[0] for pw in cmp_w])
        w1c = jnp.stack([pw[1] for pw in cmp_w])
        w2 = jnp.stack([cmp_w2_k[i], cmp_w2_v[i]])
        cmp_rows = cmp_tok.reshape(2, b, n_cmp_rows, NSA_CMP_STRIDE * NSA_KV_WIDTH)
        kc, kcT = _compress(cmp_rows, pos, w1c, w2, w2.transpose(0, 2, 1))

        ocmpT, negsel_n = _cmp_select(slopes2_n, qT, kc, kcT, overlap_t, b)
        hs = ATTN_HEADS_PER_STEP
        q_rows, pair = hs * HEAD_DIM, 2 * HEAD_DIM
        nsa = dict(batch=b, n_steps=NSA_KV_GROUPS, n_pairs=1, heads_per_pair=hs, heads_per_v=hs,
                   q_block=lambda g: QT_ROW_NSA // q_rows + g, slot_of=lambda j, g: g, clamp_sum=True)
        oselT = _attention(slopes2_n, qT, k_tok, table_sel, vT, qbias_n, negsel_n,
                           k_block=lambda g: K_COL_SLC // pair, v_block=lambda g: VT_ROW_SLC // HEAD_DIM + g,
                           name="attn_select", **nsa)
        owinT = _attention(slopes2_n, qT, k_tok, table_win, vT, qbias_n, None,
                           k_block=lambda g: K_COL_WIN // pair, v_block=lambda g: VT_ROW_WIN // HEAD_DIM + g,
                           window=NSA_WINDOW, name="attn_window", **nsa)
        negsel_m = _moba_select(qT, k_tok, b, q_row_block=QT_ROW_MOBA // MOBA_WIDTH,
                                k_col_block=K_COL_MOBA // MOBA_WIDTH)
        omobaT = _attention(slopes2_m, qT, k_tok, table_moba, vT, qbias_m, negsel_m, batch=b,
                            n_steps=MOBA_HEADS // hs, n_pairs=hs // 2, heads_per_pair=2, heads_per_v=1,
                            q_block=lambda g: QT_ROW_MOBA // q_rows + g,
                            k_block=lambda g: K_COL_MOBA // (hs // 2 * pair) + g,
                            v_block=lambda g: VT_ROW_MOBA // q_rows + g, slot_of=lambda j, g: j,
                            clamp_sum=False, name="attn_moba")

        h = _mix(h, vec(mix_norm), zgT, ocmpT, oselT, owinT, omobaT, w_merge,
                 w_up_nsa[i].astype(BF16), w_up_moba[i].astype(BF16), w_out[i].astype(BF16))

        h = _ffn(h, vec(ffn2_norm), ffn2_w1[i].astype(BF16), ffn2_w3[i].astype(BF16), ffn2_w2[i].astype(BF16),
                 tail=(p[i].reshape(n, PLE_DIM), vec(ple_norm), w_ple_gate[i].astype(BF16),
                       w_ple[i].astype(BF16)),
                 final_norm=final_norm.reshape(1, -1) if i + 1 == depth else None)
    return h.reshape(b, s, d)
```

```python
import functools

import numpy as np
import jax
import jax.numpy as jnp
from jax import lax
from jax.experimental import pallas as pl
from jax.experimental.pallas import tpu as pltpu

F32 = jnp.float32
BF16 = jnp.bfloat16

D_MODEL = 1024
HEAD_DIM = 64
NSA_HEADS = 8
NSA_KV_GROUPS = 2
NSA_HEADS_PER_GROUP = NSA_HEADS // NSA_KV_GROUPS
NSA_CMP_BLOCK = 32
NSA_CMP_STRIDE = 16
NSA_CMP_HIDDEN = 128
NSA_SEL_BLOCK = 64
NSA_SEL_TOPK = 16
NSA_WINDOW = 512
MOBA_HEADS = 8
MOBA_BLOCK = 256
MOBA_TOPK = 3
D_FF = 2816
PLE_DIM = 256
RMS_EPS = 1e-6
NEG_INF = -1e30
TINY = 1e-30
FORCE_SCORE = 1e9
LOG2E = float(np.log2(np.e))
Q_SCALE = HEAD_DIM ** -0.5 * LOG2E

NSA_WIDTH = NSA_HEADS * HEAD_DIM
NSA_KV_WIDTH = NSA_KV_GROUPS * HEAD_DIM
MOBA_WIDTH = MOBA_HEADS * HEAD_DIM
N_GATE_LOGITS = 3 * NSA_HEADS
OFF_KCMP = NSA_WIDTH
OFF_VCMP = OFF_KCMP + NSA_KV_WIDTH
OFF_KSLC = OFF_VCMP + NSA_KV_WIDTH
OFF_VSLC = OFF_KSLC + NSA_KV_WIDTH
OFF_KWIN = OFF_VSLC + NSA_KV_WIDTH
OFF_VWIN = OFF_KWIN + NSA_KV_WIDTH
OFF_GATE_LOGITS = OFF_VWIN + NSA_KV_WIDTH
OFF_MOBA_Q = OFF_GATE_LOGITS + N_GATE_LOGITS
OFF_MOBA_K = OFF_MOBA_Q + MOBA_WIDTH
OFF_MOBA_V = OFF_MOBA_K + MOBA_WIDTH
OFF_MERGE = OFF_MOBA_V + MOBA_WIDTH

LANES = 128
MXU_DIM = 256
VMEM_LIMIT = 56 * 1024 * 1024

TOKEN_TILE = 512
FF_CHUNK = 256
ATTN_TQ = 256
ATTN_TK = 256
CMP_TQ = 2048
ATTN_UNROLL = 2
ATTN_FAST_BODY = 28
ATTN_SAFE_BODY = 4
BIAS_ROW = 2 * HEAD_DIM
SEL_ROW = BIAS_ROW + 16
N_SLOPE_PIECES = 3
MOBA_SEL_ROWS = 16
ONES_ROWS = 16
MASK_BIAS_ROW = {"diag": 0, "lower": 1}
SAFE_EXCESS = 60.0

QT_ROW_NSA, QT_ROW_MOBA = 0, NSA_WIDTH
K_COL_MOBA, K_COL_SLC, K_COL_WIN = 0, MOBA_WIDTH, MOBA_WIDTH + NSA_KV_WIDTH
VT_ROW_SLC, VT_ROW_WIN, VT_ROW_MOBA = 0, NSA_KV_WIDTH, 2 * NSA_KV_WIDTH
ATTN_HEADS_PER_STEP = 4

_NT = (((1,), (1,)), ((), ()))
_TN = (((0,), (0,)), ((), ()))


def _cparams(*sem):
    return pltpu.CompilerParams(dimension_semantics=sem, vmem_limit_bytes=VMEM_LIMIT)


def _resident(shape):
    nd = len(shape)
    return pl.BlockSpec(shape, lambda *_: (0,) * nd, pipeline_mode=pl.Buffered(1))


def _rms(x, g):
    ms = jnp.mean(x * x, axis=-1, keepdims=True)
    return x * lax.rsqrt(ms + RMS_EPS) * g


def _sigmoid(x):
    return 1.0 / (1.0 + jnp.exp(-x))


def _dot(a, b):
    return jnp.dot(a, b, preferred_element_type=F32)


def _dot_f32(a, b):
    return jnp.dot(a, b, preferred_element_type=F32, precision=lax.Precision.HIGHEST)


def _dot_nt(a, b, precision=None):
    return lax.dot_general(a, b, _NT, preferred_element_type=F32, precision=precision)


def _dot_tn(a, b):
    return lax.dot_general(a, b, _TN, preferred_element_type=F32)


def _ffn_kernel(*refs, tail, final, project):
    x_ref, g_ref, w1_ref, w3_ref, w2_ref = refs[:5]
    n_in = 5
    if tail:
        p_ref, gp_ref, wpg_ref, wp_ref = refs[n_in:n_in + 4]
        n_in += 4
    if final:
        gf_ref = refs[n_in]
        n_in += 1
    if project:
        gm_ref, wq_ref, wk_ref, wc_ref, wv_ref, wg_ref = refs[n_in:n_in + 6]
        n_in += 6
    o_ref = refs[n_in]
    acc_ref = refs[-1]
    x = x_ref[...]
    xn = _rms(x, g_ref[...]).astype(BF16)
    for c in range(D_FF // FF_CHUNK):
        cols = slice(c * FF_CHUNK, (c + 1) * FF_CHUNK)
        a = _dot(xn, w1_ref[:, cols])
        b = _dot(xn, w3_ref[:, cols])
        hid = (a * _sigmoid(a) * b).astype(BF16)
        y = _dot(hid, w2_ref[cols, :])
        if c == 0:
            acc_ref[...] = y
        else:
            acc_ref[...] += y
    h = x + 0.5 * acc_ref[...]
    if tail:
        gate = _sigmoid(_dot(_rms(h, gp_ref[...]).astype(BF16), wpg_ref[...]))
        h = h + gate * _dot(p_ref[...].astype(BF16), wp_ref[...])
    if final:
        h = _rms(h, gf_ref[...])
    o_ref[...] = h
    if project:
        qT_ref, k_ref, cmp_ref, vT_ref, zgT_ref = refs[n_in + 1:n_in + 6]
        u = _rms(h, gm_ref[...]).astype(BF16)
        qT_ref[...] = (_dot_nt(wq_ref[...], u) * Q_SCALE).astype(BF16)
        k_ref[...] = _dot(u, wk_ref[...]).astype(BF16)
        for kind in range(2):
            cmp_ref[kind] = _dot(u, wc_ref[kind]).astype(BF16)
        vT_ref[...] = _dot_nt(wv_ref[...], u).astype(BF16)
        zgT_ref[...] = _dot_nt(wg_ref[...], u)


def _ffn(x, g, w1, w3, w2, tail=None, final_norm=None, project=None):
    n = x.shape[0]
    tm = TOKEN_TILE
    row = lambda w: pl.BlockSpec((tm, w), lambda i: (i, 0))
    col = lambda height: pl.BlockSpec((height, tm), lambda i: (0, i))
    args = [x, g, w1, w3, w2]
    specs = [row(D_MODEL), _resident((1, D_MODEL)), _resident(w1.shape), _resident(w3.shape),
             _resident(w2.shape)]
    out_specs = [row(D_MODEL)]
    out_shape = [jax.ShapeDtypeStruct((n, D_MODEL), F32)]
    if tail is not None:
        p, gp, wpg, wp = tail
        args += [p, gp, wpg, wp]
        specs += [row(PLE_DIM), _resident((1, D_MODEL)), _resident(wpg.shape), _resident(wp.shape)]
    if final_norm is not None:
        args.append(final_norm)
        specs.append(_resident((1, D_MODEL)))
    if project is not None:
        gm, wq_t, wk, wc, wv_t, wg_t = project
        args += [gm, wq_t, wk, wc, wv_t, wg_t]
        specs += [_resident(a.shape) for a in project]
        out_specs += [col(wq_t.shape[0]), row(wk.shape[1]),
                      pl.BlockSpec((2, tm, NSA_KV_WIDTH), lambda i: (0, i, 0)),
                      col(wv_t.shape[0]), col(LANES)]
        out_shape += [jax.ShapeDtypeStruct((wq_t.shape[0], n), BF16),
                      jax.ShapeDtypeStruct((n, wk.shape[1]), BF16),
                      jax.ShapeDtypeStruct((2, n, NSA_KV_WIDTH), BF16),
                      jax.ShapeDtypeStruct((wv_t.shape[0], n), BF16),
                      jax.ShapeDtypeStruct((LANES, n), F32)]
    outs = pl.pallas_call(
        functools.partial(_ffn_kernel, tail=tail is not None, final=final_norm is not None,
                          project=project is not None),
        grid=(n // tm,),
        in_specs=specs,
        out_specs=out_specs,
        out_shape=out_shape,
        scratch_shapes=[pltpu.VMEM((tm, D_MODEL), F32)],
        compiler_params=_cparams("parallel"),
        name="ffn" + ("_tail" if tail is not None else "") + ("_project" if project is not None else ""),
    )(*args)
    return outs if project is not None else outs[0]


def _split3(x):
    hi = x.astype(BF16)
    r = x - hi.astype(F32)
    mid = r.astype(BF16)
    return hi, mid, (r - mid.astype(F32)).astype(BF16)


def _compress_kernel(rows_ref, pos_ref, w1_ref, w2_ref, w2t_ref, o_ref, ot_ref):
    c0 = float(np.sqrt(2.0 / np.pi))
    hidden = NSA_CMP_HIDDEN
    for kind in range(2):
        rows = rows_ref[kind].astype(F32)
        first = _dot((rows + pos_ref[kind, 0]).astype(BF16), w1_ref[kind, 0])
        second = _dot((rows + pos_ref[kind, 1]).astype(BF16), w1_ref[kind, 1])
        n_rows = rows.shape[0]
        pre = first + pltpu.roll(second, n_rows - 1, 0)
        hid = pre * (0.5 * (1.0 + jnp.tanh(c0 * (pre + 0.044715 * (pre * pre * pre)))))
        for g in range(NSA_KV_GROUPS):
            hid_g = hid[:, g * hidden:(g + 1) * hidden]
            out = _dot_f32(hid_g, w2_ref[kind])
            o_ref[kind * NSA_KV_GROUPS + g] = jnp.concatenate(_split3(out) + (jnp.zeros(out.shape, BF16),), axis=1)
            ot_ref[kind * NSA_KV_GROUPS + g] = _dot_nt(w2t_ref[kind], hid_g, lax.Precision.HIGHEST).astype(BF16)


def _compress(cmp_rows, pos, w1, w2, w2t):
    _, b, n_rows, width = cmp_rows.shape
    n_out = 2 * NSA_KV_GROUPS
    return pl.pallas_call(
        _compress_kernel,
        grid=(b,),
        in_specs=[
            pl.BlockSpec((2, None, n_rows, width), lambda i: (0, i, 0, 0)),
            _resident(pos.shape), _resident(w1.shape), _resident(w2.shape), _resident(w2t.shape),
        ],
        out_specs=[
            pl.BlockSpec((None, n_out, n_rows, 4 * HEAD_DIM), lambda i: (i, 0, 0, 0)),
            pl.BlockSpec((None, n_out, HEAD_DIM, n_rows), lambda i: (i, 0, 0, 0)),
        ],
        out_shape=[jax.ShapeDtypeStruct((b, n_out, n_rows, 4 * HEAD_DIM), BF16),
                   jax.ShapeDtypeStruct((b, n_out, HEAD_DIM, n_rows), BF16)],
        compiler_params=_cparams("parallel"),
        name="compress",
    )(cmp_rows, pos, w1, w2, w2t)


def _count_rank(score, n_candidates):
    sub = 8
    tiles = [(lo, min(lo + sub, score.shape[0])) for lo in range(0, score.shape[0], sub)]
    ranks = [jnp.zeros((hi - lo,) + score.shape[1:], jnp.int32) for lo, hi in tiles]
    for j in range(n_candidates):
        row = score[j:j + 1, :]
        for n, (lo, hi) in enumerate(tiles):
            part = score[lo:hi]
            if lo > j:
                ahead = row >= part
            elif hi - 1 <= j:
                ahead = row > part
            else:
                after = lo + lax.broadcasted_iota(jnp.int32, part.shape, 0) > j
                ahead = (row > part) | ((row == part) & after)
            ranks[n] = ranks[n] + ahead.astype(jnp.int32)
    return ranks[0] if len(ranks) == 1 else jnp.concatenate(ranks, axis=0)


def _cmp_select_kernel(slopes_ref, qT_ref, kc_ref, vcT_ref, ovT_ref, oT_ref, negsel_ref):
    g = pl.program_id(1)
    qi = pl.program_id(2)
    tq = qT_ref.shape[1]
    n_cmp = kc_ref.shape[0]
    n_sel = ovT_ref.shape[0]
    hd = HEAD_DIM
    t = qi * tq + lax.broadcasted_iota(jnp.int32, (n_cmp, tq), 1)
    blk_c = lax.broadcasted_iota(jnp.int32, (n_cmp, tq), 0)
    dist = t - (blk_c * NSA_CMP_STRIDE + (NSA_CMP_BLOCK - 1))
    visible = dist >= 0
    dist_f = dist.astype(F32)
    kc3 = kc_ref[...]
    vcT = vcT_ref[...]
    p_sum = jnp.zeros((n_cmp, tq), F32)
    for hh in range(NSA_HEADS_PER_GROUP):
        slope2 = slopes_ref[g * NSA_HEADS_PER_GROUP + hh]
        q = qT_ref[hh * hd:(hh + 1) * hd, :]
        s = _dot(kc3, jnp.concatenate([q, q, q, jnp.zeros_like(q)], axis=0)) - slope2 * dist_f
        s = jnp.where(visible, s, NEG_INF)
        m = jnp.max(s, axis=0, keepdims=True)
        e = jnp.exp2(s - m)
        any_visible = m > 0.5 * NEG_INF
        p = e * jnp.where(any_visible, 1.0 / jnp.maximum(jnp.sum(e, axis=0, keepdims=True), TINY), 0.0)
        p_sum = p_sum + p
        oT_ref[hh * hd:(hh + 1) * hd, :] = _dot(vcT, p.astype(BF16))
    imp = _dot(ovT_ref[...], jnp.concatenate(_split3(p_sum), axis=0))
    ts = qi * tq + lax.broadcasted_iota(jnp.int32, (n_sel, tq), 1)
    blk = lax.broadcasted_iota(jnp.int32, (n_sel, tq), 0)
    cur = ts // NSA_SEL_BLOCK
    causal = blk * NSA_SEL_BLOCK <= ts
    forced = (blk == 0) | (blk == cur) | (blk == cur - 1)
    imp = jnp.where(causal, jnp.where(forced, FORCE_SCORE, imp), NEG_INF)
    rank = _count_rank(imp, n_sel)
    chosen = (rank < min(NSA_SEL_TOPK, n_sel)) & causal
    negsel_ref[...] = jnp.where(chosen, 0.0, NEG_INF).astype(BF16)


def _cmp_select(slopes2, qT, kc, vcT, overlap_t, batch):
    n = qT.shape[1]
    tq = min(CMP_TQ, n // batch)
    nq = n // batch // tq
    rows = NSA_HEADS_PER_GROUP * HEAD_DIM
    n_cmp = kc.shape[2]
    n_sel = overlap_t.shape[0]
    return pl.pallas_call(
        _cmp_select_kernel,
        grid=(batch, NSA_KV_GROUPS, nq),
        in_specs=[
            pl.BlockSpec(memory_space=pltpu.SMEM),
            pl.BlockSpec((rows, tq), lambda i, g, q: (g, i * nq + q)),
            pl.BlockSpec((None, None, n_cmp, 4 * HEAD_DIM), lambda i, g, q: (i, g, 0, 0)),
            pl.BlockSpec((None, None, HEAD_DIM, n_cmp), lambda i, g, q: (i, NSA_KV_GROUPS + g, 0, 0)),
            pl.BlockSpec((n_sel, 3 * n_cmp), lambda i, g, q: (0, 0)),
        ],
        out_specs=[
            pl.BlockSpec((rows, tq), lambda i, g, q: (g, i * nq + q)),
            pl.BlockSpec((None, n_sel, tq), lambda i, g, q: (g, 0, i * nq + q)),
        ],
        out_shape=[jax.ShapeDtypeStruct((NSA_WIDTH, n), F32),
                   jax.ShapeDtypeStruct((NSA_KV_GROUPS, n_sel, n), BF16)],
        compiler_params=_cparams("parallel", "parallel", "parallel"),
        name="cmp_select",
    )(slopes2, qT, kc, vcT, overlap_t)


def _moba_select_kernel(qT_ref, k_ref, negsel_ref):
    s = k_ref.shape[0]
    nb = s // MOBA_BLOCK
    hd = HEAD_DIM
    kmean = jnp.mean(k_ref[...].astype(F32).reshape(nb, MOBA_BLOCK, k_ref.shape[1]), axis=1)
    t = lax.broadcasted_iota(jnp.int32, (nb, s), 1)
    blk = lax.broadcasted_iota(jnp.int32, (nb, s), 0)
    cur = t // MOBA_BLOCK
    past = blk < cur
    for j in range(MOBA_HEADS):
        gs = _dot_f32(kmean[:, j * hd:(j + 1) * hd], qT_ref[j * hd:(j + 1) * hd, :].astype(F32))
        gs = jnp.where(past, gs, NEG_INF)
        rank = _count_rank(gs, nb)
        chosen = ((rank < min(MOBA_TOPK, nb - 1)) & past) | (blk == cur)
        negsel_ref[j, 0:nb, :] = jnp.where(chosen, 0.0, NEG_INF).astype(BF16)
        negsel_ref[j, nb:, :] = jnp.zeros((MOBA_SEL_ROWS - nb, s), BF16)


def _moba_select(qT, k_tok, batch, q_row_block, k_col_block):
    n = qT.shape[1]
    s = n // batch
    return pl.pallas_call(
        _moba_select_kernel,
        grid=(batch,),
        in_specs=[pl.BlockSpec((MOBA_WIDTH, s), lambda i: (q_row_block, i)),
                  pl.BlockSpec((s, MOBA_WIDTH), lambda i: (i, k_col_block))],
        out_specs=pl.BlockSpec((MOBA_HEADS, MOBA_SEL_ROWS, s), lambda i: (0, 0, i)),
        out_shape=jax.ShapeDtypeStruct((MOBA_HEADS, MOBA_SEL_ROWS, n), BF16),
        compiler_params=_cparams("parallel"),
        name="moba_select",
    )(qT, k_tok)


def _attn_schedule(nq, window_chunks):
    def padded(pairs, start):
        n_pad = (-(start + len(pairs) - 1)) % ATTN_UNROLL
        return pairs + [(0, 0, 0)] * n_pad

    diag = [(q, q, 1) for q in range(nq)]
    groups = [("diag", padded(diag, 0))]
    if window_chunks is None:
        full = [(q, c, 1) for q in range(nq) for c in range(q)]
        groups.append((None, full))
    else:
        assert window_chunks == 2
        done = len(groups[0][1])
        lower = padded([(q, q - 2, 1) for q in range(2, nq)], done)
        groups.append(("lower", lower))
        groups.append(("middle", [(q, q - 1, 1) for q in range(1, nq)]))
    pairs, kinds = [], []
    for kind, group in groups:
        pairs += group
        kinds += [kind] * len(group)
    n_pad = (-(len(pairs) - 1)) % ATTN_UNROLL
    pairs += [(0, 0, 0)] * n_pad
    kinds += [kinds[-1] if kinds[-1] is not None else "diag"] * n_pad
    body_kinds = []
    for j in range((len(pairs) - 1) // ATTN_UNROLL):
        ks = set(kinds[1 + j * ATTN_UNROLL:1 + (j + 1) * ATTN_UNROLL])
        assert len(ks) == 1, "a loop body must issue scores of one mask kind"
        body_kinds.append(ks.pop())
    table = np.asarray(pairs, np.int32)
    return table[:, 0], table[:, 1], table[:, 2], tuple(body_kinds)


def _attn_kernel(*refs, n_pairs, heads_per_pair, heads_per_v, has_sel, slot_of, body_kinds, clamp_sum):
    n_in = 11 if has_sel else 10
    (qtab_ref, ctab_ref, vtab_ref, slopes_ref, qT_ref, k_ref, ktab_ref, vT_ref, qbias_ref,
     maskbias_ref) = refs[:10]
    negsel_ref = refs[10] if has_sel else None
    oT_ref = refs[n_in]
    qaug_ref, vaug_ref, m_ref, acc_ref, excess_ref = refs[n_in + 1:n_in + 6]
    s_bufs = refs[n_in + 6:n_in + 8]
    cmax_bufs = refs[n_in + 8:n_in + 10]
    alpha_bufs = refs[n_in + 10:n_in + 12]
    p_bufs = refs[n_in + 12:n_in + 14]
    grp = pl.program_id(1)
    tq, tk, hd = ATTN_TQ, ATTN_TK, HEAD_DIM
    nq = qaug_ref.shape[0]
    hpp = heads_per_pair
    nh = n_pairs * hpp
    pcol = hpp * tq
    ncol = n_pairs * pcol
    n_vgroups = nh // heads_per_v
    vcol = heads_per_v * tq
    arows = hd + ONES_ROWS
    n_steps = qtab_ref.shape[0]
    unroll = ATTN_UNROLL

    for vg in range(n_vgroups):
        vaug_ref[vg * arows:vg * arows + hd, :] = vT_ref[vg * hd:(vg + 1) * hd, :]
        vaug_ref[vg * arows + hd:(vg + 1) * arows, :] = jnp.ones((ONES_ROWS, vaug_ref.shape[1]), BF16)

    for qi in range(nq):
        toks = slice(qi * tq, (qi + 1) * tq)
        for i in range(nh):
            cols = slice(i * tq, (i + 1) * tq)
            slot = slot_of(i % hpp, grp)
            q = qT_ref[i * hd:(i + 1) * hd, toks]
            zero = jnp.zeros_like(q)
            qaug_ref[qi, 0:hd, cols] = jnp.where(slot == 0, q, zero)
            qaug_ref[qi, hd:2 * hd, cols] = jnp.where(slot == 1, q, zero)
            qaug_ref[qi, BIAS_ROW:SEL_ROW, cols] = jnp.concatenate([qbias_ref[i]] * (tq // LANES), axis=1)
            fill = SEL_ROW
            if has_sel:
                n_rows = negsel_ref.shape[1]
                qaug_ref[qi, SEL_ROW:SEL_ROW + n_rows, cols] = negsel_ref[i % negsel_ref.shape[0], :, toks]
                fill = SEL_ROW + n_rows
            qaug_ref[qi, fill:, cols] = jnp.zeros((MXU_DIM - fill, tq), BF16)

    key_i = lax.broadcasted_iota(jnp.int32, (tk, ncol), 0)
    qry_row = lax.broadcasted_iota(jnp.int32, (1, ncol), 1) % tq
    big = jnp.int32(2 ** 30)

    def visible(kind, valid):
        if kind == "diag":
            return key_i <= jnp.where(valid, qry_row, -1)
        if kind == "lower":
            return key_i > jnp.where(valid, qry_row, big)
        if kind == "middle":
            return key_i >= jnp.where(valid, 0, big)
        return None

    def key_slice(t):
        return pl.ds(pl.multiple_of(ctab_ref[t] * tk, tk), tk)

    def scores(t, buf, kind):
        qi = qtab_ref[t]
        keys = key_slice(t)
        mask = visible(kind, vtab_ref[t] > 0)
        for pr in range(n_pairs):
            kaug = jnp.concatenate([k_ref[keys, pr * LANES:(pr + 1) * LANES], ktab_ref[keys, :]], axis=1)
            cols = slice(pr * pcol, (pr + 1) * pcol)
            s = _dot(kaug, qaug_ref[qi, :, cols])
            if mask is not None:
                s = jnp.where(mask[:, cols], s, NEG_INF)
            s_bufs[buf][:, cols] = s
            cmax_bufs[buf][:, cols] = jnp.max(s, axis=0, keepdims=True)

    def softmax(t, buf):
        qi = qtab_ref[t]
        m = m_ref[qi]
        m_new = jnp.maximum(m, cmax_bufs[buf][...])
        m_ref[qi] = m_new
        alpha_bufs[buf][...] = jnp.exp2(m - m_new)
        p_bufs[buf][...] = jnp.exp2(s_bufs[buf][...] - m_new).astype(BF16)

    def weighted(t, buf, rescale):
        qi = qtab_ref[t]
        keys = key_slice(t)
        for vg in range(n_vgroups):
            cols = slice(vg * vcol, (vg + 1) * vcol)
            pv = _dot(vaug_ref[vg * arows:(vg + 1) * arows, keys], p_bufs[buf][:, cols])
            old = acc_ref[qi, :, cols]
            acc_ref[qi, :, cols] = (alpha_bufs[buf][:, cols] * old if rescale else old) + pv

    def fast_scores(t, buf, kind):
        qi = qtab_ref[t]
        keys = key_slice(t)
        ref = jnp.where(vtab_ref[t] > 0, m_ref[qi], -NEG_INF)
        for pr in range(n_pairs):
            kaug = jnp.concatenate([k_ref[keys, pr * LANES:(pr + 1) * LANES], ktab_ref[keys, :]], axis=1)
            cols = slice(pr * pcol, (pr + 1) * pcol)
            s = _dot(kaug, qaug_ref[qi, :, cols])
            if kind in MASK_BIAS_ROW:
                s = s + maskbias_ref[MASK_BIAS_ROW[kind], :, cols]
            excess_ref[:, cols] = jnp.maximum(excess_ref[:, cols], jnp.max(s, axis=0, keepdims=True) - ref[:, cols])
            p_bufs[buf][:, cols] = jnp.exp2(s - ref[:, cols]).astype(BF16)

    def pipeline(step, max_body):
        first = 0
        for kind in sorted(set(body_kinds), key=body_kinds.index):
            count = body_kinds.count(kind)
            assert body_kinds[first:first + count] == (kind,) * count
            merge = max(m for m in range(1, count + 1) if count % m == 0 and m * unroll <= max_body)
            steps = merge * unroll

            @pl.loop(0, count // merge)
            def _(j, kind=kind, base=first * unroll, steps=steps):
                for u in range(steps):
                    step(base + j * steps + u, u, kind)

            first += count

    last = n_steps - 1

    for qi in range(nq):
        pos = (qi * tq + lax.broadcasted_iota(jnp.int32, (1, tq), 1)).astype(F32)
        for i in range(nh):
            m_ref[qi, :, i * tq:(i + 1) * tq] = slopes_ref[grp * nh + i] * pos
    acc_ref[...] = jnp.zeros(acc_ref.shape, F32)
    excess_ref[...] = jnp.full((1, ncol), NEG_INF, F32)
    fast_scores(0, 0, "diag")

    def fast_step(t, u, kind):
        fast_scores(t + 1, (u + 1) % 2, kind)
        weighted(t, u % 2, rescale=False)

    pipeline(fast_step, ATTN_FAST_BODY)
    weighted(last, last % 2, rescale=False)

    excess = excess_ref[...]
    @pl.when((jnp.max(excess) > SAFE_EXCESS) | (jnp.min(excess) < -SAFE_EXCESS))
    def _():
        m_ref[...] = jnp.full(m_ref.shape, NEG_INF, F32)
        acc_ref[...] = jnp.zeros(acc_ref.shape, F32)
        p_bufs[1][...] = jnp.zeros(p_bufs[1].shape, BF16)
        alpha_bufs[1][...] = jnp.ones((1, ncol), F32)
        scores(0, 0, "diag")

        def safe_step(t, u, kind):
            scores(t + 1, (u + 1) % 2, kind)
            weighted(jnp.maximum(t - 1, 0), (u + 1) % 2, rescale=True)
            softmax(t, u % 2)

        pipeline(safe_step, ATTN_SAFE_BODY)
        weighted(last - 1, (last - 1) % 2, rescale=True)
        softmax(last, last % 2)
        weighted(last, last % 2, rescale=True)

    for qi in range(nq):
        for i in range(nh):
            cols = slice(i * tq, (i + 1) * tq)
            l = acc_ref[qi, hd:hd + 1, cols]
            oT_ref[i * hd:(i + 1) * hd, qi * tq:(qi + 1) * tq] = (
                acc_ref[qi, 0:hd, cols] * (1.0 / (jnp.maximum(l, TINY) if clamp_sum else l)))


def _attention(slopes2, qT, k_tok, key_table, vT, qbias, negsel, *, batch, n_steps, n_pairs,
               heads_per_pair, heads_per_v, q_block, k_block, v_block, slot_of, window=None,
               clamp_sum, name):
    n = qT.shape[1]
    s = n // batch
    tq, tk, hd = ATTN_TQ, ATTN_TK, HEAD_DIM
    assert tq == tk and s % tq == 0
    nq = s // tq
    nh = n_pairs * heads_per_pair
    assert nh % heads_per_v == 0
    ncol = nh * tq
    arows = hd + ONES_ROWS
    n_vgroups = nh // heads_per_v
    if window is not None:
        assert window % tk == 0
    q_tab, c_tab, v_tab, body_kinds = _attn_schedule(nq, None if window is None else window // tk)
    key_i, qry_i = np.arange(tk)[:, None], (np.arange(ncol) % tq)[None, :]
    tables = {"diag": key_i <= qry_i, "lower": key_i > qry_i}
    mask_bias = jnp.asarray(np.stack([np.where(tables[k], 0.0, NEG_INF) for k in MASK_BIAS_ROW
                                      if k in body_kinds]), F32)
    smem = pl.BlockSpec(memory_space=pltpu.SMEM)
    in_specs = [
        smem, smem, smem, smem,
        pl.BlockSpec((nh * hd, s), lambda b, g: (q_block(g), b)),
        pl.BlockSpec((s, n_pairs * LANES), lambda b, g: (b, k_block(g))),
        pl.BlockSpec((s, LANES), lambda b, g: (0, 0)),
        pl.BlockSpec((n_vgroups * hd, s), lambda b, g: (v_block(g), b)),
        pl.BlockSpec((nh, 16, LANES), lambda b, g: (g, 0, 0)),
        _resident(mask_bias.shape),
    ]
    args = [jnp.asarray(q_tab), jnp.asarray(c_tab), jnp.asarray(v_tab), slopes2, qT, k_tok, key_table, vT,
            qbias, mask_bias]
    if negsel is not None:
        per_step = negsel.shape[0] // n_steps
        in_specs.append(pl.BlockSpec((per_step, negsel.shape[1], s), lambda b, g: (g, 0, b)))
        args.append(negsel)
    vec = lambda: pltpu.VMEM((1, ncol), F32)
    return pl.pallas_call(
        functools.partial(_attn_kernel, n_pairs=n_pairs, heads_per_pair=heads_per_pair,
                          heads_per_v=heads_per_v, has_sel=negsel is not None, slot_of=slot_of,
                          body_kinds=body_kinds, clamp_sum=clamp_sum),
        grid=(batch, n_steps),
        in_specs=in_specs,
        out_specs=pl.BlockSpec((nh * hd, s), lambda b, g: (g, b)),
        out_shape=jax.ShapeDtypeStruct((n_steps * nh * hd, n), F32),
        scratch_shapes=[pltpu.VMEM((nq, MXU_DIM, ncol), BF16),
                        pltpu.VMEM((n_vgroups * arows, s), BF16),
                        pltpu.VMEM((nq, 1, ncol), F32),
                        pltpu.VMEM((nq, arows, ncol), F32),
                        vec(),
                        pltpu.VMEM((tk, ncol), F32), pltpu.VMEM((tk, ncol), F32),
                        vec(), vec(), vec(), vec(),
                        pltpu.VMEM((tk, ncol), BF16), pltpu.VMEM((tk, ncol), BF16)],
        compiler_params=_cparams("parallel", "parallel"),
        name=name,
    )(*args)


def _mix_kernel(h_ref, g_ref, zgT_ref, ocmp_ref, osel_ref, owin_ref, omoba_ref,
                wmerge_ref, wn_ref, wm_ref, wo_ref, o_ref, gate_ref, onsa_ref):
    h = h_ref[...]
    u = _rms(h, g_ref[...]).astype(BF16)
    gate_ref[...] = _sigmoid(zgT_ref[0:gate_ref.shape[0], :])
    hd = HEAD_DIM
    for hh in range(NSA_HEADS):
        rows = slice(hh * hd, (hh + 1) * hd)
        o = None
        for r, branch in enumerate((ocmp_ref, osel_ref, owin_ref)):
            gate = gate_ref[3 * hh + r:3 * hh + r + 1, :]
            term = gate * branch[rows, :]
            o = term if o is None else o + term
        onsa_ref[rows, :] = o.astype(BF16)
    y_n = _dot_tn(onsa_ref[...], wn_ref[...])
    y_m = _dot_tn(omoba_ref[...].astype(BF16), wm_ref[...])
    mixed = (_sigmoid(_dot(u, wmerge_ref[:, :D_MODEL])) * y_n
             + _sigmoid(_dot(u, wmerge_ref[:, D_MODEL:])) * y_m)
    o_ref[...] = h + _dot(mixed.astype(BF16), wo_ref[...])


def _mix(h, g, zgT, ocmpT, oselT, owinT, omobaT, wmerge, wn, wm, wo):
    n = h.shape[0]
    tm = TOKEN_TILE
    rows = lambda width: pl.BlockSpec((tm, width), lambda i: (i, 0))
    cols = lambda height: pl.BlockSpec((height, tm), lambda i: (0, i))
    gate_rows = -(-N_GATE_LOGITS // 8) * 8
    return pl.pallas_call(
        _mix_kernel,
        grid=(n // tm,),
        in_specs=[rows(D_MODEL), _resident((1, D_MODEL)), cols(LANES), cols(NSA_WIDTH), cols(NSA_WIDTH),
                  cols(NSA_WIDTH), cols(MOBA_WIDTH), _resident(wmerge.shape),
                  _resident(wn.shape), _resident(wm.shape), _resident(wo.shape)],
        out_specs=rows(D_MODEL),
        out_shape=jax.ShapeDtypeStruct((n, D_MODEL), F32),
        scratch_shapes=[pltpu.VMEM((gate_rows, tm), F32), pltpu.VMEM((NSA_WIDTH, tm), BF16)],
        compiler_params=_cparams("parallel"),
        name="mix",
    )(h, g, zgT, ocmpT, oselT, owinT, omobaT, wmerge, wn, wm, wo)


def _key_table(s, block):
    pos = np.arange(s)
    tab = np.zeros((s, LANES), np.float32)
    tab[:, 0:N_SLOPE_PIECES] = ((pos // LANES) * LANES)[:, None]
    tab[:, N_SLOPE_PIECES:2 * N_SLOPE_PIECES] = (pos % LANES)[:, None]
    if block is not None:
        tab[pos, (SEL_ROW - BIAS_ROW) + pos // block] = 1.0
    return jnp.asarray(tab, BF16)


def _slope_rows(slopes2):
    s1 = slopes2.astype(BF16)
    r1 = slopes2 - s1.astype(F32)
    s2 = r1.astype(BF16)
    s3 = (r1 - s2.astype(F32)).astype(BF16)
    rows = jnp.stack([s1, s2, s3, s1, s2, s3], axis=1)
    rows = jnp.pad(rows, ((0, 0), (0, 16 - 2 * N_SLOPE_PIECES)))
    return jnp.broadcast_to(rows[:, :, None], rows.shape + (LANES,))


def _compress_weights(pos, w1):
    st, hd, gw, hidden, groups = NSA_CMP_STRIDE, HEAD_DIM, NSA_KV_WIDTH, NSA_CMP_HIDDEN, NSA_KV_GROUPS
    w = jnp.zeros((2, st, gw, groups * hidden), F32)
    for g in range(groups):
        w = w.at[:, :, g * hd:(g + 1) * hd, g * hidden:(g + 1) * hidden].set(w1.reshape(2, st, hd, hidden))
    ps = jnp.tile(pos.reshape(2, st, 1, hd), (1, 1, groups, 1))
    return ps.reshape(2, 1, st * gw), w.reshape(2, st * gw, groups * hidden).astype(BF16)


def kernel(x, p, ffn1_norm, ffn1_w1, ffn1_w3, ffn1_w2, mix_norm, w_in, cmp_pos_k, cmp_w1_k, cmp_w2_k, cmp_pos_v, cmp_w1_v, cmp_w2_v, w_up_nsa, w_up_moba, w_out, ffn2_norm, ffn2_w1, ffn2_w3, ffn2_w2, ple_norm, w_ple_gate, w_ple, final_norm):
    b, s, d = x.shape
    n = b * s
    depth = p.shape[0]
    n_all = NSA_HEADS + MOBA_HEADS
    slopes = jnp.exp2(-8.0 * (jnp.arange(n_all, dtype=F32) + 1.0) / n_all)
    slopes2_n, slopes2_m = slopes[0::2] * LOG2E, slopes[1::2] * LOG2E
    qbias_n, qbias_m = _slope_rows(slopes2_n), _slope_rows(slopes2_m)

    n_cmp_rows = s // NSA_CMP_STRIDE
    n_cmp = (s - NSA_CMP_BLOCK) // NSA_CMP_STRIDE + 1
    n_sel = s // NSA_SEL_BLOCK
    c_start = np.arange(n_cmp_rows) * NSA_CMP_STRIDE
    s_start = np.arange(n_sel) * NSA_SEL_BLOCK
    overlap_t = ((c_start[None, :] <= s_start[:, None] + NSA_SEL_BLOCK - 1)
                 & (c_start[None, :] + NSA_CMP_BLOCK - 1 >= s_start[:, None])
                 & (np.arange(n_cmp_rows)[None, :] < n_cmp))
    overlap_t = jnp.asarray(np.tile(overlap_t, (1, 3)), BF16)
    table_sel = _key_table(s, NSA_SEL_BLOCK)
    table_win = _key_table(s, None)
    table_moba = _key_table(s, MOBA_BLOCK)

    h = x.reshape(n, d)
    for i in range(depth):
        vec = lambda a: a[i].reshape(1, -1)
        wi = w_in[i]
        col = lambda lo, width: wi[:, lo:lo + width]
        wq_t = jnp.concatenate([col(0, NSA_WIDTH), col(OFF_MOBA_Q, MOBA_WIDTH)], axis=1).T.astype(BF16)
        wk = jnp.concatenate([col(OFF_MOBA_K, MOBA_WIDTH), col(OFF_KSLC, NSA_KV_WIDTH),
                              col(OFF_KWIN, NSA_KV_WIDTH)], axis=1).astype(BF16)
        wc = jnp.stack([col(OFF_KCMP, NSA_KV_WIDTH), col(OFF_VCMP, NSA_KV_WIDTH)]).astype(BF16)
        wv_t = jnp.concatenate([col(OFF_VSLC, NSA_KV_WIDTH), col(OFF_VWIN, NSA_KV_WIDTH),
                                col(OFF_MOBA_V, MOBA_WIDTH)], axis=1).T.astype(BF16)
        wg_t = jnp.pad(col(OFF_GATE_LOGITS, N_GATE_LOGITS), ((0, 0), (0, LANES - N_GATE_LOGITS))).T.astype(BF16)
        w_merge = col(OFF_MERGE, 2 * D_MODEL).astype(BF16)
        h, qT, k_tok, cmp_tok, vT, zgT = _ffn(
            h, vec(ffn1_norm), ffn1_w1[i].astype(BF16), ffn1_w3[i].astype(BF16), ffn1_w2[i].astype(BF16),
            project=(vec(mix_norm), wq_t, wk, wc, wv_t, wg_t))

        cmp_w = [_compress_weights(ps[i], w1[i]) for ps, w1 in ((cmp_pos_k, cmp_w1_k), (cmp_pos_v, cmp_w1_v))]
        pos = jnp.stack([pw[0] for pw in cmp_w])
        w1c = jnp.stack([pw[1] for pw in cmp_w])
        w2 = jnp.stack([cmp_w2_k[i], cmp_w2_v[i]])
        cmp_rows = cmp_tok.reshape(2, b, n_cmp_rows, NSA_CMP_STRIDE * NSA_KV_WIDTH)
        kc, kcT = _compress(cmp_rows, pos, w1c, w2, w2.transpose(0, 2, 1))

        ocmpT, negsel_n = _cmp_select(slopes2_n, qT, kc, kcT, overlap_t, b)
        hs = ATTN_HEADS_PER_STEP
        q_rows, pair = hs * HEAD_DIM, 2 * HEAD_DIM
        nsa = dict(batch=b, n_steps=NSA_KV_GROUPS, n_pairs=1, heads_per_pair=hs, heads_per_v=hs,
                   q_block=lambda g: QT_ROW_NSA // q_rows + g, slot_of=lambda j, g: g, clamp_sum=True)
        oselT = _attention(slopes2_n, qT, k_tok, table_sel, vT, qbias_n, negsel_n,
                           k_block=lambda g: K_COL_SLC // pair, v_block=lambda g: VT_ROW_SLC // HEAD_DIM + g,
                           name="attn_select", **nsa)
        owinT = _attention(slopes2_n, qT, k_tok, table_win, vT, qbias_n, None,
                           k_block=lambda g: K_COL_WIN // pair, v_block=lambda g: VT_ROW_WIN // HEAD_DIM + g,
                           window=NSA_WINDOW, name="attn_window", **nsa)
        negsel_m = _moba_select(qT, k_tok, b, q_row_block=QT_ROW_MOBA // MOBA_WIDTH,
                                k_col_block=K_COL_MOBA // MOBA_WIDTH)
        omobaT = _attention(slopes2_m, qT, k_tok, table_moba, vT, qbias_m, negsel_m, batch=b,
                            n_steps=MOBA_HEADS // hs, n_pairs=hs // 2, heads_per_pair=2, heads_per_v=1,
                            q_block=lambda g: QT_ROW_MOBA // q_rows + g,
                            k_block=lambda g: K_COL_MOBA // (hs // 2 * pair) + g,
                            v_block=lambda g: VT_ROW_MOBA // q_rows + g, slot_of=lambda j, g: j,
                            clamp_sum=False, name="attn_moba")

        h = _mix(h, vec(mix_norm), zgT, ocmpT, oselT, owinT, omobaT, w_merge,
                 w_up_nsa[i].astype(BF16), w_up_moba[i].astype(BF16), w_out[i].astype(BF16))

        h = _ffn(h, vec(ffn2_norm), ffn2_w1[i].astype(BF16), ffn2_w3[i].astype(BF16), ffn2_w2[i].astype(BF16),
                 tail=(p[i].reshape(n, PLE_DIM), vec(ple_norm), w_ple_gate[i].astype(BF16),
                       w_ple[i].astype(BF16)),
                 final_norm=final_norm.reshape(1, -1) if i + 1 == depth else None)
    return h.reshape(b, s, d)
```

```python
import functools

import numpy as np
import jax
import jax.numpy as jnp
from jax import lax
from jax.experimental import pallas as pl
from jax.experimental.pallas import tpu as pltpu

F32 = jnp.float32
BF16 = jnp.bfloat16

D_MODEL = 1024
HEAD_DIM = 64
NSA_HEADS = 8
NSA_KV_GROUPS = 2
NSA_HEADS_PER_GROUP = NSA_HEADS // NSA_KV_GROUPS
NSA_CMP_BLOCK = 32
NSA_CMP_STRIDE = 16
NSA_CMP_HIDDEN = 128
NSA_SEL_BLOCK = 64
NSA_SEL_TOPK = 16
NSA_WINDOW = 512
MOBA_HEADS = 8
MOBA_BLOCK = 256
MOBA_TOPK = 3
D_FF = 2816
PLE_DIM = 256
RMS_EPS = 1e-6
NEG_INF = -1e30
TINY = 1e-30
FORCE_SCORE = 1e9
LOG2E = float(np.log2(np.e))
Q_SCALE = HEAD_DIM ** -0.5 * LOG2E

NSA_WIDTH = NSA_HEADS * HEAD_DIM
NSA_KV_WIDTH = NSA_KV_GROUPS * HEAD_DIM
MOBA_WIDTH = MOBA_HEADS * HEAD_DIM
N_GATE_LOGITS = 3 * NSA_HEADS
OFF_KCMP = NSA_WIDTH
OFF_VCMP = OFF_KCMP + NSA_KV_WIDTH
OFF_KSLC = OFF_VCMP + NSA_KV_WIDTH
OFF_VSLC = OFF_KSLC + NSA_KV_WIDTH
OFF_KWIN = OFF_VSLC + NSA_KV_WIDTH
OFF_VWIN = OFF_KWIN + NSA_KV_WIDTH
OFF_GATE_LOGITS = OFF_VWIN + NSA_KV_WIDTH
OFF_MOBA_Q = OFF_GATE_LOGITS + N_GATE_LOGITS
OFF_MOBA_K = OFF_MOBA_Q + MOBA_WIDTH
OFF_MOBA_V = OFF_MOBA_K + MOBA_WIDTH
OFF_MERGE = OFF_MOBA_V + MOBA_WIDTH

LANES = 128
VMEM_LIMIT = 56 * 1024 * 1024

TOKEN_TILE = 512
FF_CHUNK = 256
ATTN_TQ = 256
ATTN_TK = 256
CMP_TQ = 2048
ATTN_UNROLL = 2
ATTN_FAST_BODY = 28
ATTN_SAFE_BODY = 4
SLOPE_ROWS = 16
N_SLOPE_PIECES = 3
MOBA_SEL_ROWS = 16
ONES_ROWS = 16
MASK_BIAS_ROW = {"diag": 0, "lower": 1}
SAFE_EXCESS = 60.0

QT_ROW_NSA, QT_ROW_MOBA = 0, NSA_WIDTH
K_COL_MOBA, K_COL_SLC, K_COL_WIN = 0, MOBA_WIDTH, MOBA_WIDTH + NSA_KV_WIDTH
VT_ROW_SLC, VT_ROW_WIN, VT_ROW_MOBA = 0, NSA_KV_WIDTH, 2 * NSA_KV_WIDTH
ATTN_HEADS_PER_STEP = 4

_NT = (((1,), (1,)), ((), ()))
_TN = (((0,), (0,)), ((), ()))


def _cparams(*sem):
    return pltpu.CompilerParams(dimension_semantics=sem, vmem_limit_bytes=VMEM_LIMIT)


def _resident(shape):
    nd = len(shape)
    return pl.BlockSpec(shape, lambda *_: (0,) * nd, pipeline_mode=pl.Buffered(1))


def _rms(x, g):
    ms = jnp.mean(x * x, axis=-1, keepdims=True)
    return x * lax.rsqrt(ms + RMS_EPS) * g


def _sigmoid(x):
    return 1.0 / (1.0 + jnp.exp(-x))


def _dot(a, b):
    return jnp.dot(a, b, preferred_element_type=F32)


def _dot_f32(a, b):
    return jnp.dot(a, b, preferred_element_type=F32, precision=lax.Precision.HIGHEST)


def _dot_nt(a, b, precision=None):
    return lax.dot_general(a, b, _NT, preferred_element_type=F32, precision=precision)


def _dot_tn(a, b):
    return lax.dot_general(a, b, _TN, preferred_element_type=F32)


def _ffn_kernel(*refs, tail, final, project):
    x_ref, g_ref, w1_ref, w3_ref, w2_ref = refs[:5]
    n_in = 5
    if tail:
        p_ref, gp_ref, wpg_ref, wp_ref = refs[n_in:n_in + 4]
        n_in += 4
    if final:
        gf_ref = refs[n_in]
        n_in += 1
    if project:
        gm_ref, wq_ref, wk_ref, wc_ref, wv_ref, wg_ref = refs[n_in:n_in + 6]
        n_in += 6
    o_ref = refs[n_in]
    acc_ref = refs[-1]
    x = x_ref[...]
    xn = _rms(x, g_ref[...]).astype(BF16)
    for c in range(D_FF // FF_CHUNK):
        cols = slice(c * FF_CHUNK, (c + 1) * FF_CHUNK)
        a = _dot(xn, w1_ref[:, cols])
        b = _dot(xn, w3_ref[:, cols])
        hid = (a * _sigmoid(a) * b).astype(BF16)
        y = _dot(hid, w2_ref[cols, :])
        if c == 0:
            acc_ref[...] = y
        else:
            acc_ref[...] += y
    h = x + 0.5 * acc_ref[...]
    if tail:
        gate = _sigmoid(_dot(_rms(h, gp_ref[...]).astype(BF16), wpg_ref[...]))
        h = h + gate * _dot(p_ref[...].astype(BF16), wp_ref[...])
    if final:
        h = _rms(h, gf_ref[...])
    o_ref[...] = h
    if project:
        qT_ref, k_ref, cmp_ref, vT_ref, zgT_ref = refs[n_in + 1:n_in + 6]
        u = _rms(h, gm_ref[...]).astype(BF16)
        qT_ref[...] = (_dot_nt(wq_ref[...], u) * Q_SCALE).astype(BF16)
        k_ref[...] = _dot(u, wk_ref[...]).astype(BF16)
        for kind in range(2):
            cmp_ref[kind] = _dot(u, wc_ref[kind]).astype(BF16)
        vT_ref[...] = _dot_nt(wv_ref[...], u).astype(BF16)
        zgT_ref[...] = _dot_nt(wg_ref[...], u)


def _ffn(x, g, w1, w3, w2, tail=None, final_norm=None, project=None):
    n = x.shape[0]
    tm = TOKEN_TILE
    row = lambda w: pl.BlockSpec((tm, w), lambda i: (i, 0))
    col = lambda height: pl.BlockSpec((height, tm), lambda i: (0, i))
    args = [x, g, w1, w3, w2]
    specs = [row(D_MODEL), _resident((1, D_MODEL)), _resident(w1.shape), _resident(w3.shape),
             _resident(w2.shape)]
    out_specs = [row(D_MODEL)]
    out_shape = [jax.ShapeDtypeStruct((n, D_MODEL), F32)]
    if tail is not None:
        p, gp, wpg, wp = tail
        args += [p, gp, wpg, wp]
        specs += [row(PLE_DIM), _resident((1, D_MODEL)), _resident(wpg.shape), _resident(wp.shape)]
    if final_norm is not None:
        args.append(final_norm)
        specs.append(_resident((1, D_MODEL)))
    if project is not None:
        gm, wq_t, wk, wc, wv_t, wg_t = project
        args += [gm, wq_t, wk, wc, wv_t, wg_t]
        specs += [_resident(a.shape) for a in project]
        out_specs += [col(wq_t.shape[0]), row(wk.shape[1]),
                      pl.BlockSpec((2, tm, NSA_KV_WIDTH), lambda i: (0, i, 0)),
                      col(wv_t.shape[0]), col(LANES)]
        out_shape += [jax.ShapeDtypeStruct((wq_t.shape[0], n), BF16),
                      jax.ShapeDtypeStruct((n, wk.shape[1]), BF16),
                      jax.ShapeDtypeStruct((2, n, NSA_KV_WIDTH), BF16),
                      jax.ShapeDtypeStruct((wv_t.shape[0], n), BF16),
                      jax.ShapeDtypeStruct((LANES, n), F32)]
    outs = pl.pallas_call(
        functools.partial(_ffn_kernel, tail=tail is not None, final=final_norm is not None,
                          project=project is not None),
        grid=(n // tm,),
        in_specs=specs,
        out_specs=out_specs,
        out_shape=out_shape,
        scratch_shapes=[pltpu.VMEM((tm, D_MODEL), F32)],
        compiler_params=_cparams("parallel"),
        name="ffn" + ("_tail" if tail is not None else "") + ("_project" if project is not None else ""),
    )(*args)
    return outs if project is not None else outs[0]


def _split3(x):
    hi = x.astype(BF16)
    r = x - hi.astype(F32)
    mid = r.astype(BF16)
    return hi, mid, (r - mid.astype(F32)).astype(BF16)


def _compress_kernel(rows_ref, pos_ref, w1_ref, w2_ref, w2t_ref, o_ref, ot_ref):
    c0 = float(np.sqrt(2.0 / np.pi))
    hidden = NSA_CMP_HIDDEN
    for kind in range(2):
        rows = rows_ref[kind].astype(F32)
        first = _dot((rows + pos_ref[kind, 0]).astype(BF16), w1_ref[kind, 0])
        second = _dot((rows + pos_ref[kind, 1]).astype(BF16), w1_ref[kind, 1])
        n_rows = rows.shape[0]
        pre = first + pltpu.roll(second, n_rows - 1, 0)
        hid = pre * (0.5 * (1.0 + jnp.tanh(c0 * (pre + 0.044715 * (pre * pre * pre)))))
        for g in range(NSA_KV_GROUPS):
            hid_g = hid[:, g * hidden:(g + 1) * hidden]
            out = _dot_f32(hid_g, w2_ref[kind])
            o_ref[kind * NSA_KV_GROUPS + g] = jnp.concatenate(_split3(out) + (jnp.zeros(out.shape, BF16),), axis=1)
            ot_ref[kind * NSA_KV_GROUPS + g] = _dot_nt(w2t_ref[kind], hid_g, lax.Precision.HIGHEST).astype(BF16)


def _compress(cmp_rows, pos, w1, w2, w2t):
    _, b, n_rows, width = cmp_rows.shape
    n_out = 2 * NSA_KV_GROUPS
    return pl.pallas_call(
        _compress_kernel,
        grid=(b,),
        in_specs=[
            pl.BlockSpec((2, None, n_rows, width), lambda i: (0, i, 0, 0)),
            _resident(pos.shape), _resident(w1.shape), _resident(w2.shape), _resident(w2t.shape),
        ],
        out_specs=[
            pl.BlockSpec((None, n_out, n_rows, 4 * HEAD_DIM), lambda i: (i, 0, 0, 0)),
            pl.BlockSpec((None, n_out, HEAD_DIM, n_rows), lambda i: (i, 0, 0, 0)),
        ],
        out_shape=[jax.ShapeDtypeStruct((b, n_out, n_rows, 4 * HEAD_DIM), BF16),
                   jax.ShapeDtypeStruct((b, n_out, HEAD_DIM, n_rows), BF16)],
        compiler_params=_cparams("parallel"),
        name="compress",
    )(cmp_rows, pos, w1, w2, w2t)


def _count_rank(score, n_candidates):
    sub = 8
    tiles = [(lo, min(lo + sub, score.shape[0])) for lo in range(0, score.shape[0], sub)]
    ranks = [jnp.zeros((hi - lo,) + score.shape[1:], jnp.int32) for lo, hi in tiles]
    for j in range(n_candidates):
        row = score[j:j + 1, :]
        for n, (lo, hi) in enumerate(tiles):
            part = score[lo:hi]
            if lo > j:
                ahead = row >= part
            elif hi - 1 <= j:
                ahead = row > part
            else:
                after = lo + lax.broadcasted_iota(jnp.int32, part.shape, 0) > j
                ahead = (row > part) | ((row == part) & after)
            ranks[n] = ranks[n] + ahead.astype(jnp.int32)
    return ranks[0] if len(ranks) == 1 else jnp.concatenate(ranks, axis=0)


def _cmp_select_kernel(slopes_ref, qT_ref, kc_ref, vcT_ref, ovT_ref, oT_ref, negsel_ref):
    g = pl.program_id(1)
    qi = pl.program_id(2)
    tq = qT_ref.shape[1]
    n_cmp = kc_ref.shape[0]
    n_sel = ovT_ref.shape[0]
    hd = HEAD_DIM
    t = qi * tq + lax.broadcasted_iota(jnp.int32, (n_cmp, tq), 1)
    blk_c = lax.broadcasted_iota(jnp.int32, (n_cmp, tq), 0)
    dist = t - (blk_c * NSA_CMP_STRIDE + (NSA_CMP_BLOCK - 1))
    visible = dist >= 0
    dist_f = dist.astype(F32)
    kc3 = kc_ref[...]
    vcT = vcT_ref[...]
    p_sum = jnp.zeros((n_cmp, tq), F32)
    for hh in range(NSA_HEADS_PER_GROUP):
        slope2 = slopes_ref[g * NSA_HEADS_PER_GROUP + hh]
        q = qT_ref[hh * hd:(hh + 1) * hd, :]
        s = _dot(kc3, jnp.concatenate([q, q, q, jnp.zeros_like(q)], axis=0)) - slope2 * dist_f
        s = jnp.where(visible, s, NEG_INF)
        m = jnp.max(s, axis=0, keepdims=True)
        e = jnp.exp2(s - m)
        any_visible = m > 0.5 * NEG_INF
        p = e * jnp.where(any_visible, 1.0 / jnp.maximum(jnp.sum(e, axis=0, keepdims=True), TINY), 0.0)
        p_sum = p_sum + p
        oT_ref[hh * hd:(hh + 1) * hd, :] = _dot(vcT, p.astype(BF16))
    imp = _dot(ovT_ref[...], jnp.concatenate(_split3(p_sum), axis=0))
    ts = qi * tq + lax.broadcasted_iota(jnp.int32, (n_sel, tq), 1)
    blk = lax.broadcasted_iota(jnp.int32, (n_sel, tq), 0)
    cur = ts // NSA_SEL_BLOCK
    causal = blk * NSA_SEL_BLOCK <= ts
    forced = (blk == 0) | (blk == cur) | (blk == cur - 1)
    imp = jnp.where(causal, jnp.where(forced, FORCE_SCORE, imp), NEG_INF)
    rank = _count_rank(imp, n_sel)
    chosen = (rank < min(NSA_SEL_TOPK, n_sel)) & causal
    negsel_ref[...] = jnp.where(chosen, 0.0, NEG_INF).astype(BF16)


def _cmp_select(slopes2, qT, kc, vcT, overlap_t, batch):
    n = qT.shape[1]
    tq = min(CMP_TQ, n // batch)
    nq = n // batch // tq
    rows = NSA_HEADS_PER_GROUP * HEAD_DIM
    n_cmp = kc.shape[2]
    n_sel = overlap_t.shape[0]
    return pl.pallas_call(
        _cmp_select_kernel,
        grid=(batch, NSA_KV_GROUPS, nq),
        in_specs=[
            pl.BlockSpec(memory_space=pltpu.SMEM),
            pl.BlockSpec((rows, tq), lambda i, g, q: (g, i * nq + q)),
            pl.BlockSpec((None, None, n_cmp, 4 * HEAD_DIM), lambda i, g, q: (i, g, 0, 0)),
            pl.BlockSpec((None, None, HEAD_DIM, n_cmp), lambda i, g, q: (i, NSA_KV_GROUPS + g, 0, 0)),
            pl.BlockSpec((n_sel, 3 * n_cmp), lambda i, g, q: (0, 0)),
        ],
        out_specs=[
            pl.BlockSpec((rows, tq), lambda i, g, q: (g, i * nq + q)),
            pl.BlockSpec((None, n_sel, tq), lambda i, g, q: (g, 0, i * nq + q)),
        ],
        out_shape=[jax.ShapeDtypeStruct((NSA_WIDTH, n), F32),
                   jax.ShapeDtypeStruct((NSA_KV_GROUPS, n_sel, n), BF16)],
        compiler_params=_cparams("parallel", "parallel", "parallel"),
        name="cmp_select",
    )(slopes2, qT, kc, vcT, overlap_t)


def _moba_select_kernel(qT_ref, k_ref, negsel_ref):
    s = k_ref.shape[0]
    nb = s // MOBA_BLOCK
    hd = HEAD_DIM
    kmean = jnp.mean(k_ref[...].astype(F32).reshape(nb, MOBA_BLOCK, k_ref.shape[1]), axis=1)
    t = lax.broadcasted_iota(jnp.int32, (nb, s), 1)
    blk = lax.broadcasted_iota(jnp.int32, (nb, s), 0)
    cur = t // MOBA_BLOCK
    past = blk < cur
    for j in range(MOBA_HEADS):
        gs = _dot_f32(kmean[:, j * hd:(j + 1) * hd], qT_ref[j * hd:(j + 1) * hd, :].astype(F32))
        gs = jnp.where(past, gs, NEG_INF)
        rank = _count_rank(gs, nb)
        chosen = ((rank < min(MOBA_TOPK, nb - 1)) & past) | (blk == cur)
        negsel_ref[j, 0:nb, :] = jnp.where(chosen, 0.0, NEG_INF).astype(BF16)
        negsel_ref[j, nb:, :] = jnp.zeros((MOBA_SEL_ROWS - nb, s), BF16)


def _moba_select(qT, k_tok, batch, q_row_block, k_col_block):
    n = qT.shape[1]
    s = n // batch
    return pl.pallas_call(
        _moba_select_kernel,
        grid=(batch,),
        in_specs=[pl.BlockSpec((MOBA_WIDTH, s), lambda i: (q_row_block, i)),
                  pl.BlockSpec((s, MOBA_WIDTH), lambda i: (i, k_col_block))],
        out_specs=pl.BlockSpec((MOBA_HEADS, MOBA_SEL_ROWS, s), lambda i: (0, 0, i)),
        out_shape=jax.ShapeDtypeStruct((MOBA_HEADS, MOBA_SEL_ROWS, n), BF16),
        compiler_params=_cparams("parallel"),
        name="moba_select",
    )(qT, k_tok)


def _attn_schedule(nq, window_chunks):
    def padded(pairs, start):
        n_pad = (-(start + len(pairs) - 1)) % ATTN_UNROLL
        return pairs + [(0, 0, 0)] * n_pad

    diag = [(q, q, 1) for q in range(nq)]
    groups = [("diag", padded(diag, 0))]
    if window_chunks is None:
        full = [(q, c, 1) for q in range(nq) for c in range(q)]
        groups.append((None, full))
    else:
        assert window_chunks == 2
        done = len(groups[0][1])
        lower = padded([(q, q - 2, 1) for q in range(2, nq)], done)
        groups.append(("lower", lower))
        groups.append(("middle", [(q, q - 1, 1) for q in range(1, nq)]))
    pairs, kinds = [], []
    for kind, group in groups:
        pairs += group
        kinds += [kind] * len(group)
    n_pad = (-(len(pairs) - 1)) % ATTN_UNROLL
    pairs += [(0, 0, 0)] * n_pad
    kinds += [kinds[-1] if kinds[-1] is not None else "diag"] * n_pad
    body_kinds = []
    for j in range((len(pairs) - 1) // ATTN_UNROLL):
        ks = set(kinds[1 + j * ATTN_UNROLL:1 + (j + 1) * ATTN_UNROLL])
        assert len(ks) == 1, "a loop body must issue scores of one mask kind"
        body_kinds.append(ks.pop())
    table = np.asarray(pairs, np.int32)
    return table[:, 0], table[:, 1], table[:, 2], tuple(body_kinds)


def _attn_kernel(*refs, n_pairs, heads_per_pair, heads_per_k, heads_per_v, has_sel, slot_of, body_kinds,
                 clamp_sum):
    n_in = 11 if has_sel else 10
    (qtab_ref, ctab_ref, vtab_ref, slopes_ref, qT_ref, k_ref, ktab_ref, vT_ref, qbias_ref,
     maskbias_ref) = refs[:10]
    negsel_ref = refs[10] if has_sel else None
    oT_ref = refs[n_in]
    qaug_ref, vaug_ref, m_ref, acc_ref, excess_ref = refs[n_in + 1:n_in + 6]
    s_bufs = refs[n_in + 6:n_in + 8]
    cmax_bufs = refs[n_in + 8:n_in + 10]
    alpha_bufs = refs[n_in + 10:n_in + 12]
    p_bufs = refs[n_in + 12:n_in + 14]
    grp = pl.program_id(1)
    tq, tk, hd = ATTN_TQ, ATTN_TK, HEAD_DIM
    nq = qaug_ref.shape[0]
    hpp = heads_per_pair
    nh = n_pairs * hpp
    pcol = hpp * tq
    ncol = n_pairs * pcol
    n_vgroups = nh // heads_per_v
    vcol = heads_per_v * tq
    arows = hd + ONES_ROWS
    n_steps = qtab_ref.shape[0]
    unroll = ATTN_UNROLL

    for vg in range(n_vgroups):
        vaug_ref[vg * arows:vg * arows + hd, :] = vT_ref[vg * hd:(vg + 1) * hd, :]
        vaug_ref[vg * arows + hd:(vg + 1) * arows, :] = jnp.ones((ONES_ROWS, vaug_ref.shape[1]), BF16)

    for qi in range(nq):
        toks = slice(qi * tq, (qi + 1) * tq)
        for i in range(nh):
            cols = slice(i * tq, (i + 1) * tq)
            slot = slot_of(i % hpp, grp)
            q = qT_ref[i * hd:(i + 1) * hd, toks]
            bias = [jnp.concatenate([qbias_ref[i]] * (tq // LANES), axis=1)]
            if has_sel:
                bias.append(negsel_ref[i % negsel_ref.shape[0], :, toks])
            used = sum(part.shape[0] for part in bias)
            bias = jnp.concatenate(bias + [jnp.zeros((hd - used, tq), BF16)], axis=0)
            qaug_ref[qi, 0:hd, cols] = jnp.where(slot == 0, q, bias)
            qaug_ref[qi, hd:2 * hd, cols] = jnp.where(slot == 0, bias, q)

    lane_half = lax.broadcasted_iota(jnp.int32, (tk, LANES), 1) // hd

    def key_operand(keys, pr, slot):
        return jnp.where(lane_half == slot, k_ref[keys, pr * LANES:(pr + 1) * LANES], ktab_ref[slot, keys, :])

    col_groups = [(pr, slot_of(j0, grp), slice((pr * hpp + j0) * tq, (pr * hpp + j0 + heads_per_k) * tq))
                  for pr in range(n_pairs) for j0 in range(0, hpp, heads_per_k)]

    key_i = lax.broadcasted_iota(jnp.int32, (tk, ncol), 0)
    qry_row = lax.broadcasted_iota(jnp.int32, (1, ncol), 1) % tq
    big = jnp.int32(2 ** 30)

    def visible(kind, valid):
        if kind == "diag":
            return key_i <= jnp.where(valid, qry_row, -1)
        if kind == "lower":
            return key_i > jnp.where(valid, qry_row, big)
        if kind == "middle":
            return key_i >= jnp.where(valid, 0, big)
        return None

    def key_slice(t):
        return pl.ds(pl.multiple_of(ctab_ref[t] * tk, tk), tk)

    def scores(t, buf, kind):
        qi = qtab_ref[t]
        keys = key_slice(t)
        mask = visible(kind, vtab_ref[t] > 0)
        for pr, slot, cols in col_groups:
            s = _dot(key_operand(keys, pr, slot), qaug_ref[qi, :, cols])
            if mask is not None:
                s = jnp.where(mask[:, cols], s, NEG_INF)
            s_bufs[buf][:, cols] = s
            cmax_bufs[buf][:, cols] = jnp.max(s, axis=0, keepdims=True)

    def softmax(t, buf):
        qi = qtab_ref[t]
        m = m_ref[qi]
        m_new = jnp.maximum(m, cmax_bufs[buf][...])
        m_ref[qi] = m_new
        alpha_bufs[buf][...] = jnp.exp2(m - m_new)
        p_bufs[buf][...] = jnp.exp2(s_bufs[buf][...] - m_new).astype(BF16)

    def weighted(t, buf, rescale):
        qi = qtab_ref[t]
        keys = key_slice(t)
        for vg in range(n_vgroups):
            cols = slice(vg * vcol, (vg + 1) * vcol)
            pv = _dot(vaug_ref[vg * arows:(vg + 1) * arows, keys], p_bufs[buf][:, cols])
            old = acc_ref[qi, :, cols]
            acc_ref[qi, :, cols] = (alpha_bufs[buf][:, cols] * old if rescale else old) + pv

    def fast_scores(t, buf, kind):
        qi = qtab_ref[t]
        keys = key_slice(t)
        ref = jnp.where(vtab_ref[t] > 0, m_ref[qi], -NEG_INF)
        for pr, slot, cols in col_groups:
            s = _dot(key_operand(keys, pr, slot), qaug_ref[qi, :, cols])
            if kind in MASK_BIAS_ROW:
                s = s + maskbias_ref[MASK_BIAS_ROW[kind], :, cols]
            excess_ref[:, cols] = jnp.maximum(excess_ref[:, cols], jnp.max(s, axis=0, keepdims=True) - ref[:, cols])
            p_bufs[buf][:, cols] = jnp.exp2(s - ref[:, cols]).astype(BF16)

    def pipeline(step, max_body):
        first = 0
        for kind in sorted(set(body_kinds), key=body_kinds.index):
            count = body_kinds.count(kind)
            assert body_kinds[first:first + count] == (kind,) * count
            merge = max(m for m in range(1, count + 1) if count % m == 0 and m * unroll <= max_body)
            steps = merge * unroll

            @pl.loop(0, count // merge)
            def _(j, kind=kind, base=first * unroll, steps=steps):
                for u in range(steps):
                    step(base + j * steps + u, u, kind)

            first += count

    last = n_steps - 1

    for qi in range(nq):
        pos = (qi * tq + lax.broadcasted_iota(jnp.int32, (1, tq), 1)).astype(F32)
        for i in range(nh):
            m_ref[qi, :, i * tq:(i + 1) * tq] = slopes_ref[grp * nh + i] * pos
    acc_ref[...] = jnp.zeros(acc_ref.shape, F32)
    excess_ref[...] = jnp.full((1, ncol), NEG_INF, F32)
    fast_scores(0, 0, "diag")

    def fast_step(t, u, kind):
        fast_scores(t + 1, (u + 1) % 2, kind)
        weighted(t, u % 2, rescale=False)

    pipeline(fast_step, ATTN_FAST_BODY)
    weighted(last, last % 2, rescale=False)

    excess = excess_ref[...]
    @pl.when((jnp.max(excess) > SAFE_EXCESS) | (jnp.min(excess) < -SAFE_EXCESS))
    def _():
        m_ref[...] = jnp.full(m_ref.shape, NEG_INF, F32)
        acc_ref[...] = jnp.zeros(acc_ref.shape, F32)
        p_bufs[1][...] = jnp.zeros(p_bufs[1].shape, BF16)
        alpha_bufs[1][...] = jnp.ones((1, ncol), F32)
        scores(0, 0, "diag")

        def safe_step(t, u, kind):
            scores(t + 1, (u + 1) % 2, kind)
            weighted(jnp.maximum(t - 1, 0), (u + 1) % 2, rescale=True)
            softmax(t, u % 2)

        pipeline(safe_step, ATTN_SAFE_BODY)
        weighted(last - 1, (last - 1) % 2, rescale=True)
        softmax(last, last % 2)
        weighted(last, last % 2, rescale=True)

    for qi in range(nq):
        for i in range(nh):
            cols = slice(i * tq, (i + 1) * tq)
            l = acc_ref[qi, hd:hd + 1, cols]
            oT_ref[i * hd:(i + 1) * hd, qi * tq:(qi + 1) * tq] = (
                acc_ref[qi, 0:hd, cols] * (1.0 / (jnp.maximum(l, TINY) if clamp_sum else l)))


def _attention(slopes2, qT, k_tok, key_table, vT, qbias, negsel, *, batch, n_steps, n_pairs,
               heads_per_pair, heads_per_k, heads_per_v, q_block, k_block, v_block, slot_of, window=None,
               clamp_sum, name):
    n = qT.shape[1]
    s = n // batch
    tq, tk, hd = ATTN_TQ, ATTN_TK, HEAD_DIM
    assert tq == tk and s % tq == 0
    nq = s // tq
    nh = n_pairs * heads_per_pair
    assert nh % heads_per_v == 0
    ncol = nh * tq
    arows = hd + ONES_ROWS
    n_vgroups = nh // heads_per_v
    if window is not None:
        assert window % tk == 0
    q_tab, c_tab, v_tab, body_kinds = _attn_schedule(nq, None if window is None else window // tk)
    key_i, qry_i = np.arange(tk)[:, None], (np.arange(ncol) % tq)[None, :]
    tables = {"diag": key_i <= qry_i, "lower": key_i > qry_i}
    mask_bias = jnp.asarray(np.stack([np.where(tables[k], 0.0, NEG_INF) for k in MASK_BIAS_ROW
                                      if k in body_kinds]), F32)
    smem = pl.BlockSpec(memory_space=pltpu.SMEM)
    in_specs = [
        smem, smem, smem, smem,
        pl.BlockSpec((nh * hd, s), lambda b, g: (q_block(g), b)),
        pl.BlockSpec((s, n_pairs * LANES), lambda b, g: (b, k_block(g))),
        _resident(key_table.shape),
        pl.BlockSpec((n_vgroups * hd, s), lambda b, g: (v_block(g), b)),
        pl.BlockSpec((nh, 16, LANES), lambda b, g: (g, 0, 0)),
        _resident(mask_bias.shape),
    ]
    args = [jnp.asarray(q_tab), jnp.asarray(c_tab), jnp.asarray(v_tab), slopes2, qT, k_tok, key_table, vT,
            qbias, mask_bias]
    if negsel is not None:
        per_step = negsel.shape[0] // n_steps
        in_specs.append(pl.BlockSpec((per_step, negsel.shape[1], s), lambda b, g: (g, 0, b)))
        args.append(negsel)
    vec = lambda: pltpu.VMEM((1, ncol), F32)
    return pl.pallas_call(
        functools.partial(_attn_kernel, n_pairs=n_pairs, heads_per_pair=heads_per_pair,
                          heads_per_k=heads_per_k, heads_per_v=heads_per_v, has_sel=negsel is not None,
                          slot_of=slot_of,
                          body_kinds=body_kinds, clamp_sum=clamp_sum),
        grid=(batch, n_steps),
        in_specs=in_specs,
        out_specs=pl.BlockSpec((nh * hd, s), lambda b, g: (g, b)),
        out_shape=jax.ShapeDtypeStruct((n_steps * nh * hd, n), F32),
        scratch_shapes=[pltpu.VMEM((nq, 2 * hd, ncol), BF16),
                        pltpu.VMEM((n_vgroups * arows, s), BF16),
                        pltpu.VMEM((nq, 1, ncol), F32),
                        pltpu.VMEM((nq, arows, ncol), F32),
                        vec(),
                        pltpu.VMEM((tk, ncol), F32), pltpu.VMEM((tk, ncol), F32),
                        vec(), vec(), vec(), vec(),
                        pltpu.VMEM((tk, ncol), BF16), pltpu.VMEM((tk, ncol), BF16)],
        compiler_params=_cparams("parallel", "parallel"),
        name=name,
    )(*args)


def _mix_kernel(h_ref, g_ref, zgT_ref, ocmp_ref, osel_ref, owin_ref, omoba_ref,
                wmerge_ref, wn_ref, wm_ref, wo_ref, o_ref, gate_ref, onsa_ref):
    h = h_ref[...]
    u = _rms(h, g_ref[...]).astype(BF16)
    gate_ref[...] = _sigmoid(zgT_ref[0:gate_ref.shape[0], :])
    hd = HEAD_DIM
    for hh in range(NSA_HEADS):
        rows = slice(hh * hd, (hh + 1) * hd)
        o = None
        for r, branch in enumerate((ocmp_ref, osel_ref, owin_ref)):
            gate = gate_ref[3 * hh + r:3 * hh + r + 1, :]
            term = gate * branch[rows, :]
            o = term if o is None else o + term
        onsa_ref[rows, :] = o.astype(BF16)
    y_n = _dot_tn(onsa_ref[...], wn_ref[...])
    y_m = _dot_tn(omoba_ref[...].astype(BF16), wm_ref[...])
    mixed = (_sigmoid(_dot(u, wmerge_ref[:, :D_MODEL])) * y_n
             + _sigmoid(_dot(u, wmerge_ref[:, D_MODEL:])) * y_m)
    o_ref[...] = h + _dot(mixed.astype(BF16), wo_ref[...])


def _mix(h, g, zgT, ocmpT, oselT, owinT, omobaT, wmerge, wn, wm, wo):
    n = h.shape[0]
    tm = TOKEN_TILE
    rows = lambda width: pl.BlockSpec((tm, width), lambda i: (i, 0))
    cols = lambda height: pl.BlockSpec((height, tm), lambda i: (0, i))
    gate_rows = -(-N_GATE_LOGITS // 8) * 8
    return pl.pallas_call(
        _mix_kernel,
        grid=(n // tm,),
        in_specs=[rows(D_MODEL), _resident((1, D_MODEL)), cols(LANES), cols(NSA_WIDTH), cols(NSA_WIDTH),
                  cols(NSA_WIDTH), cols(MOBA_WIDTH), _resident(wmerge.shape),
                  _resident(wn.shape), _resident(wm.shape), _resident(wo.shape)],
        out_specs=rows(D_MODEL),
        out_shape=jax.ShapeDtypeStruct((n, D_MODEL), F32),
        scratch_shapes=[pltpu.VMEM((gate_rows, tm), F32), pltpu.VMEM((NSA_WIDTH, tm), BF16)],
        compiler_params=_cparams("parallel"),
        name="mix",
    )(h, g, zgT, ocmpT, oselT, owinT, omobaT, wmerge, wn, wm, wo)


def _key_table(s, block):
    pos = np.arange(s)
    tab = np.zeros((s, HEAD_DIM), np.float32)
    tab[:, 0:N_SLOPE_PIECES] = ((pos // LANES) * LANES)[:, None]
    tab[:, N_SLOPE_PIECES:2 * N_SLOPE_PIECES] = (pos % LANES)[:, None]
    if block is not None:
        tab[pos, SLOPE_ROWS + pos // block] = 1.0
    zeros = np.zeros_like(tab)
    both = np.stack([np.concatenate([zeros, tab], axis=1), np.concatenate([tab, zeros], axis=1)])
    return jnp.asarray(both, BF16)


def _slope_rows(slopes2):
    s1 = slopes2.astype(BF16)
    r1 = slopes2 - s1.astype(F32)
    s2 = r1.astype(BF16)
    s3 = (r1 - s2.astype(F32)).astype(BF16)
    rows = jnp.stack([s1, s2, s3, s1, s2, s3], axis=1)
    rows = jnp.pad(rows, ((0, 0), (0, 16 - 2 * N_SLOPE_PIECES)))
    return jnp.broadcast_to(rows[:, :, None], rows.shape + (LANES,))


def _compress_weights(pos, w1):
    st, hd, gw, hidden, groups = NSA_CMP_STRIDE, HEAD_DIM, NSA_KV_WIDTH, NSA_CMP_HIDDEN, NSA_KV_GROUPS
    w = jnp.zeros((2, st, gw, groups * hidden), F32)
    for g in range(groups):
        w = w.at[:, :, g * hd:(g + 1) * hd, g * hidden:(g + 1) * hidden].set(w1.reshape(2, st, hd, hidden))
    ps = jnp.tile(pos.reshape(2, st, 1, hd), (1, 1, groups, 1))
    return ps.reshape(2, 1, st * gw), w.reshape(2, st * gw, groups * hidden).astype(BF16)


def kernel(x, p, ffn1_norm, ffn1_w1, ffn1_w3, ffn1_w2, mix_norm, w_in, cmp_pos_k, cmp_w1_k, cmp_w2_k, cmp_pos_v, cmp_w1_v, cmp_w2_v, w_up_nsa, w_up_moba, w_out, ffn2_norm, ffn2_w1, ffn2_w3, ffn2_w2, ple_norm, w_ple_gate, w_ple, final_norm):
    b, s, d = x.shape
    n = b * s
    depth = p.shape[0]
    n_all = NSA_HEADS + MOBA_HEADS
    slopes = jnp.exp2(-8.0 * (jnp.arange(n_all, dtype=F32) + 1.0) / n_all)
    slopes2_n, slopes2_m = slopes[0::2] * LOG2E, slopes[1::2] * LOG2E
    qbias_n, qbias_m = _slope_rows(slopes2_n), _slope_rows(slopes2_m)

    n_cmp_rows = s // NSA_CMP_STRIDE
    n_cmp = (s - NSA_CMP_BLOCK) // NSA_CMP_STRIDE + 1
    n_sel = s // NSA_SEL_BLOCK
    c_start = np.arange(n_cmp_rows) * NSA_CMP_STRIDE
    s_start = np.arange(n_sel) * NSA_SEL_BLOCK
    overlap_t = ((c_start[None, :] <= s_start[:, None] + NSA_SEL_BLOCK - 1)
                 & (c_start[None, :] + NSA_CMP_BLOCK - 1 >= s_start[:, None])
                 & (np.arange(n_cmp_rows)[None, :] < n_cmp))
    overlap_t = jnp.asarray(np.tile(overlap_t, (1, 3)), BF16)
    table_sel = _key_table(s, NSA_SEL_BLOCK)
    table_win = _key_table(s, None)
    table_moba = _key_table(s, MOBA_BLOCK)

    h = x.reshape(n, d)
    for i in range(depth):
        vec = lambda a: a[i].reshape(1, -1)
        wi = w_in[i]
        col = lambda lo, width: wi[:, lo:lo + width]
        wq_t = jnp.concatenate([col(0, NSA_WIDTH), col(OFF_MOBA_Q, MOBA_WIDTH)], axis=1).T.astype(BF16)
        wk = jnp.concatenate([col(OFF_MOBA_K, MOBA_WIDTH), col(OFF_KSLC, NSA_KV_WIDTH),
                              col(OFF_KWIN, NSA_KV_WIDTH)], axis=1).astype(BF16)
        wc = jnp.stack([col(OFF_KCMP, NSA_KV_WIDTH), col(OFF_VCMP, NSA_KV_WIDTH)]).astype(BF16)
        wv_t = jnp.concatenate([col(OFF_VSLC, NSA_KV_WIDTH), col(OFF_VWIN, NSA_KV_WIDTH),
                                col(OFF_MOBA_V, MOBA_WIDTH)], axis=1).T.astype(BF16)
        wg_t = jnp.pad(col(OFF_GATE_LOGITS, N_GATE_LOGITS), ((0, 0), (0, LANES - N_GATE_LOGITS))).T.astype(BF16)
        w_merge = col(OFF_MERGE, 2 * D_MODEL).astype(BF16)
        h, qT, k_tok, cmp_tok, vT, zgT = _ffn(
            h, vec(ffn1_norm), ffn1_w1[i].astype(BF16), ffn1_w3[i].astype(BF16), ffn1_w2[i].astype(BF16),
            project=(vec(mix_norm), wq_t, wk, wc, wv_t, wg_t))

        cmp_w = [_compress_weights(ps[i], w1[i]) for ps, w1 in ((cmp_pos_k, cmp_w1_k), (cmp_pos_v, cmp_w1_v))]
        pos = jnp.stack([pw[0] for pw in cmp_w])
        w1c = jnp.stack([pw[1] for pw in cmp_w])
        w2 = jnp.stack([cmp_w2_k[i], cmp_w2_v[i]])
        cmp_rows = cmp_tok.reshape(2, b, n_cmp_rows, NSA_CMP_STRIDE * NSA_KV_WIDTH)
        kc, kcT = _compress(cmp_rows, pos, w1c, w2, w2.transpose(0, 2, 1))

        ocmpT, negsel_n = _cmp_select(slopes2_n, qT, kc, kcT, overlap_t, b)
        hs = ATTN_HEADS_PER_STEP
        q_rows, pair = hs * HEAD_DIM, 2 * HEAD_DIM
        nsa = dict(batch=b, n_steps=NSA_KV_GROUPS, n_pairs=1, heads_per_pair=hs, heads_per_k=hs, heads_per_v=hs,
                   q_block=lambda g: QT_ROW_NSA // q_rows + g, slot_of=lambda j, g: g, clamp_sum=True)
        oselT = _attention(slopes2_n, qT, k_tok, table_sel, vT, qbias_n, negsel_n,
                           k_block=lambda g: K_COL_SLC // pair, v_block=lambda g: VT_ROW_SLC // HEAD_DIM + g,
                           name="attn_select", **nsa)
        owinT = _attention(slopes2_n, qT, k_tok, table_win, vT, qbias_n, None,
                           k_block=lambda g: K_COL_WIN // pair, v_block=lambda g: VT_ROW_WIN // HEAD_DIM + g,
                           window=NSA_WINDOW, name="attn_window", **nsa)
        negsel_m = _moba_select(qT, k_tok, b, q_row_block=QT_ROW_MOBA // MOBA_WIDTH,
                                k_col_block=K_COL_MOBA // MOBA_WIDTH)
        omobaT = _attention(slopes2_m, qT, k_tok, table_moba, vT, qbias_m, negsel_m, batch=b,
                            n_steps=MOBA_HEADS // hs, n_pairs=hs // 2, heads_per_pair=2, heads_per_k=1, heads_per_v=1,
                            q_block=lambda g: QT_ROW_MOBA // q_rows + g,
                            k_block=lambda g: K_COL_MOBA // (hs // 2 * pair) + g,
                            v_block=lambda g: VT_ROW_MOBA // q_rows + g, slot_of=lambda j, g: j,
                            clamp_sum=False, name="attn_moba")

        h = _mix(h, vec(mix_norm), zgT, ocmpT, oselT, owinT, omobaT, w_merge,
                 w_up_nsa[i].astype(BF16), w_up_moba[i].astype(BF16), w_out[i].astype(BF16))

        h = _ffn(h, vec(ffn2_norm), ffn2_w1[i].astype(BF16), ffn2_w3[i].astype(BF16), ffn2_w2[i].astype(BF16),
                 tail=(p[i].reshape(n, PLE_DIM), vec(ple_norm), w_ple_gate[i].astype(BF16),
                       w_ple[i].astype(BF16)),
                 final_norm=final_norm.reshape(1, -1) if i + 1 == depth else None)
    return h.reshape(b, s, d)
```

```python
import functools

import numpy as np
import jax
import jax.numpy as jnp
from jax import lax
from jax.experimental import pallas as pl
from jax.experimental.pallas import tpu as pltpu

F32 = jnp.float32
BF16 = jnp.bfloat16

D_MODEL = 1024
HEAD_DIM = 64
NSA_HEADS = 8
NSA_KV_GROUPS = 2
NSA_HEADS_PER_GROUP = NSA_HEADS // NSA_KV_GROUPS
NSA_CMP_BLOCK = 32
NSA_CMP_STRIDE = 16
NSA_CMP_HIDDEN = 128
NSA_SEL_BLOCK = 64
NSA_SEL_TOPK = 16
NSA_WINDOW = 512
MOBA_HEADS = 8
MOBA_BLOCK = 256
MOBA_TOPK = 3
D_FF = 2816
PLE_DIM = 256
RMS_EPS = 1e-6
NEG_INF = -1e30
TINY = 1e-30
FORCE_SCORE = 1e9
LOG2E = float(np.log2(np.e))
Q_SCALE = HEAD_DIM ** -0.5 * LOG2E

NSA_WIDTH = NSA_HEADS * HEAD_DIM
NSA_KV_WIDTH = NSA_KV_GROUPS * HEAD_DIM
MOBA_WIDTH = MOBA_HEADS * HEAD_DIM
N_GATE_LOGITS = 3 * NSA_HEADS
OFF_KCMP = NSA_WIDTH
OFF_VCMP = OFF_KCMP + NSA_KV_WIDTH
OFF_KSLC = OFF_VCMP + NSA_KV_WIDTH
OFF_VSLC = OFF_KSLC + NSA_KV_WIDTH
OFF_KWIN = OFF_VSLC + NSA_KV_WIDTH
OFF_VWIN = OFF_KWIN + NSA_KV_WIDTH
OFF_GATE_LOGITS = OFF_VWIN + NSA_KV_WIDTH
OFF_MOBA_Q = OFF_GATE_LOGITS + N_GATE_LOGITS
OFF_MOBA_K = OFF_MOBA_Q + MOBA_WIDTH
OFF_MOBA_V = OFF_MOBA_K + MOBA_WIDTH
OFF_MERGE = OFF_MOBA_V + MOBA_WIDTH

LANES = 128
VMEM_LIMIT = 56 * 1024 * 1024

TOKEN_TILE = 512
FF_CHUNK = 256
ATTN_TQ = 256
ATTN_TK = 256
CMP_TQ = 2048
ATTN_UNROLL = 2
ATTN_FAST_BODY = 28
ATTN_SAFE_BODY = 4
SLOPE_ROWS = 16
N_SLOPE_PIECES = 3
MOBA_SEL_ROWS = 16
ONES_ROWS = 16
MASK_BIAS_ROW = {"diag": 0, "lower": 1}
SAFE_EXCESS = 60.0

QT_ROW_NSA, QT_ROW_MOBA = 0, NSA_WIDTH
K_COL_MOBA, K_COL_SLC, K_COL_WIN = 0, MOBA_WIDTH, MOBA_WIDTH + NSA_KV_WIDTH
VT_ROW_SLC, VT_ROW_WIN, VT_ROW_MOBA = 0, NSA_KV_WIDTH, 2 * NSA_KV_WIDTH
ATTN_HEADS_PER_STEP = 4

_NT = (((1,), (1,)), ((), ()))
_TN = (((0,), (0,)), ((), ()))


def _cparams(*sem):
    return pltpu.CompilerParams(dimension_semantics=sem, vmem_limit_bytes=VMEM_LIMIT)


def _resident(shape):
    nd = len(shape)
    return pl.BlockSpec(shape, lambda *_: (0,) * nd, pipeline_mode=pl.Buffered(1))


def _rms(x, g):
    ms = jnp.mean(x * x, axis=-1, keepdims=True)
    return x * lax.rsqrt(ms + RMS_EPS) * g


def _sigmoid(x):
    return 1.0 / (1.0 + jnp.exp(-x))


def _dot(a, b):
    return jnp.dot(a, b, preferred_element_type=F32)


def _dot_f32(a, b):
    return jnp.dot(a, b, preferred_element_type=F32, precision=lax.Precision.HIGHEST)


def _dot_nt(a, b, precision=None):
    return lax.dot_general(a, b, _NT, preferred_element_type=F32, precision=precision)


def _dot_tn(a, b):
    return lax.dot_general(a, b, _TN, preferred_element_type=F32)


def _ffn_kernel(*refs, tail, final, project):
    x_ref, g_ref, w1_ref, w3_ref, w2_ref = refs[:5]
    n_in = 5
    if tail:
        p_ref, gp_ref, wpg_ref, wp_ref = refs[n_in:n_in + 4]
        n_in += 4
    if final:
        gf_ref = refs[n_in]
        n_in += 1
    if project:
        gm_ref, wq_ref, wk_ref, wc_ref, wv_ref, wg_ref = refs[n_in:n_in + 6]
        n_in += 6
    o_ref = refs[n_in]
    acc_ref = refs[-1]
    x = x_ref[...]
    xn = _rms(x, g_ref[...]).astype(BF16)
    for c in range(D_FF // FF_CHUNK):
        cols = slice(c * FF_CHUNK, (c + 1) * FF_CHUNK)
        a = _dot(xn, w1_ref[:, cols])
        b = _dot(xn, w3_ref[:, cols])
        hid = (a * _sigmoid(a) * b).astype(BF16)
        y = _dot(hid, w2_ref[cols, :])
        if c == 0:
            acc_ref[...] = y
        else:
            acc_ref[...] += y
    h = x + 0.5 * acc_ref[...]
    if tail:
        gate = _sigmoid(_dot(_rms(h, gp_ref[...]).astype(BF16), wpg_ref[...]))
        h = h + gate * _dot(p_ref[...].astype(BF16), wp_ref[...])
    if final:
        h = _rms(h, gf_ref[...])
    o_ref[...] = h
    if project:
        qT_ref, k_ref, cmp_ref, vT_ref, zgT_ref = refs[n_in + 1:n_in + 6]
        u = _rms(h, gm_ref[...]).astype(BF16)
        qT_ref[...] = (_dot_nt(wq_ref[...], u) * Q_SCALE).astype(BF16)
        k_ref[...] = _dot(u, wk_ref[...]).astype(BF16)
        for kind in range(2):
            cmp_ref[kind] = _dot(u, wc_ref[kind]).astype(BF16)
        vT_ref[...] = _dot_nt(wv_ref[...], u).astype(BF16)
        zgT_ref[...] = _dot_nt(wg_ref[...], u)


def _ffn(x, g, w1, w3, w2, tail=None, final_norm=None, project=None):
    n = x.shape[0]
    tm = TOKEN_TILE
    row = lambda w: pl.BlockSpec((tm, w), lambda i: (i, 0))
    col = lambda height: pl.BlockSpec((height, tm), lambda i: (0, i))
    args = [x, g, w1, w3, w2]
    specs = [row(D_MODEL), _resident((1, D_MODEL)), _resident(w1.shape), _resident(w3.shape),
             _resident(w2.shape)]
    out_specs = [row(D_MODEL)]
    out_shape = [jax.ShapeDtypeStruct((n, D_MODEL), F32)]
    if tail is not None:
        p, gp, wpg, wp = tail
        args += [p, gp, wpg, wp]
        specs += [row(PLE_DIM), _resident((1, D_MODEL)), _resident(wpg.shape), _resident(wp.shape)]
    if final_norm is not None:
        args.append(final_norm)
        specs.append(_resident((1, D_MODEL)))
    if project is not None:
        gm, wq_t, wk, wc, wv_t, wg_t = project
        args += [gm, wq_t, wk, wc, wv_t, wg_t]
        specs += [_resident(a.shape) for a in project]
        out_specs += [col(wq_t.shape[0]), row(wk.shape[1]),
                      pl.BlockSpec((2, tm, NSA_KV_WIDTH), lambda i: (0, i, 0)),
                      col(wv_t.shape[0]), col(LANES)]
        out_shape += [jax.ShapeDtypeStruct((wq_t.shape[0], n), BF16),
                      jax.ShapeDtypeStruct((n, wk.shape[1]), BF16),
                      jax.ShapeDtypeStruct((2, n, NSA_KV_WIDTH), BF16),
                      jax.ShapeDtypeStruct((wv_t.shape[0], n), BF16),
                      jax.ShapeDtypeStruct((LANES, n), F32)]
    outs = pl.pallas_call(
        functools.partial(_ffn_kernel, tail=tail is not None, final=final_norm is not None,
                          project=project is not None),
        grid=(n // tm,),
        in_specs=specs,
        out_specs=out_specs,
        out_shape=out_shape,
        scratch_shapes=[pltpu.VMEM((tm, D_MODEL), F32)],
        compiler_params=_cparams("parallel"),
        name="ffn" + ("_tail" if tail is not None else "") + ("_project" if project is not None else ""),
    )(*args)
    return outs if project is not None else outs[0]


def _split3(x):
    hi = x.astype(BF16)
    r = x - hi.astype(F32)
    mid = r.astype(BF16)
    return hi, mid, (r - mid.astype(F32)).astype(BF16)


def _compress_kernel(rows_ref, pos_ref, w1_ref, w2_ref, w2t_ref, o_ref, ot_ref):
    c0 = float(np.sqrt(2.0 / np.pi))
    hidden = NSA_CMP_HIDDEN
    for kind in range(2):
        rows = rows_ref[kind].astype(F32)
        first = _dot((rows + pos_ref[kind, 0]).astype(BF16), w1_ref[kind, 0])
        second = _dot((rows + pos_ref[kind, 1]).astype(BF16), w1_ref[kind, 1])
        n_rows = rows.shape[0]
        pre = first + pltpu.roll(second, n_rows - 1, 0)
        hid = pre * (0.5 * (1.0 + jnp.tanh(c0 * (pre + 0.044715 * (pre * pre * pre)))))
        for g in range(NSA_KV_GROUPS):
            hid_g = hid[:, g * hidden:(g + 1) * hidden]
            out = _dot_f32(hid_g, w2_ref[kind])
            o_ref[kind * NSA_KV_GROUPS + g] = jnp.concatenate(_split3(out) + (jnp.zeros(out.shape, BF16),), axis=1)
            ot_ref[kind * NSA_KV_GROUPS + g] = _dot_nt(w2t_ref[kind], hid_g, lax.Precision.HIGHEST).astype(BF16)


def _compress(cmp_rows, pos, w1, w2, w2t):
    _, b, n_rows, width = cmp_rows.shape
    n_out = 2 * NSA_KV_GROUPS
    return pl.pallas_call(
        _compress_kernel,
        grid=(b,),
        in_specs=[
            pl.BlockSpec((2, None, n_rows, width), lambda i: (0, i, 0, 0)),
            _resident(pos.shape), _resident(w1.shape), _resident(w2.shape), _resident(w2t.shape),
        ],
        out_specs=[
            pl.BlockSpec((None, n_out, n_rows, 4 * HEAD_DIM), lambda i: (i, 0, 0, 0)),
            pl.BlockSpec((None, n_out, HEAD_DIM, n_rows), lambda i: (i, 0, 0, 0)),
        ],
        out_shape=[jax.ShapeDtypeStruct((b, n_out, n_rows, 4 * HEAD_DIM), BF16),
                   jax.ShapeDtypeStruct((b, n_out, HEAD_DIM, n_rows), BF16)],
        compiler_params=_cparams("parallel"),
        name="compress",
    )(cmp_rows, pos, w1, w2, w2t)


def _count_rank(score, n_candidates):
    sub = 8
    tiles = [(lo, min(lo + sub, score.shape[0])) for lo in range(0, score.shape[0], sub)]
    ranks = [jnp.zeros((hi - lo,) + score.shape[1:], jnp.int32) for lo, hi in tiles]
    for j in range(n_candidates):
        row = score[j:j + 1, :]
        for n, (lo, hi) in enumerate(tiles):
            part = score[lo:hi]
            if lo > j:
                ahead = row >= part
            elif hi - 1 <= j:
                ahead = row > part
            else:
                after = lo + lax.broadcasted_iota(jnp.int32, part.shape, 0) > j
                ahead = (row > part) | ((row == part) & after)
            ranks[n] = ranks[n] + ahead.astype(jnp.int32)
    return ranks[0] if len(ranks) == 1 else jnp.concatenate(ranks, axis=0)


def _cmp_select_kernel(slopes_ref, qT_ref, kc_ref, vcT_ref, ovT_ref, oT_ref, negsel_ref):
    g = pl.program_id(1)
    qi = pl.program_id(2)
    tq = qT_ref.shape[1]
    n_cmp = kc_ref.shape[0]
    n_sel = ovT_ref.shape[0]
    hd = HEAD_DIM
    t = qi * tq + lax.broadcasted_iota(jnp.int32, (n_cmp, tq), 1)
    blk_c = lax.broadcasted_iota(jnp.int32, (n_cmp, tq), 0)
    dist = t - (blk_c * NSA_CMP_STRIDE + (NSA_CMP_BLOCK - 1))
    visible = dist >= 0
    dist_f = dist.astype(F32)
    kc3 = kc_ref[...]
    vcT = vcT_ref[...]
    p_sum = jnp.zeros((n_cmp, tq), F32)
    for hh in range(NSA_HEADS_PER_GROUP):
        slope2 = slopes_ref[g * NSA_HEADS_PER_GROUP + hh]
        q = qT_ref[hh * hd:(hh + 1) * hd, :]
        s = _dot(kc3, jnp.concatenate([q, q, q, jnp.zeros_like(q)], axis=0)) - slope2 * dist_f
        s = jnp.where(visible, s, NEG_INF)
        m = jnp.max(s, axis=0, keepdims=True)
        e = jnp.exp2(s - m)
        any_visible = m > 0.5 * NEG_INF
        p = e * jnp.where(any_visible, 1.0 / jnp.maximum(jnp.sum(e, axis=0, keepdims=True), TINY), 0.0)
        p_sum = p_sum + p
        oT_ref[hh * hd:(hh + 1) * hd, :] = _dot(vcT, p.astype(BF16))
    imp = _dot(ovT_ref[...], jnp.concatenate(_split3(p_sum), axis=0))
    ts = qi * tq + lax.broadcasted_iota(jnp.int32, (n_sel, tq), 1)
    blk = lax.broadcasted_iota(jnp.int32, (n_sel, tq), 0)
    cur = ts // NSA_SEL_BLOCK
    causal = blk * NSA_SEL_BLOCK <= ts
    forced = (blk == 0) | (blk == cur) | (blk == cur - 1)
    imp = jnp.where(causal, jnp.where(forced, FORCE_SCORE, imp), NEG_INF)
    rank = _count_rank(imp, n_sel)
    chosen = (rank < min(NSA_SEL_TOPK, n_sel)) & causal
    negsel_ref[...] = jnp.where(chosen, 0.0, NEG_INF).astype(BF16)


def _cmp_select(slopes2, qT, kc, vcT, overlap_t, batch):
    n = qT.shape[1]
    tq = min(CMP_TQ, n // batch)
    nq = n // batch // tq
    rows = NSA_HEADS_PER_GROUP * HEAD_DIM
    n_cmp = kc.shape[2]
    n_sel = overlap_t.shape[0]
    return pl.pallas_call(
        _cmp_select_kernel,
        grid=(batch, NSA_KV_GROUPS, nq),
        in_specs=[
            pl.BlockSpec(memory_space=pltpu.SMEM),
            pl.BlockSpec((rows, tq), lambda i, g, q: (g, i * nq + q)),
            pl.BlockSpec((None, None, n_cmp, 4 * HEAD_DIM), lambda i, g, q: (i, g, 0, 0)),
            pl.BlockSpec((None, None, HEAD_DIM, n_cmp), lambda i, g, q: (i, NSA_KV_GROUPS + g, 0, 0)),
            pl.BlockSpec((n_sel, 3 * n_cmp), lambda i, g, q: (0, 0)),
        ],
        out_specs=[
            pl.BlockSpec((rows, tq), lambda i, g, q: (g, i * nq + q)),
            pl.BlockSpec((None, n_sel, tq), lambda i, g, q: (g, 0, i * nq + q)),
        ],
        out_shape=[jax.ShapeDtypeStruct((NSA_WIDTH, n), F32),
                   jax.ShapeDtypeStruct((NSA_KV_GROUPS, n_sel, n), BF16)],
        compiler_params=_cparams("parallel", "parallel", "parallel"),
        name="cmp_select",
    )(slopes2, qT, kc, vcT, overlap_t)


def _moba_select_kernel(qT_ref, k_ref, negsel_ref):
    s = k_ref.shape[0]
    nb = s // MOBA_BLOCK
    hd = HEAD_DIM
    kmean = jnp.mean(k_ref[...].astype(F32).reshape(nb, MOBA_BLOCK, k_ref.shape[1]), axis=1)
    t = lax.broadcasted_iota(jnp.int32, (nb, s), 1)
    blk = lax.broadcasted_iota(jnp.int32, (nb, s), 0)
    cur = t // MOBA_BLOCK
    past = blk < cur
    for j in range(MOBA_HEADS):
        gs = _dot_f32(kmean[:, j * hd:(j + 1) * hd], qT_ref[j * hd:(j + 1) * hd, :].astype(F32))
        gs = jnp.where(past, gs, NEG_INF)
        rank = _count_rank(gs, nb)
        chosen = ((rank < min(MOBA_TOPK, nb - 1)) & past) | (blk == cur)
        negsel_ref[j, 0:nb, :] = jnp.where(chosen, 0.0, NEG_INF).astype(BF16)
        negsel_ref[j, nb:, :] = jnp.zeros((MOBA_SEL_ROWS - nb, s), BF16)


def _moba_select(qT, k_tok, batch, q_row_block, k_col_block):
    n = qT.shape[1]
    s = n // batch
    return pl.pallas_call(
        _moba_select_kernel,
        grid=(batch,),
        in_specs=[pl.BlockSpec((MOBA_WIDTH, s), lambda i: (q_row_block, i)),
                  pl.BlockSpec((s, MOBA_WIDTH), lambda i: (i, k_col_block))],
        out_specs=pl.BlockSpec((MOBA_HEADS, MOBA_SEL_ROWS, s), lambda i: (0, 0, i)),
        out_shape=jax.ShapeDtypeStruct((MOBA_HEADS, MOBA_SEL_ROWS, n), BF16),
        compiler_params=_cparams("parallel"),
        name="moba_select",
    )(qT, k_tok)


def _attn_schedule(nq, window_chunks):
    def padded(pairs, start):
        n_pad = (-(start + len(pairs) - 1)) % ATTN_UNROLL
        return pairs + [(0, 0, 0)] * n_pad

    diag = [(q, q, 1) for q in range(nq)]
    groups = [("diag", padded(diag, 0))]
    if window_chunks is None:
        full = [(q, c, 1) for q in range(nq) for c in range(q)]
        groups.append((None, full))
    else:
        assert window_chunks == 2
        done = len(groups[0][1])
        lower = padded([(q, q - 2, 1) for q in range(2, nq)], done)
        groups.append(("lower", lower))
        groups.append(("middle", [(q, q - 1, 1) for q in range(1, nq)]))
    pairs, kinds = [], []
    for kind, group in groups:
        pairs += group
        kinds += [kind] * len(group)
    n_pad = (-(len(pairs) - 1)) % ATTN_UNROLL
    pairs += [(0, 0, 0)] * n_pad
    kinds += [kinds[-1] if kinds[-1] is not None else "diag"] * n_pad
    body_kinds = []
    for j in range((len(pairs) - 1) // ATTN_UNROLL):
        ks = set(kinds[1 + j * ATTN_UNROLL:1 + (j + 1) * ATTN_UNROLL])
        assert len(ks) == 1, "a loop body must issue scores of one mask kind"
        body_kinds.append(ks.pop())
    table = np.asarray(pairs, np.int32)
    return table[:, 0], table[:, 1], table[:, 2], tuple(body_kinds)


def _attn_kernel(*refs, n_pairs, heads_per_pair, both_slots, heads_per_v, has_sel, slot_of, body_kinds,
                 clamp_sum):
    n_in = 11 if has_sel else 10
    (qtab_ref, ctab_ref, vtab_ref, slopes_ref, qT_ref, k_ref, ktab_ref, vT_ref, qbias_ref,
     maskbias_ref) = refs[:10]
    negsel_ref = refs[10] if has_sel else None
    oT_ref = refs[n_in]
    qaug_ref, vaug_ref, m_ref, acc_ref, excess_ref = refs[n_in + 1:n_in + 6]
    s_bufs = refs[n_in + 6:n_in + 8]
    cmax_bufs = refs[n_in + 8:n_in + 10]
    alpha_bufs = refs[n_in + 10:n_in + 12]
    p_bufs = refs[n_in + 12:n_in + 14]
    grp = pl.program_id(1)
    tq, tk, hd = ATTN_TQ, ATTN_TK, HEAD_DIM
    nq = qaug_ref.shape[0]
    hpp = heads_per_pair
    nh = n_pairs * hpp
    pcol = hpp * tq
    ncol = n_pairs * pcol
    n_vgroups = nh // heads_per_v
    vcol = heads_per_v * tq
    arows = hd + ONES_ROWS
    n_steps = qtab_ref.shape[0]
    unroll = ATTN_UNROLL

    for vg in range(n_vgroups):
        vaug_ref[vg * arows:vg * arows + hd, :] = vT_ref[vg * hd:(vg + 1) * hd, :]
        vaug_ref[vg * arows + hd:(vg + 1) * arows, :] = jnp.ones((ONES_ROWS, vaug_ref.shape[1]), BF16)

    for qi in range(nq):
        toks = slice(qi * tq, (qi + 1) * tq)
        for i in range(nh):
            cols = slice(i * tq, (i + 1) * tq)
            slot = slot_of(i % hpp, grp)
            q = qT_ref[i * hd:(i + 1) * hd, toks]
            bias = [jnp.concatenate([qbias_ref[i]] * (tq // LANES), axis=1)]
            if has_sel:
                bias.append(negsel_ref[i % negsel_ref.shape[0], :, toks])
            used = sum(part.shape[0] for part in bias)
            bias = jnp.concatenate(bias + [jnp.zeros((hd - used, tq), BF16)], axis=0)
            if both_slots:
                zero = jnp.zeros_like(q)
                qaug_ref[qi, 0:hd, cols] = jnp.where(slot == 0, q, zero)
                qaug_ref[qi, hd:2 * hd, cols] = jnp.where(slot == 0, zero, q)
                qaug_ref[qi, 2 * hd:3 * hd, cols] = bias
                qaug_ref[qi, 3 * hd:, cols] = zero
            else:
                qaug_ref[qi, 0:hd, cols] = jnp.where(slot == 0, q, bias)
                qaug_ref[qi, hd:2 * hd, cols] = jnp.where(slot == 0, bias, q)

    lane_half = lax.broadcasted_iota(jnp.int32, (tk, LANES), 1) // hd

    def key_operand(keys, pr, slot):
        k_pair = k_ref[keys, pr * LANES:(pr + 1) * LANES]
        if both_slots:
            return jnp.concatenate([k_pair, ktab_ref[1, keys, :]], axis=1)
        return jnp.where(lane_half == slot, k_pair, ktab_ref[slot, keys, :])

    if both_slots:
        col_groups = [(pr, None, slice(pr * pcol, (pr + 1) * pcol)) for pr in range(n_pairs)]
    else:
        col_groups = [(pr, slot_of(0, grp), slice(pr * pcol, (pr + 1) * pcol)) for pr in range(n_pairs)]

    key_i = lax.broadcasted_iota(jnp.int32, (tk, ncol), 0)
    qry_row = lax.broadcasted_iota(jnp.int32, (1, ncol), 1) % tq
    big = jnp.int32(2 ** 30)

    def visible(kind, valid):
        if kind == "diag":
            return key_i <= jnp.where(valid, qry_row, -1)
        if kind == "lower":
            return key_i > jnp.where(valid, qry_row, big)
        if kind == "middle":
            return key_i >= jnp.where(valid, 0, big)
        return None

    def key_slice(t):
        return pl.ds(pl.multiple_of(ctab_ref[t] * tk, tk), tk)

    def scores(t, buf, kind):
        qi = qtab_ref[t]
        keys = key_slice(t)
        mask = visible(kind, vtab_ref[t] > 0)
        for pr, slot, cols in col_groups:
            s = _dot(key_operand(keys, pr, slot), qaug_ref[qi, :, cols])
            if mask is not None:
                s = jnp.where(mask[:, cols], s, NEG_INF)
            s_bufs[buf][:, cols] = s
            cmax_bufs[buf][:, cols] = jnp.max(s, axis=0, keepdims=True)

    def softmax(t, buf):
        qi = qtab_ref[t]
        m = m_ref[qi]
        m_new = jnp.maximum(m, cmax_bufs[buf][...])
        m_ref[qi] = m_new
        alpha_bufs[buf][...] = jnp.exp2(m - m_new)
        p_bufs[buf][...] = jnp.exp2(s_bufs[buf][...] - m_new).astype(BF16)

    def weighted(t, buf, rescale):
        qi = qtab_ref[t]
        keys = key_slice(t)
        for vg in range(n_vgroups):
            cols = slice(vg * vcol, (vg + 1) * vcol)
            pv = _dot(vaug_ref[vg * arows:(vg + 1) * arows, keys], p_bufs[buf][:, cols])
            old = acc_ref[qi, :, cols]
            acc_ref[qi, :, cols] = (alpha_bufs[buf][:, cols] * old if rescale else old) + pv

    def fast_scores(t, buf, kind):
        qi = qtab_ref[t]
        keys = key_slice(t)
        ref = jnp.where(vtab_ref[t] > 0, m_ref[qi], -NEG_INF)
        for pr, slot, cols in col_groups:
            s = _dot(key_operand(keys, pr, slot), qaug_ref[qi, :, cols])
            if kind in MASK_BIAS_ROW:
                s = s + maskbias_ref[MASK_BIAS_ROW[kind], :, cols]
            excess_ref[:, cols] = jnp.maximum(excess_ref[:, cols], jnp.max(s, axis=0, keepdims=True) - ref[:, cols])
            p_bufs[buf][:, cols] = jnp.exp2(s - ref[:, cols]).astype(BF16)

    def pipeline(step, max_body):
        first = 0
        for kind in sorted(set(body_kinds), key=body_kinds.index):
            count = body_kinds.count(kind)
            assert body_kinds[first:first + count] == (kind,) * count
            merge = max(m for m in range(1, count + 1) if count % m == 0 and m * unroll <= max_body)
            steps = merge * unroll

            @pl.loop(0, count // merge)
            def _(j, kind=kind, base=first * unroll, steps=steps):
                for u in range(steps):
                    step(base + j * steps + u, u, kind)

            first += count

    last = n_steps - 1

    for qi in range(nq):
        pos = (qi * tq + lax.broadcasted_iota(jnp.int32, (1, tq), 1)).astype(F32)
        for i in range(nh):
            m_ref[qi, :, i * tq:(i + 1) * tq] = slopes_ref[grp * nh + i] * pos
    acc_ref[...] = jnp.zeros(acc_ref.shape, F32)
    excess_ref[...] = jnp.full((1, ncol), NEG_INF, F32)
    fast_scores(0, 0, "diag")

    def fast_step(t, u, kind):
        fast_scores(t + 1, (u + 1) % 2, kind)
        weighted(t, u % 2, rescale=False)

    pipeline(fast_step, ATTN_FAST_BODY)
    weighted(last, last % 2, rescale=False)

    excess = excess_ref[...]
    @pl.when((jnp.max(excess) > SAFE_EXCESS) | (jnp.min(excess) < -SAFE_EXCESS))
    def _():
        m_ref[...] = jnp.full(m_ref.shape, NEG_INF, F32)
        acc_ref[...] = jnp.zeros(acc_ref.shape, F32)
        p_bufs[1][...] = jnp.zeros(p_bufs[1].shape, BF16)
        alpha_bufs[1][...] = jnp.ones((1, ncol), F32)
        scores(0, 0, "diag")

        def safe_step(t, u, kind):
            scores(t + 1, (u + 1) % 2, kind)
            weighted(jnp.maximum(t - 1, 0), (u + 1) % 2, rescale=True)
            softmax(t, u % 2)

        pipeline(safe_step, ATTN_SAFE_BODY)
        weighted(last - 1, (last - 1) % 2, rescale=True)
        softmax(last, last % 2)
        weighted(last, last % 2, rescale=True)

    for qi in range(nq):
        for i in range(nh):
            cols = slice(i * tq, (i + 1) * tq)
            l = acc_ref[qi, hd:hd + 1, cols]
            oT_ref[i * hd:(i + 1) * hd, qi * tq:(qi + 1) * tq] = (
                acc_ref[qi, 0:hd, cols] * (1.0 / (jnp.maximum(l, TINY) if clamp_sum else l)))


def _attention(slopes2, qT, k_tok, key_table, vT, qbias, negsel, *, batch, n_steps, n_pairs,
               heads_per_pair, both_slots, heads_per_v, q_block, k_block, v_block, slot_of, window=None,
               clamp_sum, name):
    n = qT.shape[1]
    s = n // batch
    tq, tk, hd = ATTN_TQ, ATTN_TK, HEAD_DIM
    assert tq == tk and s % tq == 0
    nq = s // tq
    nh = n_pairs * heads_per_pair
    assert nh % heads_per_v == 0
    ncol = nh * tq
    arows = hd + ONES_ROWS
    n_vgroups = nh // heads_per_v
    if window is not None:
        assert window % tk == 0
    q_tab, c_tab, v_tab, body_kinds = _attn_schedule(nq, None if window is None else window // tk)
    key_i, qry_i = np.arange(tk)[:, None], (np.arange(ncol) % tq)[None, :]
    tables = {"diag": key_i <= qry_i, "lower": key_i > qry_i}
    mask_bias = jnp.asarray(np.stack([np.where(tables[k], 0.0, NEG_INF) for k in MASK_BIAS_ROW
                                      if k in body_kinds]), F32)
    smem = pl.BlockSpec(memory_space=pltpu.SMEM)
    in_specs = [
        smem, smem, smem, smem,
        pl.BlockSpec((nh * hd, s), lambda b, g: (q_block(g), b)),
        pl.BlockSpec((s, n_pairs * LANES), lambda b, g: (b, k_block(g))),
        _resident(key_table.shape),
        pl.BlockSpec((n_vgroups * hd, s), lambda b, g: (v_block(g), b)),
        pl.BlockSpec((nh, 16, LANES), lambda b, g: (g, 0, 0)),
        _resident(mask_bias.shape),
    ]
    args = [jnp.asarray(q_tab), jnp.asarray(c_tab), jnp.asarray(v_tab), slopes2, qT, k_tok, key_table, vT,
            qbias, mask_bias]
    if negsel is not None:
        per_step = negsel.shape[0] // n_steps
        in_specs.append(pl.BlockSpec((per_step, negsel.shape[1], s), lambda b, g: (g, 0, b)))
        args.append(negsel)
    vec = lambda: pltpu.VMEM((1, ncol), F32)
    return pl.pallas_call(
        functools.partial(_attn_kernel, n_pairs=n_pairs, heads_per_pair=heads_per_pair,
                          both_slots=both_slots, heads_per_v=heads_per_v, has_sel=negsel is not None,
                          slot_of=slot_of, body_kinds=body_kinds, clamp_sum=clamp_sum),
        grid=(batch, n_steps),
        in_specs=in_specs,
        out_specs=pl.BlockSpec((nh * hd, s), lambda b, g: (g, b)),
        out_shape=jax.ShapeDtypeStruct((n_steps * nh * hd, n), F32),
        scratch_shapes=[pltpu.VMEM((nq, (4 if both_slots else 2) * hd, ncol), BF16),
                        pltpu.VMEM((n_vgroups * arows, s), BF16),
                        pltpu.VMEM((nq, 1, ncol), F32),
                        pltpu.VMEM((nq, arows, ncol), F32),
                        vec(),
                        pltpu.VMEM((tk, ncol), F32), pltpu.VMEM((tk, ncol), F32),
                        vec(), vec(), vec(), vec(),
                        pltpu.VMEM((tk, ncol), BF16), pltpu.VMEM((tk, ncol), BF16)],
        compiler_params=_cparams("parallel", "parallel"),
        name=name,
    )(*args)


def _mix_kernel(h_ref, g_ref, zgT_ref, ocmp_ref, osel_ref, owin_ref, omoba_ref,
                wmerge_ref, wn_ref, wm_ref, wo_ref, o_ref, gate_ref, onsa_ref):
    h = h_ref[...]
    u = _rms(h, g_ref[...]).astype(BF16)
    gate_ref[...] = _sigmoid(zgT_ref[0:gate_ref.shape[0], :])
    hd = HEAD_DIM
    for hh in range(NSA_HEADS):
        rows = slice(hh * hd, (hh + 1) * hd)
        o = None
        for r, branch in enumerate((ocmp_ref, osel_ref, owin_ref)):
            gate = gate_ref[3 * hh + r:3 * hh + r + 1, :]
            term = gate * branch[rows, :]
            o = term if o is None else o + term
        onsa_ref[rows, :] = o.astype(BF16)
    y_n = _dot_tn(onsa_ref[...], wn_ref[...])
    y_m = _dot_tn(omoba_ref[...].astype(BF16), wm_ref[...])
    mixed = (_sigmoid(_dot(u, wmerge_ref[:, :D_MODEL])) * y_n
             + _sigmoid(_dot(u, wmerge_ref[:, D_MODEL:])) * y_m)
    o_ref[...] = h + _dot(mixed.astype(BF16), wo_ref[...])


def _mix(h, g, zgT, ocmpT, oselT, owinT, omobaT, wmerge, wn, wm, wo):
    n = h.shape[0]
    tm = TOKEN_TILE
    rows = lambda width: pl.BlockSpec((tm, width), lambda i: (i, 0))
    cols = lambda height: pl.BlockSpec((height, tm), lambda i: (0, i))
    gate_rows = -(-N_GATE_LOGITS // 8) * 8
    return pl.pallas_call(
        _mix_kernel,
        grid=(n // tm,),
        in_specs=[rows(D_MODEL), _resident((1, D_MODEL)), cols(LANES), cols(NSA_WIDTH), cols(NSA_WIDTH),
                  cols(NSA_WIDTH), cols(MOBA_WIDTH), _resident(wmerge.shape),
                  _resident(wn.shape), _resident(wm.shape), _resident(wo.shape)],
        out_specs=rows(D_MODEL),
        out_shape=jax.ShapeDtypeStruct((n, D_MODEL), F32),
        scratch_shapes=[pltpu.VMEM((gate_rows, tm), F32), pltpu.VMEM((NSA_WIDTH, tm), BF16)],
        compiler_params=_cparams("parallel"),
        name="mix",
    )(h, g, zgT, ocmpT, oselT, owinT, omobaT, wmerge, wn, wm, wo)


def _key_table(s, block):
    pos = np.arange(s)
    tab = np.zeros((s, HEAD_DIM), np.float32)
    tab[:, 0:N_SLOPE_PIECES] = ((pos // LANES) * LANES)[:, None]
    tab[:, N_SLOPE_PIECES:2 * N_SLOPE_PIECES] = (pos % LANES)[:, None]
    if block is not None:
        tab[pos, SLOPE_ROWS + pos // block] = 1.0
    zeros = np.zeros_like(tab)
    both = np.stack([np.concatenate([zeros, tab], axis=1), np.concatenate([tab, zeros], axis=1)])
    return jnp.asarray(both, BF16)


def _slope_rows(slopes2):
    s1 = slopes2.astype(BF16)
    r1 = slopes2 - s1.astype(F32)
    s2 = r1.astype(BF16)
    s3 = (r1 - s2.astype(F32)).astype(BF16)
    rows = jnp.stack([s1, s2, s3, s1, s2, s3], axis=1)
    rows = jnp.pad(rows, ((0, 0), (0, 16 - 2 * N_SLOPE_PIECES)))
    return jnp.broadcast_to(rows[:, :, None], rows.shape + (LANES,))


def _compress_weights(pos, w1):
    st, hd, gw, hidden, groups = NSA_CMP_STRIDE, HEAD_DIM, NSA_KV_WIDTH, NSA_CMP_HIDDEN, NSA_KV_GROUPS
    w = jnp.zeros((2, st, gw, groups * hidden), F32)
    for g in range(groups):
        w = w.at[:, :, g * hd:(g + 1) * hd, g * hidden:(g + 1) * hidden].set(w1.reshape(2, st, hd, hidden))
    ps = jnp.tile(pos.reshape(2, st, 1, hd), (1, 1, groups, 1))
    return ps.reshape(2, 1, st * gw), w.reshape(2, st * gw, groups * hidden).astype(BF16)


def kernel(x, p, ffn1_norm, ffn1_w1, ffn1_w3, ffn1_w2, mix_norm, w_in, cmp_pos_k, cmp_w1_k, cmp_w2_k, cmp_pos_v, cmp_w1_v, cmp_w2_v, w_up_nsa, w_up_moba, w_out, ffn2_norm, ffn2_w1, ffn2_w3, ffn2_w2, ple_norm, w_ple_gate, w_ple, final_norm):
    b, s, d = x.shape
    n = b * s
    depth = p.shape[0]
    n_all = NSA_HEADS + MOBA_HEADS
    slopes = jnp.exp2(-8.0 * (jnp.arange(n_all, dtype=F32) + 1.0) / n_all)
    slopes2_n, slopes2_m = slopes[0::2] * LOG2E, slopes[1::2] * LOG2E
    qbias_n, qbias_m = _slope_rows(slopes2_n), _slope_rows(slopes2_m)

    n_cmp_rows = s // NSA_CMP_STRIDE
    n_cmp = (s - NSA_CMP_BLOCK) // NSA_CMP_STRIDE + 1
    n_sel = s // NSA_SEL_BLOCK
    c_start = np.arange(n_cmp_rows) * NSA_CMP_STRIDE
    s_start = np.arange(n_sel) * NSA_SEL_BLOCK
    overlap_t = ((c_start[None, :] <= s_start[:, None] + NSA_SEL_BLOCK - 1)
                 & (c_start[None, :] + NSA_CMP_BLOCK - 1 >= s_start[:, None])
                 & (np.arange(n_cmp_rows)[None, :] < n_cmp))
    overlap_t = jnp.asarray(np.tile(overlap_t, (1, 3)), BF16)
    table_sel = _key_table(s, NSA_SEL_BLOCK)
    table_win = _key_table(s, None)
    table_moba = _key_table(s, MOBA_BLOCK)

    h = x.reshape(n, d)
    for i in range(depth):
        vec = lambda a: a[i].reshape(1, -1)
        wi = w_in[i]
        col = lambda lo, width: wi[:, lo:lo + width]
        wq_t = jnp.concatenate([col(0, NSA_WIDTH), col(OFF_MOBA_Q, MOBA_WIDTH)], axis=1).T.astype(BF16)
        wk = jnp.concatenate([col(OFF_MOBA_K, MOBA_WIDTH), col(OFF_KSLC, NSA_KV_WIDTH),
                              col(OFF_KWIN, NSA_KV_WIDTH)], axis=1).astype(BF16)
        wc = jnp.stack([col(OFF_KCMP, NSA_KV_WIDTH), col(OFF_VCMP, NSA_KV_WIDTH)]).astype(BF16)
        wv_t = jnp.concatenate([col(OFF_VSLC, NSA_KV_WIDTH), col(OFF_VWIN, NSA_KV_WIDTH),
                                col(OFF_MOBA_V, MOBA_WIDTH)], axis=1).T.astype(BF16)
        wg_t = jnp.pad(col(OFF_GATE_LOGITS, N_GATE_LOGITS), ((0, 0), (0, LANES - N_GATE_LOGITS))).T.astype(BF16)
        w_merge = col(OFF_MERGE, 2 * D_MODEL).astype(BF16)
        h, qT, k_tok, cmp_tok, vT, zgT = _ffn(
            h, vec(ffn1_norm), ffn1_w1[i].astype(BF16), ffn1_w3[i].astype(BF16), ffn1_w2[i].astype(BF16),
            project=(vec(mix_norm), wq_t, wk, wc, wv_t, wg_t))

        cmp_w = [_compress_weights(ps[i], w1[i]) for ps, w1 in ((cmp_pos_k, cmp_w1_k), (cmp_pos_v, cmp_w1_v))]
        pos = jnp.stack([pw[0] for pw in cmp_w])
        w1c = jnp.stack([pw[1] for pw in cmp_w])
        w2 = jnp.stack([cmp_w2_k[i], cmp_w2_v[i]])
        cmp_rows = cmp_tok.reshape(2, b, n_cmp_rows, NSA_CMP_STRIDE * NSA_KV_WIDTH)
        kc, kcT = _compress(cmp_rows, pos, w1c, w2, w2.transpose(0, 2, 1))

        ocmpT, negsel_n = _cmp_select(slopes2_n, qT, kc, kcT, overlap_t, b)
        hs = ATTN_HEADS_PER_STEP
        q_rows, pair = hs * HEAD_DIM, 2 * HEAD_DIM
        nsa = dict(batch=b, n_steps=NSA_KV_GROUPS, n_pairs=1, heads_per_pair=hs, both_slots=False, heads_per_v=hs,
                   q_block=lambda g: QT_ROW_NSA // q_rows + g, slot_of=lambda j, g: g, clamp_sum=True)
        oselT = _attention(slopes2_n, qT, k_tok, table_sel, vT, qbias_n, negsel_n,
                           k_block=lambda g: K_COL_SLC // pair, v_block=lambda g: VT_ROW_SLC // HEAD_DIM + g,
                           name="attn_select", **nsa)
        owinT = _attention(slopes2_n, qT, k_tok, table_win, vT, qbias_n, None,
                           k_block=lambda g: K_COL_WIN // pair, v_block=lambda g: VT_ROW_WIN // HEAD_DIM + g,
                           window=NSA_WINDOW, name="attn_window", **nsa)
        negsel_m = _moba_select(qT, k_tok, b, q_row_block=QT_ROW_MOBA // MOBA_WIDTH,
                                k_col_block=K_COL_MOBA // MOBA_WIDTH)
        omobaT = _attention(slopes2_m, qT, k_tok, table_moba, vT, qbias_m, negsel_m, batch=b,
                            n_steps=MOBA_HEADS // hs, n_pairs=hs // 2, heads_per_pair=2, both_slots=True, heads_per_v=1,
                            q_block=lambda g: QT_ROW_MOBA // q_rows + g,
                            k_block=lambda g: K_COL_MOBA // (hs // 2 * pair) + g,
                            v_block=lambda g: VT_ROW_MOBA // q_rows + g, slot_of=lambda j, g: j,
                            clamp_sum=False, name="attn_moba")

        h = _mix(h, vec(mix_norm), zgT, ocmpT, oselT, owinT, omobaT, w_merge,
                 w_up_nsa[i].astype(BF16), w_up_moba[i].astype(BF16), w_out[i].astype(BF16))

        h = _ffn(h, vec(ffn2_norm), ffn2_w1[i].astype(BF16), ffn2_w3[i].astype(BF16), ffn2_w2[i].astype(BF16),
                 tail=(p[i].reshape(n, PLE_DIM), vec(ple_norm), w_ple_gate[i].astype(BF16),
                       w_ple[i].astype(BF16)),
                 final_norm=final_norm.reshape(1, -1) if i + 1 == depth else None)
    return h.reshape(b, s, d)
```

```python
import functools

import numpy as np
import jax
import jax.numpy as jnp
from jax import lax
from jax.experimental import pallas as pl
from jax.experimental.pallas import tpu as pltpu

F32 = jnp.float32
BF16 = jnp.bfloat16

D_MODEL = 1024
HEAD_DIM = 64
NSA_HEADS = 8
NSA_KV_GROUPS = 2
NSA_HEADS_PER_GROUP = NSA_HEADS // NSA_KV_GROUPS
NSA_CMP_BLOCK = 32
NSA_CMP_STRIDE = 16
NSA_CMP_HIDDEN = 128
NSA_SEL_BLOCK = 64
NSA_SEL_TOPK = 16
NSA_WINDOW = 512
MOBA_HEADS = 8
MOBA_BLOCK = 256
MOBA_TOPK = 3
D_FF = 2816
PLE_DIM = 256
RMS_EPS = 1e-6
NEG_INF = -1e30
TINY = 1e-30
FORCE_SCORE = 1e9
LOG2E = float(np.log2(np.e))
Q_SCALE = HEAD_DIM ** -0.5 * LOG2E

NSA_WIDTH = NSA_HEADS * HEAD_DIM
NSA_KV_WIDTH = NSA_KV_GROUPS * HEAD_DIM
MOBA_WIDTH = MOBA_HEADS * HEAD_DIM
N_GATE_LOGITS = 3 * NSA_HEADS
OFF_KCMP = NSA_WIDTH
OFF_VCMP = OFF_KCMP + NSA_KV_WIDTH
OFF_KSLC = OFF_VCMP + NSA_KV_WIDTH
OFF_VSLC = OFF_KSLC + NSA_KV_WIDTH
OFF_KWIN = OFF_VSLC + NSA_KV_WIDTH
OFF_VWIN = OFF_KWIN + NSA_KV_WIDTH
OFF_GATE_LOGITS = OFF_VWIN + NSA_KV_WIDTH
OFF_MOBA_Q = OFF_GATE_LOGITS + N_GATE_LOGITS
OFF_MOBA_K = OFF_MOBA_Q + MOBA_WIDTH
OFF_MOBA_V = OFF_MOBA_K + MOBA_WIDTH
OFF_MERGE = OFF_MOBA_V + MOBA_WIDTH

LANES = 128
VMEM_LIMIT = 56 * 1024 * 1024

TOKEN_TILE = 512
FF_CHUNK = 256
ATTN_TQ = 256
ATTN_TK = 256
CMP_TQ = 2048
ATTN_UNROLL = 2
ATTN_FAST_BODY = 28
ATTN_SAFE_BODY = 4
SLOPE_ROWS = 16
N_SLOPE_PIECES = 3
N_SPLIT = 2
MOBA_SEL_ROWS = 16
ONES_ROWS = 16
MASK_BIAS_ROW = {"diag": 0, "lower": 1}
SAFE_EXCESS = 60.0

QT_ROW_NSA, QT_ROW_MOBA = 0, NSA_WIDTH
K_COL_MOBA, K_COL_SLC, K_COL_WIN = 0, MOBA_WIDTH, MOBA_WIDTH + NSA_KV_WIDTH
VT_ROW_SLC, VT_ROW_WIN, VT_ROW_MOBA = 0, NSA_KV_WIDTH, 2 * NSA_KV_WIDTH
ATTN_HEADS_PER_STEP = 4

_NT = (((1,), (1,)), ((), ()))
_TN = (((0,), (0,)), ((), ()))


def _cparams(*sem):
    return pltpu.CompilerParams(dimension_semantics=sem, vmem_limit_bytes=VMEM_LIMIT)


def _resident(shape):
    nd = len(shape)
    return pl.BlockSpec(shape, lambda *_: (0,) * nd, pipeline_mode=pl.Buffered(1))


def _rms(x, g):
    ms = jnp.mean(x * x, axis=-1, keepdims=True)
    return x * lax.rsqrt(ms + RMS_EPS) * g


def _sigmoid(x):
    return 1.0 / (1.0 + jnp.exp(-x))


def _dot(a, b):
    return jnp.dot(a, b, preferred_element_type=F32)


def _dot_f32(a, b):
    return jnp.dot(a, b, preferred_element_type=F32, precision=lax.Precision.HIGHEST)


def _dot_nt(a, b, precision=None):
    return lax.dot_general(a, b, _NT, preferred_element_type=F32, precision=precision)


def _dot_tn(a, b):
    return lax.dot_general(a, b, _TN, preferred_element_type=F32)


def _ffn_kernel(*refs, tail, final, project):
    x_ref, g_ref, w1_ref, w3_ref, w2_ref = refs[:5]
    n_in = 5
    if tail:
        p_ref, gp_ref, wpg_ref, wp_ref = refs[n_in:n_in + 4]
        n_in += 4
    if final:
        gf_ref = refs[n_in]
        n_in += 1
    if project:
        gm_ref, wq_ref, wk_ref, wc_ref, wv_ref, wg_ref = refs[n_in:n_in + 6]
        n_in += 6
    o_ref = refs[n_in]
    acc_ref = refs[-1]
    x = x_ref[...]
    xn = _rms(x, g_ref[...]).astype(BF16)
    for c in range(D_FF // FF_CHUNK):
        cols = slice(c * FF_CHUNK, (c + 1) * FF_CHUNK)
        a = _dot(xn, w1_ref[:, cols])
        b = _dot(xn, w3_ref[:, cols])
        hid = (a * _sigmoid(a) * b).astype(BF16)
        y = _dot(hid, w2_ref[cols, :])
        if c == 0:
            acc_ref[...] = y
        else:
            acc_ref[...] += y
    h = x + 0.5 * acc_ref[...]
    if tail:
        gate = _sigmoid(_dot(_rms(h, gp_ref[...]).astype(BF16), wpg_ref[...]))
        h = h + gate * _dot(p_ref[...].astype(BF16), wp_ref[...])
    if final:
        h = _rms(h, gf_ref[...])
    o_ref[...] = h
    if project:
        qT_ref, k_ref, cmp_ref, vT_ref, zgT_ref = refs[n_in + 1:n_in + 6]
        u = _rms(h, gm_ref[...]).astype(BF16)
        qT_ref[...] = (_dot_nt(wq_ref[...], u) * Q_SCALE).astype(BF16)
        k_ref[...] = _dot(u, wk_ref[...]).astype(BF16)
        for kind in range(2):
            cmp_ref[kind] = _dot(u, wc_ref[kind]).astype(BF16)
        vT_ref[...] = _dot_nt(wv_ref[...], u).astype(BF16)
        zgT_ref[...] = _dot_nt(wg_ref[...], u)


def _ffn(x, g, w1, w3, w2, tail=None, final_norm=None, project=None):
    n = x.shape[0]
    tm = TOKEN_TILE
    row = lambda w: pl.BlockSpec((tm, w), lambda i: (i, 0))
    col = lambda height: pl.BlockSpec((height, tm), lambda i: (0, i))
    args = [x, g, w1, w3, w2]
    specs = [row(D_MODEL), _resident((1, D_MODEL)), _resident(w1.shape), _resident(w3.shape),
             _resident(w2.shape)]
    out_specs = [row(D_MODEL)]
    out_shape = [jax.ShapeDtypeStruct((n, D_MODEL), F32)]
    if tail is not None:
        p, gp, wpg, wp = tail
        args += [p, gp, wpg, wp]
        specs += [row(PLE_DIM), _resident((1, D_MODEL)), _resident(wpg.shape), _resident(wp.shape)]
    if final_norm is not None:
        args.append(final_norm)
        specs.append(_resident((1, D_MODEL)))
    if project is not None:
        gm, wq_t, wk, wc, wv_t, wg_t = project
        args += [gm, wq_t, wk, wc, wv_t, wg_t]
        specs += [_resident(a.shape) for a in project]
        out_specs += [col(wq_t.shape[0]), row(wk.shape[1]),
                      pl.BlockSpec((2, tm, NSA_KV_WIDTH), lambda i: (0, i, 0)),
                      col(wv_t.shape[0]), col(LANES)]
        out_shape += [jax.ShapeDtypeStruct((wq_t.shape[0], n), BF16),
                      jax.ShapeDtypeStruct((n, wk.shape[1]), BF16),
                      jax.ShapeDtypeStruct((2, n, NSA_KV_WIDTH), BF16),
                      jax.ShapeDtypeStruct((wv_t.shape[0], n), BF16),
                      jax.ShapeDtypeStruct((LANES, n), F32)]
    outs = pl.pallas_call(
        functools.partial(_ffn_kernel, tail=tail is not None, final=final_norm is not None,
                          project=project is not None),
        grid=(n // tm,),
        in_specs=specs,
        out_specs=out_specs,
        out_shape=out_shape,
        scratch_shapes=[pltpu.VMEM((tm, D_MODEL), F32)],
        compiler_params=_cparams("parallel"),
        name="ffn" + ("_tail" if tail is not None else "") + ("_project" if project is not None else ""),
    )(*args)
    return outs if project is not None else outs[0]


def _split(x):
    pieces, rest = [], x
    for _ in range(N_SPLIT):
        piece = rest.astype(BF16)
        pieces.append(piece)
        rest = rest - piece.astype(F32)
    return tuple(pieces)


def _compress_kernel(rows_ref, pos_ref, w1_ref, w2_ref, w2t_ref, o_ref, ot_ref):
    c0 = float(np.sqrt(2.0 / np.pi))
    hidden = NSA_CMP_HIDDEN
    for kind in range(2):
        rows = rows_ref[kind].astype(F32)
        first = _dot((rows + pos_ref[kind, 0]).astype(BF16), w1_ref[kind, 0])
        second = _dot((rows + pos_ref[kind, 1]).astype(BF16), w1_ref[kind, 1])
        n_rows = rows.shape[0]
        pre = first + pltpu.roll(second, n_rows - 1, 0)
        hid = pre * (0.5 * (1.0 + jnp.tanh(c0 * (pre + 0.044715 * (pre * pre * pre)))))
        for g in range(NSA_KV_GROUPS):
            hid_g = hid[:, g * hidden:(g + 1) * hidden]
            out = _dot_f32(hid_g, w2_ref[kind])
            o_ref[kind * NSA_KV_GROUPS + g] = jnp.concatenate(_split(out), axis=1)
            ot_ref[kind * NSA_KV_GROUPS + g] = _dot_nt(w2t_ref[kind], hid_g, lax.Precision.HIGHEST).astype(BF16)


def _compress(cmp_rows, pos, w1, w2, w2t):
    _, b, n_rows, width = cmp_rows.shape
    n_out = 2 * NSA_KV_GROUPS
    return pl.pallas_call(
        _compress_kernel,
        grid=(b,),
        in_specs=[
            pl.BlockSpec((2, None, n_rows, width), lambda i: (0, i, 0, 0)),
            _resident(pos.shape), _resident(w1.shape), _resident(w2.shape), _resident(w2t.shape),
        ],
        out_specs=[
            pl.BlockSpec((None, n_out, n_rows, N_SPLIT * HEAD_DIM), lambda i: (i, 0, 0, 0)),
            pl.BlockSpec((None, n_out, HEAD_DIM, n_rows), lambda i: (i, 0, 0, 0)),
        ],
        out_shape=[jax.ShapeDtypeStruct((b, n_out, n_rows, N_SPLIT * HEAD_DIM), BF16),
                   jax.ShapeDtypeStruct((b, n_out, HEAD_DIM, n_rows), BF16)],
        compiler_params=_cparams("parallel"),
        name="compress",
    )(cmp_rows, pos, w1, w2, w2t)


def _count_rank(score, n_candidates):
    sub = 8
    tiles = [(lo, min(lo + sub, score.shape[0])) for lo in range(0, score.shape[0], sub)]
    ranks = [jnp.zeros((hi - lo,) + score.shape[1:], jnp.int32) for lo, hi in tiles]
    for j in range(n_candidates):
        row = score[j:j + 1, :]
        for n, (lo, hi) in enumerate(tiles):
            part = score[lo:hi]
            if lo > j:
                ahead = row >= part
            elif hi - 1 <= j:
                ahead = row > part
            else:
                after = lo + lax.broadcasted_iota(jnp.int32, part.shape, 0) > j
                ahead = (row > part) | ((row == part) & after)
            ranks[n] = ranks[n] + ahead.astype(jnp.int32)
    return ranks[0] if len(ranks) == 1 else jnp.concatenate(ranks, axis=0)


def _cmp_select_kernel(slopes_ref, qT_ref, kc_ref, vcT_ref, ovT_ref, oT_ref, negsel_ref):
    g = pl.program_id(1)
    qi = pl.program_id(2)
    tq = qT_ref.shape[1]
    n_cmp = kc_ref.shape[0]
    n_sel = ovT_ref.shape[0]
    hd = HEAD_DIM
    t = qi * tq + lax.broadcasted_iota(jnp.int32, (n_cmp, tq), 1)
    blk_c = lax.broadcasted_iota(jnp.int32, (n_cmp, tq), 0)
    dist = t - (blk_c * NSA_CMP_STRIDE + (NSA_CMP_BLOCK - 1))
    visible = dist >= 0
    dist_f = dist.astype(F32)
    kc_pieces = kc_ref[...]
    vcT = vcT_ref[...]
    p_sum = jnp.zeros((n_cmp, tq), F32)
    for hh in range(NSA_HEADS_PER_GROUP):
        slope2 = slopes_ref[g * NSA_HEADS_PER_GROUP + hh]
        q = qT_ref[hh * hd:(hh + 1) * hd, :]
        s = _dot(kc_pieces, jnp.concatenate([q] * N_SPLIT, axis=0)) - slope2 * dist_f
        s = jnp.where(visible, s, NEG_INF)
        m = jnp.max(s, axis=0, keepdims=True)
        e = jnp.exp2(s - m)
        any_visible = m > 0.5 * NEG_INF
        p = e * jnp.where(any_visible, 1.0 / jnp.maximum(jnp.sum(e, axis=0, keepdims=True), TINY), 0.0)
        p_sum = p_sum + p
        oT_ref[hh * hd:(hh + 1) * hd, :] = _dot(vcT, p.astype(BF16))
    imp = _dot(ovT_ref[...], jnp.concatenate(_split(p_sum), axis=0))
    ts = qi * tq + lax.broadcasted_iota(jnp.int32, (n_sel, tq), 1)
    blk = lax.broadcasted_iota(jnp.int32, (n_sel, tq), 0)
    cur = ts // NSA_SEL_BLOCK
    causal = blk * NSA_SEL_BLOCK <= ts
    forced = (blk == 0) | (blk == cur) | (blk == cur - 1)
    imp = jnp.where(causal, jnp.where(forced, FORCE_SCORE, imp), NEG_INF)
    rank = _count_rank(imp, n_sel)
    chosen = (rank < min(NSA_SEL_TOPK, n_sel)) & causal
    negsel_ref[...] = jnp.where(chosen, 0.0, NEG_INF).astype(BF16)


def _cmp_select(slopes2, qT, kc, vcT, overlap_t, batch):
    n = qT.shape[1]
    tq = min(CMP_TQ, n // batch)
    nq = n // batch // tq
    rows = NSA_HEADS_PER_GROUP * HEAD_DIM
    n_cmp = kc.shape[2]
    n_sel = overlap_t.shape[0]
    return pl.pallas_call(
        _cmp_select_kernel,
        grid=(batch, NSA_KV_GROUPS, nq),
        in_specs=[
            pl.BlockSpec(memory_space=pltpu.SMEM),
            pl.BlockSpec((rows, tq), lambda i, g, q: (g, i * nq + q)),
            pl.BlockSpec((None, None, n_cmp, N_SPLIT * HEAD_DIM), lambda i, g, q: (i, g, 0, 0)),
            pl.BlockSpec((None, None, HEAD_DIM, n_cmp), lambda i, g, q: (i, NSA_KV_GROUPS + g, 0, 0)),
            pl.BlockSpec((n_sel, N_SPLIT * n_cmp), lambda i, g, q: (0, 0)),
        ],
        out_specs=[
            pl.BlockSpec((rows, tq), lambda i, g, q: (g, i * nq + q)),
            pl.BlockSpec((None, n_sel, tq), lambda i, g, q: (g, 0, i * nq + q)),
        ],
        out_shape=[jax.ShapeDtypeStruct((NSA_WIDTH, n), F32),
                   jax.ShapeDtypeStruct((NSA_KV_GROUPS, n_sel, n), BF16)],
        compiler_params=_cparams("parallel", "parallel", "parallel"),
        name="cmp_select",
    )(slopes2, qT, kc, vcT, overlap_t)


def _moba_select_kernel(qT_ref, k_ref, negsel_ref):
    s = k_ref.shape[0]
    nb = s // MOBA_BLOCK
    hd = HEAD_DIM
    kmean = jnp.mean(k_ref[...].astype(F32).reshape(nb, MOBA_BLOCK, k_ref.shape[1]), axis=1)
    t = lax.broadcasted_iota(jnp.int32, (nb, s), 1)
    blk = lax.broadcasted_iota(jnp.int32, (nb, s), 0)
    cur = t // MOBA_BLOCK
    past = blk < cur
    for j in range(MOBA_HEADS):
        gs = _dot_f32(kmean[:, j * hd:(j + 1) * hd], qT_ref[j * hd:(j + 1) * hd, :].astype(F32))
        gs = jnp.where(past, gs, NEG_INF)
        rank = _count_rank(gs, nb)
        chosen = ((rank < min(MOBA_TOPK, nb - 1)) & past) | (blk == cur)
        negsel_ref[j, 0:nb, :] = jnp.where(chosen, 0.0, NEG_INF).astype(BF16)
        negsel_ref[j, nb:, :] = jnp.zeros((MOBA_SEL_ROWS - nb, s), BF16)


def _moba_select(qT, k_tok, batch, q_row_block, k_col_block):
    n = qT.shape[1]
    s = n // batch
    return pl.pallas_call(
        _moba_select_kernel,
        grid=(batch,),
        in_specs=[pl.BlockSpec((MOBA_WIDTH, s), lambda i: (q_row_block, i)),
                  pl.BlockSpec((s, MOBA_WIDTH), lambda i: (i, k_col_block))],
        out_specs=pl.BlockSpec((MOBA_HEADS, MOBA_SEL_ROWS, s), lambda i: (0, 0, i)),
        out_shape=jax.ShapeDtypeStruct((MOBA_HEADS, MOBA_SEL_ROWS, n), BF16),
        compiler_params=_cparams("parallel"),
        name="moba_select",
    )(qT, k_tok)


def _attn_schedule(nq, window_chunks):
    def padded(pairs, start):
        n_pad = (-(start + len(pairs) - 1)) % ATTN_UNROLL
        return pairs + [(0, 0, 0)] * n_pad

    diag = [(q, q, 1) for q in range(nq)]
    groups = [("diag", padded(diag, 0))]
    if window_chunks is None:
        full = [(q, c, 1) for q in range(nq) for c in range(q)]
        groups.append((None, full))
    else:
        assert window_chunks == 2
        done = len(groups[0][1])
        lower = padded([(q, q - 2, 1) for q in range(2, nq)], done)
        groups.append(("lower", lower))
        groups.append(("middle", [(q, q - 1, 1) for q in range(1, nq)]))
    pairs, kinds = [], []
    for kind, group in groups:
        pairs += group
        kinds += [kind] * len(group)
    n_pad = (-(len(pairs) - 1)) % ATTN_UNROLL
    pairs += [(0, 0, 0)] * n_pad
    kinds += [kinds[-1] if kinds[-1] is not None else "diag"] * n_pad
    body_kinds = []
    for j in range((len(pairs) - 1) // ATTN_UNROLL):
        ks = set(kinds[1 + j * ATTN_UNROLL:1 + (j + 1) * ATTN_UNROLL])
        assert len(ks) == 1, "a loop body must issue scores of one mask kind"
        body_kinds.append(ks.pop())
    table = np.asarray(pairs, np.int32)
    return table[:, 0], table[:, 1], table[:, 2], tuple(body_kinds)


def _attn_kernel(*refs, n_pairs, heads_per_pair, both_slots, heads_per_v, has_sel, slot_of, body_kinds,
                 clamp_sum):
    n_in = 11 if has_sel else 10
    (qtab_ref, ctab_ref, vtab_ref, slopes_ref, qT_ref, k_ref, ktab_ref, vT_ref, qbias_ref,
     maskbias_ref) = refs[:10]
    negsel_ref = refs[10] if has_sel else None
    oT_ref = refs[n_in]
    qaug_ref, vaug_ref, m_ref, acc_ref, excess_ref = refs[n_in + 1:n_in + 6]
    s_bufs = refs[n_in + 6:n_in + 8]
    cmax_bufs = refs[n_in + 8:n_in + 10]
    alpha_bufs = refs[n_in + 10:n_in + 12]
    p_bufs = refs[n_in + 12:n_in + 14]
    grp = pl.program_id(1)
    tq, tk, hd = ATTN_TQ, ATTN_TK, HEAD_DIM
    nq = qaug_ref.shape[0]
    hpp = heads_per_pair
    nh = n_pairs * hpp
    pcol = hpp * tq
    ncol = n_pairs * pcol
    n_vgroups = nh // heads_per_v
    vcol = heads_per_v * tq
    arows = hd + ONES_ROWS
    n_steps = qtab_ref.shape[0]
    unroll = ATTN_UNROLL

    for vg in range(n_vgroups):
        vaug_ref[vg * arows:vg * arows + hd, :] = vT_ref[vg * hd:(vg + 1) * hd, :]
        vaug_ref[vg * arows + hd:(vg + 1) * arows, :] = jnp.ones((ONES_ROWS, vaug_ref.shape[1]), BF16)

    for qi in range(nq):
        toks = slice(qi * tq, (qi + 1) * tq)
        for i in range(nh):
            cols = slice(i * tq, (i + 1) * tq)
            slot = slot_of(i % hpp, grp)
            q = qT_ref[i * hd:(i + 1) * hd, toks]
            bias = [jnp.concatenate([qbias_ref[i]] * (tq // LANES), axis=1)]
            if has_sel:
                bias.append(negsel_ref[i % negsel_ref.shape[0], :, toks])
            used = sum(part.shape[0] for part in bias)
            bias = jnp.concatenate(bias + [jnp.zeros((hd - used, tq), BF16)], axis=0)
            if both_slots:
                zero = jnp.zeros_like(q)
                qaug_ref[qi, 0:hd, cols] = jnp.where(slot == 0, q, zero)
                qaug_ref[qi, hd:2 * hd, cols] = jnp.where(slot == 0, zero, q)
                qaug_ref[qi, 2 * hd:3 * hd, cols] = bias
                qaug_ref[qi, 3 * hd:, cols] = zero
            else:
                qaug_ref[qi, 0:hd, cols] = jnp.where(slot == 0, q, bias)
                qaug_ref[qi, hd:2 * hd, cols] = jnp.where(slot == 0, bias, q)

    lane_half = lax.broadcasted_iota(jnp.int32, (tk, LANES), 1) // hd

    def key_operand(keys, pr, slot):
        k_pair = k_ref[keys, pr * LANES:(pr + 1) * LANES]
        if both_slots:
            return jnp.concatenate([k_pair, ktab_ref[1, keys, :]], axis=1)
        return jnp.where(lane_half == slot, k_pair, ktab_ref[slot, keys, :])

    if both_slots:
        col_groups = [(pr, None, slice(pr * pcol, (pr + 1) * pcol)) for pr in range(n_pairs)]
    else:
        col_groups = [(pr, slot_of(0, grp), slice(pr * pcol, (pr + 1) * pcol)) for pr in range(n_pairs)]

    key_i = lax.broadcasted_iota(jnp.int32, (tk, ncol), 0)
    qry_row = lax.broadcasted_iota(jnp.int32, (1, ncol), 1) % tq
    big = jnp.int32(2 ** 30)

    def visible(kind, valid):
        if kind == "diag":
            return key_i <= jnp.where(valid, qry_row, -1)
        if kind == "lower":
            return key_i > jnp.where(valid, qry_row, big)
        if kind == "middle":
            return key_i >= jnp.where(valid, 0, big)
        return None

    def key_slice(t):
        return pl.ds(pl.multiple_of(ctab_ref[t] * tk, tk), tk)

    def scores(t, buf, kind):
        qi = qtab_ref[t]
        keys = key_slice(t)
        mask = visible(kind, vtab_ref[t] > 0)
        for pr, slot, cols in col_groups:
            s = _dot(key_operand(keys, pr, slot), qaug_ref[qi, :, cols])
            if mask is not None:
                s = jnp.where(mask[:, cols], s, NEG_INF)
            s_bufs[buf][:, cols] = s
            cmax_bufs[buf][:, cols] = jnp.max(s, axis=0, keepdims=True)

    def softmax(t, buf):
        qi = qtab_ref[t]
        m = m_ref[qi]
        m_new = jnp.maximum(m, cmax_bufs[buf][...])
        m_ref[qi] = m_new
        alpha_bufs[buf][...] = jnp.exp2(m - m_new)
        p_bufs[buf][...] = jnp.exp2(s_bufs[buf][...] - m_new).astype(BF16)

    def weighted(t, buf, rescale):
        qi = qtab_ref[t]
        keys = key_slice(t)
        for vg in range(n_vgroups):
            cols = slice(vg * vcol, (vg + 1) * vcol)
            pv = _dot(vaug_ref[vg * arows:(vg + 1) * arows, keys], p_bufs[buf][:, cols])
            old = acc_ref[qi, :, cols]
            acc_ref[qi, :, cols] = (alpha_bufs[buf][:, cols] * old if rescale else old) + pv

    def fast_scores(t, buf, kind):
        qi = qtab_ref[t]
        keys = key_slice(t)
        ref = jnp.where(vtab_ref[t] > 0, m_ref[qi], -NEG_INF)
        for pr, slot, cols in col_groups:
            s = _dot(key_operand(keys, pr, slot), qaug_ref[qi, :, cols])
            if kind in MASK_BIAS_ROW:
                s = s + maskbias_ref[MASK_BIAS_ROW[kind], :, cols]
            excess_ref[:, cols] = jnp.maximum(excess_ref[:, cols], jnp.max(s, axis=0, keepdims=True) - ref[:, cols])
            p_bufs[buf][:, cols] = jnp.exp2(s - ref[:, cols]).astype(BF16)

    def pipeline(step, max_body):
        first = 0
        for kind in sorted(set(body_kinds), key=body_kinds.index):
            count = body_kinds.count(kind)
            assert body_kinds[first:first + count] == (kind,) * count
            merge = max(m for m in range(1, count + 1) if count % m == 0 and m * unroll <= max_body)
            steps = merge * unroll

            @pl.loop(0, count // merge)
            def _(j, kind=kind, base=first * unroll, steps=steps):
                for u in range(steps):
                    step(base + j * steps + u, u, kind)

            first += count

    last = n_steps - 1

    for qi in range(nq):
        pos = (qi * tq + lax.broadcasted_iota(jnp.int32, (1, tq), 1)).astype(F32)
        for i in range(nh):
            m_ref[qi, :, i * tq:(i + 1) * tq] = slopes_ref[grp * nh + i] * pos
    acc_ref[...] = jnp.zeros(acc_ref.shape, F32)
    excess_ref[...] = jnp.full((1, ncol), NEG_INF, F32)
    fast_scores(0, 0, "diag")

    def fast_step(t, u, kind):
        fast_scores(t + 1, (u + 1) % 2, kind)
        weighted(t, u % 2, rescale=False)

    pipeline(fast_step, ATTN_FAST_BODY)
    weighted(last, last % 2, rescale=False)

    excess = excess_ref[...]
    @pl.when((jnp.max(excess) > SAFE_EXCESS) | (jnp.min(excess) < -SAFE_EXCESS))
    def _():
        m_ref[...] = jnp.full(m_ref.shape, NEG_INF, F32)
        acc_ref[...] = jnp.zeros(acc_ref.shape, F32)
        p_bufs[1][...] = jnp.zeros(p_bufs[1].shape, BF16)
        alpha_bufs[1][...] = jnp.ones((1, ncol), F32)
        scores(0, 0, "diag")

        def safe_step(t, u, kind):
            scores(t + 1, (u + 1) % 2, kind)
            weighted(jnp.maximum(t - 1, 0), (u + 1) % 2, rescale=True)
            softmax(t, u % 2)

        pipeline(safe_step, ATTN_SAFE_BODY)
        weighted(last - 1, (last - 1) % 2, rescale=True)
        softmax(last, last % 2)
        weighted(last, last % 2, rescale=True)

    for qi in range(nq):
        for i in range(nh):
            cols = slice(i * tq, (i + 1) * tq)
            l = acc_ref[qi, hd:hd + 1, cols]
            oT_ref[i * hd:(i + 1) * hd, qi * tq:(qi + 1) * tq] = (
                acc_ref[qi, 0:hd, cols] * (1.0 / (jnp.maximum(l, TINY) if clamp_sum else l)))


def _attention(slopes2, qT, k_tok, key_table, vT, qbias, negsel, *, batch, n_steps, n_pairs,
               heads_per_pair, both_slots, heads_per_v, q_block, k_block, v_block, slot_of, window=None,
               clamp_sum, name):
    n = qT.shape[1]
    s = n // batch
    tq, tk, hd = ATTN_TQ, ATTN_TK, HEAD_DIM
    assert tq == tk and s % tq == 0
    nq = s // tq
    nh = n_pairs * heads_per_pair
    assert nh % heads_per_v == 0
    ncol = nh * tq
    arows = hd + ONES_ROWS
    n_vgroups = nh // heads_per_v
    if window is not None:
        assert window % tk == 0
    q_tab, c_tab, v_tab, body_kinds = _attn_schedule(nq, None if window is None else window // tk)
    key_i, qry_i = np.arange(tk)[:, None], (np.arange(ncol) % tq)[None, :]
    tables = {"diag": key_i <= qry_i, "lower": key_i > qry_i}
    mask_bias = jnp.asarray(np.stack([np.where(tables[k], 0.0, NEG_INF) for k in MASK_BIAS_ROW
                                      if k in body_kinds]), F32)
    smem = pl.BlockSpec(memory_space=pltpu.SMEM)
    in_specs = [
        smem, smem, smem, smem,
        pl.BlockSpec((nh * hd, s), lambda b, g: (q_block(g), b)),
        pl.BlockSpec((s, n_pairs * LANES), lambda b, g: (b, k_block(g))),
        _resident(key_table.shape),
        pl.BlockSpec((n_vgroups * hd, s), lambda b, g: (v_block(g), b)),
        pl.BlockSpec((nh, 16, LANES), lambda b, g: (g, 0, 0)),
        _resident(mask_bias.shape),
    ]
    args = [jnp.asarray(q_tab), jnp.asarray(c_tab), jnp.asarray(v_tab), slopes2, qT, k_tok, key_table, vT,
            qbias, mask_bias]
    if negsel is not None:
        per_step = negsel.shape[0] // n_steps
        in_specs.append(pl.BlockSpec((per_step, negsel.shape[1], s), lambda b, g: (g, 0, b)))
        args.append(negsel)
    vec = lambda: pltpu.VMEM((1, ncol), F32)
    return pl.pallas_call(
        functools.partial(_attn_kernel, n_pairs=n_pairs, heads_per_pair=heads_per_pair,
                          both_slots=both_slots, heads_per_v=heads_per_v, has_sel=negsel is not None,
                          slot_of=slot_of, body_kinds=body_kinds, clamp_sum=clamp_sum),
        grid=(batch, n_steps),
        in_specs=in_specs,
        out_specs=pl.BlockSpec((nh * hd, s), lambda b, g: (g, b)),
        out_shape=jax.ShapeDtypeStruct((n_steps * nh * hd, n), F32),
        scratch_shapes=[pltpu.VMEM((nq, (4 if both_slots else 2) * hd, ncol), BF16),
                        pltpu.VMEM((n_vgroups * arows, s), BF16),
                        pltpu.VMEM((nq, 1, ncol), F32),
                        pltpu.VMEM((nq, arows, ncol), F32),
                        vec(),
                        pltpu.VMEM((tk, ncol), F32), pltpu.VMEM((tk, ncol), F32),
                        vec(), vec(), vec(), vec(),
                        pltpu.VMEM((tk, ncol), BF16), pltpu.VMEM((tk, ncol), BF16)],
        compiler_params=_cparams("parallel", "parallel"),
        name=name,
    )(*args)


def _mix_kernel(h_ref, g_ref, zgT_ref, ocmp_ref, osel_ref, owin_ref, omoba_ref,
                wmerge_ref, wn_ref, wm_ref, wo_ref, o_ref, gate_ref, onsa_ref):
    h = h_ref[...]
    u = _rms(h, g_ref[...]).astype(BF16)
    gate_ref[...] = _sigmoid(zgT_ref[0:gate_ref.shape[0], :])
    hd = HEAD_DIM
    for hh in range(NSA_HEADS):
        rows = slice(hh * hd, (hh + 1) * hd)
        o = None
        for r, branch in enumerate((ocmp_ref, osel_ref, owin_ref)):
            gate = gate_ref[3 * hh + r:3 * hh + r + 1, :]
            term = gate * branch[rows, :]
            o = term if o is None else o + term
        onsa_ref[rows, :] = o.astype(BF16)
    y_n = _dot_tn(onsa_ref[...], wn_ref[...])
    y_m = _dot_tn(omoba_ref[...].astype(BF16), wm_ref[...])
    mixed = (_sigmoid(_dot(u, wmerge_ref[:, :D_MODEL])) * y_n
             + _sigmoid(_dot(u, wmerge_ref[:, D_MODEL:])) * y_m)
    o_ref[...] = h + _dot(mixed.astype(BF16), wo_ref[...])


def _mix(h, g, zgT, ocmpT, oselT, owinT, omobaT, wmerge, wn, wm, wo):
    n = h.shape[0]
    tm = TOKEN_TILE
    rows = lambda width: pl.BlockSpec((tm, width), lambda i: (i, 0))
    cols = lambda height: pl.BlockSpec((height, tm), lambda i: (0, i))
    gate_rows = -(-N_GATE_LOGITS // 8) * 8
    return pl.pallas_call(
        _mix_kernel,
        grid=(n // tm,),
        in_specs=[rows(D_MODEL), _resident((1, D_MODEL)), cols(LANES), cols(NSA_WIDTH), cols(NSA_WIDTH),
                  cols(NSA_WIDTH), cols(MOBA_WIDTH), _resident(wmerge.shape),
                  _resident(wn.shape), _resident(wm.shape), _resident(wo.shape)],
        out_specs=rows(D_MODEL),
        out_shape=jax.ShapeDtypeStruct((n, D_MODEL), F32),
        scratch_shapes=[pltpu.VMEM((gate_rows, tm), F32), pltpu.VMEM((NSA_WIDTH, tm), BF16)],
        compiler_params=_cparams("parallel"),
        name="mix",
    )(h, g, zgT, ocmpT, oselT, owinT, omobaT, wmerge, wn, wm, wo)


def _key_table(s, block):
    pos = np.arange(s)
    tab = np.zeros((s, HEAD_DIM), np.float32)
    tab[:, 0:N_SLOPE_PIECES] = ((pos // LANES) * LANES)[:, None]
    tab[:, N_SLOPE_PIECES:2 * N_SLOPE_PIECES] = (pos % LANES)[:, None]
    if block is not None:
        tab[pos, SLOPE_ROWS + pos // block] = 1.0
    zeros = np.zeros_like(tab)
    both = np.stack([np.concatenate([zeros, tab], axis=1), np.concatenate([tab, zeros], axis=1)])
    return jnp.asarray(both, BF16)


def _slope_rows(slopes2):
    s1 = slopes2.astype(BF16)
    r1 = slopes2 - s1.astype(F32)
    s2 = r1.astype(BF16)
    s3 = (r1 - s2.astype(F32)).astype(BF16)
    rows = jnp.stack([s1, s2, s3, s1, s2, s3], axis=1)
    rows = jnp.pad(rows, ((0, 0), (0, 16 - 2 * N_SLOPE_PIECES)))
    return jnp.broadcast_to(rows[:, :, None], rows.shape + (LANES,))


def _compress_weights(pos, w1):
    st, hd, gw, hidden, groups = NSA_CMP_STRIDE, HEAD_DIM, NSA_KV_WIDTH, NSA_CMP_HIDDEN, NSA_KV_GROUPS
    w = jnp.zeros((2, st, gw, groups * hidden), F32)
    for g in range(groups):
        w = w.at[:, :, g * hd:(g + 1) * hd, g * hidden:(g + 1) * hidden].set(w1.reshape(2, st, hd, hidden))
    ps = jnp.tile(pos.reshape(2, st, 1, hd), (1, 1, groups, 1))
    return ps.reshape(2, 1, st * gw), w.reshape(2, st * gw, groups * hidden).astype(BF16)


def kernel(x, p, ffn1_norm, ffn1_w1, ffn1_w3, ffn1_w2, mix_norm, w_in, cmp_pos_k, cmp_w1_k, cmp_w2_k, cmp_pos_v, cmp_w1_v, cmp_w2_v, w_up_nsa, w_up_moba, w_out, ffn2_norm, ffn2_w1, ffn2_w3, ffn2_w2, ple_norm, w_ple_gate, w_ple, final_norm):
    b, s, d = x.shape
    n = b * s
    depth = p.shape[0]
    n_all = NSA_HEADS + MOBA_HEADS
    slopes = jnp.exp2(-8.0 * (jnp.arange(n_all, dtype=F32) + 1.0) / n_all)
    slopes2_n, slopes2_m = slopes[0::2] * LOG2E, slopes[1::2] * LOG2E
    qbias_n, qbias_m = _slope_rows(slopes2_n), _slope_rows(slopes2_m)

    n_cmp_rows = s // NSA_CMP_STRIDE
    n_cmp = (s - NSA_CMP_BLOCK) // NSA_CMP_STRIDE + 1
    n_sel = s // NSA_SEL_BLOCK
    c_start = np.arange(n_cmp_rows) * NSA_CMP_STRIDE
    s_start = np.arange(n_sel) * NSA_SEL_BLOCK
    overlap_t = ((c_start[None, :] <= s_start[:, None] + NSA_SEL_BLOCK - 1)
                 & (c_start[None, :] + NSA_CMP_BLOCK - 1 >= s_start[:, None])
                 & (np.arange(n_cmp_rows)[None, :] < n_cmp))
    overlap_t = jnp.asarray(np.tile(overlap_t, (1, N_SPLIT)), BF16)
    table_sel = _key_table(s, NSA_SEL_BLOCK)
    table_win = _key_table(s, None)
    table_moba = _key_table(s, MOBA_BLOCK)

    h = x.reshape(n, d)
    for i in range(depth):
        vec = lambda a: a[i].reshape(1, -1)
        wi = w_in[i]
        col = lambda lo, width: wi[:, lo:lo + width]
        wq_t = jnp.concatenate([col(0, NSA_WIDTH), col(OFF_MOBA_Q, MOBA_WIDTH)], axis=1).T.astype(BF16)
        wk = jnp.concatenate([col(OFF_MOBA_K, MOBA_WIDTH), col(OFF_KSLC, NSA_KV_WIDTH),
                              col(OFF_KWIN, NSA_KV_WIDTH)], axis=1).astype(BF16)
        wc = jnp.stack([col(OFF_KCMP, NSA_KV_WIDTH), col(OFF_VCMP, NSA_KV_WIDTH)]).astype(BF16)
        wv_t = jnp.concatenate([col(OFF_VSLC, NSA_KV_WIDTH), col(OFF_VWIN, NSA_KV_WIDTH),
                                col(OFF_MOBA_V, MOBA_WIDTH)], axis=1).T.astype(BF16)
        wg_t = jnp.pad(col(OFF_GATE_LOGITS, N_GATE_LOGITS), ((0, 0), (0, LANES - N_GATE_LOGITS))).T.astype(BF16)
        w_merge = col(OFF_MERGE, 2 * D_MODEL).astype(BF16)
        h, qT, k_tok, cmp_tok, vT, zgT = _ffn(
            h, vec(ffn1_norm), ffn1_w1[i].astype(BF16), ffn1_w3[i].astype(BF16), ffn1_w2[i].astype(BF16),
            project=(vec(mix_norm), wq_t, wk, wc, wv_t, wg_t))

        cmp_w = [_compress_weights(ps[i], w1[i]) for ps, w1 in ((cmp_pos_k, cmp_w1_k), (cmp_pos_v, cmp_w1_v))]
        pos = jnp.stack([pw[0] for pw in cmp_w])
        w1c = jnp.stack([pw[1] for pw in cmp_w])
        w2 = jnp.stack([cmp_w2_k[i], cmp_w2_v[i]])
        cmp_rows = cmp_tok.reshape(2, b, n_cmp_rows, NSA_CMP_STRIDE * NSA_KV_WIDTH)
        kc, kcT = _compress(cmp_rows, pos, w1c, w2, w2.transpose(0, 2, 1))

        ocmpT, negsel_n = _cmp_select(slopes2_n, qT, kc, kcT, overlap_t, b)
        hs = ATTN_HEADS_PER_STEP
        q_rows, pair = hs * HEAD_DIM, 2 * HEAD_DIM
        nsa = dict(batch=b, n_steps=NSA_KV_GROUPS, n_pairs=1, heads_per_pair=hs, both_slots=False, heads_per_v=hs,
                   q_block=lambda g: QT_ROW_NSA // q_rows + g, slot_of=lambda j, g: g, clamp_sum=True)
        oselT = _attention(slopes2_n, qT, k_tok, table_sel, vT, qbias_n, negsel_n,
                           k_block=lambda g: K_COL_SLC // pair, v_block=lambda g: VT_ROW_SLC // HEAD_DIM + g,
                           name="attn_select", **nsa)
        owinT = _attention(slopes2_n, qT, k_tok, table_win, vT, qbias_n, None,
                           k_block=lambda g: K_COL_WIN // pair, v_block=lambda g: VT_ROW_WIN // HEAD_DIM + g,
                           window=NSA_WINDOW, name="attn_window", **nsa)
        negsel_m = _moba_select(qT, k_tok, b, q_row_block=QT_ROW_MOBA // MOBA_WIDTH,
                                k_col_block=K_COL_MOBA // MOBA_WIDTH)
        omobaT = _attention(slopes2_m, qT, k_tok, table_moba, vT, qbias_m, negsel_m, batch=b,
                            n_steps=MOBA_HEADS // hs, n_pairs=hs // 2, heads_per_pair=2, both_slots=True, heads_per_v=1,
                            q_block=lambda g: QT_ROW_MOBA // q_rows + g,
                            k_block=lambda g: K_COL_MOBA // (hs // 2 * pair) + g,
                            v_block=lambda g: VT_ROW_MOBA // q_rows + g, slot_of=lambda j, g: j,
                            clamp_sum=False, name="attn_moba")

        h = _mix(h, vec(mix_norm), zgT, ocmpT, oselT, owinT, omobaT, w_merge,
                 w_up_nsa[i].astype(BF16), w_up_moba[i].astype(BF16), w_out[i].astype(BF16))

        h = _ffn(h, vec(ffn2_norm), ffn2_w1[i].astype(BF16), ffn2_w3[i].astype(BF16), ffn2_w2[i].astype(BF16),
                 tail=(p[i].reshape(n, PLE_DIM), vec(ple_norm), w_ple_gate[i].astype(BF16),
                       w_ple[i].astype(BF16)),
                 final_norm=final_norm.reshape(1, -1) if i + 1 == depth else None)
    return h.reshape(b, s, d)
```

```python
import functools

import numpy as np
import jax
import jax.numpy as jnp
from jax import lax
from jax.experimental import pallas as pl
from jax.experimental.pallas import tpu as pltpu

F32 = jnp.float32
BF16 = jnp.bfloat16

D_MODEL = 1024
HEAD_DIM = 64
NSA_HEADS = 8
NSA_KV_GROUPS = 2
NSA_HEADS_PER_GROUP = NSA_HEADS // NSA_KV_GROUPS
NSA_CMP_BLOCK = 32
NSA_CMP_STRIDE = 16
NSA_CMP_HIDDEN = 128
NSA_SEL_BLOCK = 64
NSA_SEL_TOPK = 16
NSA_WINDOW = 512
MOBA_HEADS = 8
MOBA_BLOCK = 256
MOBA_TOPK = 3
D_FF = 2816
PLE_DIM = 256
RMS_EPS = 1e-6
NEG_INF = -1e30
TINY = 1e-30
FORCE_SCORE = 1e9
LOG2E = float(np.log2(np.e))
Q_SCALE = HEAD_DIM ** -0.5 * LOG2E

NSA_WIDTH = NSA_HEADS * HEAD_DIM
NSA_KV_WIDTH = NSA_KV_GROUPS * HEAD_DIM
MOBA_WIDTH = MOBA_HEADS * HEAD_DIM
N_GATE_LOGITS = 3 * NSA_HEADS
OFF_KCMP = NSA_WIDTH
OFF_VCMP = OFF_KCMP + NSA_KV_WIDTH
OFF_KSLC = OFF_VCMP + NSA_KV_WIDTH
OFF_VSLC = OFF_KSLC + NSA_KV_WIDTH
OFF_KWIN = OFF_VSLC + NSA_KV_WIDTH
OFF_VWIN = OFF_KWIN + NSA_KV_WIDTH
OFF_GATE_LOGITS = OFF_VWIN + NSA_KV_WIDTH
OFF_MOBA_Q = OFF_GATE_LOGITS + N_GATE_LOGITS
OFF_MOBA_K = OFF_MOBA_Q + MOBA_WIDTH
OFF_MOBA_V = OFF_MOBA_K + MOBA_WIDTH
OFF_MERGE = OFF_MOBA_V + MOBA_WIDTH

LANES = 128
VMEM_LIMIT = 56 * 1024 * 1024

TOKEN_TILE = 512
FF_CHUNK = 256
ATTN_TILE = 512
WINDOW_TILE = 256
CMP_TQ = 2048
ATTN_UNROLL = 2
ATTN_FAST_BODY = 28
ATTN_SAFE_BODY = 4
SLOPE_ROWS = 16
N_SLOPE_PIECES = 3
N_SPLIT = 2
MOBA_SEL_ROWS = 16
ONES_ROWS = 16
MASK_BIAS_ROW = {"diag": 0, "lower": 1}
SAFE_EXCESS = 60.0

QT_ROW_NSA, QT_ROW_MOBA = 0, NSA_WIDTH
K_COL_MOBA, K_COL_SLC, K_COL_WIN = 0, MOBA_WIDTH, MOBA_WIDTH + NSA_KV_WIDTH
VT_ROW_SLC, VT_ROW_WIN, VT_ROW_MOBA = 0, NSA_KV_WIDTH, 2 * NSA_KV_WIDTH
ATTN_HEADS_PER_STEP = 4

_NT = (((1,), (1,)), ((), ()))
_TN = (((0,), (0,)), ((), ()))


def _cparams(*sem):
    return pltpu.CompilerParams(dimension_semantics=sem, vmem_limit_bytes=VMEM_LIMIT)


def _resident(shape):
    nd = len(shape)
    return pl.BlockSpec(shape, lambda *_: (0,) * nd, pipeline_mode=pl.Buffered(1))


def _rms(x, g):
    ms = jnp.mean(x * x, axis=-1, keepdims=True)
    return x * lax.rsqrt(ms + RMS_EPS) * g


def _sigmoid(x):
    return 1.0 / (1.0 + jnp.exp(-x))


def _dot(a, b):
    return jnp.dot(a, b, preferred_element_type=F32)


def _dot_f32(a, b):
    return jnp.dot(a, b, preferred_element_type=F32, precision=lax.Precision.HIGHEST)


def _dot_nt(a, b, precision=None):
    return lax.dot_general(a, b, _NT, preferred_element_type=F32, precision=precision)


def _dot_tn(a, b):
    return lax.dot_general(a, b, _TN, preferred_element_type=F32)


def _ffn_kernel(*refs, tail, final, project):
    x_ref, g_ref, w1_ref, w3_ref, w2_ref = refs[:5]
    n_in = 5
    if tail:
        p_ref, gp_ref, wpg_ref, wp_ref = refs[n_in:n_in + 4]
        n_in += 4
    if final:
        gf_ref = refs[n_in]
        n_in += 1
    if project:
        gm_ref, wq_ref, wk_ref, wc_ref, wv_ref, wg_ref = refs[n_in:n_in + 6]
        n_in += 6
    o_ref = refs[n_in]
    acc_ref = refs[-1]
    x = x_ref[...]
    xn = _rms(x, g_ref[...]).astype(BF16)
    for c in range(D_FF // FF_CHUNK):
        cols = slice(c * FF_CHUNK, (c + 1) * FF_CHUNK)
        a = _dot(xn, w1_ref[:, cols])
        b = _dot(xn, w3_ref[:, cols])
        hid = (a * _sigmoid(a) * b).astype(BF16)
        y = _dot(hid, w2_ref[cols, :])
        if c == 0:
            acc_ref[...] = y
        else:
            acc_ref[...] += y
    h = x + 0.5 * acc_ref[...]
    if tail:
        gate = _sigmoid(_dot(_rms(h, gp_ref[...]).astype(BF16), wpg_ref[...]))
        h = h + gate * _dot(p_ref[...].astype(BF16), wp_ref[...])
    if final:
        h = _rms(h, gf_ref[...])
    o_ref[...] = h
    if project:
        qT_ref, k_ref, cmp_ref, vT_ref, zgT_ref = refs[n_in + 1:n_in + 6]
        u = _rms(h, gm_ref[...]).astype(BF16)
        qT_ref[...] = (_dot_nt(wq_ref[...], u) * Q_SCALE).astype(BF16)
        k_ref[...] = _dot(u, wk_ref[...]).astype(BF16)
        for kind in range(2):
            cmp_ref[kind] = _dot(u, wc_ref[kind]).astype(BF16)
        vT_ref[...] = _dot_nt(wv_ref[...], u).astype(BF16)
        zgT_ref[...] = _dot_nt(wg_ref[...], u)


def _ffn(x, g, w1, w3, w2, tail=None, final_norm=None, project=None):
    n = x.shape[0]
    tm = TOKEN_TILE
    row = lambda w: pl.BlockSpec((tm, w), lambda i: (i, 0))
    col = lambda height: pl.BlockSpec((height, tm), lambda i: (0, i))
    args = [x, g, w1, w3, w2]
    specs = [row(D_MODEL), _resident((1, D_MODEL)), _resident(w1.shape), _resident(w3.shape),
             _resident(w2.shape)]
    out_specs = [row(D_MODEL)]
    out_shape = [jax.ShapeDtypeStruct((n, D_MODEL), F32)]
    if tail is not None:
        p, gp, wpg, wp = tail
        args += [p, gp, wpg, wp]
        specs += [row(PLE_DIM), _resident((1, D_MODEL)), _resident(wpg.shape), _resident(wp.shape)]
    if final_norm is not None:
        args.append(final_norm)
        specs.append(_resident((1, D_MODEL)))
    if project is not None:
        gm, wq_t, wk, wc, wv_t, wg_t = project
        args += [gm, wq_t, wk, wc, wv_t, wg_t]
        specs += [_resident(a.shape) for a in project]
        out_specs += [col(wq_t.shape[0]), row(wk.shape[1]),
                      pl.BlockSpec((2, tm, NSA_KV_WIDTH), lambda i: (0, i, 0)),
                      col(wv_t.shape[0]), col(LANES)]
        out_shape += [jax.ShapeDtypeStruct((wq_t.shape[0], n), BF16),
                      jax.ShapeDtypeStruct((n, wk.shape[1]), BF16),
                      jax.ShapeDtypeStruct((2, n, NSA_KV_WIDTH), BF16),
                      jax.ShapeDtypeStruct((wv_t.shape[0], n), BF16),
                      jax.ShapeDtypeStruct((LANES, n), F32)]
    outs = pl.pallas_call(
        functools.partial(_ffn_kernel, tail=tail is not None, final=final_norm is not None,
                          project=project is not None),
        grid=(n // tm,),
        in_specs=specs,
        out_specs=out_specs,
        out_shape=out_shape,
        scratch_shapes=[pltpu.VMEM((tm, D_MODEL), F32)],
        compiler_params=_cparams("parallel"),
        name="ffn" + ("_tail" if tail is not None else "") + ("_project" if project is not None else ""),
    )(*args)
    return outs if project is not None else outs[0]


def _split(x):
    pieces, rest = [], x
    for _ in range(N_SPLIT):
        piece = rest.astype(BF16)
        pieces.append(piece)
        rest = rest - piece.astype(F32)
    return tuple(pieces)


def _compress_kernel(rows_ref, pos_ref, w1_ref, w2_ref, w2t_ref, o_ref, ot_ref):
    c0 = float(np.sqrt(2.0 / np.pi))
    hidden = NSA_CMP_HIDDEN
    for kind in range(2):
        rows = rows_ref[kind].astype(F32)
        first = _dot((rows + pos_ref[kind, 0]).astype(BF16), w1_ref[kind, 0])
        second = _dot((rows + pos_ref[kind, 1]).astype(BF16), w1_ref[kind, 1])
        n_rows = rows.shape[0]
        pre = first + pltpu.roll(second, n_rows - 1, 0)
        hid = pre * (0.5 * (1.0 + jnp.tanh(c0 * (pre + 0.044715 * (pre * pre * pre)))))
        for g in range(NSA_KV_GROUPS):
            hid_g = hid[:, g * hidden:(g + 1) * hidden]
            out = _dot_f32(hid_g, w2_ref[kind])
            o_ref[kind * NSA_KV_GROUPS + g] = jnp.concatenate(_split(out), axis=1)
            ot_ref[kind * NSA_KV_GROUPS + g] = _dot_nt(w2t_ref[kind], hid_g, lax.Precision.HIGHEST).astype(BF16)


def _compress(cmp_rows, pos, w1, w2, w2t):
    _, b, n_rows, width = cmp_rows.shape
    n_out = 2 * NSA_KV_GROUPS
    return pl.pallas_call(
        _compress_kernel,
        grid=(b,),
        in_specs=[
            pl.BlockSpec((2, None, n_rows, width), lambda i: (0, i, 0, 0)),
            _resident(pos.shape), _resident(w1.shape), _resident(w2.shape), _resident(w2t.shape),
        ],
        out_specs=[
            pl.BlockSpec((None, n_out, n_rows, N_SPLIT * HEAD_DIM), lambda i: (i, 0, 0, 0)),
            pl.BlockSpec((None, n_out, HEAD_DIM, n_rows), lambda i: (i, 0, 0, 0)),
        ],
        out_shape=[jax.ShapeDtypeStruct((b, n_out, n_rows, N_SPLIT * HEAD_DIM), BF16),
                   jax.ShapeDtypeStruct((b, n_out, HEAD_DIM, n_rows), BF16)],
        compiler_params=_cparams("parallel"),
        name="compress",
    )(cmp_rows, pos, w1, w2, w2t)


def _count_rank(score, n_candidates):
    sub = 8
    tiles = [(lo, min(lo + sub, score.shape[0])) for lo in range(0, score.shape[0], sub)]
    ranks = [jnp.zeros((hi - lo,) + score.shape[1:], jnp.int32) for lo, hi in tiles]
    for j in range(n_candidates):
        row = score[j:j + 1, :]
        for n, (lo, hi) in enumerate(tiles):
            part = score[lo:hi]
            if lo > j:
                ahead = row >= part
            elif hi - 1 <= j:
                ahead = row > part
            else:
                after = lo + lax.broadcasted_iota(jnp.int32, part.shape, 0) > j
                ahead = (row > part) | ((row == part) & after)
            ranks[n] = ranks[n] + ahead.astype(jnp.int32)
    return ranks[0] if len(ranks) == 1 else jnp.concatenate(ranks, axis=0)


def _cmp_select_kernel(slopes_ref, qT_ref, kc_ref, vcT_ref, ovT_ref, oT_ref, negsel_ref):
    g = pl.program_id(1)
    qi = pl.program_id(2)
    tq = qT_ref.shape[1]
    n_cmp = kc_ref.shape[0]
    n_sel = ovT_ref.shape[0]
    hd = HEAD_DIM
    t = qi * tq + lax.broadcasted_iota(jnp.int32, (n_cmp, tq), 1)
    blk_c = lax.broadcasted_iota(jnp.int32, (n_cmp, tq), 0)
    dist = t - (blk_c * NSA_CMP_STRIDE + (NSA_CMP_BLOCK - 1))
    visible = dist >= 0
    dist_f = dist.astype(F32)
    kc_pieces = kc_ref[...]
    vcT = vcT_ref[...]
    p_sum = jnp.zeros((n_cmp, tq), F32)
    for hh in range(NSA_HEADS_PER_GROUP):
        slope2 = slopes_ref[g * NSA_HEADS_PER_GROUP + hh]
        q = qT_ref[hh * hd:(hh + 1) * hd, :]
        s = _dot(kc_pieces, jnp.concatenate([q] * N_SPLIT, axis=0)) - slope2 * dist_f
        s = jnp.where(visible, s, NEG_INF)
        m = jnp.max(s, axis=0, keepdims=True)
        e = jnp.exp2(s - m)
        any_visible = m > 0.5 * NEG_INF
        p = e * jnp.where(any_visible, 1.0 / jnp.maximum(jnp.sum(e, axis=0, keepdims=True), TINY), 0.0)
        p_sum = p_sum + p
        oT_ref[hh * hd:(hh + 1) * hd, :] = _dot(vcT, p.astype(BF16))
    imp = _dot(ovT_ref[...], jnp.concatenate(_split(p_sum), axis=0))
    ts = qi * tq + lax.broadcasted_iota(jnp.int32, (n_sel, tq), 1)
    blk = lax.broadcasted_iota(jnp.int32, (n_sel, tq), 0)
    cur = ts // NSA_SEL_BLOCK
    causal = blk * NSA_SEL_BLOCK <= ts
    forced = (blk == 0) | (blk == cur) | (blk == cur - 1)
    imp = jnp.where(causal, jnp.where(forced, FORCE_SCORE, imp), NEG_INF)
    rank = _count_rank(imp, n_sel)
    chosen = (rank < min(NSA_SEL_TOPK, n_sel)) & causal
    negsel_ref[...] = jnp.where(chosen, 0.0, NEG_INF).astype(BF16)


def _cmp_select(slopes2, qT, kc, vcT, overlap_t, batch):
    n = qT.shape[1]
    tq = min(CMP_TQ, n // batch)
    nq = n // batch // tq
    rows = NSA_HEADS_PER_GROUP * HEAD_DIM
    n_cmp = kc.shape[2]
    n_sel = overlap_t.shape[0]
    return pl.pallas_call(
        _cmp_select_kernel,
        grid=(batch, NSA_KV_GROUPS, nq),
        in_specs=[
            pl.BlockSpec(memory_space=pltpu.SMEM),
            pl.BlockSpec((rows, tq), lambda i, g, q: (g, i * nq + q)),
            pl.BlockSpec((None, None, n_cmp, N_SPLIT * HEAD_DIM), lambda i, g, q: (i, g, 0, 0)),
            pl.BlockSpec((None, None, HEAD_DIM, n_cmp), lambda i, g, q: (i, NSA_KV_GROUPS + g, 0, 0)),
            pl.BlockSpec((n_sel, N_SPLIT * n_cmp), lambda i, g, q: (0, 0)),
        ],
        out_specs=[
            pl.BlockSpec((rows, tq), lambda i, g, q: (g, i * nq + q)),
            pl.BlockSpec((None, n_sel, tq), lambda i, g, q: (g, 0, i * nq + q)),
        ],
        out_shape=[jax.ShapeDtypeStruct((NSA_WIDTH, n), F32),
                   jax.ShapeDtypeStruct((NSA_KV_GROUPS, n_sel, n), BF16)],
        compiler_params=_cparams("parallel", "parallel", "parallel"),
        name="cmp_select",
    )(slopes2, qT, kc, vcT, overlap_t)


def _moba_select_kernel(qT_ref, k_ref, negsel_ref):
    s = k_ref.shape[0]
    nb = s // MOBA_BLOCK
    hd = HEAD_DIM
    kmean = jnp.mean(k_ref[...].astype(F32).reshape(nb, MOBA_BLOCK, k_ref.shape[1]), axis=1)
    t = lax.broadcasted_iota(jnp.int32, (nb, s), 1)
    blk = lax.broadcasted_iota(jnp.int32, (nb, s), 0)
    cur = t // MOBA_BLOCK
    past = blk < cur
    for j in range(MOBA_HEADS):
        gs = _dot_f32(kmean[:, j * hd:(j + 1) * hd], qT_ref[j * hd:(j + 1) * hd, :].astype(F32))
        gs = jnp.where(past, gs, NEG_INF)
        rank = _count_rank(gs, nb)
        chosen = ((rank < min(MOBA_TOPK, nb - 1)) & past) | (blk == cur)
        negsel_ref[j, 0:nb, :] = jnp.where(chosen, 0.0, NEG_INF).astype(BF16)
        negsel_ref[j, nb:, :] = jnp.zeros((MOBA_SEL_ROWS - nb, s), BF16)


def _moba_select(qT, k_tok, batch, q_row_block, k_col_block):
    n = qT.shape[1]
    s = n // batch
    return pl.pallas_call(
        _moba_select_kernel,
        grid=(batch,),
        in_specs=[pl.BlockSpec((MOBA_WIDTH, s), lambda i: (q_row_block, i)),
                  pl.BlockSpec((s, MOBA_WIDTH), lambda i: (i, k_col_block))],
        out_specs=pl.BlockSpec((MOBA_HEADS, MOBA_SEL_ROWS, s), lambda i: (0, 0, i)),
        out_shape=jax.ShapeDtypeStruct((MOBA_HEADS, MOBA_SEL_ROWS, n), BF16),
        compiler_params=_cparams("parallel"),
        name="moba_select",
    )(qT, k_tok)


def _attn_schedule(nq, window_chunks):
    def padded(pairs, start):
        n_pad = (-(start + len(pairs) - 1)) % ATTN_UNROLL
        return pairs + [(0, 0, 0)] * n_pad

    diag = [(q, q, 1) for q in range(nq)]
    groups = [("diag", padded(diag, 0))]
    if window_chunks is None:
        full = [(q, c, 1) for q in range(nq) for c in range(q)]
        groups.append((None, full))
    else:
        assert window_chunks == 2
        done = len(groups[0][1])
        lower = padded([(q, q - 2, 1) for q in range(2, nq)], done)
        groups.append(("lower", lower))
        groups.append(("middle", [(q, q - 1, 1) for q in range(1, nq)]))
    pairs, kinds = [], []
    for kind, group in groups:
        pairs += group
        kinds += [kind] * len(group)
    n_pad = (-(len(pairs) - 1)) % ATTN_UNROLL
    pairs += [(0, 0, 0)] * n_pad
    kinds += [kinds[-1] if kinds[-1] is not None else "diag"] * n_pad
    body_kinds = []
    for j in range((len(pairs) - 1) // ATTN_UNROLL):
        ks = set(kinds[1 + j * ATTN_UNROLL:1 + (j + 1) * ATTN_UNROLL])
        assert len(ks) == 1, "a loop body must issue scores of one mask kind"
        body_kinds.append(ks.pop())
    table = np.asarray(pairs, np.int32)
    return table[:, 0], table[:, 1], table[:, 2], tuple(body_kinds)


def _attn_kernel(*refs, tile, n_pairs, heads_per_pair, both_slots, heads_per_v, has_sel, slot_of, body_kinds,
                 clamp_sum):
    n_in = 11 if has_sel else 10
    (qtab_ref, ctab_ref, vtab_ref, slopes_ref, qT_ref, k_ref, ktab_ref, vT_ref, qbias_ref,
     maskbias_ref) = refs[:10]
    negsel_ref = refs[10] if has_sel else None
    oT_ref = refs[n_in]
    qaug_ref, vaug_ref, m_ref, acc_ref, excess_ref = refs[n_in + 1:n_in + 6]
    s_bufs = refs[n_in + 6:n_in + 8]
    cmax_bufs = refs[n_in + 8:n_in + 10]
    alpha_bufs = refs[n_in + 10:n_in + 12]
    p_bufs = refs[n_in + 12:n_in + 14]
    grp = pl.program_id(1)
    tq, tk, hd = tile, tile, HEAD_DIM
    nq = qaug_ref.shape[0]
    hpp = heads_per_pair
    nh = n_pairs * hpp
    pcol = hpp * tq
    ncol = n_pairs * pcol
    n_vgroups = nh // heads_per_v
    vcol = heads_per_v * tq
    arows = hd + ONES_ROWS
    n_steps = qtab_ref.shape[0]
    unroll = ATTN_UNROLL

    for vg in range(n_vgroups):
        vaug_ref[vg * arows:vg * arows + hd, :] = vT_ref[vg * hd:(vg + 1) * hd, :]
        vaug_ref[vg * arows + hd:(vg + 1) * arows, :] = jnp.ones((ONES_ROWS, vaug_ref.shape[1]), BF16)

    for qi in range(nq):
        toks = slice(qi * tq, (qi + 1) * tq)
        for i in range(nh):
            cols = slice(i * tq, (i + 1) * tq)
            slot = slot_of(i % hpp, grp)
            q = qT_ref[i * hd:(i + 1) * hd, toks]
            bias = [jnp.concatenate([qbias_ref[i]] * (tq // LANES), axis=1)]
            if has_sel:
                bias.append(negsel_ref[i % negsel_ref.shape[0], :, toks])
            used = sum(part.shape[0] for part in bias)
            bias = jnp.concatenate(bias + [jnp.zeros((hd - used, tq), BF16)], axis=0)
            if both_slots:
                zero = jnp.zeros_like(q)
                qaug_ref[qi, 0:hd, cols] = jnp.where(slot == 0, q, zero)
                qaug_ref[qi, hd:2 * hd, cols] = jnp.where(slot == 0, zero, q)
                qaug_ref[qi, 2 * hd:3 * hd, cols] = bias
                qaug_ref[qi, 3 * hd:, cols] = zero
            else:
                qaug_ref[qi, 0:hd, cols] = jnp.where(slot == 0, q, bias)
                qaug_ref[qi, hd:2 * hd, cols] = jnp.where(slot == 0, bias, q)

    lane_half = lax.broadcasted_iota(jnp.int32, (tk, LANES), 1) // hd

    def key_operand(keys, pr, slot):
        k_pair = k_ref[keys, pr * LANES:(pr + 1) * LANES]
        if both_slots:
            return jnp.concatenate([k_pair, ktab_ref[1, keys, :]], axis=1)
        return jnp.where(lane_half == slot, k_pair, ktab_ref[slot, keys, :])

    if both_slots:
        col_groups = [(pr, None, slice(pr * pcol, (pr + 1) * pcol)) for pr in range(n_pairs)]
    else:
        col_groups = [(pr, slot_of(0, grp), slice(pr * pcol, (pr + 1) * pcol)) for pr in range(n_pairs)]

    key_i = lax.broadcasted_iota(jnp.int32, (tk, ncol), 0)
    qry_row = lax.broadcasted_iota(jnp.int32, (1, ncol), 1) % tq
    big = jnp.int32(2 ** 30)

    def visible(kind, valid):
        if kind == "diag":
            return key_i <= jnp.where(valid, qry_row, -1)
        if kind == "lower":
            return key_i > jnp.where(valid, qry_row, big)
        if kind == "middle":
            return key_i >= jnp.where(valid, 0, big)
        return None

    def key_slice(t):
        return pl.ds(pl.multiple_of(ctab_ref[t] * tk, tk), tk)

    def scores(t, buf, kind):
        qi = qtab_ref[t]
        keys = key_slice(t)
        mask = visible(kind, vtab_ref[t] > 0)
        for pr, slot, cols in col_groups:
            s = _dot(key_operand(keys, pr, slot), qaug_ref[qi, :, cols])
            if mask is not None:
                s = jnp.where(mask[:, cols], s, NEG_INF)
            s_bufs[buf][:, cols] = s
            cmax_bufs[buf][:, cols] = jnp.max(s, axis=0, keepdims=True)

    def softmax(t, buf):
        qi = qtab_ref[t]
        m = m_ref[qi]
        m_new = jnp.maximum(m, cmax_bufs[buf][...])
        m_ref[qi] = m_new
        alpha_bufs[buf][...] = jnp.exp2(m - m_new)
        p_bufs[buf][...] = jnp.exp2(s_bufs[buf][...] - m_new).astype(BF16)

    def weighted(t, buf, rescale):
        qi = qtab_ref[t]
        keys = key_slice(t)
        for vg in range(n_vgroups):
            cols = slice(vg * vcol, (vg + 1) * vcol)
            pv = _dot(vaug_ref[vg * arows:(vg + 1) * arows, keys], p_bufs[buf][:, cols])
            old = acc_ref[qi, :, cols]
            acc_ref[qi, :, cols] = (alpha_bufs[buf][:, cols] * old if rescale else old) + pv

    def fast_scores(t, buf, kind):
        qi = qtab_ref[t]
        keys = key_slice(t)
        ref = jnp.where(vtab_ref[t] > 0, m_ref[qi], -NEG_INF)
        for pr, slot, cols in col_groups:
            s = _dot(key_operand(keys, pr, slot), qaug_ref[qi, :, cols])
            if kind in MASK_BIAS_ROW:
                s = s + maskbias_ref[MASK_BIAS_ROW[kind], :, cols]
            excess_ref[:, cols] = jnp.maximum(excess_ref[:, cols], jnp.max(s, axis=0, keepdims=True) - ref[:, cols])
            p_bufs[buf][:, cols] = jnp.exp2(s - ref[:, cols]).astype(BF16)

    def pipeline(step, max_body):
        first = 0
        for kind in sorted(set(body_kinds), key=body_kinds.index):
            count = body_kinds.count(kind)
            assert body_kinds[first:first + count] == (kind,) * count
            merge = max(m for m in range(1, count + 1) if count % m == 0 and m * unroll <= max_body)
            steps = merge * unroll

            @pl.loop(0, count // merge)
            def _(j, kind=kind, base=first * unroll, steps=steps):
                for u in range(steps):
                    step(base + j * steps + u, u, kind)

            first += count

    last = n_steps - 1

    for qi in range(nq):
        pos = (qi * tq + lax.broadcasted_iota(jnp.int32, (1, tq), 1)).astype(F32)
        for i in range(nh):
            m_ref[qi, :, i * tq:(i + 1) * tq] = slopes_ref[grp * nh + i] * pos
    acc_ref[...] = jnp.zeros(acc_ref.shape, F32)
    excess_ref[...] = jnp.full((1, ncol), NEG_INF, F32)
    fast_scores(0, 0, "diag")

    def fast_step(t, u, kind):
        fast_scores(t + 1, (u + 1) % 2, kind)
        weighted(t, u % 2, rescale=False)

    pipeline(fast_step, ATTN_FAST_BODY)
    weighted(last, last % 2, rescale=False)

    excess = excess_ref[...]
    @pl.when((jnp.max(excess) > SAFE_EXCESS) | (jnp.min(excess) < -SAFE_EXCESS))
    def _():
        m_ref[...] = jnp.full(m_ref.shape, NEG_INF, F32)
        acc_ref[...] = jnp.zeros(acc_ref.shape, F32)
        p_bufs[1][...] = jnp.zeros(p_bufs[1].shape, BF16)
        alpha_bufs[1][...] = jnp.ones((1, ncol), F32)
        scores(0, 0, "diag")

        def safe_step(t, u, kind):
            scores(t + 1, (u + 1) % 2, kind)
            weighted(jnp.maximum(t - 1, 0), (u + 1) % 2, rescale=True)
            softmax(t, u % 2)

        pipeline(safe_step, ATTN_SAFE_BODY)
        weighted(last - 1, (last - 1) % 2, rescale=True)
        softmax(last, last % 2)
        weighted(last, last % 2, rescale=True)

    for qi in range(nq):
        for i in range(nh):
            cols = slice(i * tq, (i + 1) * tq)
            l = acc_ref[qi, hd:hd + 1, cols]
            oT_ref[i * hd:(i + 1) * hd, qi * tq:(qi + 1) * tq] = (
                acc_ref[qi, 0:hd, cols] * (1.0 / (jnp.maximum(l, TINY) if clamp_sum else l)))


def _attention(slopes2, qT, k_tok, key_table, vT, qbias, negsel, *, batch, n_steps, n_pairs,
               heads_per_pair, both_slots, heads_per_v, q_block, k_block, v_block, slot_of, tile, window=None,
               clamp_sum, name):
    n = qT.shape[1]
    s = n // batch
    tq, tk, hd = tile, tile, HEAD_DIM
    assert s % tq == 0
    nq = s // tq
    nh = n_pairs * heads_per_pair
    assert nh % heads_per_v == 0
    ncol = nh * tq
    arows = hd + ONES_ROWS
    n_vgroups = nh // heads_per_v
    if window is not None:
        assert window % tk == 0
    q_tab, c_tab, v_tab, body_kinds = _attn_schedule(nq, None if window is None else window // tk)
    key_i, qry_i = np.arange(tk)[:, None], (np.arange(ncol) % tq)[None, :]
    tables = {"diag": key_i <= qry_i, "lower": key_i > qry_i}
    mask_bias = jnp.asarray(np.stack([np.where(tables[k], 0.0, NEG_INF) for k in MASK_BIAS_ROW
                                      if k in body_kinds]), F32)
    smem = pl.BlockSpec(memory_space=pltpu.SMEM)
    in_specs = [
        smem, smem, smem, smem,
        pl.BlockSpec((nh * hd, s), lambda b, g: (q_block(g), b)),
        pl.BlockSpec((s, n_pairs * LANES), lambda b, g: (b, k_block(g))),
        _resident(key_table.shape),
        pl.BlockSpec((n_vgroups * hd, s), lambda b, g: (v_block(g), b)),
        pl.BlockSpec((nh, 16, LANES), lambda b, g: (g, 0, 0)),
        _resident(mask_bias.shape),
    ]
    args = [jnp.asarray(q_tab), jnp.asarray(c_tab), jnp.asarray(v_tab), slopes2, qT, k_tok, key_table, vT,
            qbias, mask_bias]
    if negsel is not None:
        per_step = negsel.shape[0] // n_steps
        in_specs.append(pl.BlockSpec((per_step, negsel.shape[1], s), lambda b, g: (g, 0, b)))
        args.append(negsel)
    vec = lambda: pltpu.VMEM((1, ncol), F32)
    return pl.pallas_call(
        functools.partial(_attn_kernel, tile=tile, n_pairs=n_pairs, heads_per_pair=heads_per_pair,
                          both_slots=both_slots, heads_per_v=heads_per_v, has_sel=negsel is not None,
                          slot_of=slot_of, body_kinds=body_kinds, clamp_sum=clamp_sum),
        grid=(batch, n_steps),
        in_specs=in_specs,
        out_specs=pl.BlockSpec((nh * hd, s), lambda b, g: (g, b)),
        out_shape=jax.ShapeDtypeStruct((n_steps * nh * hd, n), F32),
        scratch_shapes=[pltpu.VMEM((nq, (4 if both_slots else 2) * hd, ncol), BF16),
                        pltpu.VMEM((n_vgroups * arows, s), BF16),
                        pltpu.VMEM((nq, 1, ncol), F32),
                        pltpu.VMEM((nq, arows, ncol), F32),
                        vec(),
                        pltpu.VMEM((tk, ncol), F32), pltpu.VMEM((tk, ncol), F32),
                        vec(), vec(), vec(), vec(),
                        pltpu.VMEM((tk, ncol), BF16), pltpu.VMEM((tk, ncol), BF16)],
        compiler_params=_cparams("parallel", "parallel"),
        name=name,
    )(*args)


def _mix_kernel(h_ref, g_ref, zgT_ref, ocmp_ref, osel_ref, owin_ref, omoba_ref,
                wmerge_ref, wn_ref, wm_ref, wo_ref, o_ref, gate_ref, onsa_ref):
    h = h_ref[...]
    u = _rms(h, g_ref[...]).astype(BF16)
    gate_ref[...] = _sigmoid(zgT_ref[0:gate_ref.shape[0], :])
    hd = HEAD_DIM
    for hh in range(NSA_HEADS):
        rows = slice(hh * hd, (hh + 1) * hd)
        o = None
        for r, branch in enumerate((ocmp_ref, osel_ref, owin_ref)):
            gate = gate_ref[3 * hh + r:3 * hh + r + 1, :]
            term = gate * branch[rows, :]
            o = term if o is None else o + term
        onsa_ref[rows, :] = o.astype(BF16)
    y_n = _dot_tn(onsa_ref[...], wn_ref[...])
    y_m = _dot_tn(omoba_ref[...].astype(BF16), wm_ref[...])
    mixed = (_sigmoid(_dot(u, wmerge_ref[:, :D_MODEL])) * y_n
             + _sigmoid(_dot(u, wmerge_ref[:, D_MODEL:])) * y_m)
    o_ref[...] = h + _dot(mixed.astype(BF16), wo_ref[...])


def _mix(h, g, zgT, ocmpT, oselT, owinT, omobaT, wmerge, wn, wm, wo):
    n = h.shape[0]
    tm = TOKEN_TILE
    rows = lambda width: pl.BlockSpec((tm, width), lambda i: (i, 0))
    cols = lambda height: pl.BlockSpec((height, tm), lambda i: (0, i))
    gate_rows = -(-N_GATE_LOGITS // 8) * 8
    return pl.pallas_call(
        _mix_kernel,
        grid=(n // tm,),
        in_specs=[rows(D_MODEL), _resident((1, D_MODEL)), cols(LANES), cols(NSA_WIDTH), cols(NSA_WIDTH),
                  cols(NSA_WIDTH), cols(MOBA_WIDTH), _resident(wmerge.shape),
                  _resident(wn.shape), _resident(wm.shape), _resident(wo.shape)],
        out_specs=rows(D_MODEL),
        out_shape=jax.ShapeDtypeStruct((n, D_MODEL), F32),
        scratch_shapes=[pltpu.VMEM((gate_rows, tm), F32), pltpu.VMEM((NSA_WIDTH, tm), BF16)],
        compiler_params=_cparams("parallel"),
        name="mix",
    )(h, g, zgT, ocmpT, oselT, owinT, omobaT, wmerge, wn, wm, wo)


def _key_table(s, block):
    pos = np.arange(s)
    tab = np.zeros((s, HEAD_DIM), np.float32)
    tab[:, 0:N_SLOPE_PIECES] = ((pos // LANES) * LANES)[:, None]
    tab[:, N_SLOPE_PIECES:2 * N_SLOPE_PIECES] = (pos % LANES)[:, None]
    if block is not None:
        tab[pos, SLOPE_ROWS + pos // block] = 1.0
    zeros = np.zeros_like(tab)
    both = np.stack([np.concatenate([zeros, tab], axis=1), np.concatenate([tab, zeros], axis=1)])
    return jnp.asarray(both, BF16)


def _slope_rows(slopes2):
    s1 = slopes2.astype(BF16)
    r1 = slopes2 - s1.astype(F32)
    s2 = r1.astype(BF16)
    s3 = (r1 - s2.astype(F32)).astype(BF16)
    rows = jnp.stack([s1, s2, s3, s1, s2, s3], axis=1)
    rows = jnp.pad(rows, ((0, 0), (0, 16 - 2 * N_SLOPE_PIECES)))
    return jnp.broadcast_to(rows[:, :, None], rows.shape + (LANES,))


def _compress_weights(pos, w1):
    st, hd, gw, hidden, groups = NSA_CMP_STRIDE, HEAD_DIM, NSA_KV_WIDTH, NSA_CMP_HIDDEN, NSA_KV_GROUPS
    w = jnp.zeros((2, st, gw, groups * hidden), F32)
    for g in range(groups):
        w = w.at[:, :, g * hd:(g + 1) * hd, g * hidden:(g + 1) * hidden].set(w1.reshape(2, st, hd, hidden))
    ps = jnp.tile(pos.reshape(2, st, 1, hd), (1, 1, groups, 1))
    return ps.reshape(2, 1, st * gw), w.reshape(2, st * gw, groups * hidden).astype(BF16)


def kernel(x, p, ffn1_norm, ffn1_w1, ffn1_w3, ffn1_w2, mix_norm, w_in, cmp_pos_k, cmp_w1_k, cmp_w2_k, cmp_pos_v, cmp_w1_v, cmp_w2_v, w_up_nsa, w_up_moba, w_out, ffn2_norm, ffn2_w1, ffn2_w3, ffn2_w2, ple_norm, w_ple_gate, w_ple, final_norm):
    b, s, d = x.shape
    n = b * s
    depth = p.shape[0]
    n_all = NSA_HEADS + MOBA_HEADS
    slopes = jnp.exp2(-8.0 * (jnp.arange(n_all, dtype=F32) + 1.0) / n_all)
    slopes2_n, slopes2_m = slopes[0::2] * LOG2E, slopes[1::2] * LOG2E
    qbias_n, qbias_m = _slope_rows(slopes2_n), _slope_rows(slopes2_m)

    n_cmp_rows = s // NSA_CMP_STRIDE
    n_cmp = (s - NSA_CMP_BLOCK) // NSA_CMP_STRIDE + 1
    n_sel = s // NSA_SEL_BLOCK
    c_start = np.arange(n_cmp_rows) * NSA_CMP_STRIDE
    s_start = np.arange(n_sel) * NSA_SEL_BLOCK
    overlap_t = ((c_start[None, :] <= s_start[:, None] + NSA_SEL_BLOCK - 1)
                 & (c_start[None, :] + NSA_CMP_BLOCK - 1 >= s_start[:, None])
                 & (np.arange(n_cmp_rows)[None, :] < n_cmp))
    overlap_t = jnp.asarray(np.tile(overlap_t, (1, N_SPLIT)), BF16)
    table_sel = _key_table(s, NSA_SEL_BLOCK)
    table_win = _key_table(s, None)
    table_moba = _key_table(s, MOBA_BLOCK)

    h = x.reshape(n, d)
    for i in range(depth):
        vec = lambda a: a[i].reshape(1, -1)
        wi = w_in[i]
        col = lambda lo, width: wi[:, lo:lo + width]
        wq_t = jnp.concatenate([col(0, NSA_WIDTH), col(OFF_MOBA_Q, MOBA_WIDTH)], axis=1).T.astype(BF16)
        wk = jnp.concatenate([col(OFF_MOBA_K, MOBA_WIDTH), col(OFF_KSLC, NSA_KV_WIDTH),
                              col(OFF_KWIN, NSA_KV_WIDTH)], axis=1).astype(BF16)
        wc = jnp.stack([col(OFF_KCMP, NSA_KV_WIDTH), col(OFF_VCMP, NSA_KV_WIDTH)]).astype(BF16)
        wv_t = jnp.concatenate([col(OFF_VSLC, NSA_KV_WIDTH), col(OFF_VWIN, NSA_KV_WIDTH),
                                col(OFF_MOBA_V, MOBA_WIDTH)], axis=1).T.astype(BF16)
        wg_t = jnp.pad(col(OFF_GATE_LOGITS, N_GATE_LOGITS), ((0, 0), (0, LANES - N_GATE_LOGITS))).T.astype(BF16)
        w_merge = col(OFF_MERGE, 2 * D_MODEL).astype(BF16)
        h, qT, k_tok, cmp_tok, vT, zgT = _ffn(
            h, vec(ffn1_norm), ffn1_w1[i].astype(BF16), ffn1_w3[i].astype(BF16), ffn1_w2[i].astype(BF16),
            project=(vec(mix_norm), wq_t, wk, wc, wv_t, wg_t))

        cmp_w = [_compress_weights(ps[i], w1[i]) for ps, w1 in ((cmp_pos_k, cmp_w1_k), (cmp_pos_v, cmp_w1_v))]
        pos = jnp.stack([pw[0] for pw in cmp_w])
        w1c = jnp.stack([pw[1] for pw in cmp_w])
        w2 = jnp.stack([cmp_w2_k[i], cmp_w2_v[i]])
        cmp_rows = cmp_tok.reshape(2, b, n_cmp_rows, NSA_CMP_STRIDE * NSA_KV_WIDTH)
        kc, kcT = _compress(cmp_rows, pos, w1c, w2, w2.transpose(0, 2, 1))

        ocmpT, negsel_n = _cmp_select(slopes2_n, qT, kc, kcT, overlap_t, b)
        hs = ATTN_HEADS_PER_STEP
        q_rows, pair = hs * HEAD_DIM, 2 * HEAD_DIM
        nsa = dict(batch=b, n_steps=NSA_KV_GROUPS, n_pairs=1, heads_per_pair=hs, both_slots=False, heads_per_v=hs,
                   q_block=lambda g: QT_ROW_NSA // q_rows + g, slot_of=lambda j, g: g, clamp_sum=True)
        oselT = _attention(slopes2_n, qT, k_tok, table_sel, vT, qbias_n, negsel_n,
                           k_block=lambda g: K_COL_SLC // pair, v_block=lambda g: VT_ROW_SLC // HEAD_DIM + g,
                           tile=ATTN_TILE, name="attn_select", **nsa)
        owinT = _attention(slopes2_n, qT, k_tok, table_win, vT, qbias_n, None,
                           k_block=lambda g: K_COL_WIN // pair, v_block=lambda g: VT_ROW_WIN // HEAD_DIM + g,
                           tile=WINDOW_TILE, window=NSA_WINDOW, name="attn_window", **nsa)
        negsel_m = _moba_select(qT, k_tok, b, q_row_block=QT_ROW_MOBA // MOBA_WIDTH,
                                k_col_block=K_COL_MOBA // MOBA_WIDTH)
        omobaT = _attention(slopes2_m, qT, k_tok, table_moba, vT, qbias_m, negsel_m, batch=b,
                            n_steps=MOBA_HEADS // hs, n_pairs=hs // 2, heads_per_pair=2, both_slots=True, heads_per_v=1,
                            q_block=lambda g: QT_ROW_MOBA // q_rows + g,
                            k_block=lambda g: K_COL_MOBA // (hs // 2 * pair) + g,
                            v_block=lambda g: VT_ROW_MOBA // q_rows + g, slot_of=lambda j, g: j,
                            tile=ATTN_TILE, clamp_sum=False, name="attn_moba")

        h = _mix(h, vec(mix_norm), zgT, ocmpT, oselT, owinT, omobaT, w_merge,
                 w_up_nsa[i].astype(BF16), w_up_moba[i].astype(BF16), w_out[i].astype(BF16))

        h = _ffn(h, vec(ffn2_norm), ffn2_w1[i].astype(BF16), ffn2_w3[i].astype(BF16), ffn2_w2[i].astype(BF16),
                 tail=(p[i].reshape(n, PLE_DIM), vec(ple_norm), w_ple_gate[i].astype(BF16),
                       w_ple[i].astype(BF16)),
                 final_norm=final_norm.reshape(1, -1) if i + 1 == depth else None)
    return h.reshape(b, s, d)
```

```python
import functools

import numpy as np
import jax
import jax.numpy as jnp
from jax import lax
from jax.experimental import pallas as pl
from jax.experimental.pallas import tpu as pltpu

F32 = jnp.float32
BF16 = jnp.bfloat16

D_MODEL = 1024
HEAD_DIM = 64
NSA_HEADS = 8
NSA_KV_GROUPS = 2
NSA_HEADS_PER_GROUP = NSA_HEADS // NSA_KV_GROUPS
NSA_CMP_BLOCK = 32
NSA_CMP_STRIDE = 16
NSA_CMP_HIDDEN = 128
NSA_SEL_BLOCK = 64
NSA_SEL_TOPK = 16
NSA_WINDOW = 512
MOBA_HEADS = 8
MOBA_BLOCK = 256
MOBA_TOPK = 3
D_FF = 2816
PLE_DIM = 256
RMS_EPS = 1e-6
NEG_INF = -1e30
TINY = 1e-30
FORCE_SCORE = 1e9
LOG2E = float(np.log2(np.e))
Q_SCALE = HEAD_DIM ** -0.5 * LOG2E

NSA_WIDTH = NSA_HEADS * HEAD_DIM
NSA_KV_WIDTH = NSA_KV_GROUPS * HEAD_DIM
MOBA_WIDTH = MOBA_HEADS * HEAD_DIM
N_GATE_LOGITS = 3 * NSA_HEADS
OFF_KCMP = NSA_WIDTH
OFF_VCMP = OFF_KCMP + NSA_KV_WIDTH
OFF_KSLC = OFF_VCMP + NSA_KV_WIDTH
OFF_VSLC = OFF_KSLC + NSA_KV_WIDTH
OFF_KWIN = OFF_VSLC + NSA_KV_WIDTH
OFF_VWIN = OFF_KWIN + NSA_KV_WIDTH
OFF_GATE_LOGITS = OFF_VWIN + NSA_KV_WIDTH
OFF_MOBA_Q = OFF_GATE_LOGITS + N_GATE_LOGITS
OFF_MOBA_K = OFF_MOBA_Q + MOBA_WIDTH
OFF_MOBA_V = OFF_MOBA_K + MOBA_WIDTH
OFF_MERGE = OFF_MOBA_V + MOBA_WIDTH

LANES = 128
VMEM_LIMIT = 56 * 1024 * 1024

TOKEN_TILE = 512
FF_CHUNK = 256
ATTN_TQ = 256
ATTN_TK = 256
CMP_TQ = 2048
ATTN_UNROLL = 2
ATTN_FAST_BODY = 28
ATTN_SAFE_BODY = 4
SLOPE_ROWS = 16
N_SLOPE_PIECES = 3
N_SPLIT = 2
MOBA_SEL_ROWS = 16
ONES_ROWS = 16
MASK_BIAS_ROW = {"diag": 0, "lower": 1}
SAFE_EXCESS = 60.0
SAFE_ABS = 1e37

QT_ROW_NSA, QT_ROW_MOBA = 0, NSA_WIDTH
K_COL_MOBA, K_COL_SLC, K_COL_WIN = 0, MOBA_WIDTH, MOBA_WIDTH + NSA_KV_WIDTH
VT_ROW_SLC, VT_ROW_WIN, VT_ROW_MOBA = 0, NSA_KV_WIDTH, 2 * NSA_KV_WIDTH
ATTN_HEADS_PER_STEP = 4

_NT = (((1,), (1,)), ((), ()))
_TN = (((0,), (0,)), ((), ()))


def _cparams(*sem):
    return pltpu.CompilerParams(dimension_semantics=sem, vmem_limit_bytes=VMEM_LIMIT)


def _resident(shape):
    nd = len(shape)
    return pl.BlockSpec(shape, lambda *_: (0,) * nd, pipeline_mode=pl.Buffered(1))


def _rms(x, g):
    ms = jnp.mean(x * x, axis=-1, keepdims=True)
    return x * lax.rsqrt(ms + RMS_EPS) * g


def _sigmoid(x):
    return 1.0 / (1.0 + jnp.exp(-x))


def _dot(a, b):
    return jnp.dot(a, b, preferred_element_type=F32)


def _dot_f32(a, b):
    return jnp.dot(a, b, preferred_element_type=F32, precision=lax.Precision.HIGHEST)


def _dot_nt(a, b, precision=None):
    return lax.dot_general(a, b, _NT, preferred_element_type=F32, precision=precision)


def _dot_tn(a, b):
    return lax.dot_general(a, b, _TN, preferred_element_type=F32)


def _ffn_kernel(*refs, tail, final, project):
    x_ref, g_ref, w1_ref, w3_ref, w2_ref = refs[:5]
    n_in = 5
    if tail:
        p_ref, gp_ref, wpg_ref, wp_ref = refs[n_in:n_in + 4]
        n_in += 4
    if final:
        gf_ref = refs[n_in]
        n_in += 1
    if project:
        gm_ref, wq_ref, wk_ref, wc_ref, wv_ref, wg_ref = refs[n_in:n_in + 6]
        n_in += 6
    o_ref = refs[n_in]
    acc_ref = refs[-1]
    x = x_ref[...]
    xn = _rms(x, g_ref[...]).astype(BF16)
    for c in range(D_FF // FF_CHUNK):
        cols = slice(c * FF_CHUNK, (c + 1) * FF_CHUNK)
        a = _dot(xn, w1_ref[:, cols])
        b = _dot(xn, w3_ref[:, cols])
        hid = (a * _sigmoid(a) * b).astype(BF16)
        y = _dot(hid, w2_ref[cols, :])
        if c == 0:
            acc_ref[...] = y
        else:
            acc_ref[...] += y
    h = x + 0.5 * acc_ref[...]
    if tail:
        gate = _sigmoid(_dot(_rms(h, gp_ref[...]).astype(BF16), wpg_ref[...]))
        h = h + gate * _dot(p_ref[...].astype(BF16), wp_ref[...])
    if final:
        h = _rms(h, gf_ref[...])
    o_ref[...] = h
    if project:
        qT_ref, k_ref, cmp_ref, vT_ref, zgT_ref = refs[n_in + 1:n_in + 6]
        u = _rms(h, gm_ref[...]).astype(BF16)
        qT_ref[...] = (_dot_nt(wq_ref[...], u) * Q_SCALE).astype(BF16)
        k_ref[...] = _dot(u, wk_ref[...]).astype(BF16)
        for kind in range(2):
            cmp_ref[kind] = _dot(u, wc_ref[kind]).astype(BF16)
        vT_ref[...] = _dot_nt(wv_ref[...], u).astype(BF16)
        zgT_ref[...] = _dot_nt(wg_ref[...], u)


def _ffn(x, g, w1, w3, w2, tail=None, final_norm=None, project=None):
    n = x.shape[0]
    tm = TOKEN_TILE
    row = lambda w: pl.BlockSpec((tm, w), lambda i: (i, 0))
    col = lambda height: pl.BlockSpec((height, tm), lambda i: (0, i))
    args = [x, g, w1, w3, w2]
    specs = [row(D_MODEL), _resident((1, D_MODEL)), _resident(w1.shape), _resident(w3.shape),
             _resident(w2.shape)]
    out_specs = [row(D_MODEL)]
    out_shape = [jax.ShapeDtypeStruct((n, D_MODEL), F32)]
    if tail is not None:
        p, gp, wpg, wp = tail
        args += [p, gp, wpg, wp]
        specs += [row(PLE_DIM), _resident((1, D_MODEL)), _resident(wpg.shape), _resident(wp.shape)]
    if final_norm is not None:
        args.append(final_norm)
        specs.append(_resident((1, D_MODEL)))
    if project is not None:
        gm, wq_t, wk, wc, wv_t, wg_t = project
        args += [gm, wq_t, wk, wc, wv_t, wg_t]
        specs += [_resident(a.shape) for a in project]
        out_specs += [col(wq_t.shape[0]), row(wk.shape[1]),
                      pl.BlockSpec((2, tm, NSA_KV_WIDTH), lambda i: (0, i, 0)),
                      col(wv_t.shape[0]), col(LANES)]
        out_shape += [jax.ShapeDtypeStruct((wq_t.shape[0], n), BF16),
                      jax.ShapeDtypeStruct((n, wk.shape[1]), BF16),
                      jax.ShapeDtypeStruct((2, n, NSA_KV_WIDTH), BF16),
                      jax.ShapeDtypeStruct((wv_t.shape[0], n), BF16),
                      jax.ShapeDtypeStruct((LANES, n), F32)]
    outs = pl.pallas_call(
        functools.partial(_ffn_kernel, tail=tail is not None, final=final_norm is not None,
                          project=project is not None),
        grid=(n // tm,),
        in_specs=specs,
        out_specs=out_specs,
        out_shape=out_shape,
        scratch_shapes=[pltpu.VMEM((tm, D_MODEL), F32)],
        compiler_params=_cparams("parallel"),
        name="ffn" + ("_tail" if tail is not None else "") + ("_project" if project is not None else ""),
    )(*args)
    return outs if project is not None else outs[0]


def _split(x):
    pieces, rest = [], x
    for _ in range(N_SPLIT):
        piece = rest.astype(BF16)
        pieces.append(piece)
        rest = rest - piece.astype(F32)
    return tuple(pieces)


def _compress_kernel(rows_ref, pos_ref, w1_ref, w2_ref, w2t_ref, o_ref, ot_ref):
    c0 = float(np.sqrt(2.0 / np.pi))
    hidden = NSA_CMP_HIDDEN
    for kind in range(2):
        rows = rows_ref[kind].astype(F32)
        first = _dot((rows + pos_ref[kind, 0]).astype(BF16), w1_ref[kind, 0])
        second = _dot((rows + pos_ref[kind, 1]).astype(BF16), w1_ref[kind, 1])
        n_rows = rows.shape[0]
        pre = first + pltpu.roll(second, n_rows - 1, 0)
        hid = pre * (0.5 * (1.0 + jnp.tanh(c0 * (pre + 0.044715 * (pre * pre * pre)))))
        for g in range(NSA_KV_GROUPS):
            hid_g = hid[:, g * hidden:(g + 1) * hidden]
            out = _dot_f32(hid_g, w2_ref[kind])
            o_ref[kind * NSA_KV_GROUPS + g] = jnp.concatenate(_split(out), axis=1)
            ot_ref[kind * NSA_KV_GROUPS + g] = _dot_nt(w2t_ref[kind], hid_g, lax.Precision.HIGHEST).astype(BF16)


def _compress(cmp_rows, pos, w1, w2, w2t):
    _, b, n_rows, width = cmp_rows.shape
    n_out = 2 * NSA_KV_GROUPS
    return pl.pallas_call(
        _compress_kernel,
        grid=(b,),
        in_specs=[
            pl.BlockSpec((2, None, n_rows, width), lambda i: (0, i, 0, 0)),
            _resident(pos.shape), _resident(w1.shape), _resident(w2.shape), _resident(w2t.shape),
        ],
        out_specs=[
            pl.BlockSpec((None, n_out, n_rows, N_SPLIT * HEAD_DIM), lambda i: (i, 0, 0, 0)),
            pl.BlockSpec((None, n_out, HEAD_DIM, n_rows), lambda i: (i, 0, 0, 0)),
        ],
        out_shape=[jax.ShapeDtypeStruct((b, n_out, n_rows, N_SPLIT * HEAD_DIM), BF16),
                   jax.ShapeDtypeStruct((b, n_out, HEAD_DIM, n_rows), BF16)],
        compiler_params=_cparams("parallel"),
        name="compress",
    )(cmp_rows, pos, w1, w2, w2t)


def _count_rank(score, n_candidates):
    sub = 8
    tiles = [(lo, min(lo + sub, score.shape[0])) for lo in range(0, score.shape[0], sub)]
    ranks = [jnp.zeros((hi - lo,) + score.shape[1:], jnp.int32) for lo, hi in tiles]
    for j in range(n_candidates):
        row = score[j:j + 1, :]
        for n, (lo, hi) in enumerate(tiles):
            part = score[lo:hi]
            if lo > j:
                ahead = row >= part
            elif hi - 1 <= j:
                ahead = row > part
            else:
                after = lo + lax.broadcasted_iota(jnp.int32, part.shape, 0) > j
                ahead = (row > part) | ((row == part) & after)
            ranks[n] = ranks[n] + ahead.astype(jnp.int32)
    return ranks[0] if len(ranks) == 1 else jnp.concatenate(ranks, axis=0)


def _cmp_select_kernel(slopes_ref, qT_ref, kc_ref, vcT_ref, ovT_ref, oT_ref, negsel_ref):
    g = pl.program_id(1)
    qi = pl.program_id(2)
    tq = qT_ref.shape[1]
    n_cmp = kc_ref.shape[0]
    n_sel = ovT_ref.shape[0]
    hd = HEAD_DIM
    t = qi * tq + lax.broadcasted_iota(jnp.int32, (n_cmp, tq), 1)
    blk_c = lax.broadcasted_iota(jnp.int32, (n_cmp, tq), 0)
    dist = t - (blk_c * NSA_CMP_STRIDE + (NSA_CMP_BLOCK - 1))
    visible = dist >= 0
    dist_f = dist.astype(F32)
    kc_pieces = kc_ref[...]
    vcT = vcT_ref[...]
    p_sum = jnp.zeros((n_cmp, tq), F32)
    for hh in range(NSA_HEADS_PER_GROUP):
        slope2 = slopes_ref[g * NSA_HEADS_PER_GROUP + hh]
        q = qT_ref[hh * hd:(hh + 1) * hd, :]
        s = _dot(kc_pieces, jnp.concatenate([q] * N_SPLIT, axis=0)) - slope2 * dist_f
        s = jnp.where(visible, s, NEG_INF)
        m = jnp.max(s, axis=0, keepdims=True)
        e = jnp.exp2(s - m)
        any_visible = m > 0.5 * NEG_INF
        p = e * jnp.where(any_visible, 1.0 / jnp.maximum(jnp.sum(e, axis=0, keepdims=True), TINY), 0.0)
        p_sum = p_sum + p
        oT_ref[hh * hd:(hh + 1) * hd, :] = _dot(vcT, p.astype(BF16))
    imp = _dot(ovT_ref[...], jnp.concatenate(_split(p_sum), axis=0))
    ts = qi * tq + lax.broadcasted_iota(jnp.int32, (n_sel, tq), 1)
    blk = lax.broadcasted_iota(jnp.int32, (n_sel, tq), 0)
    cur = ts // NSA_SEL_BLOCK
    causal = blk * NSA_SEL_BLOCK <= ts
    forced = (blk == 0) | (blk == cur) | (blk == cur - 1)
    imp = jnp.where(causal, jnp.where(forced, FORCE_SCORE, imp), NEG_INF)
    rank = _count_rank(imp, n_sel)
    chosen = (rank < min(NSA_SEL_TOPK, n_sel)) & causal
    negsel_ref[...] = jnp.where(chosen, 0.0, NEG_INF).astype(BF16)


def _cmp_select(slopes2, qT, kc, vcT, overlap_t, batch):
    n = qT.shape[1]
    tq = min(CMP_TQ, n // batch)
    nq = n // batch // tq
    rows = NSA_HEADS_PER_GROUP * HEAD_DIM
    n_cmp = kc.shape[2]
    n_sel = overlap_t.shape[0]
    return pl.pallas_call(
        _cmp_select_kernel,
        grid=(batch, NSA_KV_GROUPS, nq),
        in_specs=[
            pl.BlockSpec(memory_space=pltpu.SMEM),
            pl.BlockSpec((rows, tq), lambda i, g, q: (g, i * nq + q)),
            pl.BlockSpec((None, None, n_cmp, N_SPLIT * HEAD_DIM), lambda i, g, q: (i, g, 0, 0)),
            pl.BlockSpec((None, None, HEAD_DIM, n_cmp), lambda i, g, q: (i, NSA_KV_GROUPS + g, 0, 0)),
            pl.BlockSpec((n_sel, N_SPLIT * n_cmp), lambda i, g, q: (0, 0)),
        ],
        out_specs=[
            pl.BlockSpec((rows, tq), lambda i, g, q: (g, i * nq + q)),
            pl.BlockSpec((None, n_sel, tq), lambda i, g, q: (g, 0, i * nq + q)),
        ],
        out_shape=[jax.ShapeDtypeStruct((NSA_WIDTH, n), F32),
                   jax.ShapeDtypeStruct((NSA_KV_GROUPS, n_sel, n), BF16)],
        compiler_params=_cparams("parallel", "parallel", "parallel"),
        name="cmp_select",
    )(slopes2, qT, kc, vcT, overlap_t)


def _moba_select_kernel(qT_ref, k_ref, negsel_ref):
    s = k_ref.shape[0]
    nb = s // MOBA_BLOCK
    hd = HEAD_DIM
    kmean = jnp.mean(k_ref[...].astype(F32).reshape(nb, MOBA_BLOCK, k_ref.shape[1]), axis=1)
    t = lax.broadcasted_iota(jnp.int32, (nb, s), 1)
    blk = lax.broadcasted_iota(jnp.int32, (nb, s), 0)
    cur = t // MOBA_BLOCK
    past = blk < cur
    for j in range(MOBA_HEADS):
        gs = _dot_f32(kmean[:, j * hd:(j + 1) * hd], qT_ref[j * hd:(j + 1) * hd, :].astype(F32))
        gs = jnp.where(past, gs, NEG_INF)
        rank = _count_rank(gs, nb)
        chosen = ((rank < min(MOBA_TOPK, nb - 1)) & past) | (blk == cur)
        negsel_ref[j, 0:nb, :] = jnp.where(chosen, 0.0, NEG_INF).astype(BF16)
        negsel_ref[j, nb:, :] = jnp.zeros((MOBA_SEL_ROWS - nb, s), BF16)


def _moba_select(qT, k_tok, batch, q_row_block, k_col_block):
    n = qT.shape[1]
    s = n // batch
    return pl.pallas_call(
        _moba_select_kernel,
        grid=(batch,),
        in_specs=[pl.BlockSpec((MOBA_WIDTH, s), lambda i: (q_row_block, i)),
                  pl.BlockSpec((s, MOBA_WIDTH), lambda i: (i, k_col_block))],
        out_specs=pl.BlockSpec((MOBA_HEADS, MOBA_SEL_ROWS, s), lambda i: (0, 0, i)),
        out_shape=jax.ShapeDtypeStruct((MOBA_HEADS, MOBA_SEL_ROWS, n), BF16),
        compiler_params=_cparams("parallel"),
        name="moba_select",
    )(qT, k_tok)


def _attn_schedule(nq, window_chunks):
    def padded(pairs, start):
        n_pad = (-(start + len(pairs) - 1)) % ATTN_UNROLL
        return pairs + [(0, 0, 0)] * n_pad

    diag = [(q, q, 1) for q in range(nq)]
    groups = [("diag", padded(diag, 0))]
    if window_chunks is None:
        full = [(q, c, 1) for q in range(nq) for c in range(q)]
        groups.append((None, full))
    else:
        assert window_chunks == 2
        done = len(groups[0][1])
        lower = padded([(q, q - 2, 1) for q in range(2, nq)], done)
        groups.append(("lower", lower))
        groups.append(("middle", [(q, q - 1, 1) for q in range(1, nq)]))
    pairs, kinds = [], []
    for kind, group in groups:
        pairs += group
        kinds += [kind] * len(group)
    n_pad = (-(len(pairs) - 1)) % ATTN_UNROLL
    pairs += [(0, 0, 0)] * n_pad
    kinds += [kinds[-1] if kinds[-1] is not None else "diag"] * n_pad
    body_kinds = []
    for j in range((len(pairs) - 1) // ATTN_UNROLL):
        ks = set(kinds[1 + j * ATTN_UNROLL:1 + (j + 1) * ATTN_UNROLL])
        assert len(ks) == 1, "a loop body must issue scores of one mask kind"
        body_kinds.append(ks.pop())
    table = np.asarray(pairs, np.int32)
    return table[:, 0], table[:, 1], table[:, 2], tuple(body_kinds)


def _attn_kernel(*refs, n_pairs, heads_per_pair, both_slots, heads_per_v, has_sel, slot_of, body_kinds,
                 clamp_sum):
    n_in = 11 if has_sel else 10
    (qtab_ref, ctab_ref, vtab_ref, slopes_ref, qT_ref, k_ref, ktab_ref, vT_ref, qbias_ref,
     maskbias_ref) = refs[:10]
    negsel_ref = refs[10] if has_sel else None
    oT_ref = refs[n_in]
    qaug_ref, vaug_ref, m_ref, acc_ref = refs[n_in + 1:n_in + 5]
    s_bufs = refs[n_in + 5:n_in + 7]
    cmax_bufs = refs[n_in + 7:n_in + 9]
    alpha_bufs = refs[n_in + 9:n_in + 11]
    p_bufs = refs[n_in + 11:n_in + 13]
    grp = pl.program_id(1)
    tq, tk, hd = ATTN_TQ, ATTN_TK, HEAD_DIM
    nq = qaug_ref.shape[0]
    hpp = heads_per_pair
    nh = n_pairs * hpp
    pcol = hpp * tq
    ncol = n_pairs * pcol
    n_vgroups = nh // heads_per_v
    vcol = heads_per_v * tq
    arows = hd + ONES_ROWS
    n_steps = qtab_ref.shape[0]
    unroll = ATTN_UNROLL

    for vg in range(n_vgroups):
        vaug_ref[vg * arows:vg * arows + hd, :] = vT_ref[vg * hd:(vg + 1) * hd, :]
        vaug_ref[vg * arows + hd:(vg + 1) * arows, :] = jnp.ones((ONES_ROWS, vaug_ref.shape[1]), BF16)

    for qi in range(nq):
        toks = slice(qi * tq, (qi + 1) * tq)
        for i in range(nh):
            cols = slice(i * tq, (i + 1) * tq)
            slot = slot_of(i % hpp, grp)
            q = qT_ref[i * hd:(i + 1) * hd, toks]
            bias = [jnp.concatenate([qbias_ref[i]] * (tq // LANES), axis=1)]
            if has_sel:
                bias.append(negsel_ref[i % negsel_ref.shape[0], :, toks])
            used = sum(part.shape[0] for part in bias)
            bias = jnp.concatenate(bias + [jnp.zeros((hd - used, tq), BF16)], axis=0)
            if both_slots:
                zero = jnp.zeros_like(q)
                qaug_ref[qi, 0:hd, cols] = jnp.where(slot == 0, q, zero)
                qaug_ref[qi, hd:2 * hd, cols] = jnp.where(slot == 0, zero, q)
                qaug_ref[qi, 2 * hd:3 * hd, cols] = bias
                qaug_ref[qi, 3 * hd:, cols] = zero
            else:
                qaug_ref[qi, 0:hd, cols] = jnp.where(slot == 0, q, bias)
                qaug_ref[qi, hd:2 * hd, cols] = jnp.where(slot == 0, bias, q)

    lane_half = lax.broadcasted_iota(jnp.int32, (tk, LANES), 1) // hd

    def key_operand(keys, pr, slot):
        k_pair = k_ref[keys, pr * LANES:(pr + 1) * LANES]
        if both_slots:
            return jnp.concatenate([k_pair, ktab_ref[1, keys, :]], axis=1)
        return jnp.where(lane_half == slot, k_pair, ktab_ref[slot, keys, :])

    if both_slots:
        col_groups = [(pr, None, slice(pr * pcol, (pr + 1) * pcol)) for pr in range(n_pairs)]
    else:
        col_groups = [(pr, slot_of(0, grp), slice(pr * pcol, (pr + 1) * pcol)) for pr in range(n_pairs)]

    key_i = lax.broadcasted_iota(jnp.int32, (tk, ncol), 0)
    qry_row = lax.broadcasted_iota(jnp.int32, (1, ncol), 1) % tq
    big = jnp.int32(2 ** 30)

    def visible(kind, valid):
        if kind == "diag":
            return key_i <= jnp.where(valid, qry_row, -1)
        if kind == "lower":
            return key_i > jnp.where(valid, qry_row, big)
        if kind == "middle":
            return key_i >= jnp.where(valid, 0, big)
        return None

    def key_slice(t):
        return pl.ds(pl.multiple_of(ctab_ref[t] * tk, tk), tk)

    def scores(t, buf, kind):
        qi = qtab_ref[t]
        keys = key_slice(t)
        mask = visible(kind, vtab_ref[t] > 0)
        for pr, slot, cols in col_groups:
            s = _dot(key_operand(keys, pr, slot), qaug_ref[qi, :, cols])
            if mask is not None:
                s = jnp.where(mask[:, cols], s, NEG_INF)
            s_bufs[buf][:, cols] = s
            cmax_bufs[buf][:, cols] = jnp.max(s, axis=0, keepdims=True)

    def softmax(t, buf):
        qi = qtab_ref[t]
        m = m_ref[qi]
        m_new = jnp.maximum(m, cmax_bufs[buf][...])
        m_ref[qi] = m_new
        alpha_bufs[buf][...] = jnp.exp2(m - m_new)
        p_bufs[buf][...] = jnp.exp2(s_bufs[buf][...] - m_new).astype(BF16)

    def weighted(t, buf, rescale):
        qi = qtab_ref[t]
        keys = key_slice(t)
        for vg in range(n_vgroups):
            cols = slice(vg * vcol, (vg + 1) * vcol)
            pv = _dot(vaug_ref[vg * arows:(vg + 1) * arows, keys], p_bufs[buf][:, cols])
            old = acc_ref[qi, :, cols]
            acc_ref[qi, :, cols] = (alpha_bufs[buf][:, cols] * old if rescale else old) + pv

    def fast_scores(t, buf, kind):
        qi = qtab_ref[t]
        keys = key_slice(t)
        ref = jnp.where(vtab_ref[t] > 0, m_ref[qi], -NEG_INF)
        for pr, slot, cols in col_groups:
            s = _dot(key_operand(keys, pr, slot), qaug_ref[qi, :, cols])
            if kind in MASK_BIAS_ROW:
                s = s + maskbias_ref[MASK_BIAS_ROW[kind], :, cols]
            p_bufs[buf][:, cols] = jnp.exp2(s - ref[:, cols]).astype(BF16)

    def pipeline(step, max_body):
        first = 0
        for kind in sorted(set(body_kinds), key=body_kinds.index):
            count = body_kinds.count(kind)
            assert body_kinds[first:first + count] == (kind,) * count
            merge = max(m for m in range(1, count + 1) if count % m == 0 and m * unroll <= max_body)
            steps = merge * unroll

            @pl.loop(0, count // merge)
            def _(j, kind=kind, base=first * unroll, steps=steps):
                for u in range(steps):
                    step(base + j * steps + u, u, kind)

            first += count

    last = n_steps - 1

    for qi in range(nq):
        pos = (qi * tq + lax.broadcasted_iota(jnp.int32, (1, tq), 1)).astype(F32)
        for i in range(nh):
            m_ref[qi, :, i * tq:(i + 1) * tq] = slopes_ref[grp * nh + i] * pos
    acc_ref[...] = jnp.zeros(acc_ref.shape, F32)
    fast_scores(0, 0, "diag")

    def fast_step(t, u, kind):
        fast_scores(t + 1, (u + 1) % 2, kind)
        weighted(t, u % 2, rescale=False)

    pipeline(fast_step, ATTN_FAST_BODY)
    weighted(last, last % 2, rescale=False)

    sums = acc_ref[:, hd:hd + 1, :]
    in_range = ((jnp.min(sums) > 2.0 ** -SAFE_EXCESS) & (jnp.max(sums) < 2.0 ** SAFE_EXCESS)
                & (jnp.max(jnp.abs(acc_ref[...])) < SAFE_ABS))
    @pl.when(jnp.logical_not(in_range))
    def _():
        m_ref[...] = jnp.full(m_ref.shape, NEG_INF, F32)
        acc_ref[...] = jnp.zeros(acc_ref.shape, F32)
        p_bufs[1][...] = jnp.zeros(p_bufs[1].shape, BF16)
        alpha_bufs[1][...] = jnp.ones((1, ncol), F32)
        scores(0, 0, "diag")

        def safe_step(t, u, kind):
            scores(t + 1, (u + 1) % 2, kind)
            weighted(jnp.maximum(t - 1, 0), (u + 1) % 2, rescale=True)
            softmax(t, u % 2)

        pipeline(safe_step, ATTN_SAFE_BODY)
        weighted(last - 1, (last - 1) % 2, rescale=True)
        softmax(last, last % 2)
        weighted(last, last % 2, rescale=True)

    for qi in range(nq):
        for i in range(nh):
            cols = slice(i * tq, (i + 1) * tq)
            l = acc_ref[qi, hd:hd + 1, cols]
            oT_ref[i * hd:(i + 1) * hd, qi * tq:(qi + 1) * tq] = (
                acc_ref[qi, 0:hd, cols] * (1.0 / (jnp.maximum(l, TINY) if clamp_sum else l)))


def _attention(slopes2, qT, k_tok, key_table, vT, qbias, negsel, *, batch, n_steps, n_pairs,
               heads_per_pair, both_slots, heads_per_v, q_block, k_block, v_block, slot_of, window=None,
               clamp_sum, name):
    n = qT.shape[1]
    s = n // batch
    tq, tk, hd = ATTN_TQ, ATTN_TK, HEAD_DIM
    assert tq == tk and s % tq == 0
    nq = s // tq
    nh = n_pairs * heads_per_pair
    assert nh % heads_per_v == 0
    ncol = nh * tq
    arows = hd + ONES_ROWS
    n_vgroups = nh // heads_per_v
    if window is not None:
        assert window % tk == 0
    q_tab, c_tab, v_tab, body_kinds = _attn_schedule(nq, None if window is None else window // tk)
    key_i, qry_i = np.arange(tk)[:, None], (np.arange(ncol) % tq)[None, :]
    tables = {"diag": key_i <= qry_i, "lower": key_i > qry_i}
    mask_bias = jnp.asarray(np.stack([np.where(tables[k], 0.0, NEG_INF) for k in MASK_BIAS_ROW
                                      if k in body_kinds]), F32)
    smem = pl.BlockSpec(memory_space=pltpu.SMEM)
    in_specs = [
        smem, smem, smem, smem,
        pl.BlockSpec((nh * hd, s), lambda b, g: (q_block(g), b)),
        pl.BlockSpec((s, n_pairs * LANES), lambda b, g: (b, k_block(g))),
        _resident(key_table.shape),
        pl.BlockSpec((n_vgroups * hd, s), lambda b, g: (v_block(g), b)),
        pl.BlockSpec((nh, 16, LANES), lambda b, g: (g, 0, 0)),
        _resident(mask_bias.shape),
    ]
    args = [jnp.asarray(q_tab), jnp.asarray(c_tab), jnp.asarray(v_tab), slopes2, qT, k_tok, key_table, vT,
            qbias, mask_bias]
    if negsel is not None:
        per_step = negsel.shape[0] // n_steps
        in_specs.append(pl.BlockSpec((per_step, negsel.shape[1], s), lambda b, g: (g, 0, b)))
        args.append(negsel)
    vec = lambda: pltpu.VMEM((1, ncol), F32)
    return pl.pallas_call(
        functools.partial(_attn_kernel, n_pairs=n_pairs, heads_per_pair=heads_per_pair,
                          both_slots=both_slots, heads_per_v=heads_per_v, has_sel=negsel is not None,
                          slot_of=slot_of, body_kinds=body_kinds, clamp_sum=clamp_sum),
        grid=(batch, n_steps),
        in_specs=in_specs,
        out_specs=pl.BlockSpec((nh * hd, s), lambda b, g: (g, b)),
        out_shape=jax.ShapeDtypeStruct((n_steps * nh * hd, n), F32),
        scratch_shapes=[pltpu.VMEM((nq, (4 if both_slots else 2) * hd, ncol), BF16),
                        pltpu.VMEM((n_vgroups * arows, s), BF16),
                        pltpu.VMEM((nq, 1, ncol), F32),
                        pltpu.VMEM((nq, arows, ncol), F32),
                        pltpu.VMEM((tk, ncol), F32), pltpu.VMEM((tk, ncol), F32),
                        vec(), vec(), vec(), vec(),
                        pltpu.VMEM((tk, ncol), BF16), pltpu.VMEM((tk, ncol), BF16)],
        compiler_params=_cparams("parallel", "parallel"),
        name=name,
    )(*args)


def _mix_kernel(h_ref, g_ref, zgT_ref, ocmp_ref, osel_ref, owin_ref, omoba_ref,
                wmerge_ref, wn_ref, wm_ref, wo_ref, o_ref, gate_ref, onsa_ref):
    h = h_ref[...]
    u = _rms(h, g_ref[...]).astype(BF16)
    gate_ref[...] = _sigmoid(zgT_ref[0:gate_ref.shape[0], :])
    hd = HEAD_DIM
    for hh in range(NSA_HEADS):
        rows = slice(hh * hd, (hh + 1) * hd)
        o = None
        for r, branch in enumerate((ocmp_ref, osel_ref, owin_ref)):
            gate = gate_ref[3 * hh + r:3 * hh + r + 1, :]
            term = gate * branch[rows, :]
            o = term if o is None else o + term
        onsa_ref[rows, :] = o.astype(BF16)
    y_n = _dot_tn(onsa_ref[...], wn_ref[...])
    y_m = _dot_tn(omoba_ref[...].astype(BF16), wm_ref[...])
    mixed = (_sigmoid(_dot(u, wmerge_ref[:, :D_MODEL])) * y_n
             + _sigmoid(_dot(u, wmerge_ref[:, D_MODEL:])) * y_m)
    o_ref[...] = h + _dot(mixed.astype(BF16), wo_ref[...])


def _mix(h, g, zgT, ocmpT, oselT, owinT, omobaT, wmerge, wn, wm, wo):
    n = h.shape[0]
    tm = TOKEN_TILE
    rows = lambda width: pl.BlockSpec((tm, width), lambda i: (i, 0))
    cols = lambda height: pl.BlockSpec((height, tm), lambda i: (0, i))
    gate_rows = -(-N_GATE_LOGITS // 8) * 8
    return pl.pallas_call(
        _mix_kernel,
        grid=(n // tm,),
        in_specs=[rows(D_MODEL), _resident((1, D_MODEL)), cols(LANES), cols(NSA_WIDTH), cols(NSA_WIDTH),
                  cols(NSA_WIDTH), cols(MOBA_WIDTH), _resident(wmerge.shape),
                  _resident(wn.shape), _resident(wm.shape), _resident(wo.shape)],
        out_specs=rows(D_MODEL),
        out_shape=jax.ShapeDtypeStruct((n, D_MODEL), F32),
        scratch_shapes=[pltpu.VMEM((gate_rows, tm), F32), pltpu.VMEM((NSA_WIDTH, tm), BF16)],
        compiler_params=_cparams("parallel"),
        name="mix",
    )(h, g, zgT, ocmpT, oselT, owinT, omobaT, wmerge, wn, wm, wo)


def _key_table(s, block):
    pos = np.arange(s)
    tab = np.zeros((s, HEAD_DIM), np.float32)
    tab[:, 0:N_SLOPE_PIECES] = ((pos // LANES) * LANES)[:, None]
    tab[:, N_SLOPE_PIECES:2 * N_SLOPE_PIECES] = (pos % LANES)[:, None]
    if block is not None:
        tab[pos, SLOPE_ROWS + pos // block] = 1.0
    zeros = np.zeros_like(tab)
    both = np.stack([np.concatenate([zeros, tab], axis=1), np.concatenate([tab, zeros], axis=1)])
    return jnp.asarray(both, BF16)


def _slope_rows(slopes2):
    s1 = slopes2.astype(BF16)
    r1 = slopes2 - s1.astype(F32)
    s2 = r1.astype(BF16)
    s3 = (r1 - s2.astype(F32)).astype(BF16)
    rows = jnp.stack([s1, s2, s3, s1, s2, s3], axis=1)
    rows = jnp.pad(rows, ((0, 0), (0, 16 - 2 * N_SLOPE_PIECES)))
    return jnp.broadcast_to(rows[:, :, None], rows.shape + (LANES,))


def _compress_weights(pos, w1):
    st, hd, gw, hidden, groups = NSA_CMP_STRIDE, HEAD_DIM, NSA_KV_WIDTH, NSA_CMP_HIDDEN, NSA_KV_GROUPS
    w = jnp.zeros((2, st, gw, groups * hidden), F32)
    for g in range(groups):
        w = w.at[:, :, g * hd:(g + 1) * hd, g * hidden:(g + 1) * hidden].set(w1.reshape(2, st, hd, hidden))
    ps = jnp.tile(pos.reshape(2, st, 1, hd), (1, 1, groups, 1))
    return ps.reshape(2, 1, st * gw), w.reshape(2, st * gw, groups * hidden).astype(BF16)


def kernel(x, p, ffn1_norm, ffn1_w1, ffn1_w3, ffn1_w2, mix_norm, w_in, cmp_pos_k, cmp_w1_k, cmp_w2_k, cmp_pos_v, cmp_w1_v, cmp_w2_v, w_up_nsa, w_up_moba, w_out, ffn2_norm, ffn2_w1, ffn2_w3, ffn2_w2, ple_norm, w_ple_gate, w_ple, final_norm):
    b, s, d = x.shape
    n = b * s
    depth = p.shape[0]
    n_all = NSA_HEADS + MOBA_HEADS
    slopes = jnp.exp2(-8.0 * (jnp.arange(n_all, dtype=F32) + 1.0) / n_all)
    slopes2_n, slopes2_m = slopes[0::2] * LOG2E, slopes[1::2] * LOG2E
    qbias_n, qbias_m = _slope_rows(slopes2_n), _slope_rows(slopes2_m)

    n_cmp_rows = s // NSA_CMP_STRIDE
    n_cmp = (s - NSA_CMP_BLOCK) // NSA_CMP_STRIDE + 1
    n_sel = s // NSA_SEL_BLOCK
    c_start = np.arange(n_cmp_rows) * NSA_CMP_STRIDE
    s_start = np.arange(n_sel) * NSA_SEL_BLOCK
    overlap_t = ((c_start[None, :] <= s_start[:, None] + NSA_SEL_BLOCK - 1)
                 & (c_start[None, :] + NSA_CMP_BLOCK - 1 >= s_start[:, None])
                 & (np.arange(n_cmp_rows)[None, :] < n_cmp))
    overlap_t = jnp.asarray(np.tile(overlap_t, (1, N_SPLIT)), BF16)
    table_sel = _key_table(s, NSA_SEL_BLOCK)
    table_win = _key_table(s, None)
    table_moba = _key_table(s, MOBA_BLOCK)

    h = x.reshape(n, d)
    for i in range(depth):
        vec = lambda a: a[i].reshape(1, -1)
        wi = w_in[i]
        col = lambda lo, width: wi[:, lo:lo + width]
        wq_t = jnp.concatenate([col(0, NSA_WIDTH), col(OFF_MOBA_Q, MOBA_WIDTH)], axis=1).T.astype(BF16)
        wk = jnp.concatenate([col(OFF_MOBA_K, MOBA_WIDTH), col(OFF_KSLC, NSA_KV_WIDTH),
                              col(OFF_KWIN, NSA_KV_WIDTH)], axis=1).astype(BF16)
        wc = jnp.stack([col(OFF_KCMP, NSA_KV_WIDTH), col(OFF_VCMP, NSA_KV_WIDTH)]).astype(BF16)
        wv_t = jnp.concatenate([col(OFF_VSLC, NSA_KV_WIDTH), col(OFF_VWIN, NSA_KV_WIDTH),
                                col(OFF_MOBA_V, MOBA_WIDTH)], axis=1).T.astype(BF16)
        wg_t = jnp.pad(col(OFF_GATE_LOGITS, N_GATE_LOGITS), ((0, 0), (0, LANES - N_GATE_LOGITS))).T.astype(BF16)
        w_merge = col(OFF_MERGE, 2 * D_MODEL).astype(BF16)
        h, qT, k_tok, cmp_tok, vT, zgT = _ffn(
            h, vec(ffn1_norm), ffn1_w1[i].astype(BF16), ffn1_w3[i].astype(BF16), ffn1_w2[i].astype(BF16),
            project=(vec(mix_norm), wq_t, wk, wc, wv_t, wg_t))

        cmp_w = [_compress_weights(ps[i], w1[i]) for ps, w1 in ((cmp_pos_k, cmp_w1_k), (cmp_pos_v, cmp_w1_v))]
        pos = jnp.stack([pw[0] for pw in cmp_w])
        w1c = jnp.stack([pw[1] for pw in cmp_w])
        w2 = jnp.stack([cmp_w2_k[i], cmp_w2_v[i]])
        cmp_rows = cmp_tok.reshape(2, b, n_cmp_rows, NSA_CMP_STRIDE * NSA_KV_WIDTH)
        kc, kcT = _compress(cmp_rows, pos, w1c, w2, w2.transpose(0, 2, 1))

        ocmpT, negsel_n = _cmp_select(slopes2_n, qT, kc, kcT, overlap_t, b)
        hs = ATTN_HEADS_PER_STEP
        q_rows, pair = hs * HEAD_DIM, 2 * HEAD_DIM
        nsa = dict(batch=b, n_steps=NSA_KV_GROUPS, n_pairs=1, heads_per_pair=hs, both_slots=False, heads_per_v=hs,
                   q_block=lambda g: QT_ROW_NSA // q_rows + g, slot_of=lambda j, g: g, clamp_sum=True)
        oselT = _attention(slopes2_n, qT, k_tok, table_sel, vT, qbias_n, negsel_n,
                           k_block=lambda g: K_COL_SLC // pair, v_block=lambda g: VT_ROW_SLC // HEAD_DIM + g,
                           name="attn_select", **nsa)
        owinT = _attention(slopes2_n, qT, k_tok, table_win, vT, qbias_n, None,
                           k_block=lambda g: K_COL_WIN // pair, v_block=lambda g: VT_ROW_WIN // HEAD_DIM + g,
                           window=NSA_WINDOW, name="attn_window", **nsa)
        negsel_m = _moba_select(qT, k_tok, b, q_row_block=QT_ROW_MOBA // MOBA_WIDTH,
                                k_col_block=K_COL_MOBA // MOBA_WIDTH)
        omobaT = _attention(slopes2_m, qT, k_tok, table_moba, vT, qbias_m, negsel_m, batch=b,
                            n_steps=MOBA_HEADS // hs, n_pairs=hs // 2, heads_per_pair=2, both_slots=True, heads_per_v=1,
                            q_block=lambda g: QT_ROW_MOBA // q_rows + g,
                            k_block=lambda g: K_COL_MOBA // (hs // 2 * pair) + g,
                            v_block=lambda g: VT_ROW_MOBA // q_rows + g, slot_of=lambda j, g: j,
                            clamp_sum=False, name="attn_moba")

        h = _mix(h, vec(mix_norm), zgT, ocmpT, oselT, owinT, omobaT, w_merge,
                 w_up_nsa[i].astype(BF16), w_up_moba[i].astype(BF16), w_out[i].astype(BF16))

        h = _ffn(h, vec(ffn2_norm), ffn2_w1[i].astype(BF16), ffn2_w3[i].astype(BF16), ffn2_w2[i].astype(BF16),
                 tail=(p[i].reshape(n, PLE_DIM), vec(ple_norm), w_ple_gate[i].astype(BF16),
                       w_ple[i].astype(BF16)),
                 final_norm=final_norm.reshape(1, -1) if i + 1 == depth else None)
    return h.reshape(b, s, d)
```
